```python
import math
import jax, jax.numpy as jnp
from jax import lax
import numpy as np

D_MODEL = 2048
BATCH = 16
SEQ = 2048
DEPTH = 4

N_META = 16
GRID_W = 64
Q_BLOCK = 128
ROPE_THETA = 10000.0
NORM_EPS = 1e-6
N_BRANCH = 3

C_CONV = D_MODEL // 2
CONV_K = 31

GQA_HEADS = 8
GQA_KV_HEADS = 2
GQA_HEAD_DIM = 128

MLA_HEADS = 8
MLA_Q_RANK = D_MODEL // 4
MLA_KV_RANK = D_MODEL // 4
MLA_NOPE_DIM = 128
MLA_ROPE_DIM = 64
MLA_V_DIM = 128
MLA_QK_DIM = MLA_NOPE_DIM + MLA_ROPE_DIM

D_FF = -(-8 * D_MODEL // (3 * 256)) * 256

IN_SIZES = (
    2 * C_CONV,
    GQA_HEADS * GQA_HEAD_DIM,
    GQA_KV_HEADS * GQA_HEAD_DIM,
    GQA_KV_HEADS * GQA_HEAD_DIM,
    MLA_Q_RANK,
    MLA_KV_RANK,
    MLA_ROPE_DIM,
    N_BRANCH * D_MODEL,
)
D_IN = sum(IN_SIZES)

kernel_name = 'hybrid_conv_gqa_mla_encoder'


def rmsnorm(x, g):
    xf = x.astype(jnp.float32)
    y = xf * lax.rsqrt(jnp.mean(xf * xf, axis=-1, keepdims=True) + NORM_EPS)
    return (y * g.astype(jnp.float32)).astype(x.dtype)


def layernorm(x, g, b):
    xf = x.astype(jnp.float32)
    mu = jnp.mean(xf, axis=-1, keepdims=True)
    xc = xf - mu
    y = xc * lax.rsqrt(jnp.mean(xc * xc, axis=-1, keepdims=True) + NORM_EPS)
    return (y * g.astype(jnp.float32) + b.astype(jnp.float32)).astype(x.dtype)


def grid_positions(n_tok):
    rows = n_tok // GRID_W
    row = jnp.repeat(jnp.arange(rows, dtype=jnp.float32), GRID_W)
    col = jnp.tile(jnp.arange(GRID_W, dtype=jnp.float32), rows)
    meta = jnp.zeros((N_META,), jnp.float32)
    return jnp.concatenate([meta, row]), jnp.concatenate([meta, col])


def apply_rope(x, pos):
    dim = x.shape[-1]
    half = dim // 2
    inv = ROPE_THETA ** (-jnp.arange(half, dtype=jnp.float32) / half)
    ang = pos[:, None] * inv[None, :]
    cos = jnp.cos(ang)[None, :, None, :].astype(x.dtype)
    sin = jnp.sin(ang)[None, :, None, :].astype(x.dtype)
    x1, x2 = x[..., :half], x[..., half:]
    return jnp.concatenate([x1 * cos - x2 * sin, x2 * cos + x1 * sin], axis=-1)


def axial_rope(x, row_pos, col_pos):
    half = x.shape[-1] // 2
    return jnp.concatenate([apply_rope(x[..., :half], row_pos),
                            apply_rope(x[..., half:], col_pos)], axis=-1)


def _attend(qb, k, v, scale):
    s = jnp.einsum('bqhgd,bkhd->bhgqk', qb, k, preferred_element_type=jnp.float32) * scale
    p = jax.nn.softmax(s, axis=-1).astype(v.dtype)
    return jnp.einsum('bhgqk,bkhe->bqhge', p, v)


def bidir_block_attention(q, k, v, scale):
    b, l, hq, dk = q.shape
    hkv, dv = k.shape[2], v.shape[-1]
    g = hq // hkv
    q = q.reshape(b, l, hkv, g, dk)
    out_meta = _attend(q[:, :N_META], k, v, scale)
    n_tok = l - N_META
    nblk = n_tok // Q_BLOCK
    qr = q[:, N_META:].reshape(b, nblk, Q_BLOCK, hkv, g, dk).transpose(1, 0, 2, 3, 4, 5)
    out_r = lax.map(lambda qb: _attend(qb, k, v, scale), qr)
    out_r = out_r.transpose(1, 0, 2, 3, 4, 5).reshape(b, n_tok, hkv, g, dv)
    out = jnp.concatenate([out_meta, out_r], axis=1)
    return out.reshape(b, l, hq * dv)


def conv_branch(u2, dw, cb, ln_g, ln_b, w_pw):
    a, gte = jnp.split(u2, 2, axis=-1)
    z = a * jax.nn.sigmoid(gte)
    z = lax.conv_general_dilated(z, dw, window_strides=(1,),
                                 padding=[(CONV_K // 2, CONV_K // 2)],
                                 dimension_numbers=('NWC', 'WIO', 'NWC'),
                                 feature_group_count=C_CONV) + cb
    z = jax.nn.silu(layernorm(z, ln_g, ln_b))
    return z @ w_pw


def gqa_branch(q, k, v, qn_g, kn_g, w_o, row_pos, col_pos):
    b, l, _ = q.shape
    q = rmsnorm(q.reshape(b, l, GQA_HEADS, GQA_HEAD_DIM), qn_g)
    k = rmsnorm(k.reshape(b, l, GQA_KV_HEADS, GQA_HEAD_DIM), kn_g)
    v = v.reshape(b, l, GQA_KV_HEADS, GQA_HEAD_DIM)
    q = axial_rope(q, row_pos, col_pos)
    k = axial_rope(k, row_pos, col_pos)
    o = bidir_block_attention(q, k, v, 1.0 / math.sqrt(GQA_HEAD_DIM))
    return o @ w_o


def mla_branch(cq, ckv, kpe, qn_g, w_uq, kvn_g, w_ukv, w_o, row_pos, col_pos):
    b, l, _ = cq.shape
    q = (rmsnorm(cq, qn_g) @ w_uq).reshape(b, l, MLA_HEADS, MLA_QK_DIM)
    q_nope, q_pe = q[..., :MLA_NOPE_DIM], q[..., MLA_NOPE_DIM:]
    q_pe = axial_rope(q_pe, row_pos, col_pos)
    kv = (rmsnorm(ckv, kvn_g) @ w_ukv).reshape(b, l, MLA_HEADS, MLA_NOPE_DIM + MLA_V_DIM)
    k_nope, v = kv[..., :MLA_NOPE_DIM], kv[..., MLA_NOPE_DIM:]
    k_pe = axial_rope(kpe.reshape(b, l, 1, MLA_ROPE_DIM), row_pos, col_pos)
    k = jnp.concatenate([k_nope, jnp.broadcast_to(k_pe, (b, l, MLA_HEADS, MLA_ROPE_DIM))], axis=-1)
    q = jnp.concatenate([q_nope, q_pe], axis=-1)
    o = bidir_block_attention(q, k, v, 1.0 / math.sqrt(MLA_QK_DIM))
    return o @ w_o


def _fwd_setup_inputs(seed: int = 0) -> dict:
    key = jax.random.key(seed)
    ks = jax.random.split(key, 24)

    def nrm(k, shape, scale):
        return jax.random.normal(k, shape, jnp.float32) * scale

    def gain(k, shape):
        return 1.0 + 0.02 * jax.random.normal(k, shape, jnp.float32)

    L = DEPTH
    return {
        'x': nrm(ks[0], (BATCH, SEQ, D_MODEL), 1.0),
        'meta_tokens': nrm(ks[1], (N_META, D_MODEL), 1.0),
        'mix_norm_g': gain(ks[2], (L, D_MODEL)),
        'w_in': nrm(ks[3], (L, D_MODEL, D_IN), D_MODEL ** -0.5),
        'conv_dw': nrm(ks[4], (L, CONV_K, 1, C_CONV), CONV_K ** -0.5),
        'conv_b': nrm(ks[5], (L, C_CONV), 0.02),
        'conv_ln_g': gain(ks[6], (L, C_CONV)),
        'conv_ln_b': nrm(ks[7], (L, C_CONV), 0.02),
        'w_conv_out': nrm(ks[8], (L, C_CONV, D_MODEL), C_CONV ** -0.5),
        'gqa_q_norm_g': gain(ks[9], (L, GQA_HEAD_DIM)),
        'gqa_k_norm_g': gain(ks[10], (L, GQA_HEAD_DIM)),
        'w_gqa_out': nrm(ks[11], (L, GQA_HEADS * GQA_HEAD_DIM, D_MODEL), (GQA_HEADS * GQA_HEAD_DIM) ** -0.5),
        'mla_q_norm_g': gain(ks[12], (L, MLA_Q_RANK)),
        'w_mla_uq': nrm(ks[13], (L, MLA_Q_RANK, MLA_HEADS * MLA_QK_DIM), MLA_Q_RANK ** -0.5),
        'mla_kv_norm_g': gain(ks[14], (L, MLA_KV_RANK)),
        'w_mla_ukv': nrm(ks[15], (L, MLA_KV_RANK, MLA_HEADS * (MLA_NOPE_DIM + MLA_V_DIM)), MLA_KV_RANK ** -0.5),
        'w_mla_out': nrm(ks[16], (L, MLA_HEADS * MLA_V_DIM, D_MODEL), (MLA_HEADS * MLA_V_DIM) ** -0.5),
        'gate_b': nrm(ks[17], (L, N_BRANCH * D_MODEL), 0.1),
        'w_out': nrm(ks[18], (L, D_MODEL, D_MODEL), 0.5 * D_MODEL ** -0.5),
        'ffn_norm_g': gain(ks[19], (L, D_MODEL)),
        'w_ffn_gate': nrm(ks[20], (L, D_MODEL, D_FF), D_MODEL ** -0.5),
        'w_ffn_up': nrm(ks[21], (L, D_MODEL, D_FF), D_MODEL ** -0.5),
        'w_ffn_down': nrm(ks[22], (L, D_FF, D_MODEL), 0.5 * D_FF ** -0.5),
        'final_norm_g': gain(ks[23], (D_MODEL,)),
    }


def _fwd_reference(x, meta_tokens, mix_norm_g, w_in, conv_dw, conv_b, conv_ln_g, conv_ln_b, w_conv_out,
              gqa_q_norm_g, gqa_k_norm_g, w_gqa_out, mla_q_norm_g, w_mla_uq, mla_kv_norm_g,
              w_mla_ukv, w_mla_out, gate_b, w_out, ffn_norm_g, w_ffn_gate, w_ffn_up, w_ffn_down,
              final_norm_g):
    b, n_tok, d = x.shape
    meta = jnp.broadcast_to(meta_tokens.astype(x.dtype)[None], (b, N_META, d))
    h = jnp.concatenate([meta, x], axis=1)
    row_pos, col_pos = grid_positions(n_tok)

    split_points = []
    acc = 0
    for s in IN_SIZES[:-1]:
        acc += s
        split_points.append(acc)

    for i in range(DEPTH):
        u = rmsnorm(h, mix_norm_g[i])
        proj = u @ w_in[i]
        (u_conv, q_g, k_g, v_g, c_q, c_kv, k_pe, gate_logits) = jnp.split(proj, split_points, axis=-1)

        y_a = conv_branch(u_conv, conv_dw[i], conv_b[i], conv_ln_g[i], conv_ln_b[i], w_conv_out[i])
        y_b = gqa_branch(q_g, k_g, v_g, gqa_q_norm_g[i], gqa_k_norm_g[i], w_gqa_out[i], row_pos, col_pos)
        y_c = mla_branch(c_q, c_kv, k_pe, mla_q_norm_g[i], w_mla_uq[i], mla_kv_norm_g[i],
                         w_mla_ukv[i], w_mla_out[i], row_pos, col_pos)

        gates = jax.nn.sigmoid(gate_logits + gate_b[i]).reshape(b, -1, N_BRANCH, d)
        merged = gates[:, :, 0] * y_a + gates[:, :, 1] * y_b + gates[:, :, 2] * y_c
        h = h + merged @ w_out[i]

        v = rmsnorm(h, ffn_norm_g[i])
        h = h + (jax.nn.silu(v @ w_ffn_gate[i]) * (v @ w_ffn_up[i])) @ w_ffn_down[i]

    h = rmsnorm(h, final_norm_g)
    return h[:, N_META:]


import jax as _jax
import jax.numpy as _jnp

TWIN_FORMAT = 'train_step'
FWD_PARAMS = ['x', 'meta_tokens', 'mix_norm_g', 'w_in', 'conv_dw', 'conv_b', 'conv_ln_g', 'conv_ln_b', 'w_conv_out', 'gqa_q_norm_g', 'gqa_k_norm_g', 'w_gqa_out', 'mla_q_norm_g', 'w_mla_uq', 'mla_kv_norm_g', 'w_mla_ukv', 'w_mla_out', 'gate_b', 'w_out', 'ffn_norm_g', 'w_ffn_gate', 'w_ffn_up', 'w_ffn_down', 'final_norm_g']
TWIN_WEIGHTS = ['meta_tokens', 'mix_norm_g', 'w_in', 'conv_dw', 'conv_b', 'conv_ln_g', 'conv_ln_b', 'w_conv_out', 'gqa_q_norm_g', 'gqa_k_norm_g', 'w_gqa_out', 'mla_q_norm_g', 'w_mla_uq', 'mla_kv_norm_g', 'w_mla_ukv', 'w_mla_out', 'gate_b', 'w_out', 'ffn_norm_g', 'w_ffn_gate', 'w_ffn_up', 'w_ffn_down', 'final_norm_g']
TWIN_DIFF_INPUT = 'x'
TWIN_INPUTS = ['x', 'meta_tokens', 'mix_norm_g', 'w_in', 'conv_dw', 'conv_b', 'conv_ln_g', 'conv_ln_b', 'w_conv_out', 'gqa_q_norm_g', 'gqa_k_norm_g', 'w_gqa_out', 'mla_q_norm_g', 'w_mla_uq', 'mla_kv_norm_g', 'w_mla_ukv', 'w_mla_out', 'gate_b', 'w_out', 'ffn_norm_g', 'w_ffn_gate', 'w_ffn_up', 'w_ffn_down', 'final_norm_g', 'loss_target', 'm_meta_tokens', 'm_mix_norm_g', 'm_w_in', 'm_conv_dw', 'm_conv_b', 'm_conv_ln_g', 'm_conv_ln_b', 'm_w_conv_out', 'm_gqa_q_norm_g', 'm_gqa_k_norm_g', 'm_w_gqa_out', 'm_mla_q_norm_g', 'm_w_mla_uq', 'm_mla_kv_norm_g', 'm_w_mla_ukv', 'm_w_mla_out', 'm_gate_b', 'm_w_out', 'm_ffn_norm_g', 'm_w_ffn_gate', 'm_w_ffn_up', 'm_w_ffn_down', 'm_final_norm_g', 'v_meta_tokens', 'v_mix_norm_g', 'v_w_in', 'v_conv_dw', 'v_conv_b', 'v_conv_ln_g', 'v_conv_ln_b', 'v_w_conv_out', 'v_gqa_q_norm_g', 'v_gqa_k_norm_g', 'v_w_gqa_out', 'v_mla_q_norm_g', 'v_w_mla_uq', 'v_mla_kv_norm_g', 'v_w_mla_ukv', 'v_w_mla_out', 'v_gate_b', 'v_w_out', 'v_ffn_norm_g', 'v_w_ffn_gate', 'v_w_ffn_up', 'v_w_ffn_down', 'v_final_norm_g']
TWIN_OUTPUTS = ['loss', 'grad_x', 'grad_meta_tokens', 'grad_mix_norm_g', 'grad_w_in', 'grad_conv_dw', 'grad_conv_b', 'grad_conv_ln_g', 'grad_conv_ln_b', 'grad_w_conv_out', 'grad_gqa_q_norm_g', 'grad_gqa_k_norm_g', 'grad_w_gqa_out', 'grad_mla_q_norm_g', 'grad_w_mla_uq', 'grad_mla_kv_norm_g', 'grad_w_mla_ukv', 'grad_w_mla_out', 'grad_gate_b', 'grad_w_out', 'grad_ffn_norm_g', 'grad_w_ffn_gate', 'grad_w_ffn_up', 'grad_w_ffn_down', 'grad_final_norm_g', 'delta_meta_tokens', 'delta_mix_norm_g', 'delta_w_in', 'delta_conv_dw', 'delta_conv_b', 'delta_conv_ln_g', 'delta_conv_ln_b', 'delta_w_conv_out', 'delta_gqa_q_norm_g', 'delta_gqa_k_norm_g', 'delta_w_gqa_out', 'delta_mla_q_norm_g', 'delta_w_mla_uq', 'delta_mla_kv_norm_g', 'delta_w_mla_ukv', 'delta_w_mla_out', 'delta_gate_b', 'delta_w_out', 'delta_ffn_norm_g', 'delta_w_ffn_gate', 'delta_w_ffn_up', 'delta_w_ffn_down', 'delta_final_norm_g', 'new_m_meta_tokens', 'new_m_mix_norm_g', 'new_m_w_in', 'new_m_conv_dw', 'new_m_conv_b', 'new_m_conv_ln_g', 'new_m_conv_ln_b', 'new_m_w_conv_out', 'new_m_gqa_q_norm_g', 'new_m_gqa_k_norm_g', 'new_m_w_gqa_out', 'new_m_mla_q_norm_g', 'new_m_w_mla_uq', 'new_m_mla_kv_norm_g', 'new_m_w_mla_ukv', 'new_m_w_mla_out', 'new_m_gate_b', 'new_m_w_out', 'new_m_ffn_norm_g', 'new_m_w_ffn_gate', 'new_m_w_ffn_up', 'new_m_w_ffn_down', 'new_m_final_norm_g', 'new_v_meta_tokens', 'new_v_mix_norm_g', 'new_v_w_in', 'new_v_conv_dw', 'new_v_conv_b', 'new_v_conv_ln_g', 'new_v_conv_ln_b', 'new_v_w_conv_out', 'new_v_gqa_q_norm_g', 'new_v_gqa_k_norm_g', 'new_v_w_gqa_out', 'new_v_mla_q_norm_g', 'new_v_w_mla_uq', 'new_v_mla_kv_norm_g', 'new_v_w_mla_ukv', 'new_v_w_mla_out', 'new_v_gate_b', 'new_v_w_out', 'new_v_ffn_norm_g', 'new_v_w_ffn_gate', 'new_v_w_ffn_up', 'new_v_w_ffn_down', 'new_v_final_norm_g']
TWIN_LEAF_KINDS = {'loss': 'loss', 'grad_x': 'grad_x', 'grad_meta_tokens': 'grad_w', 'grad_mix_norm_g': 'grad_w', 'grad_w_in': 'grad_w', 'grad_conv_dw': 'grad_w', 'grad_conv_b': 'grad_w', 'grad_conv_ln_g': 'grad_w', 'grad_conv_ln_b': 'grad_w', 'grad_w_conv_out': 'grad_w', 'grad_gqa_q_norm_g': 'grad_w', 'grad_gqa_k_norm_g': 'grad_w', 'grad_w_gqa_out': 'grad_w', 'grad_mla_q_norm_g': 'grad_w', 'grad_w_mla_uq': 'grad_w', 'grad_mla_kv_norm_g': 'grad_w', 'grad_w_mla_ukv': 'grad_w', 'grad_w_mla_out': 'grad_w', 'grad_gate_b': 'grad_w', 'grad_w_out': 'grad_w', 'grad_ffn_norm_g': 'grad_w', 'grad_w_ffn_gate': 'grad_w', 'grad_w_ffn_up': 'grad_w', 'grad_w_ffn_down': 'grad_w', 'grad_final_norm_g': 'grad_w', 'delta_meta_tokens': 'delta_w', 'delta_mix_norm_g': 'delta_w', 'delta_w_in': 'delta_w', 'delta_conv_dw': 'delta_w', 'delta_conv_b': 'delta_w', 'delta_conv_ln_g': 'delta_w', 'delta_conv_ln_b': 'delta_w', 'delta_w_conv_out': 'delta_w', 'delta_gqa_q_norm_g': 'delta_w', 'delta_gqa_k_norm_g': 'delta_w', 'delta_w_gqa_out': 'delta_w', 'delta_mla_q_norm_g': 'delta_w', 'delta_w_mla_uq': 'delta_w', 'delta_mla_kv_norm_g': 'delta_w', 'delta_w_mla_ukv': 'delta_w', 'delta_w_mla_out': 'delta_w', 'delta_gate_b': 'delta_w', 'delta_w_out': 'delta_w', 'delta_ffn_norm_g': 'delta_w', 'delta_w_ffn_gate': 'delta_w', 'delta_w_ffn_up': 'delta_w', 'delta_w_ffn_down': 'delta_w', 'delta_final_norm_g': 'delta_w', 'new_m_meta_tokens': 'new_m', 'new_m_mix_norm_g': 'new_m', 'new_m_w_in': 'new_m', 'new_m_conv_dw': 'new_m', 'new_m_conv_b': 'new_m', 'new_m_conv_ln_g': 'new_m', 'new_m_conv_ln_b': 'new_m', 'new_m_w_conv_out': 'new_m', 'new_m_gqa_q_norm_g': 'new_m', 'new_m_gqa_k_norm_g': 'new_m', 'new_m_w_gqa_out': 'new_m', 'new_m_mla_q_norm_g': 'new_m', 'new_m_w_mla_uq': 'new_m', 'new_m_mla_kv_norm_g': 'new_m', 'new_m_w_mla_ukv': 'new_m', 'new_m_w_mla_out': 'new_m', 'new_m_gate_b': 'new_m', 'new_m_w_out': 'new_m', 'new_m_ffn_norm_g': 'new_m', 'new_m_w_ffn_gate': 'new_m', 'new_m_w_ffn_up': 'new_m', 'new_m_w_ffn_down': 'new_m', 'new_m_final_norm_g': 'new_m', 'new_v_meta_tokens': 'new_v', 'new_v_mix_norm_g': 'new_v', 'new_v_w_in': 'new_v', 'new_v_conv_dw': 'new_v', 'new_v_conv_b': 'new_v', 'new_v_conv_ln_g': 'new_v', 'new_v_conv_ln_b': 'new_v', 'new_v_w_conv_out': 'new_v', 'new_v_gqa_q_norm_g': 'new_v', 'new_v_gqa_k_norm_g': 'new_v', 'new_v_w_gqa_out': 'new_v', 'new_v_mla_q_norm_g': 'new_v', 'new_v_w_mla_uq': 'new_v', 'new_v_mla_kv_norm_g': 'new_v', 'new_v_w_mla_ukv': 'new_v', 'new_v_w_mla_out': 'new_v', 'new_v_gate_b': 'new_v', 'new_v_w_out': 'new_v', 'new_v_ffn_norm_g': 'new_v', 'new_v_w_ffn_gate': 'new_v', 'new_v_w_ffn_up': 'new_v', 'new_v_w_ffn_down': 'new_v', 'new_v_final_norm_g': 'new_v'}


def _forward(args):
    return _fwd_reference(*[args[k] for k in FWD_PARAMS])


def _output_shape():
    out = _jax.eval_shape(lambda: _forward(_fwd_setup_inputs(0)))
    return out.shape, out.dtype

N_MICROBATCH = 1
ADAM_LR = 0.001
ADAM_B1 = 0.9
ADAM_B2 = 0.999
ADAM_EPS = 1e-08
ADAM_WD = 0.01
ADAM_STEP = 10
PER_EXAMPLE_BATCH_AXIS = {'x': 0, 'loss_target': 0}
SHARED_INPUTS = []
_WEIGHT_DTYPES = {'meta_tokens': _jnp.float32, 'mix_norm_g': _jnp.float32, 'w_in': _jnp.float32, 'conv_dw': _jnp.float32, 'conv_b': _jnp.float32, 'conv_ln_g': _jnp.float32, 'conv_ln_b': _jnp.float32, 'w_conv_out': _jnp.float32, 'gqa_q_norm_g': _jnp.float32, 'gqa_k_norm_g': _jnp.float32, 'w_gqa_out': _jnp.float32, 'mla_q_norm_g': _jnp.float32, 'w_mla_uq': _jnp.float32, 'mla_kv_norm_g': _jnp.float32, 'w_mla_ukv': _jnp.float32, 'w_mla_out': _jnp.float32, 'gate_b': _jnp.float32, 'w_out': _jnp.float32, 'ffn_norm_g': _jnp.float32, 'w_ffn_gate': _jnp.float32, 'w_ffn_up': _jnp.float32, 'w_ffn_down': _jnp.float32, 'final_norm_g': _jnp.float32}
MOMENT_SCALE = {'meta_tokens': 1.191290e-03, 'mix_norm_g': 1.837809e-02, 'w_in': 7.982633e-03, 'conv_dw': 2.135009e-02, 'conv_b': 4.138025e-02, 'conv_ln_g': 2.538614e-02, 'conv_ln_b': 2.259106e-02, 'w_conv_out': 1.478693e-02, 'gqa_q_norm_g': 9.062703e-03, 'gqa_k_norm_g': 9.260385e-03, 'w_gqa_out': 2.877077e-03, 'mla_q_norm_g': 5.251900e-03, 'w_mla_uq': 2.954501e-03, 'mla_kv_norm_g': 7.731923e-03, 'w_mla_ukv': 3.713012e-03, 'w_mla_out': 3.026269e-03, 'gate_b': 3.460491e-03, 'w_out': 3.046571e-02, 'ffn_norm_g': 3.614662e-02, 'w_ffn_gate': 1.547952e-02, 'w_ffn_up': 1.499087e-02, 'w_ffn_down': 4.969703e-02, 'final_norm_g': 1.598694e+01}


def _to_microbatches(a, axis):
    t = _jnp.moveaxis(a, axis, 0)
    t = t.reshape((N_MICROBATCH, t.shape[0] // N_MICROBATCH) + t.shape[1:])
    return _jnp.moveaxis(t, 1, axis + 1)


def setup_inputs(seed: int = 0) -> dict:
    inp = _fwd_setup_inputs(seed)
    key = _jax.random.fold_in(_jax.random.key(seed), 7919)
    shape, _ = _output_shape()
    out = dict(inp)
    out["loss_target"] = _jax.random.normal(_jax.random.fold_in(key, 0), shape, _jnp.float32)
    for i, name in enumerate(TWIN_WEIGHTS):
        w = inp[name].astype(_jnp.float32)
        if MOMENT_SCALE is None:
            s = _jnp.sqrt(_jnp.mean(_jnp.square(w)) + 1e-30)
        else:
            s = MOMENT_SCALE[name]
        km, kv = _jax.random.split(_jax.random.fold_in(key, i + 1))
        out[name] = w
        out["m_" + name] = s * _jax.random.normal(km, w.shape, _jnp.float32)
        out["v_" + name] = (s * s) * _jax.random.uniform(kv, w.shape, _jnp.float32, 0.5, 1.5)
    if N_MICROBATCH > 1:
        for name, axis in PER_EXAMPLE_BATCH_AXIS.items():
            out[name] = _to_microbatches(out[name], axis)
    return {'x': out['x'], 'meta_tokens': out['meta_tokens'], 'mix_norm_g': out['mix_norm_g'], 'w_in': out['w_in'], 'conv_dw': out['conv_dw'], 'conv_b': out['conv_b'], 'conv_ln_g': out['conv_ln_g'], 'conv_ln_b': out['conv_ln_b'], 'w_conv_out': out['w_conv_out'], 'gqa_q_norm_g': out['gqa_q_norm_g'], 'gqa_k_norm_g': out['gqa_k_norm_g'], 'w_gqa_out': out['w_gqa_out'], 'mla_q_norm_g': out['mla_q_norm_g'], 'w_mla_uq': out['w_mla_uq'], 'mla_kv_norm_g': out['mla_kv_norm_g'], 'w_mla_ukv': out['w_mla_ukv'], 'w_mla_out': out['w_mla_out'], 'gate_b': out['gate_b'], 'w_out': out['w_out'], 'ffn_norm_g': out['ffn_norm_g'], 'w_ffn_gate': out['w_ffn_gate'], 'w_ffn_up': out['w_ffn_up'], 'w_ffn_down': out['w_ffn_down'], 'final_norm_g': out['final_norm_g'], 'loss_target': out['loss_target'], 'm_meta_tokens': out['m_meta_tokens'], 'm_mix_norm_g': out['m_mix_norm_g'], 'm_w_in': out['m_w_in'], 'm_conv_dw': out['m_conv_dw'], 'm_conv_b': out['m_conv_b'], 'm_conv_ln_g': out['m_conv_ln_g'], 'm_conv_ln_b': out['m_conv_ln_b'], 'm_w_conv_out': out['m_w_conv_out'], 'm_gqa_q_norm_g': out['m_gqa_q_norm_g'], 'm_gqa_k_norm_g': out['m_gqa_k_norm_g'], 'm_w_gqa_out': out['m_w_gqa_out'], 'm_mla_q_norm_g': out['m_mla_q_norm_g'], 'm_w_mla_uq': out['m_w_mla_uq'], 'm_mla_kv_norm_g': out['m_mla_kv_norm_g'], 'm_w_mla_ukv': out['m_w_mla_ukv'], 'm_w_mla_out': out['m_w_mla_out'], 'm_gate_b': out['m_gate_b'], 'm_w_out': out['m_w_out'], 'm_ffn_norm_g': out['m_ffn_norm_g'], 'm_w_ffn_gate': out['m_w_ffn_gate'], 'm_w_ffn_up': out['m_w_ffn_up'], 'm_w_ffn_down': out['m_w_ffn_down'], 'm_final_norm_g': out['m_final_norm_g'], 'v_meta_tokens': out['v_meta_tokens'], 'v_mix_norm_g': out['v_mix_norm_g'], 'v_w_in': out['v_w_in'], 'v_conv_dw': out['v_conv_dw'], 'v_conv_b': out['v_conv_b'], 'v_conv_ln_g': out['v_conv_ln_g'], 'v_conv_ln_b': out['v_conv_ln_b'], 'v_w_conv_out': out['v_w_conv_out'], 'v_gqa_q_norm_g': out['v_gqa_q_norm_g'], 'v_gqa_k_norm_g': out['v_gqa_k_norm_g'], 'v_w_gqa_out': out['v_w_gqa_out'], 'v_mla_q_norm_g': out['v_mla_q_norm_g'], 'v_w_mla_uq': out['v_w_mla_uq'], 'v_mla_kv_norm_g': out['v_mla_kv_norm_g'], 'v_w_mla_ukv': out['v_w_mla_ukv'], 'v_w_mla_out': out['v_w_mla_out'], 'v_gate_b': out['v_gate_b'], 'v_w_out': out['v_w_out'], 'v_ffn_norm_g': out['v_ffn_norm_g'], 'v_w_ffn_gate': out['v_w_ffn_gate'], 'v_w_ffn_up': out['v_w_ffn_up'], 'v_w_ffn_down': out['v_w_ffn_down'], 'v_final_norm_g': out['v_final_norm_g']}


def _loss(weights, diff, rest, loss_target):
    with _jax.named_scope("forward"):
        args = {**rest, TWIN_DIFF_INPUT: diff, **{k: w.astype(_WEIGHT_DTYPES[k]) for k, w in weights.items()}}
        y = _forward(args)
    with _jax.named_scope("loss_head"):
        err = _jnp.square(y.astype(_jnp.float32) - loss_target)
        return 0.5 * _jnp.sum(_jnp.mean(err, axis=-1)) if err.ndim else 0.5 * err


def _adamw(w, g, m, v):
    m = ADAM_B1 * m + (1.0 - ADAM_B1) * g
    v = ADAM_B2 * v + (1.0 - ADAM_B2) * _jnp.square(g)
    m_hat = m / (1.0 - ADAM_B1 ** ADAM_STEP)
    v_hat = v / (1.0 - ADAM_B2 ** ADAM_STEP)
    delta = -ADAM_LR * (m_hat / (_jnp.sqrt(v_hat) + ADAM_EPS) + ADAM_WD * w)
    return delta, m, v


def reference(x, meta_tokens, mix_norm_g, w_in, conv_dw, conv_b, conv_ln_g, conv_ln_b, w_conv_out, gqa_q_norm_g, gqa_k_norm_g, w_gqa_out, mla_q_norm_g, w_mla_uq, mla_kv_norm_g, w_mla_ukv, w_mla_out, gate_b, w_out, ffn_norm_g, w_ffn_gate, w_ffn_up, w_ffn_down, final_norm_g, loss_target, m_meta_tokens, m_mix_norm_g, m_w_in, m_conv_dw, m_conv_b, m_conv_ln_g, m_conv_ln_b, m_w_conv_out, m_gqa_q_norm_g, m_gqa_k_norm_g, m_w_gqa_out, m_mla_q_norm_g, m_w_mla_uq, m_mla_kv_norm_g, m_w_mla_ukv, m_w_mla_out, m_gate_b, m_w_out, m_ffn_norm_g, m_w_ffn_gate, m_w_ffn_up, m_w_ffn_down, m_final_norm_g, v_meta_tokens, v_mix_norm_g, v_w_in, v_conv_dw, v_conv_b, v_conv_ln_g, v_conv_ln_b, v_w_conv_out, v_gqa_q_norm_g, v_gqa_k_norm_g, v_w_gqa_out, v_mla_q_norm_g, v_w_mla_uq, v_mla_kv_norm_g, v_w_mla_ukv, v_w_mla_out, v_gate_b, v_w_out, v_ffn_norm_g, v_w_ffn_gate, v_w_ffn_up, v_w_ffn_down, v_final_norm_g):
    given = dict(x=x, meta_tokens=meta_tokens, mix_norm_g=mix_norm_g, w_in=w_in, conv_dw=conv_dw, conv_b=conv_b, conv_ln_g=conv_ln_g, conv_ln_b=conv_ln_b, w_conv_out=w_conv_out, gqa_q_norm_g=gqa_q_norm_g, gqa_k_norm_g=gqa_k_norm_g, w_gqa_out=w_gqa_out, mla_q_norm_g=mla_q_norm_g, w_mla_uq=w_mla_uq, mla_kv_norm_g=mla_kv_norm_g, w_mla_ukv=w_mla_ukv, w_mla_out=w_mla_out, gate_b=gate_b, w_out=w_out, ffn_norm_g=ffn_norm_g, w_ffn_gate=w_ffn_gate, w_ffn_up=w_ffn_up, w_ffn_down=w_ffn_down, final_norm_g=final_norm_g, loss_target=loss_target, m_meta_tokens=m_meta_tokens, m_mix_norm_g=m_mix_norm_g, m_w_in=m_w_in, m_conv_dw=m_conv_dw, m_conv_b=m_conv_b, m_conv_ln_g=m_conv_ln_g, m_conv_ln_b=m_conv_ln_b, m_w_conv_out=m_w_conv_out, m_gqa_q_norm_g=m_gqa_q_norm_g, m_gqa_k_norm_g=m_gqa_k_norm_g, m_w_gqa_out=m_w_gqa_out, m_mla_q_norm_g=m_mla_q_norm_g, m_w_mla_uq=m_w_mla_uq, m_mla_kv_norm_g=m_mla_kv_norm_g, m_w_mla_ukv=m_w_mla_ukv, m_w_mla_out=m_w_mla_out, m_gate_b=m_gate_b, m_w_out=m_w_out, m_ffn_norm_g=m_ffn_norm_g, m_w_ffn_gate=m_w_ffn_gate, m_w_ffn_up=m_w_ffn_up, m_w_ffn_down=m_w_ffn_down, m_final_norm_g=m_final_norm_g, v_meta_tokens=v_meta_tokens, v_mix_norm_g=v_mix_norm_g, v_w_in=v_w_in, v_conv_dw=v_conv_dw, v_conv_b=v_conv_b, v_conv_ln_g=v_conv_ln_g, v_conv_ln_b=v_conv_ln_b, v_w_conv_out=v_w_conv_out, v_gqa_q_norm_g=v_gqa_q_norm_g, v_gqa_k_norm_g=v_gqa_k_norm_g, v_w_gqa_out=v_w_gqa_out, v_mla_q_norm_g=v_mla_q_norm_g, v_w_mla_uq=v_w_mla_uq, v_mla_kv_norm_g=v_mla_kv_norm_g, v_w_mla_ukv=v_w_mla_ukv, v_w_mla_out=v_w_mla_out, v_gate_b=v_gate_b, v_w_out=v_w_out, v_ffn_norm_g=v_ffn_norm_g, v_w_ffn_gate=v_w_ffn_gate, v_w_ffn_up=v_w_ffn_up, v_w_ffn_down=v_w_ffn_down, v_final_norm_g=v_final_norm_g)
    weights = {n: given[n] for n in TWIN_WEIGHTS}
    shared = {n: given[n] for n in SHARED_INPUTS}
    per_example = {n: given[n] for n in ['x']}
    grad_fn = _jax.value_and_grad(_loss, argnums=(0, 1))

    def one_microbatch(ex, loss_target):
        ex = dict(ex)
        diff = ex.pop(TWIN_DIFF_INPUT)
        return grad_fn(weights, diff, {**shared, **ex}, loss_target)

    if N_MICROBATCH == 1:
        loss, (grad_w, grad_x) = one_microbatch(per_example, given["loss_target"])
    else:
        def body(carry, xs):
            loss_sum, grad_sum = carry
            l_k, (gw_k, gx_k) = one_microbatch(xs[0], xs[1])
            with _jax.named_scope("update"):
                return (loss_sum + l_k, _jax.tree.map(_jnp.add, grad_sum, gw_k)), gx_k

        init = (_jnp.zeros((), _jnp.float32), _jax.tree.map(_jnp.zeros_like, weights))
        (loss, grad_w), grad_x = _jax.lax.scan(body, init, (per_example, given["loss_target"]))
    with _jax.named_scope("update"):
        delta_w, new_m, new_v = {}, {}, {}
        for n in TWIN_WEIGHTS:
            delta_w[n], new_m[n], new_v[n] = _adamw(weights[n], grad_w[n], given["m_" + n], given["v_" + n])
    return (loss, grad_x, *[grad_w[n] for n in TWIN_WEIGHTS], *[delta_w[n] for n in TWIN_WEIGHTS],
            *[new_m[n] for n in TWIN_WEIGHTS], *[new_v[n] for n in TWIN_WEIGHTS])
```

```python
import math

import jax
import jax.numpy as jnp
from jax import lax
from jax.experimental import pallas as pl
from jax.experimental.pallas import tpu as pltpu

F32 = jnp.float32
BF16 = jnp.bfloat16
SDS = jax.ShapeDtypeStruct
MESH = pl.DeviceIdType.MESH

N_META = 16
GRID_W = 64
ROPE_THETA = 10000.0
NORM_EPS = 1e-6
CONV_K = 31
CONV_HALO = 16
HEAD = 128
GQA_HEADS = 8
GQA_KV_HEADS = 2
MLA_HEADS = 8
MLA_ROPE = 64
MLA_SLOT = 256
N_BRANCH = 3

ADAM_LR = 0.001
ADAM_B1 = 0.9
ADAM_B2 = 0.999
ADAM_EPS = 1e-08
ADAM_WD = 0.01
ADAM_STEP = 10

LANES = 128
SUBLANES_BF16 = 16
VMEM_LIMIT_BYTES = 56 * 2 ** 20
ROW_BLOCK_BYTES = 1 << 20
N_CHIPS = 4
N_DEV = 8
PACK_COLS = 1024
PACK_QUANTUM = 2 * SUBLANES_BF16 * PACK_COLS


def _cparams():
    return pltpu.CompilerParams(vmem_limit_bytes=VMEM_LIMIT_BYTES)


def _tile(n, cap, mult):
    best = None
    for d in range(mult, min(n, cap) + 1, mult):
        if n % d == 0:
            best = d
    return n if best is None else best


def _row_tile(rows, width, mult=SUBLANES_BF16):
    return _tile(rows, max(mult, ROW_BLOCK_BYTES // (4 * width)), mult)


def _sigmoid(x):
    return 1.0 / (1.0 + jnp.exp(-x))


def _mm(a, b, *, mode="nn", add=None, out_dtype=F32, name):
    if mode == "nn":
        (m, k), n = a.shape, b.shape[1]
        assert b.shape[0] == k
    elif mode == "nt":
        (m, k), n = a.shape, b.shape[0]
        assert b.shape[1] == k
    else:
        (k, m), n = a.shape, b.shape[1]
        assert b.shape[0] == k
    if mode == "tn":
        tm = _tile(m, 1024, LANES)
        tk = _tile(k, 2176, SUBLANES_BF16)
    else:
        tm = _tile(m, 1088, SUBLANES_BF16)
        tk = _tile(k, 2176, LANES)
    tn = _tile(n, 1024, LANES)
    nk = k // tk
    if mode == "nn":
        a_spec = pl.BlockSpec((tm, tk), lambda i, j, kk: (i, kk))
        b_spec = pl.BlockSpec((tk, tn), lambda i, j, kk: (kk, j))
        dn = (((1,), (0,)), ((), ()))
    elif mode == "nt":
        a_spec = pl.BlockSpec((tm, tk), lambda i, j, kk: (i, kk))
        b_spec = pl.BlockSpec((tn, tk), lambda i, j, kk: (j, kk))
        dn = (((1,), (1,)), ((), ()))
    else:
        a_spec = pl.BlockSpec((tk, tm), lambda i, j, kk: (kk, i))
        b_spec = pl.BlockSpec((tk, tn), lambda i, j, kk: (kk, j))
        dn = (((0,), (0,)), ((), ()))
    o_spec = pl.BlockSpec((tm, tn), lambda i, j, kk: (i, j))
    has_add = add is not None

    def body(a_ref, b_ref, *rest):
        if has_add:
            add_ref, o_ref, acc = rest
        else:
            o_ref, acc = rest
        kk = pl.program_id(2)
        part = lax.dot_general(a_ref[...].astype(BF16), b_ref[...].astype(BF16), dn,
                               preferred_element_type=F32)

        @pl.when(kk == 0)
        def _():
            acc[...] = part

        @pl.when(kk > 0)
        def _():
            acc[...] += part

        @pl.when(kk == nk - 1)
        def _():
            r = acc[...]
            if has_add:
                r = r + add_ref[...]
            o_ref[...] = r.astype(o_ref.dtype)

    in_specs = [a_spec, b_spec] + ([o_spec] if has_add else [])
    args = (a, b) + ((add,) if has_add else ())
    return pl.pallas_call(
        body, grid=(m // tm, n // tn, nk), in_specs=in_specs, out_specs=o_spec,
        out_shape=SDS((m, n), out_dtype), scratch_shapes=[pltpu.VMEM((tm, tn), F32)],
        compiler_params=_cparams(), name=name)(*args)


def _rmsnorm_fwd(x, g, *, width, cblk, name):
    t = x.shape[0]
    tr = _row_tile(t, width)

    def body(x_ref, g_ref, o_ref):
        xv = x_ref[...]
        r = lax.rsqrt(jnp.mean(xv * xv, axis=-1, keepdims=True) + NORM_EPS)
        o_ref[...] = (xv * r * g_ref[...]).astype(o_ref.dtype)

    return pl.pallas_call(
        body, grid=(t // tr,),
        in_specs=[pl.BlockSpec((tr, width), lambda i: (i, cblk)), pl.BlockSpec((1, width), lambda i: (0, 0))],
        out_specs=pl.BlockSpec((tr, width), lambda i: (i, 0)),
        out_shape=SDS((t, width), BF16), compiler_params=_cparams(), name=name)(x, g)


def _rmsnorm_bwd(x, g, dy, *, width, cblk, res=None, name):
    t = x.shape[0]
    tr = _row_tile(t, width)
    has_res = res is not None

    def body(x_ref, g_ref, dy_ref, *rest):
        if has_res:
            res_ref, dx_ref, dxb_ref, dg_ref = rest
        else:
            dx_ref, dxb_ref, dg_ref = rest
        xv = x_ref[...]
        dyv = dy_ref[...].astype(F32)
        r = lax.rsqrt(jnp.mean(xv * xv, axis=-1, keepdims=True) + NORM_EPS)
        gy = dyv * g_ref[...]
        dx = r * gy - xv * (r * r * r) * jnp.mean(gy * xv, axis=-1, keepdims=True)
        if has_res:
            dx = dx + res_ref[...]
        dx_ref[...] = dx
        dxb_ref[...] = dx.astype(BF16)
        part = jnp.sum(dyv * xv * r, axis=0, keepdims=True)

        @pl.when(pl.program_id(0) == 0)
        def _():
            dg_ref[...] = part

        @pl.when(pl.program_id(0) > 0)
        def _():
            dg_ref[...] += part

    row = pl.BlockSpec((tr, width), lambda i: (i, 0))
    vec = pl.BlockSpec((1, width), lambda i: (0, 0))
    in_specs = [pl.BlockSpec((tr, width), lambda i: (i, cblk)), vec, row] + ([row] if has_res else [])
    args = (x, g, dy) + ((res,) if has_res else ())
    return pl.pallas_call(
        body, grid=(t // tr,), in_specs=in_specs, out_specs=[row, row, vec],
        out_shape=[SDS((t, width), F32), SDS((t, width), BF16), SDS((1, width), F32)],
        compiler_params=_cparams(), name=name)(*args)


def _valid_rows(tile_index, tr, lp, lo, hi):
    pos = (tile_index % (lp // tr)) * tr + lax.broadcasted_iota(jnp.int32, (tr, 1), 0)
    return (pos >= lo) & (pos < hi)


def _loss_head(h, g, target, *, lp, l_valid, name):
    t, d = h.shape
    tr = _tile(lp, max(SUBLANES_BF16, ROW_BLOCK_BYTES // (4 * d)), SUBLANES_BF16)

    def body(x_ref, g_ref, t_ref, loss_ref, dx_ref, dxb_ref, dg_ref):
        i = pl.program_id(0)
        xv = x_ref[...]
        r = lax.rsqrt(jnp.mean(xv * xv, axis=-1, keepdims=True) + NORM_EPS)
        y = xv * r * g_ref[...]
        valid = _valid_rows(i, tr, lp, N_META, l_valid)
        err = jnp.where(valid, y - t_ref[...], 0.0)
        lpart = 0.5 * jnp.sum(jnp.mean(err * err, axis=-1, keepdims=True), axis=0, keepdims=True)
        dyv = err * (1.0 / d)
        gy = dyv * g_ref[...]
        dx = r * gy - xv * (r * r * r) * jnp.mean(gy * xv, axis=-1, keepdims=True)
        dx_ref[...] = dx
        dxb_ref[...] = dx.astype(BF16)
        gpart = jnp.sum(dyv * xv * r, axis=0, keepdims=True)

        @pl.when(i == 0)
        def _():
            dg_ref[...] = gpart
            loss_ref[...] = jnp.broadcast_to(lpart, loss_ref.shape)

        @pl.when(i > 0)
        def _():
            dg_ref[...] += gpart
            loss_ref[...] += jnp.broadcast_to(lpart, loss_ref.shape)

    row = pl.BlockSpec((tr, d), lambda i: (i, 0))
    vec = pl.BlockSpec((1, d), lambda i: (0, 0))
    return pl.pallas_call(
        body, grid=(t // tr,), in_specs=[row, vec, row],
        out_specs=[pl.BlockSpec((1, LANES), lambda i: (0, 0)), row, row, vec],
        out_shape=[SDS((1, LANES), F32), SDS((t, d), F32), SDS((t, d), BF16), SDS((1, d), F32)],
        compiler_params=_cparams(), name=name)(h, g, target)


def _ln_silu_fwd(yc, g, b, *, name):
    t, c = yc.shape
    tr = _row_tile(t, c)

    def body(x_ref, g_ref, b_ref, o_ref):
        xv = x_ref[...]
        xc = xv - jnp.mean(xv, axis=-1, keepdims=True)
        r = lax.rsqrt(jnp.mean(xc * xc, axis=-1, keepdims=True) + NORM_EPS)
        ln = xc * r * g_ref[...] + b_ref[...]
        o_ref[...] = (ln * _sigmoid(ln)).astype(o_ref.dtype)

    row = pl.BlockSpec((tr, c), lambda i: (i, 0))
    vec = pl.BlockSpec((1, c), lambda i: (0, 0))
    return pl.pallas_call(body, grid=(t // tr,), in_specs=[row, vec, vec], out_specs=row,
                          out_shape=SDS((t, c), BF16), compiler_params=_cparams(), name=name)(yc, g, b)


def _ln_silu_bwd(yc, g, b, ds, *, name):
    t, c = yc.shape
    tr = _row_tile(t, c)

    def body(x_ref, g_ref, b_ref, ds_ref, dx_ref, dg_ref, db_ref):
        xv = x_ref[...]
        xc = xv - jnp.mean(xv, axis=-1, keepdims=True)
        r = lax.rsqrt(jnp.mean(xc * xc, axis=-1, keepdims=True) + NORM_EPS)
        xh = xc * r
        ln = xh * g_ref[...] + b_ref[...]
        sg = _sigmoid(ln)
        dln = ds_ref[...].astype(F32) * (sg * (1.0 + ln * (1.0 - sg)))
        gy = dln * g_ref[...]
        dx_ref[...] = r * (gy - jnp.mean(gy, axis=-1, keepdims=True)
                           - xh * jnp.mean(gy * xh, axis=-1, keepdims=True))
        gpart = jnp.sum(dln * xh, axis=0, keepdims=True)
        bpart = jnp.sum(dln, axis=0, keepdims=True)

        @pl.when(pl.program_id(0) == 0)
        def _():
            dg_ref[...] = gpart
            db_ref[...] = bpart

        @pl.when(pl.program_id(0) > 0)
        def _():
            dg_ref[...] += gpart
            db_ref[...] += bpart

    row = pl.BlockSpec((tr, c), lambda i: (i, 0))
    vec = pl.BlockSpec((1, c), lambda i: (0, 0))
    return pl.pallas_call(
        body, grid=(t // tr,), in_specs=[row, vec, vec, row], out_specs=[row, vec, vec],
        out_shape=[SDS((t, c), F32), SDS((1, c), F32), SDS((1, c), F32)],
        compiler_params=_cparams(), name=name)(yc, g, b, ds)


def _swiglu_fwd(gate, up, *, name):
    t, f = gate.shape
    tr = _row_tile(t, f)

    def body(g_ref, u_ref, o_ref):
        gv = g_ref[...]
        o_ref[...] = (gv * _sigmoid(gv) * u_ref[...]).astype(o_ref.dtype)

    row = pl.BlockSpec((tr, f), lambda i: (i, 0))
    return pl.pallas_call(body, grid=(t // tr,), in_specs=[row, row], out_specs=row,
                          out_shape=SDS((t, f), BF16), compiler_params=_cparams(), name=name)(gate, up)


def _swiglu_bwd(gate, up, dact, *, name):
    t, f = gate.shape
    tr = _row_tile(t, f)

    def body(g_ref, u_ref, d_ref, dg_ref, du_ref):
        gv = g_ref[...]
        sg = _sigmoid(gv)
        dv = d_ref[...]
        dg_ref[...] = (dv * u_ref[...] * (sg * (1.0 + gv * (1.0 - sg)))).astype(dg_ref.dtype)
        du_ref[...] = (dv * (gv * sg)).astype(du_ref.dtype)

    row = pl.BlockSpec((tr, f), lambda i: (i, 0))
    return pl.pallas_call(body, grid=(t // tr,), in_specs=[row, row, row], out_specs=[row, row],
                          out_shape=[SDS((t, f), BF16), SDS((t, f), BF16)],
                          compiler_params=_cparams(), name=name)(gate, up, dact)


def _merge_tiles(t, d):
    tc = _tile(d, 512, LANES)
    return _row_tile(t, tc), tc


def _merge_fwd(proj, gate_b, ya, yb, yc, *, gate_off, name):
    t, d = ya.shape
    tr, tc = _merge_tiles(t, d)
    nc = d // tc
    assert gate_off % tc == 0
    g0 = gate_off // tc

    def body(l0, l1, l2, b0, b1, b2, ya_ref, yb_ref, yc_ref, o_ref):
        acc = _sigmoid(l0[...] + b0[...]) * ya_ref[...]
        acc += _sigmoid(l1[...] + b1[...]) * yb_ref[...]
        acc += _sigmoid(l2[...] + b2[...]) * yc_ref[...]
        o_ref[...] = acc.astype(o_ref.dtype)

    lspec = [pl.BlockSpec((tr, tc), lambda i, j, br=br: (i, g0 + br * nc + j)) for br in range(N_BRANCH)]
    bspec = [pl.BlockSpec((1, tc), lambda i, j, br=br: (0, br * nc + j)) for br in range(N_BRANCH)]
    yspec = pl.BlockSpec((tr, tc), lambda i, j: (i, j))
    return pl.pallas_call(
        body, grid=(t // tr, nc), in_specs=lspec + bspec + [yspec] * 3, out_specs=yspec,
        out_shape=SDS((t, d), BF16), compiler_params=_cparams(), name=name)(
            proj, proj, proj, gate_b, gate_b, gate_b, ya, yb, yc)


def _merge_bwd(proj, gate_b, ya, yb, yc, dm, *, gate_off, name):
    t, d = ya.shape
    tr, tc = _merge_tiles(t, d)
    nc = d // tc
    g0 = gate_off // tc

    def body(l0, l1, l2, b0, b1, b2, ya_ref, yb_ref, yc_ref, dm_ref,
             dya, dyb, dyc, dl0, dl1, dl2, db0, db1, db2):
        i = pl.program_id(1)
        dmv = dm_ref[...]
        for l_ref, b_ref, y_ref, dy_ref, dl_ref, db_ref in (
                (l0, b0, ya_ref, dya, dl0, db0), (l1, b1, yb_ref, dyb, dl1, db1), (l2, b2, yc_ref, dyc, dl2, db2)):
            gt = _sigmoid(l_ref[...] + b_ref[...])
            dy_ref[...] = (dmv * gt).astype(dy_ref.dtype)
            dl = dmv * y_ref[...] * gt * (1.0 - gt)
            dl_ref[...] = dl.astype(dl_ref.dtype)
            part = jnp.sum(dl, axis=0, keepdims=True)

            @pl.when(i == 0)
            def _(db_ref=db_ref, part=part):
                db_ref[...] = part

            @pl.when(i > 0)
            def _(db_ref=db_ref, part=part):
                db_ref[...] += part

    lspec = [pl.BlockSpec((tr, tc), lambda j, i, br=br: (i, g0 + br * nc + j)) for br in range(N_BRANCH)]
    bspec = [pl.BlockSpec((1, tc), lambda j, i, br=br: (0, br * nc + j)) for br in range(N_BRANCH)]
    yspec = pl.BlockSpec((tr, tc), lambda j, i: (i, j))
    vspec = pl.BlockSpec((1, tc), lambda j, i: (0, j))
    return pl.pallas_call(
        body, grid=(nc, t // tr), in_specs=lspec + bspec + [yspec] * 4,
        out_specs=[yspec] * 6 + [vspec] * 3,
        out_shape=[SDS((t, d), BF16)] * 6 + [SDS((1, d), F32)] * 3,
        compiler_params=_cparams(), name=name)(proj, proj, proj, gate_b, gate_b, gate_b, ya, yb, yc, dm)


def _conv_fwd(proj3, dw, cb, *, c_conv, l_valid, name):
    b, lp, _ = proj3.shape
    cw = _tile(c_conv, 256, LANES)
    nc = c_conv // cw

    def body(a_ref, gte_ref, dw_ref, cb_ref, o_ref, zs):
        rows = lax.broadcasted_iota(jnp.int32, (lp, 1), 0)
        z = jnp.where(rows < l_valid, a_ref[...] * _sigmoid(gte_ref[...]), 0.0)
        zs[pl.ds(0, CONV_HALO), :] = jnp.zeros((CONV_HALO, cw), F32)
        zs[pl.ds(CONV_HALO, lp), :] = z
        zs[pl.ds(CONV_HALO + lp, CONV_HALO), :] = jnp.zeros((CONV_HALO, cw), F32)
        acc = jnp.broadcast_to(cb_ref[...], (lp, cw))
        for j in range(CONV_K):
            acc = acc + zs[pl.ds(j + 1, lp), :] * dw_ref[pl.ds(j, 1), :]
        o_ref[...] = acc

    seq = lambda off: pl.BlockSpec((None, lp, cw), lambda bi, ci: (bi, 0, off + ci))
    return pl.pallas_call(
        body, grid=(b, nc),
        in_specs=[seq(0), seq(nc), pl.BlockSpec((CONV_K, cw), lambda bi, ci: (0, ci)),
                  pl.BlockSpec((1, cw), lambda bi, ci: (0, ci))],
        out_specs=seq(0), out_shape=SDS((b, lp, c_conv), F32),
        scratch_shapes=[pltpu.VMEM((lp + 2 * CONV_HALO, cw), F32)],
        compiler_params=_cparams(), name=name)(proj3, proj3, dw, cb)


def _conv_bwd(proj3, dw, dyc3, *, c_conv, l_valid, name):
    b, lp, _ = proj3.shape
    cw = _tile(c_conv, 256, LANES)
    nc = c_conv // cw

    def body(a_ref, gte_ref, dw_ref, dy_ref, da_ref, dgte_ref, ddw_ref, dcb_ref, zs, dys):
        bi = pl.program_id(1)
        rows = lax.broadcasted_iota(jnp.int32, (lp, 1), 0)
        valid = rows < l_valid
        av = a_ref[...]
        sg = _sigmoid(gte_ref[...])
        dyv = dy_ref[...]
        zero = jnp.zeros((CONV_HALO, cw), F32)
        zs[pl.ds(0, CONV_HALO), :] = zero
        zs[pl.ds(CONV_HALO, lp), :] = jnp.where(valid, av * sg, 0.0)
        zs[pl.ds(CONV_HALO + lp, CONV_HALO), :] = zero
        dys[pl.ds(0, CONV_HALO), :] = zero
        dys[pl.ds(CONV_HALO, lp), :] = dyv
        dys[pl.ds(CONV_HALO + lp, CONV_HALO), :] = zero

        @pl.when(bi == 0)
        def _():
            ddw_ref[...] = jnp.zeros_like(ddw_ref)
            dcb_ref[...] = jnp.zeros_like(dcb_ref)

        dz = jnp.zeros((lp, cw), F32)
        for j in range(CONV_K):
            dz = dz + dys[pl.ds(CONV_K - j, lp), :] * dw_ref[pl.ds(j, 1), :]
            ddw_ref[pl.ds(j, 1), :] += jnp.sum(dyv * zs[pl.ds(j + 1, lp), :], axis=0, keepdims=True)
        dcb_ref[...] += jnp.sum(dyv, axis=0, keepdims=True)
        dz = jnp.where(valid, dz, 0.0)
        da_ref[...] = (dz * sg).astype(da_ref.dtype)
        dgte_ref[...] = (dz * av * sg * (1.0 - sg)).astype(dgte_ref.dtype)

    seq = lambda off: pl.BlockSpec((None, lp, cw), lambda ci, bi: (bi, 0, off + ci))
    return pl.pallas_call(
        body, grid=(nc, b),
        in_specs=[seq(0), seq(nc), pl.BlockSpec((CONV_K, cw), lambda ci, bi: (0, ci)), seq(0)],
        out_specs=[seq(0), seq(0), pl.BlockSpec((CONV_K, cw), lambda ci, bi: (0, ci)),
                   pl.BlockSpec((1, cw), lambda ci, bi: (0, ci))],
        out_shape=[SDS((b, lp, c_conv), BF16), SDS((b, lp, c_conv), BF16),
                   SDS((CONV_K, c_conv), F32), SDS((1, c_conv), F32)],
        scratch_shapes=[pltpu.VMEM((lp + 2 * CONV_HALO, cw), F32)] * 2,
        compiler_params=_cparams(), name=name)(proj3, proj3, dw, dyc3)


def _swap_halves(x, group):
    half = group // 2
    lane = lax.broadcasted_iota(jnp.int32, x.shape, 1)
    up = pltpu.roll(x, LANES - half, 1)
    down = pltpu.roll(x, half, 1)
    return jnp.where((lane % group) < half, up, down)


def _rope(x, cos, sin, group):
    return x * cos + _swap_halves(x, group) * sin


def _rope_bwd(dy, cos, sin, group):
    return dy * cos + _swap_halves(dy * sin, group)


def _rope_tables(lp, seq, dim, lanes):
    quarter = dim // 4
    tok = jnp.arange(seq, dtype=jnp.int32)
    zeros = jnp.zeros((N_META,), F32)
    pad = jnp.zeros((lp - N_META - seq,), F32)
    row = jnp.concatenate([zeros, (tok // GRID_W).astype(F32), pad])
    col = jnp.concatenate([zeros, (tok % GRID_W).astype(F32), pad])
    inv = ROPE_THETA ** (-jnp.arange(quarter, dtype=F32) / quarter)
    ar, ac = row[:, None] * inv[None, :], col[:, None] * inv[None, :]
    cos = jnp.concatenate([jnp.cos(ar), jnp.cos(ar), jnp.cos(ac), jnp.cos(ac)], axis=1)
    sin = jnp.concatenate([-jnp.sin(ar), jnp.sin(ar), -jnp.sin(ac), jnp.sin(ac)], axis=1)
    if lanes > dim:
        cos = jnp.concatenate([cos, jnp.ones((lp, lanes - dim), F32)], axis=1)
        sin = jnp.concatenate([sin, jnp.zeros((lp, lanes - dim), F32)], axis=1)
    return cos, sin


def _gqa_prep_fwd(proj, qg, kg, cos, sin, *, q_off, k_off, v_off, lp, name):
    t = proj.shape[0]
    qw, kw = GQA_HEADS * HEAD, GQA_KV_HEADS * HEAD
    tr = _tile(lp, max(SUBLANES_BF16, ROW_BLOCK_BYTES // (4 * qw)), SUBLANES_BF16)
    npos = lp // tr
    assert q_off % qw == 0 and k_off % kw == 0 and v_off % kw == 0

    def body(q_ref, k_ref, v_ref, qg_ref, kg_ref, cos_ref, sin_ref, qo_ref, ko_ref, vo_ref):
        cosv, sinv = cos_ref[...], sin_ref[...]
        for src, g_ref, dst, heads in ((q_ref, qg_ref, qo_ref, GQA_HEADS), (k_ref, kg_ref, ko_ref, GQA_KV_HEADS)):
            for hh in range(heads):
                xh = src[:, hh * HEAD:(hh + 1) * HEAD]
                r = lax.rsqrt(jnp.mean(xh * xh, axis=-1, keepdims=True) + NORM_EPS)
                dst[:, hh * HEAD:(hh + 1) * HEAD] = _rope(xh * r * g_ref[...], cosv, sinv, 64).astype(dst.dtype)
        vo_ref[...] = v_ref[...].astype(vo_ref.dtype)

    tab = pl.BlockSpec((tr, HEAD), lambda i: (i % npos, 0))
    vec = pl.BlockSpec((1, HEAD), lambda i: (0, 0))
    return pl.pallas_call(
        body, grid=(t // tr,),
        in_specs=[pl.BlockSpec((tr, qw), lambda i: (i, q_off // qw)),
                  pl.BlockSpec((tr, kw), lambda i: (i, k_off // kw)),
                  pl.BlockSpec((tr, kw), lambda i: (i, v_off // kw)), vec, vec, tab, tab],
        out_specs=[pl.BlockSpec((tr, qw), lambda i: (i, 0)), pl.BlockSpec((tr, kw), lambda i: (i, 0)),
                   pl.BlockSpec((tr, kw), lambda i: (i, 0))],
        out_shape=[SDS((t, qw), BF16), SDS((t, kw), BF16), SDS((t, kw), BF16)],
        compiler_params=_cparams(), name=name)(proj, proj, proj, qg, kg, cos, sin)


def _gqa_prep_bwd(proj, qg, kg, cos, sin, dqr, dkr, *, q_off, k_off, lp, name):
    t = proj.shape[0]
    qw, kw = GQA_HEADS * HEAD, GQA_KV_HEADS * HEAD
    tr = _tile(lp, max(SUBLANES_BF16, ROW_BLOCK_BYTES // (4 * qw)), SUBLANES_BF16)
    npos = lp // tr

    def body(q_ref, k_ref, qg_ref, kg_ref, cos_ref, sin_ref, dqr_ref, dkr_ref, dq_ref, dk_ref, dqg_ref, dkg_ref):
        cosv, sinv = cos_ref[...], sin_ref[...]
        for src, g_ref, dy_ref, dx_ref, dg_ref, heads in (
                (q_ref, qg_ref, dqr_ref, dq_ref, dqg_ref, GQA_HEADS),
                (k_ref, kg_ref, dkr_ref, dk_ref, dkg_ref, GQA_KV_HEADS)):
            gpart = jnp.zeros((1, HEAD), F32)
            for hh in range(heads):
                sl = slice(hh * HEAD, (hh + 1) * HEAD)
                xh = src[:, sl]
                r = lax.rsqrt(jnp.mean(xh * xh, axis=-1, keepdims=True) + NORM_EPS)
                dxn = _rope_bwd(dy_ref[:, sl], cosv, sinv, 64)
                gy = dxn * g_ref[...]
                dx = r * gy - xh * (r * r * r) * jnp.mean(gy * xh, axis=-1, keepdims=True)
                dx_ref[:, sl] = dx.astype(dx_ref.dtype)
                gpart = gpart + jnp.sum(dxn * xh * r, axis=0, keepdims=True)

            @pl.when(pl.program_id(0) == 0)
            def _(dg_ref=dg_ref, gpart=gpart):
                dg_ref[...] = gpart

            @pl.when(pl.program_id(0) > 0)
            def _(dg_ref=dg_ref, gpart=gpart):
                dg_ref[...] += gpart

    tab = pl.BlockSpec((tr, HEAD), lambda i: (i % npos, 0))
    vec = pl.BlockSpec((1, HEAD), lambda i: (0, 0))
    qrow = pl.BlockSpec((tr, qw), lambda i: (i, 0))
    krow = pl.BlockSpec((tr, kw), lambda i: (i, 0))
    return pl.pallas_call(
        body, grid=(t // tr,),
        in_specs=[pl.BlockSpec((tr, qw), lambda i: (i, q_off // qw)),
                  pl.BlockSpec((tr, kw), lambda i: (i, k_off // kw)), vec, vec, tab, tab, qrow, krow],
        out_specs=[qrow, krow, vec, vec],
        out_shape=[SDS((t, qw), BF16), SDS((t, kw), BF16), SDS((1, HEAD), F32), SDS((1, HEAD), F32)],
        compiler_params=_cparams(), name=name)(proj, proj, qg, kg, cos, sin, dqr, dkr)


def _mla_rope_fwd(q_pre, k_pre, proj, cos, sin, *, kpe_off, lp, name):
    t, w = q_pre.shape
    tr = _tile(lp, max(SUBLANES_BF16, ROW_BLOCK_BYTES // (4 * w)), SUBLANES_BF16)
    npos = lp // tr
    assert kpe_off % LANES == 0

    def body(q_ref, k_ref, kpe_ref, cos_ref, sin_ref, qo_ref, ko_ref):
        cosv, sinv = cos_ref[...], sin_ref[...]
        kr = _rope(kpe_ref[...], cosv, sinv, 32).astype(ko_ref.dtype)
        for hh in range(MLA_HEADS):
            base = hh * MLA_SLOT
            qo_ref[:, base:base + HEAD] = q_ref[:, base:base + HEAD].astype(qo_ref.dtype)
            qo_ref[:, base + HEAD:base + MLA_SLOT] = _rope(
                q_ref[:, base + HEAD:base + MLA_SLOT], cosv, sinv, 32).astype(qo_ref.dtype)
            ko_ref[:, base:base + HEAD] = k_ref[:, base:base + HEAD].astype(ko_ref.dtype)
            ko_ref[:, base + HEAD:base + MLA_SLOT] = kr

    row = pl.BlockSpec((tr, w), lambda i: (i, 0))
    tab = pl.BlockSpec((tr, LANES), lambda i: (i % npos, 0))
    return pl.pallas_call(
        body, grid=(t // tr,),
        in_specs=[row, row, pl.BlockSpec((tr, LANES), lambda i: (i, kpe_off // LANES)), tab, tab],
        out_specs=[row, row], out_shape=[SDS((t, w), BF16), SDS((t, w), BF16)],
        compiler_params=_cparams(), name=name)(q_pre, k_pre, proj, cos, sin)


def _mla_rope_bwd(dq, dk, cos, sin, *, lp, name):
    t, w = dq.shape
    tr = _tile(lp, max(SUBLANES_BF16, ROW_BLOCK_BYTES // (4 * w)), SUBLANES_BF16)
    npos = lp // tr

    def body(dq_ref, dk_ref, cos_ref, sin_ref, dqo_ref, dko_ref, dkpe_ref):
        cosv, sinv = cos_ref[...], sin_ref[...]
        pe = jnp.zeros((tr, LANES), F32)
        for hh in range(MLA_HEADS):
            base = hh * MLA_SLOT
            dqo_ref[:, base:base + HEAD] = dq_ref[:, base:base + HEAD].astype(dqo_ref.dtype)
            dqo_ref[:, base + HEAD:base + MLA_SLOT] = _rope_bwd(
                dq_ref[:, base + HEAD:base + MLA_SLOT], cosv, sinv, 32).astype(dqo_ref.dtype)
            dko_ref[:, base:base + HEAD] = dk_ref[:, base:base + HEAD].astype(dko_ref.dtype)
            dko_ref[:, base + HEAD:base + MLA_SLOT] = jnp.zeros((tr, LANES), dko_ref.dtype)
            pe = pe + dk_ref[:, base + HEAD:base + MLA_SLOT]
        dkpe_ref[...] = _rope_bwd(pe, cosv, sinv, 32).astype(dkpe_ref.dtype)

    row = pl.BlockSpec((tr, w), lambda i: (i, 0))
    tab = pl.BlockSpec((tr, LANES), lambda i: (i % npos, 0))
    return pl.pallas_call(
        body, grid=(t // tr,), in_specs=[row, row, tab, tab],
        out_specs=[row, row, pl.BlockSpec((tr, LANES), lambda i: (i, 0))],
        out_shape=[SDS((t, w), BF16), SDS((t, w), BF16), SDS((t, LANES), BF16)],
        compiler_params=_cparams(), name=name)(dq, dk, cos, sin)


def _softmax_rows(q, k, scale, lp, l_valid):
    s = lax.dot_general(q, k, (((1,), (1,)), ((), ())), preferred_element_type=F32) * scale
    cols = lax.broadcasted_iota(jnp.int32, (1, lp), 1)
    s = jnp.where(cols < l_valid, s, -1e30)
    e = jnp.exp(s - jnp.max(s, axis=-1, keepdims=True))
    return e / jnp.sum(e, axis=-1, keepdims=True)


def _attn_fwd(q, k, v, *, batch, lp, l_valid, heads, kv_heads, dk, dv, scale, name):
    group = heads // kv_heads
    tq = _tile(lp, 512, SUBLANES_BF16)
    nq = lp // tq

    def body(q_ref, k_ref, v_ref, o_ref):
        p = _softmax_rows(q_ref[...], k_ref[...], scale, lp, l_valid)
        o_ref[...] = jnp.dot(p.astype(BF16), v_ref[...], preferred_element_type=F32).astype(o_ref.dtype)

    return pl.pallas_call(
        body, grid=(batch, heads, nq),
        in_specs=[pl.BlockSpec((tq, dk), lambda b, h, i: (b * nq + i, h)),
                  pl.BlockSpec((lp, dk), lambda b, h, i: (b, h // group)),
                  pl.BlockSpec((lp, dv), lambda b, h, i: (b, h // group))],
        out_specs=pl.BlockSpec((tq, dv), lambda b, h, i: (b * nq + i, h)),
        out_shape=SDS((batch * lp, heads * dv), BF16), compiler_params=_cparams(), name=name)(q, k, v)


def _attn_bwd(q, k, v, do, *, batch, lp, l_valid, heads, kv_heads, dk, dv, scale, name):
    group = heads // kv_heads
    tq = _tile(lp, 512, SUBLANES_BF16)
    nq = lp // tq

    def body(q_ref, k_ref, v_ref, do_ref, dq_ref, dk_ref, dv_ref):
        first = (pl.program_id(2) == 0) & (pl.program_id(3) == 0)
        qv, kv, dov = q_ref[...], k_ref[...], do_ref[...]
        p = _softmax_rows(qv, kv, scale, lp, l_valid)
        dp = lax.dot_general(dov, v_ref[...], (((1,), (1,)), ((), ())), preferred_element_type=F32)
        ds = (p * (dp - jnp.sum(p * dp, axis=-1, keepdims=True)) * scale).astype(BF16)
        dq_ref[...] = jnp.dot(ds, kv, preferred_element_type=F32)
        dkp = lax.dot_general(ds, qv, (((0,), (0,)), ((), ())), preferred_element_type=F32)
        dvp = lax.dot_general(p.astype(BF16), dov, (((0,), (0,)), ((), ())), preferred_element_type=F32)

        @pl.when(first)
        def _():
            dk_ref[...] = dkp
            dv_ref[...] = dvp

        @pl.when(jnp.logical_not(first))
        def _():
            dk_ref[...] += dkp
            dv_ref[...] += dvp

    return pl.pallas_call(
        body, grid=(batch, kv_heads, group, nq),
        in_specs=[pl.BlockSpec((tq, dk), lambda b, hk, g, i: (b * nq + i, hk * group + g)),
                  pl.BlockSpec((lp, dk), lambda b, hk, g, i: (b, hk)),
                  pl.BlockSpec((lp, dv), lambda b, hk, g, i: (b, hk)),
                  pl.BlockSpec((tq, dv), lambda b, hk, g, i: (b * nq + i, hk * group + g))],
        out_specs=[pl.BlockSpec((tq, dk), lambda b, hk, g, i: (b * nq + i, hk * group + g)),
                   pl.BlockSpec((lp, dk), lambda b, hk, g, i: (b, hk)),
                   pl.BlockSpec((lp, dv), lambda b, hk, g, i: (b, hk))],
        out_shape=[SDS((batch * lp, heads * dk), F32), SDS((batch * lp, kv_heads * dk), F32),
                   SDS((batch * lp, kv_heads * dv), F32)],
        compiler_params=_cparams(), name=name)(q, k, v, do)


def _meta_grad(dh3, *, name):
    b, _, d = dh3.shape

    def body(x_ref, o_ref):
        @pl.when(pl.program_id(0) == 0)
        def _():
            o_ref[...] = x_ref[...]

        @pl.when(pl.program_id(0) > 0)
        def _():
            o_ref[...] += x_ref[...]

    return pl.pallas_call(
        body, grid=(b,), in_specs=[pl.BlockSpec((None, N_META, d), lambda bi: (bi, 0, 0))],
        out_specs=pl.BlockSpec((N_META, d), lambda bi: (0, 0)), out_shape=SDS((N_META, d), F32),
        compiler_params=_cparams(), name=name)(dh3)


def _sum_slots(parts, *, out_dtype, name):
    _, r, c = parts[0].shape
    tr = _row_tile(r, c)
    counts = [p.shape[0] for p in parts]

    def body(*refs):
        o_ref = refs[-1]
        acc = None
        for ref, cnt in zip(refs[:-1], counts):
            for s in range(cnt):
                term = ref[s].astype(F32)
                acc = term if acc is None else acc + term
        o_ref[...] = acc.astype(o_ref.dtype)

    return pl.pallas_call(
        body, grid=(r // tr,),
        in_specs=[pl.BlockSpec((cnt, tr, c), lambda i: (0, i, 0)) for cnt in counts],
        out_specs=pl.BlockSpec((tr, c), lambda i: (i, 0)), out_shape=SDS((r, c), out_dtype),
        compiler_params=_cparams(), name=name)(*parts)


def _adamw(w, g, m, v, *, name):
    shape = w.shape
    cols = shape[-1]
    rows = w.size // cols
    tr = _tile(rows, max(8, ROW_BLOCK_BYTES // (4 * cols)), 8)
    bc1 = 1.0 - ADAM_B1 ** ADAM_STEP
    bc2 = 1.0 - ADAM_B2 ** ADAM_STEP

    def body(w_ref, g_ref, m_ref, v_ref, d_ref, mo_ref, vo_ref):
        gv = g_ref[...]
        mn = ADAM_B1 * m_ref[...] + (1.0 - ADAM_B1) * gv
        vn = ADAM_B2 * v_ref[...] + (1.0 - ADAM_B2) * (gv * gv)
        mo_ref[...] = mn
        vo_ref[...] = vn
        d_ref[...] = -ADAM_LR * ((mn / bc1) / (jnp.sqrt(vn / bc2) + ADAM_EPS) + ADAM_WD * w_ref[...])

    spec = pl.BlockSpec((tr, cols), lambda i: (i, 0))
    outs = pl.pallas_call(
        body, grid=(rows // tr,), in_specs=[spec] * 4, out_specs=[spec] * 3,
        out_shape=[SDS((rows, cols), F32)] * 3, compiler_params=_cparams(), name=name)(
            *[a.reshape(rows, cols) for a in (w, g, m, v)])
    return tuple(o.reshape(shape) for o in outs)


def _mesh_pos():
    return lax.axis_index("x"), lax.axis_index("y"), lax.axis_index("c")


def _other_chips(x, y):
    return [(1 - x, y), (x, 1 - y), (1 - x, 1 - y)]


def _chip_allgather_small(vec, *, name):
    r, c = vec.shape

    def body(x_ref, out_ref, send_sems, recv_sems):
        x, y, cc = _mesh_pos()
        mine = 2 * x + y
        out_ref[mine] = x_ref[...]
        chips = _other_chips(x, y)
        sends = [pltpu.make_async_remote_copy(
            src_ref=x_ref, dst_ref=out_ref.at[mine], send_sem=send_sems.at[j], recv_sem=recv_sems.at[j],
            device_id=(px, py, cc), device_id_type=MESH) for j, (px, py) in enumerate(chips)]
        for cp in sends:
            cp.start()
        for j, (px, py) in enumerate(chips):
            pltpu.make_async_remote_copy(
                src_ref=x_ref, dst_ref=out_ref.at[2 * px + py], send_sem=send_sems.at[j],
                recv_sem=recv_sems.at[j], device_id=(px, py, cc), device_id_type=MESH).wait_recv()
        for cp in sends:
            cp.wait_send()

    return pl.pallas_call(
        body, out_shape=SDS((N_CHIPS, r, c), F32),
        in_specs=[pl.BlockSpec(memory_space=pltpu.VMEM)], out_specs=pl.BlockSpec(memory_space=pltpu.VMEM),
        scratch_shapes=[pltpu.SemaphoreType.DMA((3,)), pltpu.SemaphoreType.DMA((3,))],
        compiler_params=_cparams(), name=name)(vec)


def _allreduce_small(vec, *, name):
    r, c = vec.shape

    def body(x_ref, out_ref, buf, send_sems, recv_sems):
        x, y, cc = _mesh_pos()
        me = 4 * x + 2 * y + cc
        buf[me] = x_ref[...]

        def peer(kk):
            fx, fy, fc = (kk >> 2) & 1, (kk >> 1) & 1, kk & 1
            px = x if fx == 0 else 1 - x
            py = y if fy == 0 else 1 - y
            pc = cc if fc == 0 else 1 - cc
            return (px, py, pc), 4 * px + 2 * py + pc

        sends = []
        for kk in range(1, N_DEV):
            dev, _ = peer(kk)
            sends.append(pltpu.make_async_remote_copy(
                src_ref=x_ref, dst_ref=buf.at[me], send_sem=send_sems.at[kk - 1], recv_sem=recv_sems.at[kk - 1],
                device_id=dev, device_id_type=MESH))
        for cp in sends:
            cp.start()
        for kk in range(1, N_DEV):
            dev, slot = peer(kk)
            pltpu.make_async_remote_copy(
                src_ref=x_ref, dst_ref=buf.at[slot], send_sem=send_sems.at[kk - 1], recv_sem=recv_sems.at[kk - 1],
                device_id=dev, device_id_type=MESH).wait_recv()
        for cp in sends:
            cp.wait_send()
        acc = buf[0]
        for s in range(1, N_DEV):
            acc = acc + buf[s]
        out_ref[...] = acc

    return pl.pallas_call(
        body, out_shape=SDS((r, c), F32),
        in_specs=[pl.BlockSpec(memory_space=pltpu.VMEM)], out_specs=pl.BlockSpec(memory_space=pltpu.VMEM),
        scratch_shapes=[pltpu.VMEM((N_DEV, r, c), F32), pltpu.SemaphoreType.DMA((N_DEV - 1,)),
                        pltpu.SemaphoreType.DMA((N_DEV - 1,))],
        compiler_params=_cparams(), name=name)(vec)


_HBM = pl.BlockSpec(memory_space=pltpu.HBM)


def _weights_allgather(wp, *, name):
    _, r, c = wp.shape

    def body(w_ref, out_ref, send_sems, recv_sems, local_sems):
        x, y, cc = _mesh_pos()
        mine = 2 * x + y
        chips = _other_chips(x, y)
        local = [pltpu.make_async_copy(w_ref.at[hf], out_ref.at[hf, mine], local_sems.at[hf]) for hf in range(2)]
        for cp in local:
            cp.start()

        def over_ici(j, px, py, slot):
            return pltpu.make_async_remote_copy(
                src_ref=w_ref.at[cc], dst_ref=out_ref.at[cc, slot], send_sem=send_sems.at[j],
                recv_sem=recv_sems.at[j], device_id=(px, py, cc), device_id_type=MESH)

        def to_sibling(j, half, slot):
            return pltpu.make_async_remote_copy(
                src_ref=out_ref.at[half, slot], dst_ref=out_ref.at[half, slot], send_sem=send_sems.at[3 + j],
                recv_sem=recv_sems.at[3 + j], device_id=(x, y, 1 - cc), device_id_type=MESH)

        first = [over_ici(j, px, py, mine) for j, (px, py) in enumerate(chips)]
        for cp in first:
            cp.start()
        passed = [to_sibling(j, cc, 2 * px + py) for j, (px, py) in enumerate(chips)]
        for j, (px, py) in enumerate(chips):
            over_ici(j, px, py, 2 * px + py).wait_recv()
            passed[j].start()
        for j, (px, py) in enumerate(chips):
            to_sibling(j, 1 - cc, 2 * px + py).wait_recv()
        for cp in first + passed:
            cp.wait_send()
        for cp in local:
            cp.wait()

    return pl.pallas_call(
        body, out_shape=SDS((2, N_CHIPS, r, c), wp.dtype), in_specs=[_HBM], out_specs=_HBM,
        scratch_shapes=[pltpu.SemaphoreType.DMA((6,)), pltpu.SemaphoreType.DMA((6,)),
                        pltpu.SemaphoreType.DMA((2,))],
        name=name)(wp)


def _grads_to_sibling(gp, *, name):
    _, s, r, c = gp.shape

    def body(g_ref, out_ref, send_sem, recv_sem):
        x, y, cc = _mesh_pos()
        cp = pltpu.make_async_remote_copy(
            src_ref=g_ref.at[1 - cc], dst_ref=out_ref, send_sem=send_sem, recv_sem=recv_sem,
            device_id=(x, y, 1 - cc), device_id_type=MESH)
        cp.start()
        cp.wait()

    return pl.pallas_call(
        body, out_shape=SDS((s, r, c), gp.dtype), in_specs=[_HBM], out_specs=_HBM,
        scratch_shapes=[pltpu.SemaphoreType.DMA, pltpu.SemaphoreType.DMA], name=name)(gp)


def _grads_to_chips(part, *, name):
    s, r, c = part.shape

    def body(p_ref, out_ref, send_sems, recv_sems, local_sem):
        x, y, cc = _mesh_pos()
        mine = 2 * x + y
        chips = _other_chips(x, y)
        local = pltpu.make_async_copy(p_ref.at[mine], out_ref.at[mine], local_sem)
        local.start()

        def copy(j, px, py, src_slot, dst_slot):
            return pltpu.make_async_remote_copy(
                src_ref=p_ref.at[src_slot], dst_ref=out_ref.at[dst_slot], send_sem=send_sems.at[j],
                recv_sem=recv_sems.at[j], device_id=(px, py, cc), device_id_type=MESH)

        sends = [copy(j, px, py, 2 * px + py, mine) for j, (px, py) in enumerate(chips)]
        for cp in sends:
            cp.start()
        for j, (px, py) in enumerate(chips):
            copy(j, px, py, mine, 2 * px + py).wait_recv()
        for cp in sends:
            cp.wait_send()
        local.wait()

    return pl.pallas_call(
        body, out_shape=SDS((s, r, c), part.dtype), in_specs=[_HBM], out_specs=_HBM,
        scratch_shapes=[pltpu.SemaphoreType.DMA((3,)), pltpu.SemaphoreType.DMA((3,)), pltpu.SemaphoreType.DMA],
        name=name)(part)


def _share_halves(tot, *, name):
    r, c = tot.shape

    def body(t_ref, out_ref, send_sem, recv_sem, local_sem):
        x, y, cc = _mesh_pos()
        local = pltpu.make_async_copy(t_ref, out_ref.at[cc], local_sem)
        local.start()
        cp = pltpu.make_async_remote_copy(
            src_ref=t_ref, dst_ref=out_ref.at[cc], send_sem=send_sem, recv_sem=recv_sem,
            device_id=(x, y, 1 - cc), device_id_type=MESH)
        cp.start()
        pltpu.make_async_remote_copy(
            src_ref=t_ref, dst_ref=out_ref.at[1 - cc], send_sem=send_sem, recv_sem=recv_sem,
            device_id=(x, y, 1 - cc), device_id_type=MESH).wait_recv()
        cp.wait_send()
        local.wait()

    return pl.pallas_call(
        body, out_shape=SDS((2, r, c), tot.dtype), in_specs=[_HBM], out_specs=_HBM,
        scratch_shapes=[pltpu.SemaphoreType.DMA, pltpu.SemaphoreType.DMA, pltpu.SemaphoreType.DMA],
        name=name)(tot)


def _pack_rows(flat, quantum, cols):
    n = flat.shape[-1]
    padded = -(-n // quantum) * quantum
    if padded != n:
        flat = jnp.pad(flat, [(0, 0)] * (flat.ndim - 1) + [(0, padded - n)])
    return flat.reshape(flat.shape[:-1] + (padded // cols, cols))


def _pack_small(arrays):
    flat = jnp.concatenate([a.astype(F32).reshape(-1) for a in arrays])
    return _pack_rows(flat, 8 * LANES, LANES)


def _unpack_small(flat, shapes):
    out, off = [], 0
    for shp in shapes:
        n = math.prod(shp)
        out.append(flat[off:off + n].reshape(shp))
        off += n
    return out


BIG = (("w_in", 1), ("w_conv_out", 1), ("w_gqa_out", 1), ("w_mla_uq", 1), ("w_mla_ukv", 1), ("w_mla_out", 1),
       ("w_out", 0), ("w_ffn_gate", 1), ("w_ffn_up", 1), ("w_ffn_down", 0))


def _pack_shard(shards):
    flat = jnp.concatenate([shards[nm].reshape(-1) for nm, _ in BIG])
    packed = _pack_rows(flat, PACK_QUANTUM, PACK_COLS)
    return packed.reshape(2, packed.shape[0] // 2, PACK_COLS)


def _unpack_gathered(gathered, shard_shapes):
    _, s, r, c = gathered.shape
    flat = gathered.transpose(1, 0, 2, 3).reshape(s, 2 * r * c)
    out, off = {}, 0
    for nm, axis in BIG:
        k, n = shard_shapes[nm]
        blk = flat[:, off:off + k * n].reshape(s, k, n)
        off += k * n
        out[nm] = blk.reshape(s * k, n) if axis == 0 else blk.transpose(1, 0, 2).reshape(k, s * n)
    return out


def _pack_full_grads(grads, shard_shapes):
    rows = []
    for nm, axis in BIG:
        k, n = shard_shapes[nm]
        gfull = grads[nm]
        if axis == 0:
            rows.append(gfull.reshape(N_CHIPS, k * n))
        else:
            rows.append(gfull.reshape(k, N_CHIPS, n).transpose(1, 0, 2).reshape(N_CHIPS, k * n))
    packed = _pack_rows(jnp.concatenate(rows, axis=1), PACK_QUANTUM, PACK_COLS)
    r = packed.shape[1] // 2
    return packed.reshape(N_CHIPS, 2, r, PACK_COLS).transpose(1, 0, 2, 3)


def _unpack_shard_grads(gsh, shard_shapes):
    flat = gsh.reshape(-1)
    out, off = {}, 0
    for nm, _ in BIG:
        k, n = shard_shapes[nm]
        out[nm] = flat[off:off + k * n].reshape(k, n)
        off += k * n
    return out


def _step(p, mom_m, mom_v, x, loss_target):
    b, seq, d = x.shape
    depth = p["w_in"].shape[0]
    l_valid = N_META + seq
    lp = -(-l_valid // LANES) * LANES
    t = b * lp
    c_conv = d // 2
    q_rank = p["mla_q_norm_g"].shape[1]
    kv_rank = p["mla_kv_norm_g"].shape[1]
    qw, kw = GQA_HEADS * HEAD, GQA_KV_HEADS * HEAD
    off_q = 2 * c_conv
    off_k = off_q + qw
    off_v = off_k + kw
    off_cq = off_v + kw
    off_ckv = off_cq + q_rank
    off_gate = off_ckv + kv_rank
    off_kpe = off_gate + N_BRANCH * d
    d_inp = off_kpe + LANES
    assert off_cq % q_rank == 0 and off_ckv % kv_rank == 0
    xi, yi, ci = _mesh_pos()
    chip = 2 * xi + yi

    small_sh = [p["meta_tokens"], p["conv_dw"]]
    got = _chip_allgather_small(_pack_small(small_sh), name="gather_small")
    got = [_unpack_small(got[s].reshape(-1), [a.shape for a in small_sh]) for s in range(N_CHIPS)]
    meta_full = jnp.concatenate([g[0] for g in got], axis=1)
    conv_dw_full = jnp.concatenate([g[1] for g in got], axis=-1).reshape(depth, CONV_K, c_conv)

    shard_shapes = {nm: p[nm].shape[1:] for nm, _ in BIG}
    weights = []
    for li in range(depth):
        packed = _pack_shard({nm: p[nm][li].astype(BF16) for nm, _ in BIG})
        full = _unpack_gathered(_weights_allgather(packed, name="gather_weights"), shard_shapes)
        w_in = full["w_in"]
        full["w_in"] = jnp.concatenate(
            [w_in[:, :off_gate], w_in[:, off_gate + MLA_ROPE:], w_in[:, off_gate:off_gate + MLA_ROPE],
             jnp.zeros((d, LANES - MLA_ROPE), BF16)], axis=1)
        uq = full["w_mla_uq"].reshape(q_rank, MLA_HEADS, HEAD + MLA_ROPE)
        full["w_uq"] = jnp.pad(uq, ((0, 0), (0, 0), (0, MLA_SLOT - HEAD - MLA_ROPE))).reshape(
            q_rank, MLA_HEADS * MLA_SLOT)
        ukv = full["w_mla_ukv"].reshape(kv_rank, MLA_HEADS, 2 * HEAD)
        full["w_uk"] = jnp.pad(ukv[:, :, :HEAD], ((0, 0), (0, 0), (0, MLA_SLOT - HEAD))).reshape(
            kv_rank, MLA_HEADS * MLA_SLOT)
        full["w_uv"] = ukv[:, :, HEAD:].reshape(kv_rank, MLA_HEADS * HEAD)
        weights.append(full)

    cos_g, sin_g = _rope_tables(lp, seq, HEAD, HEAD)
    cos_m, sin_m = _rope_tables(lp, seq, MLA_ROPE, LANES)
    gqa_scale = 1.0 / math.sqrt(HEAD)
    mla_scale = 1.0 / math.sqrt(HEAD + MLA_ROPE)
    attn_kw = dict(batch=b, lp=lp, l_valid=l_valid)
    gqa_kw = dict(heads=GQA_HEADS, kv_heads=GQA_KV_HEADS, dk=HEAD, dv=HEAD, scale=gqa_scale, **attn_kw)
    mla_kw = dict(heads=MLA_HEADS, kv_heads=MLA_HEADS, dk=MLA_SLOT, dv=HEAD, scale=mla_scale, **attn_kw)

    h = jnp.concatenate([jnp.broadcast_to(meta_full[None], (b, N_META, d)), x,
                         jnp.zeros((b, lp - l_valid, d), F32)], axis=1).reshape(t, d)
    saved = []
    for li in range(depth):
        w = weights[li]
        row = lambda nm: p[nm][li][None, :]
        s = {"h_in": h}
        s["u"] = _rmsnorm_fwd(h, row("mix_norm_g"), width=d, cblk=0, name="mix_norm")
        proj = _mm(s["u"], w["w_in"], name="in_proj")
        s["proj"] = proj
        proj3 = proj.reshape(b, lp, d_inp)
        s["yconv"] = _conv_fwd(proj3, conv_dw_full[li], row("conv_b"), c_conv=c_conv, l_valid=l_valid,
                               name="conv").reshape(t, c_conv)
        s["sconv"] = _ln_silu_fwd(s["yconv"], row("conv_ln_g"), row("conv_ln_b"), name="conv_ln_silu")
        s["ya"] = _mm(s["sconv"], w["w_conv_out"], name="conv_out")
        s["gq"], s["gk"], s["gv"] = _gqa_prep_fwd(
            proj, row("gqa_q_norm_g"), row("gqa_k_norm_g"), cos_g, sin_g,
            q_off=off_q, k_off=off_k, v_off=off_v, lp=lp, name="gqa_prep")
        s["go"] = _attn_fwd(s["gq"], s["gk"], s["gv"], name="gqa_attn", **gqa_kw)
        s["yb"] = _mm(s["go"], w["w_gqa_out"], name="gqa_out")
        s["cqn"] = _rmsnorm_fwd(proj, row("mla_q_norm_g"), width=q_rank, cblk=off_cq // q_rank, name="mla_q_norm")
        s["kvn"] = _rmsnorm_fwd(proj, row("mla_kv_norm_g"), width=kv_rank, cblk=off_ckv // kv_rank,
                                name="mla_kv_norm")
        q_pre = _mm(s["cqn"], w["w_uq"], name="mla_uq")
        k_pre = _mm(s["kvn"], w["w_uk"], name="mla_uk")
        s["mv"] = _mm(s["kvn"], w["w_uv"], out_dtype=BF16, name="mla_uv")
        s["mq"], s["mk"] = _mla_rope_fwd(q_pre, k_pre, proj, cos_m, sin_m, kpe_off=off_kpe, lp=lp, name="mla_rope")
        s["mo"] = _attn_fwd(s["mq"], s["mk"], s["mv"], name="mla_attn", **mla_kw)
        s["yc"] = _mm(s["mo"], w["w_mla_out"], name="mla_out")
        s["merged"] = _merge_fwd(proj, row("gate_b"), s["ya"], s["yb"], s["yc"], gate_off=off_gate, name="merge")
        h = _mm(s["merged"], w["w_out"], add=h, name="mix_out")
        s["h_mid"] = h
        s["v"] = _rmsnorm_fwd(h, row("ffn_norm_g"), width=d, cblk=0, name="ffn_norm")
        s["gate"] = _mm(s["v"], w["w_ffn_gate"], name="ffn_gate")
        s["up"] = _mm(s["v"], w["w_ffn_up"], name="ffn_up")
        s["act"] = _swiglu_fwd(s["gate"], s["up"], name="swiglu")
        h = _mm(s["act"], w["w_ffn_down"], add=h, name="ffn_down")
        saved.append(s)

    target = jnp.pad(loss_target, ((0, 0), (N_META, lp - l_valid), (0, 0))).reshape(t, d)
    loss_part, dh, dhb, g_final = _loss_head(h, p["final_norm_g"][None, :], target, lp=lp, l_valid=l_valid,
                                             name="loss_head")
    loss = lax.psum(loss_part[0, 0], ("x", "y", "c"))

    small_names = ["mix_norm_g", "conv_dw", "conv_b", "conv_ln_g", "conv_ln_b", "gqa_q_norm_g", "gqa_k_norm_g",
                   "mla_q_norm_g", "mla_kv_norm_g", "gate_b", "ffn_norm_g"]
    small_g = {nm: [None] * depth for nm in small_names}
    shard_g = {nm: [None] * depth for nm, _ in BIG}
    for li in reversed(range(depth)):
        w, s = weights[li], saved[li]
        row = lambda nm: p[nm][li][None, :]
        big = {}
        dact = _mm(dhb, w["w_ffn_down"], mode="nt", name="d_ffn_act")
        big["w_ffn_down"] = _mm(s["act"], dhb, mode="tn", out_dtype=BF16, name="dw_ffn_down")
        dgate, dup = _swiglu_bwd(s["gate"], s["up"], dact, name="d_swiglu")
        dv = _mm(dgate, w["w_ffn_gate"], mode="nt", name="d_ffn_v_gate")
        dv = _mm(dup, w["w_ffn_up"], mode="nt", add=dv, name="d_ffn_v_up")
        big["w_ffn_gate"] = _mm(s["v"], dgate, mode="tn", out_dtype=BF16, name="dw_ffn_gate")
        big["w_ffn_up"] = _mm(s["v"], dup, mode="tn", out_dtype=BF16, name="dw_ffn_up")
        dh, dhb, g = _rmsnorm_bwd(s["h_mid"], row("ffn_norm_g"), dv, width=d, cblk=0, res=dh, name="d_ffn_norm")
        small_g["ffn_norm_g"][li] = g[0]
        dm = _mm(dhb, w["w_out"], mode="nt", name="d_merged")
        big["w_out"] = _mm(s["merged"], dhb, mode="tn", out_dtype=BF16, name="dw_out")
        dya, dyb, dyc, dl0, dl1, dl2, db0, db1, db2 = _merge_bwd(
            s["proj"], row("gate_b"), s["ya"], s["yb"], s["yc"], dm, gate_off=off_gate, name="d_merge")
        small_g["gate_b"][li] = jnp.concatenate([db0[0], db1[0], db2[0]])
        ds = _mm(dya, w["w_conv_out"], mode="nt", name="d_conv_s")
        big["w_conv_out"] = _mm(s["sconv"], dya, mode="tn", out_dtype=BF16, name="dw_conv_out")
        dyconv, g, gb = _ln_silu_bwd(s["yconv"], row("conv_ln_g"), row("conv_ln_b"), ds, name="d_conv_ln_silu")
        small_g["conv_ln_g"][li], small_g["conv_ln_b"][li] = g[0], gb[0]
        proj3 = s["proj"].reshape(b, lp, d_inp)
        da, dgte, ddw, dcb = _conv_bwd(proj3, conv_dw_full[li], dyconv.reshape(b, lp, c_conv), c_conv=c_conv,
                                       l_valid=l_valid, name="d_conv")
        small_g["conv_dw"][li], small_g["conv_b"][li] = ddw, dcb[0]
        dgo = _mm(dyb, w["w_gqa_out"], mode="nt", out_dtype=BF16, name="d_gqa_o")
        big["w_gqa_out"] = _mm(s["go"], dyb, mode="tn", out_dtype=BF16, name="dw_gqa_out")
        dgq, dgk, dgv = _attn_bwd(s["gq"], s["gk"], s["gv"], dgo, name="d_gqa_attn", **gqa_kw)
        dq_g, dk_g, gq, gk = _gqa_prep_bwd(s["proj"], row("gqa_q_norm_g"), row("gqa_k_norm_g"), cos_g, sin_g,
                                           dgq, dgk, q_off=off_q, k_off=off_k, lp=lp, name="d_gqa_prep")
        small_g["gqa_q_norm_g"][li], small_g["gqa_k_norm_g"][li] = gq[0], gk[0]
        dmo = _mm(dyc, w["w_mla_out"], mode="nt", out_dtype=BF16, name="d_mla_o")
        big["w_mla_out"] = _mm(s["mo"], dyc, mode="tn", out_dtype=BF16, name="dw_mla_out")
        dmq, dmk, dmv = _attn_bwd(s["mq"], s["mk"], s["mv"], dmo, name="d_mla_attn", **mla_kw)
        dq_pre, dk_pre, dkpe = _mla_rope_bwd(dmq, dmk, cos_m, sin_m, lp=lp, name="d_mla_rope")
        dmvb = dmv.astype(BF16)
        g_uq = _mm(s["cqn"], dq_pre, mode="tn", out_dtype=BF16, name="dw_mla_uq")
        g_uk = _mm(s["kvn"], dk_pre, mode="tn", out_dtype=BF16, name="dw_mla_uk")
        g_uv = _mm(s["kvn"], dmvb, mode="tn", out_dtype=BF16, name="dw_mla_uv")
        big["w_mla_uq"] = g_uq.reshape(q_rank, MLA_HEADS, MLA_SLOT)[:, :, :HEAD + MLA_ROPE].reshape(q_rank, -1)
        big["w_mla_ukv"] = jnp.concatenate(
            [g_uk.reshape(kv_rank, MLA_HEADS, MLA_SLOT)[:, :, :HEAD], g_uv.reshape(kv_rank, MLA_HEADS, HEAD)],
            axis=-1).reshape(kv_rank, -1)
        dcqn = _mm(dq_pre, w["w_uq"], mode="nt", name="d_mla_cqn")
        dkvn = _mm(dk_pre, w["w_uk"], mode="nt", name="d_mla_kvn_k")
        dkvn = _mm(dmvb, w["w_uv"], mode="nt", add=dkvn, name="d_mla_kvn_v")
        _, dcq, g = _rmsnorm_bwd(s["proj"], row("mla_q_norm_g"), dcqn, width=q_rank, cblk=off_cq // q_rank,
                                 name="d_mla_q_norm")
        small_g["mla_q_norm_g"][li] = g[0]
        _, dckv, g = _rmsnorm_bwd(s["proj"], row("mla_kv_norm_g"), dkvn, width=kv_rank, cblk=off_ckv // kv_rank,
                                  name="d_mla_kv_norm")
        small_g["mla_kv_norm_g"][li] = g[0]
        dproj = jnp.concatenate(
            [da.reshape(t, c_conv), dgte.reshape(t, c_conv), dq_g, dk_g, dgv.astype(BF16), dcq, dckv,
             dl0, dl1, dl2, dkpe], axis=1)
        du = _mm(dproj, w["w_in"], mode="nt", name="d_u")
        g_in = _mm(s["u"], dproj, mode="tn", out_dtype=BF16, name="dw_in")
        big["w_in"] = jnp.concatenate(
            [g_in[:, :off_gate], g_in[:, off_kpe:off_kpe + MLA_ROPE], g_in[:, off_gate:off_kpe]], axis=1)
        dh, dhb, g = _rmsnorm_bwd(s["h_in"], row("mix_norm_g"), du, width=d, cblk=0, res=dh, name="d_mix_norm")
        small_g["mix_norm_g"][li] = g[0]
        gp = _pack_full_grads(big, shard_shapes)
        from_sib = _grads_to_sibling(gp, name="grads_to_sibling")
        mine = lax.dynamic_index_in_dim(gp, ci, 0, keepdims=False)
        flat_rows = (1, N_CHIPS * gp.shape[2], PACK_COLS)
        part = _sum_slots([mine.reshape(flat_rows), from_sib.reshape(flat_rows)], out_dtype=BF16,
                          name="sum_sibling").reshape(from_sib.shape)
        from_chips = _grads_to_chips(part, name="grads_to_chips")
        tot = _sum_slots([from_chips], out_dtype=F32, name="sum_chips")
        gsh = _unpack_shard_grads(_share_halves(tot, name="share_halves"), shard_shapes)
        for nm, _ in BIG:
            shard_g[nm][li] = gsh[nm]

    dh3 = dh.reshape(b, lp, d)
    grad_x = dh3[:, N_META:l_valid]
    g_meta = _meta_grad(dh3, name="meta_grad")

    small_list = [g_meta, g_final[0]] + [jnp.stack(small_g[nm]) for nm in small_names]
    red = _allreduce_small(_pack_small(small_list), name="allreduce_small")
    red = _unpack_small(red.reshape(-1), [a.shape for a in small_list])
    grads = {"meta_tokens": lax.dynamic_slice_in_dim(red[0], chip * (d // N_CHIPS), d // N_CHIPS, axis=1),
             "final_norm_g": red[1]}
    for nm, val in zip(small_names, red[2:]):
        grads[nm] = val
    cs = c_conv // N_CHIPS
    grads["conv_dw"] = lax.dynamic_slice_in_dim(grads["conv_dw"], chip * cs, cs, axis=2).reshape(p["conv_dw"].shape)
    for nm, _ in BIG:
        grads[nm] = jnp.stack(shard_g[nm])

    delta, new_m, new_v = {}, {}, {}
    for nm in p:
        delta[nm], new_m[nm], new_v[nm] = _adamw(p[nm], grads[nm], mom_m[nm], mom_v[nm], name="adamw_" + nm)
    return loss, grad_x, grads, delta, new_m, new_v


WEIGHT_NAMES = ("meta_tokens", "mix_norm_g", "w_in", "conv_dw", "conv_b", "conv_ln_g", "conv_ln_b", "w_conv_out",
                "gqa_q_norm_g", "gqa_k_norm_g", "w_gqa_out", "mla_q_norm_g", "w_mla_uq", "mla_kv_norm_g",
                "w_mla_ukv", "w_mla_out", "gate_b", "w_out", "ffn_norm_g", "w_ffn_gate", "w_ffn_up", "w_ffn_down",
                "final_norm_g")


def kernel(x, meta_tokens, mix_norm_g, w_in, conv_dw, conv_b, conv_ln_g, conv_ln_b, w_conv_out, gqa_q_norm_g, gqa_k_norm_g, w_gqa_out, mla_q_norm_g, w_mla_uq, mla_kv_norm_g, w_mla_ukv, w_mla_out, gate_b, w_out, ffn_norm_g, w_ffn_gate, w_ffn_up, w_ffn_down, final_norm_g, loss_target, m_meta_tokens, m_mix_norm_g, m_w_in, m_conv_dw, m_conv_b, m_conv_ln_g, m_conv_ln_b, m_w_conv_out, m_gqa_q_norm_g, m_gqa_k_norm_g, m_w_gqa_out, m_mla_q_norm_g, m_w_mla_uq, m_mla_kv_norm_g, m_w_mla_ukv, m_w_mla_out, m_gate_b, m_w_out, m_ffn_norm_g, m_w_ffn_gate, m_w_ffn_up, m_w_ffn_down, m_final_norm_g, v_meta_tokens, v_mix_norm_g, v_w_in, v_conv_dw, v_conv_b, v_conv_ln_g, v_conv_ln_b, v_w_conv_out, v_gqa_q_norm_g, v_gqa_k_norm_g, v_w_gqa_out, v_mla_q_norm_g, v_w_mla_uq, v_mla_kv_norm_g, v_w_mla_ukv, v_w_mla_out, v_gate_b, v_w_out, v_ffn_norm_g, v_w_ffn_gate, v_w_ffn_up, v_w_ffn_down, v_final_norm_g):
    ws = (meta_tokens, mix_norm_g, w_in, conv_dw, conv_b, conv_ln_g, conv_ln_b, w_conv_out, gqa_q_norm_g,
          gqa_k_norm_g, w_gqa_out, mla_q_norm_g, w_mla_uq, mla_kv_norm_g, w_mla_ukv, w_mla_out, gate_b, w_out,
          ffn_norm_g, w_ffn_gate, w_ffn_up, w_ffn_down, final_norm_g)
    ms = (m_meta_tokens, m_mix_norm_g, m_w_in, m_conv_dw, m_conv_b, m_conv_ln_g, m_conv_ln_b, m_w_conv_out,
          m_gqa_q_norm_g, m_gqa_k_norm_g, m_w_gqa_out, m_mla_q_norm_g, m_w_mla_uq, m_mla_kv_norm_g, m_w_mla_ukv,
          m_w_mla_out, m_gate_b, m_w_out, m_ffn_norm_g, m_w_ffn_gate, m_w_ffn_up, m_w_ffn_down, m_final_norm_g)
    vs = (v_meta_tokens, v_mix_norm_g, v_w_in, v_conv_dw, v_conv_b, v_conv_ln_g, v_conv_ln_b, v_w_conv_out,
          v_gqa_q_norm_g, v_gqa_k_norm_g, v_w_gqa_out, v_mla_q_norm_g, v_w_mla_uq, v_mla_kv_norm_g, v_w_mla_ukv,
          v_w_mla_out, v_gate_b, v_w_out, v_ffn_norm_g, v_w_ffn_gate, v_w_ffn_up, v_w_ffn_down, v_final_norm_g)
    p = dict(zip(WEIGHT_NAMES, ws))
    loss, grad_x, grads, delta, new_m, new_v = _step(p, dict(zip(WEIGHT_NAMES, ms)), dict(zip(WEIGHT_NAMES, vs)),
                                                     x, loss_target)
    return (loss, grad_x, *[grads[n] for n in WEIGHT_NAMES], *[delta[n] for n in WEIGHT_NAMES],
            *[new_m[n] for n in WEIGHT_NAMES], *[new_v[n] for n in WEIGHT_NAMES])
```

```python
import math

import jax
import jax.numpy as jnp
from jax import lax
from jax.experimental import pallas as pl
from jax.experimental.pallas import tpu as pltpu

F32 = jnp.float32
BF16 = jnp.bfloat16
SDS = jax.ShapeDtypeStruct
MESH = pl.DeviceIdType.MESH

N_META = 16
GRID_W = 64
ROPE_THETA = 10000.0
NORM_EPS = 1e-6
CONV_K = 31
CONV_HALO = 16
HEAD = 128
GQA_HEADS = 8
GQA_KV_HEADS = 2
MLA_HEADS = 8
MLA_ROPE = 64
MLA_SLOT = 256
N_BRANCH = 3

ADAM_LR = 0.001
ADAM_B1 = 0.9
ADAM_B2 = 0.999
ADAM_EPS = 1e-08
ADAM_WD = 0.01
ADAM_STEP = 10

LANES = 128
SUBLANES_BF16 = 16
VMEM_LIMIT_BYTES = 56 * 2 ** 20
ROW_BLOCK_BYTES = 1 << 20
N_CHIPS = 4
N_DEV = 8


def _cparams():
    return pltpu.CompilerParams(vmem_limit_bytes=VMEM_LIMIT_BYTES)


def _tile(n, cap, mult):
    best = None
    for d in range(mult, min(n, cap) + 1, mult):
        if n % d == 0:
            best = d
    return n if best is None else best


def _row_tile(rows, width, mult=SUBLANES_BF16):
    return _tile(rows, max(mult, ROW_BLOCK_BYTES // (4 * width)), mult)


def _sigmoid(x):
    return 1.0 / (1.0 + jnp.exp(-x))


def _mm(a, b, *, mode="nn", add=None, out_dtype=F32, b_sharded=False, out_shards=None, name):
    if b_sharded:
        s_b, r_b, c_b = b.shape
        b_rows, b_cols = r_b, s_b * c_b
    else:
        b_rows, b_cols = b.shape
    if mode == "nn":
        (m, k), n = a.shape, b_cols
        assert b_rows == k
    elif mode == "nt":
        (m, k), n = a.shape, b_rows
        assert b_cols == k
    else:
        (k, m), n = a.shape, b_cols
        assert b_rows == k
    n_unit = n
    if b_sharded and mode != "nt":
        n_unit = c_b
    if out_shards is not None:
        assert n % out_shards == 0
        n_unit = math.gcd(n_unit, n // out_shards)
    k_unit = c_b if (b_sharded and mode == "nt") else k
    if mode == "tn":
        tm = _tile(m, 1024, LANES)
        tk = _tile(k_unit, 2176, SUBLANES_BF16)
    else:
        tm = _tile(m, 1088, SUBLANES_BF16)
        tk = _tile(k_unit, 2176, LANES)
    tn = _tile(n_unit, 1408, LANES)
    nk = k // tk
    if mode == "nn":
        a_spec = pl.BlockSpec((tm, tk), lambda i, j, kk: (i, kk))
        dn = (((1,), (0,)), ((), ()))
        if b_sharded:
            per = c_b // tn
            b_spec = pl.BlockSpec((None, tk, tn), lambda i, j, kk: (j // per, kk, j % per))
        else:
            b_spec = pl.BlockSpec((tk, tn), lambda i, j, kk: (kk, j))
    elif mode == "nt":
        a_spec = pl.BlockSpec((tm, tk), lambda i, j, kk: (i, kk))
        dn = (((1,), (1,)), ((), ()))
        if b_sharded:
            per = c_b // tk
            b_spec = pl.BlockSpec((None, tn, tk), lambda i, j, kk: (kk // per, j, kk % per))
        else:
            b_spec = pl.BlockSpec((tn, tk), lambda i, j, kk: (j, kk))
    else:
        a_spec = pl.BlockSpec((tk, tm), lambda i, j, kk: (kk, i))
        dn = (((0,), (0,)), ((), ()))
        if b_sharded:
            per = c_b // tn
            b_spec = pl.BlockSpec((None, tk, tn), lambda i, j, kk: (j // per, kk, j % per))
        else:
            b_spec = pl.BlockSpec((tk, tn), lambda i, j, kk: (kk, j))
    if out_shards is None:
        o_spec = pl.BlockSpec((tm, tn), lambda i, j, kk: (i, j))
        out_shape = SDS((m, n), out_dtype)
    else:
        per_o = (n // out_shards) // tn
        o_spec = pl.BlockSpec((None, tm, tn), lambda i, j, kk: (j // per_o, i, j % per_o))
        out_shape = SDS((out_shards, m, n // out_shards), out_dtype)
    has_add = add is not None
    assert not (has_add and out_shards is not None)

    def body(a_ref, b_ref, *rest):
        if has_add:
            add_ref, o_ref, acc = rest
        else:
            o_ref, acc = rest
        kk = pl.program_id(2)
        part = lax.dot_general(a_ref[...].astype(BF16), b_ref[...].astype(BF16), dn,
                               preferred_element_type=F32)

        @pl.when(kk == 0)
        def _():
            acc[...] = part

        @pl.when(kk > 0)
        def _():
            acc[...] += part

        @pl.when(kk == nk - 1)
        def _():
            r = acc[...]
            if has_add:
                r = r + add_ref[...]
            o_ref[...] = r.astype(o_ref.dtype)

    in_specs = [a_spec, b_spec] + ([o_spec] if has_add else [])
    args = (a, b) + ((add,) if has_add else ())
    return pl.pallas_call(
        body, grid=(m // tm, n // tn, nk), in_specs=in_specs, out_specs=o_spec,
        out_shape=out_shape, scratch_shapes=[pltpu.VMEM((tm, tn), F32)],
        compiler_params=_cparams(), name=name)(*args)


def _rmsnorm_fwd(x, g, *, width, cblk, name):
    t = x.shape[0]
    tr = _row_tile(t, width)

    def body(x_ref, g_ref, o_ref):
        xv = x_ref[...]
        r = lax.rsqrt(jnp.mean(xv * xv, axis=-1, keepdims=True) + NORM_EPS)
        o_ref[...] = (xv * r * g_ref[...]).astype(o_ref.dtype)

    return pl.pallas_call(
        body, grid=(t // tr,),
        in_specs=[pl.BlockSpec((tr, width), lambda i: (i, cblk)), pl.BlockSpec((1, width), lambda i: (0, 0))],
        out_specs=pl.BlockSpec((tr, width), lambda i: (i, 0)),
        out_shape=SDS((t, width), BF16), compiler_params=_cparams(), name=name)(x, g)


def _rmsnorm_bwd(x, g, dy, *, width, cblk, res=None, name):
    t = x.shape[0]
    tr = _row_tile(t, width)
    has_res = res is not None

    def body(x_ref, g_ref, dy_ref, *rest):
        if has_res:
            res_ref, dx_ref, dxb_ref, dg_ref = rest
        else:
            dx_ref, dxb_ref, dg_ref = rest
        xv = x_ref[...]
        dyv = dy_ref[...].astype(F32)
        r = lax.rsqrt(jnp.mean(xv * xv, axis=-1, keepdims=True) + NORM_EPS)
        gy = dyv * g_ref[...]
        dx = r * gy - xv * (r * r * r) * jnp.mean(gy * xv, axis=-1, keepdims=True)
        if has_res:
            dx = dx + res_ref[...]
        dx_ref[...] = dx
        dxb_ref[...] = dx.astype(BF16)
        part = jnp.sum(dyv * xv * r, axis=0, keepdims=True)

        @pl.when(pl.program_id(0) == 0)
        def _():
            dg_ref[...] = part

        @pl.when(pl.program_id(0) > 0)
        def _():
            dg_ref[...] += part

    row = pl.BlockSpec((tr, width), lambda i: (i, 0))
    vec = pl.BlockSpec((1, width), lambda i: (0, 0))
    in_specs = [pl.BlockSpec((tr, width), lambda i: (i, cblk)), vec, row] + ([row] if has_res else [])
    args = (x, g, dy) + ((res,) if has_res else ())
    return pl.pallas_call(
        body, grid=(t // tr,), in_specs=in_specs, out_specs=[row, row, vec],
        out_shape=[SDS((t, width), F32), SDS((t, width), BF16), SDS((1, width), F32)],
        compiler_params=_cparams(), name=name)(*args)


def _valid_rows(tile_index, tr, lp, lo, hi):
    pos = (tile_index % (lp // tr)) * tr + lax.broadcasted_iota(jnp.int32, (tr, 1), 0)
    return (pos >= lo) & (pos < hi)


def _loss_head(h, g, target, *, lp, l_valid, name):
    t, d = h.shape
    tr = _tile(lp, max(SUBLANES_BF16, ROW_BLOCK_BYTES // (4 * d)), SUBLANES_BF16)

    def body(x_ref, g_ref, t_ref, loss_ref, dx_ref, dxb_ref, dg_ref):
        i = pl.program_id(0)
        xv = x_ref[...]
        r = lax.rsqrt(jnp.mean(xv * xv, axis=-1, keepdims=True) + NORM_EPS)
        y = xv * r * g_ref[...]
        valid = _valid_rows(i, tr, lp, N_META, l_valid)
        err = jnp.where(valid, y - t_ref[...], 0.0)
        lpart = 0.5 * jnp.sum(jnp.mean(err * err, axis=-1, keepdims=True), axis=0, keepdims=True)
        dyv = err * (1.0 / d)
        gy = dyv * g_ref[...]
        dx = r * gy - xv * (r * r * r) * jnp.mean(gy * xv, axis=-1, keepdims=True)
        dx_ref[...] = dx
        dxb_ref[...] = dx.astype(BF16)
        gpart = jnp.sum(dyv * xv * r, axis=0, keepdims=True)

        @pl.when(i == 0)
        def _():
            dg_ref[...] = gpart
            loss_ref[...] = jnp.broadcast_to(lpart, loss_ref.shape)

        @pl.when(i > 0)
        def _():
            dg_ref[...] += gpart
            loss_ref[...] += jnp.broadcast_to(lpart, loss_ref.shape)

    row = pl.BlockSpec((tr, d), lambda i: (i, 0))
    vec = pl.BlockSpec((1, d), lambda i: (0, 0))
    return pl.pallas_call(
        body, grid=(t // tr,), in_specs=[row, vec, row],
        out_specs=[pl.BlockSpec((1, LANES), lambda i: (0, 0)), row, row, vec],
        out_shape=[SDS((1, LANES), F32), SDS((t, d), F32), SDS((t, d), BF16), SDS((1, d), F32)],
        compiler_params=_cparams(), name=name)(h, g, target)


def _ln_silu_fwd(yc, g, b, *, name):
    t, c = yc.shape
    tr = _row_tile(t, c)

    def body(x_ref, g_ref, b_ref, o_ref):
        xv = x_ref[...]
        xc = xv - jnp.mean(xv, axis=-1, keepdims=True)
        r = lax.rsqrt(jnp.mean(xc * xc, axis=-1, keepdims=True) + NORM_EPS)
        ln = xc * r * g_ref[...] + b_ref[...]
        o_ref[...] = (ln * _sigmoid(ln)).astype(o_ref.dtype)

    row = pl.BlockSpec((tr, c), lambda i: (i, 0))
    vec = pl.BlockSpec((1, c), lambda i: (0, 0))
    return pl.pallas_call(body, grid=(t // tr,), in_specs=[row, vec, vec], out_specs=row,
                          out_shape=SDS((t, c), BF16), compiler_params=_cparams(), name=name)(yc, g, b)


def _ln_silu_bwd(yc, g, b, ds, *, name):
    t, c = yc.shape
    tr = _row_tile(t, c)

    def body(x_ref, g_ref, b_ref, ds_ref, dx_ref, dg_ref, db_ref):
        xv = x_ref[...]
        xc = xv - jnp.mean(xv, axis=-1, keepdims=True)
        r = lax.rsqrt(jnp.mean(xc * xc, axis=-1, keepdims=True) + NORM_EPS)
        xh = xc * r
        ln = xh * g_ref[...] + b_ref[...]
        sg = _sigmoid(ln)
        dln = ds_ref[...].astype(F32) * (sg * (1.0 + ln * (1.0 - sg)))
        gy = dln * g_ref[...]
        dx_ref[...] = r * (gy - jnp.mean(gy, axis=-1, keepdims=True)
                           - xh * jnp.mean(gy * xh, axis=-1, keepdims=True))
        gpart = jnp.sum(dln * xh, axis=0, keepdims=True)
        bpart = jnp.sum(dln, axis=0, keepdims=True)

        @pl.when(pl.program_id(0) == 0)
        def _():
            dg_ref[...] = gpart
            db_ref[...] = bpart

        @pl.when(pl.program_id(0) > 0)
        def _():
            dg_ref[...] += gpart
            db_ref[...] += bpart

    row = pl.BlockSpec((tr, c), lambda i: (i, 0))
    vec = pl.BlockSpec((1, c), lambda i: (0, 0))
    return pl.pallas_call(
        body, grid=(t // tr,), in_specs=[row, vec, vec, row], out_specs=[row, vec, vec],
        out_shape=[SDS((t, c), F32), SDS((1, c), F32), SDS((1, c), F32)],
        compiler_params=_cparams(), name=name)(yc, g, b, ds)


def _swiglu_fwd(gate, up, *, name):
    t, f = gate.shape
    tr = _row_tile(t, f)

    def body(g_ref, u_ref, o_ref):
        gv = g_ref[...]
        o_ref[...] = (gv * _sigmoid(gv) * u_ref[...]).astype(o_ref.dtype)

    row = pl.BlockSpec((tr, f), lambda i: (i, 0))
    return pl.pallas_call(body, grid=(t // tr,), in_specs=[row, row], out_specs=row,
                          out_shape=SDS((t, f), BF16), compiler_params=_cparams(), name=name)(gate, up)


def _swiglu_bwd(gate, up, dact, *, name):
    t, f = gate.shape
    tr = _row_tile(t, f)

    def body(g_ref, u_ref, d_ref, dg_ref, du_ref):
        gv = g_ref[...]
        sg = _sigmoid(gv)
        dv = d_ref[...]
        dg_ref[...] = (dv * u_ref[...] * (sg * (1.0 + gv * (1.0 - sg)))).astype(dg_ref.dtype)
        du_ref[...] = (dv * (gv * sg)).astype(du_ref.dtype)

    row = pl.BlockSpec((tr, f), lambda i: (i, 0))
    return pl.pallas_call(body, grid=(t // tr,), in_specs=[row, row, row], out_specs=[row, row],
                          out_shape=[SDS((t, f), BF16), SDS((t, f), BF16)],
                          compiler_params=_cparams(), name=name)(gate, up, dact)


def _merge_tiles(t, d):
    tc = _tile(d, 512, LANES)
    return _row_tile(t, tc), tc


def _merge_fwd(proj, gate_b, ya, yb, yc, *, gate_off, name):
    t, d = ya.shape
    tr, tc = _merge_tiles(t, d)
    nc = d // tc
    assert gate_off % tc == 0
    g0 = gate_off // tc

    def body(l0, l1, l2, b0, b1, b2, ya_ref, yb_ref, yc_ref, o_ref):
        acc = _sigmoid(l0[...] + b0[...]) * ya_ref[...]
        acc += _sigmoid(l1[...] + b1[...]) * yb_ref[...]
        acc += _sigmoid(l2[...] + b2[...]) * yc_ref[...]
        o_ref[...] = acc.astype(o_ref.dtype)

    lspec = [pl.BlockSpec((tr, tc), lambda i, j, br=br: (i, g0 + br * nc + j)) for br in range(N_BRANCH)]
    bspec = [pl.BlockSpec((1, tc), lambda i, j, br=br: (0, br * nc + j)) for br in range(N_BRANCH)]
    yspec = pl.BlockSpec((tr, tc), lambda i, j: (i, j))
    return pl.pallas_call(
        body, grid=(t // tr, nc), in_specs=lspec + bspec + [yspec] * 3, out_specs=yspec,
        out_shape=SDS((t, d), BF16), compiler_params=_cparams(), name=name)(
            proj, proj, proj, gate_b, gate_b, gate_b, ya, yb, yc)


def _merge_bwd(proj, gate_b, ya, yb, yc, dm, *, gate_off, name):
    t, d = ya.shape
    tr, tc = _merge_tiles(t, d)
    nc = d // tc
    g0 = gate_off // tc

    def body(l0, l1, l2, b0, b1, b2, ya_ref, yb_ref, yc_ref, dm_ref,
             dya, dyb, dyc, dl0, dl1, dl2, db0, db1, db2):
        i = pl.program_id(1)
        dmv = dm_ref[...]
        for l_ref, b_ref, y_ref, dy_ref, dl_ref, db_ref in (
                (l0, b0, ya_ref, dya, dl0, db0), (l1, b1, yb_ref, dyb, dl1, db1), (l2, b2, yc_ref, dyc, dl2, db2)):
            gt = _sigmoid(l_ref[...] + b_ref[...])
            dy_ref[...] = (dmv * gt).astype(dy_ref.dtype)
            dl = dmv * y_ref[...] * gt * (1.0 - gt)
            dl_ref[...] = dl.astype(dl_ref.dtype)
            part = jnp.sum(dl, axis=0, keepdims=True)

            @pl.when(i == 0)
            def _(db_ref=db_ref, part=part):
                db_ref[...] = part

            @pl.when(i > 0)
            def _(db_ref=db_ref, part=part):
                db_ref[...] += part

    lspec = [pl.BlockSpec((tr, tc), lambda j, i, br=br: (i, g0 + br * nc + j)) for br in range(N_BRANCH)]
    bspec = [pl.BlockSpec((1, tc), lambda j, i, br=br: (0, br * nc + j)) for br in range(N_BRANCH)]
    yspec = pl.BlockSpec((tr, tc), lambda j, i: (i, j))
    vspec = pl.BlockSpec((1, tc), lambda j, i: (0, j))
    return pl.pallas_call(
        body, grid=(nc, t // tr), in_specs=lspec + bspec + [yspec] * 4,
        out_specs=[yspec] * 6 + [vspec] * 3,
        out_shape=[SDS((t, d), BF16)] * 6 + [SDS((1, d), F32)] * 3,
        compiler_params=_cparams(), name=name)(proj, proj, proj, gate_b, gate_b, gate_b, ya, yb, yc, dm)


def _conv_fwd(proj3, dw, cb, *, c_conv, l_valid, name):
    b, lp, _ = proj3.shape
    cw = _tile(c_conv, 256, LANES)
    nc = c_conv // cw

    def body(a_ref, gte_ref, dw_ref, cb_ref, o_ref, zs):
        rows = lax.broadcasted_iota(jnp.int32, (lp, 1), 0)
        z = jnp.where(rows < l_valid, a_ref[...] * _sigmoid(gte_ref[...]), 0.0)
        zs[pl.ds(0, CONV_HALO), :] = jnp.zeros((CONV_HALO, cw), F32)
        zs[pl.ds(CONV_HALO, lp), :] = z
        zs[pl.ds(CONV_HALO + lp, CONV_HALO), :] = jnp.zeros((CONV_HALO, cw), F32)
        acc = jnp.broadcast_to(cb_ref[...], (lp, cw))
        for j in range(CONV_K):
            acc = acc + zs[pl.ds(j + 1, lp), :] * dw_ref[pl.ds(j, 1), :]
        o_ref[...] = acc

    seq = lambda off: pl.BlockSpec((None, lp, cw), lambda bi, ci: (bi, 0, off + ci))
    return pl.pallas_call(
        body, grid=(b, nc),
        in_specs=[seq(0), seq(nc), pl.BlockSpec((CONV_K, cw), lambda bi, ci: (0, ci)),
                  pl.BlockSpec((1, cw), lambda bi, ci: (0, ci))],
        out_specs=seq(0), out_shape=SDS((b, lp, c_conv), F32),
        scratch_shapes=[pltpu.VMEM((lp + 2 * CONV_HALO, cw), F32)],
        compiler_params=_cparams(), name=name)(proj3, proj3, dw, cb)


def _conv_bwd(proj3, dw, dyc3, *, c_conv, l_valid, name):
    b, lp, _ = proj3.shape
    cw = _tile(c_conv, 256, LANES)
    nc = c_conv // cw

    def body(a_ref, gte_ref, dw_ref, dy_ref, da_ref, dgte_ref, ddw_ref, dcb_ref, zs, dys):
        bi = pl.program_id(1)
        rows = lax.broadcasted_iota(jnp.int32, (lp, 1), 0)
        valid = rows < l_valid
        av = a_ref[...]
        sg = _sigmoid(gte_ref[...])
        dyv = dy_ref[...]
        zero = jnp.zeros((CONV_HALO, cw), F32)
        zs[pl.ds(0, CONV_HALO), :] = zero
        zs[pl.ds(CONV_HALO, lp), :] = jnp.where(valid, av * sg, 0.0)
        zs[pl.ds(CONV_HALO + lp, CONV_HALO), :] = zero
        dys[pl.ds(0, CONV_HALO), :] = zero
        dys[pl.ds(CONV_HALO, lp), :] = dyv
        dys[pl.ds(CONV_HALO + lp, CONV_HALO), :] = zero

        @pl.when(bi == 0)
        def _():
            ddw_ref[...] = jnp.zeros_like(ddw_ref)
            dcb_ref[...] = jnp.zeros_like(dcb_ref)

        dz = jnp.zeros((lp, cw), F32)
        for j in range(CONV_K):
            dz = dz + dys[pl.ds(CONV_K - j, lp), :] * dw_ref[pl.ds(j, 1), :]
            ddw_ref[pl.ds(j, 1), :] += jnp.sum(dyv * zs[pl.ds(j + 1, lp), :], axis=0, keepdims=True)
        dcb_ref[...] += jnp.sum(dyv, axis=0, keepdims=True)
        dz = jnp.where(valid, dz, 0.0)
        da_ref[...] = (dz * sg).astype(da_ref.dtype)
        dgte_ref[...] = (dz * av * sg * (1.0 - sg)).astype(dgte_ref.dtype)

    seq = lambda off: pl.BlockSpec((None, lp, cw), lambda ci, bi: (bi, 0, off + ci))
    return pl.pallas_call(
        body, grid=(nc, b),
        in_specs=[seq(0), seq(nc), pl.BlockSpec((CONV_K, cw), lambda ci, bi: (0, ci)), seq(0)],
        out_specs=[seq(0), seq(0), pl.BlockSpec((CONV_K, cw), lambda ci, bi: (0, ci)),
                   pl.BlockSpec((1, cw), lambda ci, bi: (0, ci))],
        out_shape=[SDS((b, lp, c_conv), BF16), SDS((b, lp, c_conv), BF16),
                   SDS((CONV_K, c_conv), F32), SDS((1, c_conv), F32)],
        scratch_shapes=[pltpu.VMEM((lp + 2 * CONV_HALO, cw), F32)] * 2,
        compiler_params=_cparams(), name=name)(proj3, proj3, dw, dyc3)


def _swap_halves(x, group):
    half = group // 2
    lane = lax.broadcasted_iota(jnp.int32, x.shape, 1)
    up = pltpu.roll(x, LANES - half, 1)
    down = pltpu.roll(x, half, 1)
    return jnp.where((lane % group) < half, up, down)


def _rope(x, cos, sin, group):
    return x * cos + _swap_halves(x, group) * sin


def _rope_bwd(dy, cos, sin, group):
    return dy * cos + _swap_halves(dy * sin, group)


def _rope_tables(lp, seq, dim, lanes):
    quarter = dim // 4
    tok = jnp.arange(seq, dtype=jnp.int32)
    zeros = jnp.zeros((N_META,), F32)
    pad = jnp.zeros((lp - N_META - seq,), F32)
    row = jnp.concatenate([zeros, (tok // GRID_W).astype(F32), pad])
    col = jnp.concatenate([zeros, (tok % GRID_W).astype(F32), pad])
    inv = ROPE_THETA ** (-jnp.arange(quarter, dtype=F32) / quarter)
    ar, ac = row[:, None] * inv[None, :], col[:, None] * inv[None, :]
    cos = jnp.concatenate([jnp.cos(ar), jnp.cos(ar), jnp.cos(ac), jnp.cos(ac)], axis=1)
    sin = jnp.concatenate([-jnp.sin(ar), jnp.sin(ar), -jnp.sin(ac), jnp.sin(ac)], axis=1)
    if lanes > dim:
        cos = jnp.concatenate([cos, jnp.ones((lp, lanes - dim), F32)], axis=1)
        sin = jnp.concatenate([sin, jnp.zeros((lp, lanes - dim), F32)], axis=1)
    return cos, sin


def _gqa_prep_fwd(proj, qg, kg, cos, sin, *, q_off, k_off, v_off, lp, name):
    t = proj.shape[0]
    qw, kw = GQA_HEADS * HEAD, GQA_KV_HEADS * HEAD
    tr = _tile(lp, max(SUBLANES_BF16, ROW_BLOCK_BYTES // (4 * qw)), SUBLANES_BF16)
    npos = lp // tr
    assert q_off % qw == 0 and k_off % kw == 0 and v_off % kw == 0

    def body(q_ref, k_ref, v_ref, qg_ref, kg_ref, cos_ref, sin_ref, qo_ref, ko_ref, vo_ref):
        cosv, sinv = cos_ref[...], sin_ref[...]
        for src, g_ref, dst, heads in ((q_ref, qg_ref, qo_ref, GQA_HEADS), (k_ref, kg_ref, ko_ref, GQA_KV_HEADS)):
            for hh in range(heads):
                xh = src[:, hh * HEAD:(hh + 1) * HEAD]
                r = lax.rsqrt(jnp.mean(xh * xh, axis=-1, keepdims=True) + NORM_EPS)
                dst[:, hh * HEAD:(hh + 1) * HEAD] = _rope(xh * r * g_ref[...], cosv, sinv, 64).astype(dst.dtype)
        vo_ref[...] = v_ref[...].astype(vo_ref.dtype)

    tab = pl.BlockSpec((tr, HEAD), lambda i: (i % npos, 0))
    vec = pl.BlockSpec((1, HEAD), lambda i: (0, 0))
    return pl.pallas_call(
        body, grid=(t // tr,),
        in_specs=[pl.BlockSpec((tr, qw), lambda i: (i, q_off // qw)),
                  pl.BlockSpec((tr, kw), lambda i: (i, k_off // kw)),
                  pl.BlockSpec((tr, kw), lambda i: (i, v_off // kw)), vec, vec, tab, tab],
        out_specs=[pl.BlockSpec((tr, qw), lambda i: (i, 0)), pl.BlockSpec((tr, kw), lambda i: (i, 0)),
                   pl.BlockSpec((tr, kw), lambda i: (i, 0))],
        out_shape=[SDS((t, qw), BF16), SDS((t, kw), BF16), SDS((t, kw), BF16)],
        compiler_params=_cparams(), name=name)(proj, proj, proj, qg, kg, cos, sin)


def _gqa_prep_bwd(proj, qg, kg, cos, sin, dqr, dkr, *, q_off, k_off, lp, name):
    t = proj.shape[0]
    qw, kw = GQA_HEADS * HEAD, GQA_KV_HEADS * HEAD
    tr = _tile(lp, max(SUBLANES_BF16, ROW_BLOCK_BYTES // (4 * qw)), SUBLANES_BF16)
    npos = lp // tr

    def body(q_ref, k_ref, qg_ref, kg_ref, cos_ref, sin_ref, dqr_ref, dkr_ref, dq_ref, dk_ref, dqg_ref, dkg_ref):
        cosv, sinv = cos_ref[...], sin_ref[...]
        for src, g_ref, dy_ref, dx_ref, dg_ref, heads in (
                (q_ref, qg_ref, dqr_ref, dq_ref, dqg_ref, GQA_HEADS),
                (k_ref, kg_ref, dkr_ref, dk_ref, dkg_ref, GQA_KV_HEADS)):
            gpart = jnp.zeros((1, HEAD), F32)
            for hh in range(heads):
                sl = slice(hh * HEAD, (hh + 1) * HEAD)
                xh = src[:, sl]
                r = lax.rsqrt(jnp.mean(xh * xh, axis=-1, keepdims=True) + NORM_EPS)
                dxn = _rope_bwd(dy_ref[:, sl], cosv, sinv, 64)
                gy = dxn * g_ref[...]
                dx = r * gy - xh * (r * r * r) * jnp.mean(gy * xh, axis=-1, keepdims=True)
                dx_ref[:, sl] = dx.astype(dx_ref.dtype)
                gpart = gpart + jnp.sum(dxn * xh * r, axis=0, keepdims=True)

            @pl.when(pl.program_id(0) == 0)
            def _(dg_ref=dg_ref, gpart=gpart):
                dg_ref[...] = gpart

            @pl.when(pl.program_id(0) > 0)
            def _(dg_ref=dg_ref, gpart=gpart):
                dg_ref[...] += gpart

    tab = pl.BlockSpec((tr, HEAD), lambda i: (i % npos, 0))
    vec = pl.BlockSpec((1, HEAD), lambda i: (0, 0))
    qrow = pl.BlockSpec((tr, qw), lambda i: (i, 0))
    krow = pl.BlockSpec((tr, kw), lambda i: (i, 0))
    return pl.pallas_call(
        body, grid=(t // tr,),
        in_specs=[pl.BlockSpec((tr, qw), lambda i: (i, q_off // qw)),
                  pl.BlockSpec((tr, kw), lambda i: (i, k_off // kw)), vec, vec, tab, tab, qrow, krow],
        out_specs=[qrow, krow, vec, vec],
        out_shape=[SDS((t, qw), BF16), SDS((t, kw), BF16), SDS((1, HEAD), F32), SDS((1, HEAD), F32)],
        compiler_params=_cparams(), name=name)(proj, proj, qg, kg, cos, sin, dqr, dkr)


def _mla_rope_fwd(q_pre, k_pre, proj, cos, sin, *, kpe_off, lp, name):
    t, w = q_pre.shape
    tr = _tile(lp, max(SUBLANES_BF16, ROW_BLOCK_BYTES // (4 * w)), SUBLANES_BF16)
    npos = lp // tr
    assert kpe_off % LANES == 0

    def body(q_ref, k_ref, kpe_ref, cos_ref, sin_ref, qo_ref, ko_ref):
        cosv, sinv = cos_ref[...], sin_ref[...]
        kr = _rope(kpe_ref[...], cosv, sinv, 32).astype(ko_ref.dtype)
        for hh in range(MLA_HEADS):
            base = hh * MLA_SLOT
            qo_ref[:, base:base + HEAD] = q_ref[:, base:base + HEAD].astype(qo_ref.dtype)
            qo_ref[:, base + HEAD:base + MLA_SLOT] = _rope(
                q_ref[:, base + HEAD:base + MLA_SLOT], cosv, sinv, 32).astype(qo_ref.dtype)
            ko_ref[:, base:base + HEAD] = k_ref[:, base:base + HEAD].astype(ko_ref.dtype)
            ko_ref[:, base + HEAD:base + MLA_SLOT] = kr

    row = pl.BlockSpec((tr, w), lambda i: (i, 0))
    tab = pl.BlockSpec((tr, LANES), lambda i: (i % npos, 0))
    return pl.pallas_call(
        body, grid=(t // tr,),
        in_specs=[row, row, pl.BlockSpec((tr, LANES), lambda i: (i, kpe_off // LANES)), tab, tab],
        out_specs=[row, row], out_shape=[SDS((t, w), BF16), SDS((t, w), BF16)],
        compiler_params=_cparams(), name=name)(q_pre, k_pre, proj, cos, sin)


def _mla_rope_bwd(dq, dk, cos, sin, *, lp, name):
    t, w = dq.shape
    tr = _tile(lp, max(SUBLANES_BF16, ROW_BLOCK_BYTES // (4 * w)), SUBLANES_BF16)
    npos = lp // tr

    def body(dq_ref, dk_ref, cos_ref, sin_ref, dqo_ref, dko_ref, dkpe_ref):
        cosv, sinv = cos_ref[...], sin_ref[...]
        pe = jnp.zeros((tr, LANES), F32)
        for hh in range(MLA_HEADS):
            base = hh * MLA_SLOT
            dqo_ref[:, base:base + HEAD] = dq_ref[:, base:base + HEAD].astype(dqo_ref.dtype)
            dqo_ref[:, base + HEAD:base + MLA_SLOT] = _rope_bwd(
                dq_ref[:, base + HEAD:base + MLA_SLOT], cosv, sinv, 32).astype(dqo_ref.dtype)
            dko_ref[:, base:base + HEAD] = dk_ref[:, base:base + HEAD].astype(dko_ref.dtype)
            dko_ref[:, base + HEAD:base + MLA_SLOT] = jnp.zeros((tr, LANES), dko_ref.dtype)
            pe = pe + dk_ref[:, base + HEAD:base + MLA_SLOT]
        dkpe_ref[...] = _rope_bwd(pe, cosv, sinv, 32).astype(dkpe_ref.dtype)

    row = pl.BlockSpec((tr, w), lambda i: (i, 0))
    tab = pl.BlockSpec((tr, LANES), lambda i: (i % npos, 0))
    return pl.pallas_call(
        body, grid=(t // tr,), in_specs=[row, row, tab, tab],
        out_specs=[row, row, pl.BlockSpec((tr, LANES), lambda i: (i, 0))],
        out_shape=[SDS((t, w), BF16), SDS((t, w), BF16), SDS((t, LANES), BF16)],
        compiler_params=_cparams(), name=name)(dq, dk, cos, sin)


def _softmax_rows(q, k, scale, lp, l_valid):
    s = lax.dot_general(q, k, (((1,), (1,)), ((), ())), preferred_element_type=F32) * scale
    cols = lax.broadcasted_iota(jnp.int32, (1, lp), 1)
    s = jnp.where(cols < l_valid, s, -1e30)
    e = jnp.exp(s - jnp.max(s, axis=-1, keepdims=True))
    return e * (1.0 / jnp.sum(e, axis=-1, keepdims=True))


def _attn_fwd(q, k, v, *, batch, lp, l_valid, heads, kv_heads, dk, dv, scale, name):
    group = heads // kv_heads
    tq = _tile(lp, 512, SUBLANES_BF16)
    nq = lp // tq

    def body(q_ref, k_ref, v_ref, o_ref):
        p = _softmax_rows(q_ref[...], k_ref[...], scale, lp, l_valid)
        o_ref[...] = jnp.dot(p.astype(BF16), v_ref[...], preferred_element_type=F32).astype(o_ref.dtype)

    return pl.pallas_call(
        body, grid=(batch, heads, nq),
        in_specs=[pl.BlockSpec((tq, dk), lambda b, h, i: (b * nq + i, h)),
                  pl.BlockSpec((lp, dk), lambda b, h, i: (b, h // group)),
                  pl.BlockSpec((lp, dv), lambda b, h, i: (b, h // group))],
        out_specs=pl.BlockSpec((tq, dv), lambda b, h, i: (b * nq + i, h)),
        out_shape=SDS((batch * lp, heads * dv), BF16), compiler_params=_cparams(), name=name)(q, k, v)


def _attn_bwd(q, k, v, do, *, batch, lp, l_valid, heads, kv_heads, dk, dv, scale, name):
    group = heads // kv_heads
    tq = _tile(lp, 512, SUBLANES_BF16)
    nq = lp // tq

    def body(q_ref, k_ref, v_ref, do_ref, dq_ref, dk_ref, dv_ref):
        first = (pl.program_id(2) == 0) & (pl.program_id(3) == 0)
        qv, kv, dov = q_ref[...], k_ref[...], do_ref[...]
        p = _softmax_rows(qv, kv, scale, lp, l_valid)
        dp = lax.dot_general(dov, v_ref[...], (((1,), (1,)), ((), ())), preferred_element_type=F32)
        ds = (p * (dp - jnp.sum(p * dp, axis=-1, keepdims=True)) * scale).astype(BF16)
        dq_ref[...] = jnp.dot(ds, kv, preferred_element_type=F32)
        dkp = lax.dot_general(ds, qv, (((0,), (0,)), ((), ())), preferred_element_type=F32)
        dvp = lax.dot_general(p.astype(BF16), dov, (((0,), (0,)), ((), ())), preferred_element_type=F32)

        @pl.when(first)
        def _():
            dk_ref[...] = dkp
            dv_ref[...] = dvp

        @pl.when(jnp.logical_not(first))
        def _():
            dk_ref[...] += dkp
            dv_ref[...] += dvp

    return pl.pallas_call(
        body, grid=(batch, kv_heads, group, nq),
        in_specs=[pl.BlockSpec((tq, dk), lambda b, hk, g, i: (b * nq + i, hk * group + g)),
                  pl.BlockSpec((lp, dk), lambda b, hk, g, i: (b, hk)),
                  pl.BlockSpec((lp, dv), lambda b, hk, g, i: (b, hk)),
                  pl.BlockSpec((tq, dv), lambda b, hk, g, i: (b * nq + i, hk * group + g))],
        out_specs=[pl.BlockSpec((tq, dk), lambda b, hk, g, i: (b * nq + i, hk * group + g)),
                   pl.BlockSpec((lp, dk), lambda b, hk, g, i: (b, hk)),
                   pl.BlockSpec((lp, dv), lambda b, hk, g, i: (b, hk))],
        out_shape=[SDS((batch * lp, heads * dk), F32), SDS((batch * lp, kv_heads * dk), F32),
                   SDS((batch * lp, kv_heads * dv), F32)],
        compiler_params=_cparams(), name=name)(q, k, v, do)


def _meta_grad(dh3, *, name):
    b, _, d = dh3.shape

    def body(x_ref, o_ref):
        @pl.when(pl.program_id(0) == 0)
        def _():
            o_ref[...] = x_ref[...]

        @pl.when(pl.program_id(0) > 0)
        def _():
            o_ref[...] += x_ref[...]

    return pl.pallas_call(
        body, grid=(b,), in_specs=[pl.BlockSpec((None, N_META, d), lambda bi: (bi, 0, 0))],
        out_specs=pl.BlockSpec((N_META, d), lambda bi: (0, 0)), out_shape=SDS((N_META, d), F32),
        compiler_params=_cparams(), name=name)(dh3)


def _sum_slots(parts, *, out_dtype, name):
    _, r, c = parts[0].shape
    tr = _row_tile(r, c)
    counts = [p.shape[0] for p in parts]

    def body(*refs):
        o_ref = refs[-1]
        acc = None
        for ref, cnt in zip(refs[:-1], counts):
            for s in range(cnt):
                term = ref[s].astype(F32)
                acc = term if acc is None else acc + term
        o_ref[...] = acc.astype(o_ref.dtype)

    return pl.pallas_call(
        body, grid=(r // tr,),
        in_specs=[pl.BlockSpec((cnt, tr, c), lambda i: (0, i, 0)) for cnt in counts],
        out_specs=pl.BlockSpec((tr, c), lambda i: (i, 0)), out_shape=SDS((r, c), out_dtype),
        compiler_params=_cparams(), name=name)(*parts)


def _adamw(w, g, m, v, *, name):
    shape = w.shape
    cols = shape[-1]
    rows = w.size // cols
    tr = _tile(rows, max(8, ROW_BLOCK_BYTES // (4 * cols)), 8)
    bc1 = 1.0 - ADAM_B1 ** ADAM_STEP
    bc2 = 1.0 - ADAM_B2 ** ADAM_STEP

    def body(w_ref, g_ref, m_ref, v_ref, d_ref, mo_ref, vo_ref):
        gv = g_ref[...]
        mn = ADAM_B1 * m_ref[...] + (1.0 - ADAM_B1) * gv
        vn = ADAM_B2 * v_ref[...] + (1.0 - ADAM_B2) * (gv * gv)
        mo_ref[...] = mn
        vo_ref[...] = vn
        d_ref[...] = -ADAM_LR * ((mn / bc1) / (jnp.sqrt(vn / bc2) + ADAM_EPS) + ADAM_WD * w_ref[...])

    spec = pl.BlockSpec((tr, cols), lambda i: (i, 0))
    outs = pl.pallas_call(
        body, grid=(rows // tr,), in_specs=[spec] * 4, out_specs=[spec] * 3,
        out_shape=[SDS((rows, cols), F32)] * 3, compiler_params=_cparams(), name=name)(
            *[a.reshape(rows, cols) for a in (w, g, m, v)])
    return tuple(o.reshape(shape) for o in outs)


def _mesh_pos():
    return lax.axis_index("x"), lax.axis_index("y"), lax.axis_index("c")


def _other_chips(x, y):
    return [(1 - x, y), (x, 1 - y), (1 - x, 1 - y)]


def _chip_allgather_small(vec, *, name):
    r, c = vec.shape

    def body(x_ref, out_ref, send_sems, recv_sems):
        x, y, cc = _mesh_pos()
        mine = 2 * x + y
        out_ref[mine] = x_ref[...]
        chips = _other_chips(x, y)
        sends = [pltpu.make_async_remote_copy(
            src_ref=x_ref, dst_ref=out_ref.at[mine], send_sem=send_sems.at[j], recv_sem=recv_sems.at[j],
            device_id=(px, py, cc), device_id_type=MESH) for j, (px, py) in enumerate(chips)]
        for cp in sends:
            cp.start()
        for j, (px, py) in enumerate(chips):
            pltpu.make_async_remote_copy(
                src_ref=x_ref, dst_ref=out_ref.at[2 * px + py], send_sem=send_sems.at[j],
                recv_sem=recv_sems.at[j], device_id=(px, py, cc), device_id_type=MESH).wait_recv()
        for cp in sends:
            cp.wait_send()

    return pl.pallas_call(
        body, out_shape=SDS((N_CHIPS, r, c), F32),
        in_specs=[pl.BlockSpec(memory_space=pltpu.VMEM)], out_specs=pl.BlockSpec(memory_space=pltpu.VMEM),
        scratch_shapes=[pltpu.SemaphoreType.DMA((3,)), pltpu.SemaphoreType.DMA((3,))],
        compiler_params=_cparams(), name=name)(vec)


def _allreduce_small(vec, *, name):
    r, c = vec.shape

    def body(x_ref, out_ref, buf, send_sems, recv_sems):
        x, y, cc = _mesh_pos()
        me = 4 * x + 2 * y + cc
        buf[me] = x_ref[...]

        def peer(kk):
            fx, fy, fc = (kk >> 2) & 1, (kk >> 1) & 1, kk & 1
            px = x if fx == 0 else 1 - x
            py = y if fy == 0 else 1 - y
            pc = cc if fc == 0 else 1 - cc
            return (px, py, pc), 4 * px + 2 * py + pc

        sends = []
        for kk in range(1, N_DEV):
            dev, _ = peer(kk)
            sends.append(pltpu.make_async_remote_copy(
                src_ref=x_ref, dst_ref=buf.at[me], send_sem=send_sems.at[kk - 1], recv_sem=recv_sems.at[kk - 1],
                device_id=dev, device_id_type=MESH))
        for cp in sends:
            cp.start()
        for kk in range(1, N_DEV):
            dev, slot = peer(kk)
            pltpu.make_async_remote_copy(
                src_ref=x_ref, dst_ref=buf.at[slot], send_sem=send_sems.at[kk - 1], recv_sem=recv_sems.at[kk - 1],
                device_id=dev, device_id_type=MESH).wait_recv()
        for cp in sends:
            cp.wait_send()
        acc = buf[0]
        for s in range(1, N_DEV):
            acc = acc + buf[s]
        out_ref[...] = acc

    return pl.pallas_call(
        body, out_shape=SDS((r, c), F32),
        in_specs=[pl.BlockSpec(memory_space=pltpu.VMEM)], out_specs=pl.BlockSpec(memory_space=pltpu.VMEM),
        scratch_shapes=[pltpu.VMEM((N_DEV, r, c), F32), pltpu.SemaphoreType.DMA((N_DEV - 1,)),
                        pltpu.SemaphoreType.DMA((N_DEV - 1,))],
        compiler_params=_cparams(), name=name)(vec)


_HBM = pl.BlockSpec(memory_space=pltpu.HBM)


def _half(rows, half):
    hk = rows // 2
    assert hk % SUBLANES_BF16 == 0
    return pl.ds(pl.multiple_of(half * hk, SUBLANES_BF16), hk)


def _weights_allgather(shards, *, name):
    nw = len(shards)

    def body(*refs):
        w_refs, out_refs = refs[:nw], refs[nw:2 * nw]
        send_sems, recv_sems = refs[2 * nw:]
        x, y, cc = _mesh_pos()
        mine = 2 * x + y
        chips = _other_chips(x, y)

        def over_ici(i, j, px, py, slot):
            rows = _half(shards[i].shape[0], cc)
            return pltpu.make_async_remote_copy(
                src_ref=w_refs[i].at[rows], dst_ref=out_refs[i].at[slot, rows], send_sem=send_sems.at[j, i],
                recv_sem=recv_sems.at[j, i], device_id=(px, py, cc), device_id_type=MESH)

        def to_sibling(i, j, half, slot):
            rows = _half(shards[i].shape[0], half)
            return pltpu.make_async_remote_copy(
                src_ref=out_refs[i].at[slot, rows], dst_ref=out_refs[i].at[slot, rows],
                send_sem=send_sems.at[3 + j, i], recv_sem=recv_sems.at[3 + j, i],
                device_id=(x, y, 1 - cc), device_id_type=MESH)

        first = [over_ici(i, j, px, py, mine) for j, (px, py) in enumerate(chips) for i in range(nw)]
        for cp in first:
            cp.start()
        passed = []
        for j, (px, py) in enumerate(chips):
            for i in range(nw):
                over_ici(i, j, px, py, 2 * px + py).wait_recv()
                passed.append(to_sibling(i, j, cc, 2 * px + py))
                passed[-1].start()
        for j, (px, py) in enumerate(chips):
            for i in range(nw):
                to_sibling(i, j, 1 - cc, 2 * px + py).wait_recv()
        for cp in first + passed:
            cp.wait_send()

    return pl.pallas_call(
        body, out_shape=[SDS((N_CHIPS,) + s.shape, s.dtype) for s in shards],
        in_specs=[_HBM] * nw, out_specs=[_HBM] * nw,
        scratch_shapes=[pltpu.SemaphoreType.DMA((6, nw)), pltpu.SemaphoreType.DMA((6, nw))],
        name=name)(*shards)


def _grads_to_sibling(gs, *, name):
    nw = len(gs)

    def body(*refs):
        g_refs, out_refs = refs[:nw], refs[nw:2 * nw]
        send_sems, recv_sems = refs[2 * nw:]
        x, y, cc = _mesh_pos()
        cps = [pltpu.make_async_remote_copy(
            src_ref=g_refs[i].at[pl.ds(0, N_CHIPS), _half(gs[i].shape[1], 1 - cc)], dst_ref=out_refs[i],
            send_sem=send_sems.at[i], recv_sem=recv_sems.at[i], device_id=(x, y, 1 - cc), device_id_type=MESH)
            for i in range(nw)]
        for cp in cps:
            cp.start()
        for cp in cps:
            cp.wait()

    return pl.pallas_call(
        body, out_shape=[SDS((N_CHIPS, g.shape[1] // 2, g.shape[2]), g.dtype) for g in gs],
        in_specs=[_HBM] * nw, out_specs=[_HBM] * nw,
        scratch_shapes=[pltpu.SemaphoreType.DMA((nw,)), pltpu.SemaphoreType.DMA((nw,))], name=name)(*gs)


def _grads_to_chips(parts, *, name):
    nw = len(parts)

    def body(*refs):
        p_refs, out_refs = refs[:nw], refs[nw:2 * nw]
        send_sems, recv_sems = refs[2 * nw:]
        x, y, cc = _mesh_pos()
        cps = [pltpu.make_async_remote_copy(
            src_ref=p_refs[i].at[2 * px + py], dst_ref=out_refs[i].at[j], send_sem=send_sems.at[j, i],
            recv_sem=recv_sems.at[j, i], device_id=(px, py, cc), device_id_type=MESH)
            for j, (px, py) in enumerate(_other_chips(x, y)) for i in range(nw)]
        for cp in cps:
            cp.start()
        for cp in cps:
            cp.wait()

    return pl.pallas_call(
        body, out_shape=[SDS((3,) + pt.shape[1:], pt.dtype) for pt in parts],
        in_specs=[_HBM] * nw, out_specs=[_HBM] * nw,
        scratch_shapes=[pltpu.SemaphoreType.DMA((3, nw)), pltpu.SemaphoreType.DMA((3, nw))], name=name)(*parts)


def _share_halves(tots, *, name):
    nw = len(tots)

    def body(*refs):
        t_refs, out_refs = refs[:nw], refs[nw:2 * nw]
        send_sems, recv_sems = refs[2 * nw:]
        x, y, cc = _mesh_pos()
        cps = [pltpu.make_async_remote_copy(
            src_ref=t_refs[i], dst_ref=out_refs[i], send_sem=send_sems.at[i], recv_sem=recv_sems.at[i],
            device_id=(x, y, 1 - cc), device_id_type=MESH) for i in range(nw)]
        for cp in cps:
            cp.start()
        for cp in cps:
            cp.wait()

    return pl.pallas_call(
        body, out_shape=[SDS(tt.shape, tt.dtype) for tt in tots],
        in_specs=[_HBM] * nw, out_specs=[_HBM] * nw,
        scratch_shapes=[pltpu.SemaphoreType.DMA((nw,)), pltpu.SemaphoreType.DMA((nw,))], name=name)(*tots)


def _pack_small(arrays):
    flat = jnp.concatenate([a.astype(F32).reshape(-1) for a in arrays])
    n = flat.shape[0]
    padded = -(-n // (8 * LANES)) * (8 * LANES)
    return jnp.pad(flat, (0, padded - n)).reshape(padded // LANES, LANES)


def _unpack_small(flat, shapes):
    out, off = [], 0
    for shp in shapes:
        n = math.prod(shp)
        out.append(flat[off:off + n].reshape(shp))
        off += n
    return out


BIG = (("w_in", 1), ("w_conv_out", 1), ("w_gqa_out", 1), ("w_mla_uq", 1), ("w_mla_ukv", 1), ("w_mla_out", 1),
       ("w_out", 0), ("w_ffn_gate", 1), ("w_ffn_up", 1), ("w_ffn_down", 0))
BIG_NAMES = tuple(nm for nm, _ in BIG)


def _cols_to_slots(full):
    k, n = full.shape
    return full.reshape(k, N_CHIPS, n // N_CHIPS).transpose(1, 0, 2)


def _slots_to_cols(slots):
    s, k, n = slots.shape
    return slots.transpose(1, 0, 2).reshape(k, s * n)


def _step(p, mom_m, mom_v, x, loss_target):
    b, seq, d = x.shape
    depth = p["w_in"].shape[0]
    l_valid = N_META + seq
    lp = -(-l_valid // LANES) * LANES
    t = b * lp
    c_conv = d // 2
    q_rank = p["mla_q_norm_g"].shape[1]
    kv_rank = p["mla_kv_norm_g"].shape[1]
    qw, kw = GQA_HEADS * HEAD, GQA_KV_HEADS * HEAD
    off_q = 2 * c_conv
    off_k = off_q + qw
    off_v = off_k + kw
    off_cq = off_v + kw
    off_ckv = off_cq + q_rank
    off_gate = off_ckv + kv_rank
    off_kpe = off_gate + N_BRANCH * d
    d_inp = off_kpe + LANES
    assert off_cq % q_rank == 0 and off_ckv % kv_rank == 0
    xi, yi, ci = _mesh_pos()
    chip = 2 * xi + yi

    small_sh = [p["meta_tokens"], p["conv_dw"]]
    got = _chip_allgather_small(_pack_small(small_sh), name="gather_small")
    got = [_unpack_small(got[s].reshape(-1), [a.shape for a in small_sh]) for s in range(N_CHIPS)]
    meta_full = jnp.concatenate([g[0] for g in got], axis=1)
    conv_dw_full = jnp.concatenate([g[1] for g in got], axis=-1).reshape(depth, CONV_K, c_conv)

    weights = []
    for li in range(depth):
        own = [p[nm][li].astype(BF16) for nm in BIG_NAMES]
        got = _weights_allgather(own, name="gather_weights")
        full = {nm: lax.dynamic_update_slice(g, o[None], (chip, 0, 0)) for nm, g, o in zip(BIG_NAMES, got, own)}
        w_in = _slots_to_cols(full["w_in"])
        full["w_in"] = jnp.concatenate(
            [w_in[:, :off_gate], w_in[:, off_gate + MLA_ROPE:], w_in[:, off_gate:off_gate + MLA_ROPE],
             jnp.zeros((d, LANES - MLA_ROPE), BF16)], axis=1)
        full["w_out"] = full["w_out"].reshape(-1, d)
        full["w_ffn_down"] = full["w_ffn_down"].reshape(-1, d)
        uq = _slots_to_cols(full["w_mla_uq"]).reshape(q_rank, MLA_HEADS, HEAD + MLA_ROPE)
        full["w_uq"] = jnp.pad(uq, ((0, 0), (0, 0), (0, MLA_SLOT - HEAD - MLA_ROPE))).reshape(
            q_rank, MLA_HEADS * MLA_SLOT)
        ukv = _slots_to_cols(full["w_mla_ukv"]).reshape(kv_rank, MLA_HEADS, 2 * HEAD)
        full["w_uk"] = jnp.pad(ukv[:, :, :HEAD], ((0, 0), (0, 0), (0, MLA_SLOT - HEAD))).reshape(
            kv_rank, MLA_HEADS * MLA_SLOT)
        full["w_uv"] = ukv[:, :, HEAD:].reshape(kv_rank, MLA_HEADS * HEAD)
        weights.append(full)

    cos_g, sin_g = _rope_tables(lp, seq, HEAD, HEAD)
    cos_m, sin_m = _rope_tables(lp, seq, MLA_ROPE, LANES)
    gqa_scale = 1.0 / math.sqrt(HEAD)
    mla_scale = 1.0 / math.sqrt(HEAD + MLA_ROPE)
    attn_kw = dict(batch=b, lp=lp, l_valid=l_valid)
    gqa_kw = dict(heads=GQA_HEADS, kv_heads=GQA_KV_HEADS, dk=HEAD, dv=HEAD, scale=gqa_scale, **attn_kw)
    mla_kw = dict(heads=MLA_HEADS, kv_heads=MLA_HEADS, dk=MLA_SLOT, dv=HEAD, scale=mla_scale, **attn_kw)

    h = jnp.concatenate([jnp.broadcast_to(meta_full[None], (b, N_META, d)), x,
                         jnp.zeros((b, lp - l_valid, d), F32)], axis=1).reshape(t, d)
    saved = []
    for li in range(depth):
        w = weights[li]
        row = lambda nm: p[nm][li][None, :]
        s = {"h_in": h}
        s["u"] = _rmsnorm_fwd(h, row("mix_norm_g"), width=d, cblk=0, name="mix_norm")
        proj = _mm(s["u"], w["w_in"], name="in_proj")
        s["proj"] = proj
        proj3 = proj.reshape(b, lp, d_inp)
        s["yconv"] = _conv_fwd(proj3, conv_dw_full[li], row("conv_b"), c_conv=c_conv, l_valid=l_valid,
                               name="conv").reshape(t, c_conv)
        s["sconv"] = _ln_silu_fwd(s["yconv"], row("conv_ln_g"), row("conv_ln_b"), name="conv_ln_silu")
        s["ya"] = _mm(s["sconv"], w["w_conv_out"], b_sharded=True, name="conv_out")
        s["gq"], s["gk"], s["gv"] = _gqa_prep_fwd(
            proj, row("gqa_q_norm_g"), row("gqa_k_norm_g"), cos_g, sin_g,
            q_off=off_q, k_off=off_k, v_off=off_v, lp=lp, name="gqa_prep")
        s["go"] = _attn_fwd(s["gq"], s["gk"], s["gv"], name="gqa_attn", **gqa_kw)
        s["yb"] = _mm(s["go"], w["w_gqa_out"], b_sharded=True, name="gqa_out")
        s["cqn"] = _rmsnorm_fwd(proj, row("mla_q_norm_g"), width=q_rank, cblk=off_cq // q_rank, name="mla_q_norm")
        s["kvn"] = _rmsnorm_fwd(proj, row("mla_kv_norm_g"), width=kv_rank, cblk=off_ckv // kv_rank,
                                name="mla_kv_norm")
        q_pre = _mm(s["cqn"], w["w_uq"], name="mla_uq")
        k_pre = _mm(s["kvn"], w["w_uk"], name="mla_uk")
        s["mv"] = _mm(s["kvn"], w["w_uv"], out_dtype=BF16, name="mla_uv")
        s["mq"], s["mk"] = _mla_rope_fwd(q_pre, k_pre, proj, cos_m, sin_m, kpe_off=off_kpe, lp=lp, name="mla_rope")
        s["mo"] = _attn_fwd(s["mq"], s["mk"], s["mv"], name="mla_attn", **mla_kw)
        s["yc"] = _mm(s["mo"], w["w_mla_out"], b_sharded=True, name="mla_out")
        s["merged"] = _merge_fwd(proj, row("gate_b"), s["ya"], s["yb"], s["yc"], gate_off=off_gate, name="merge")
        h = _mm(s["merged"], w["w_out"], add=h, name="mix_out")
        s["h_mid"] = h
        s["v"] = _rmsnorm_fwd(h, row("ffn_norm_g"), width=d, cblk=0, name="ffn_norm")
        s["gate"] = _mm(s["v"], w["w_ffn_gate"], b_sharded=True, name="ffn_gate")
        s["up"] = _mm(s["v"], w["w_ffn_up"], b_sharded=True, name="ffn_up")
        s["act"] = _swiglu_fwd(s["gate"], s["up"], name="swiglu")
        h = _mm(s["act"], w["w_ffn_down"], add=h, name="ffn_down")
        saved.append(s)

    target = jnp.pad(loss_target, ((0, 0), (N_META, lp - l_valid), (0, 0))).reshape(t, d)
    loss_part, dh, dhb, g_final = _loss_head(h, p["final_norm_g"][None, :], target, lp=lp, l_valid=l_valid,
                                             name="loss_head")
    loss = lax.psum(loss_part[0, 0], ("x", "y", "c"))

    small_names = ["mix_norm_g", "conv_dw", "conv_b", "conv_ln_g", "conv_ln_b", "gqa_q_norm_g", "gqa_k_norm_g",
                   "mla_q_norm_g", "mla_kv_norm_g", "gate_b", "ffn_norm_g"]
    small_g = {nm: [None] * depth for nm in small_names}
    shard_g = {nm: [None] * depth for nm in BIG_NAMES}
    for li in reversed(range(depth)):
        w, s = weights[li], saved[li]
        row = lambda nm: p[nm][li][None, :]
        big = {}
        dact = _mm(dhb, w["w_ffn_down"], mode="nt", name="d_ffn_act")
        big["w_ffn_down"] = _mm(s["act"], dhb, mode="tn", out_dtype=BF16, name="dw_ffn_down").reshape(N_CHIPS, -1, d)
        dgate, dup = _swiglu_bwd(s["gate"], s["up"], dact, name="d_swiglu")
        dv = _mm(dgate, w["w_ffn_gate"], mode="nt", b_sharded=True, name="d_ffn_v_gate")
        dv = _mm(dup, w["w_ffn_up"], mode="nt", b_sharded=True, add=dv, name="d_ffn_v_up")
        big["w_ffn_gate"] = _mm(s["v"], dgate, mode="tn", out_dtype=BF16, out_shards=N_CHIPS, name="dw_ffn_gate")
        big["w_ffn_up"] = _mm(s["v"], dup, mode="tn", out_dtype=BF16, out_shards=N_CHIPS, name="dw_ffn_up")
        dh, dhb, g = _rmsnorm_bwd(s["h_mid"], row("ffn_norm_g"), dv, width=d, cblk=0, res=dh, name="d_ffn_norm")
        small_g["ffn_norm_g"][li] = g[0]
        dm = _mm(dhb, w["w_out"], mode="nt", name="d_merged")
        big["w_out"] = _mm(s["merged"], dhb, mode="tn", out_dtype=BF16, name="dw_out").reshape(N_CHIPS, -1, d)
        dya, dyb, dyc, dl0, dl1, dl2, db0, db1, db2 = _merge_bwd(
            s["proj"], row("gate_b"), s["ya"], s["yb"], s["yc"], dm, gate_off=off_gate, name="d_merge")
        small_g["gate_b"][li] = jnp.concatenate([db0[0], db1[0], db2[0]])
        ds = _mm(dya, w["w_conv_out"], mode="nt", b_sharded=True, name="d_conv_s")
        big["w_conv_out"] = _mm(s["sconv"], dya, mode="tn", out_dtype=BF16, out_shards=N_CHIPS, name="dw_conv_out")
        dyconv, g, gb = _ln_silu_bwd(s["yconv"], row("conv_ln_g"), row("conv_ln_b"), ds, name="d_conv_ln_silu")
        small_g["conv_ln_g"][li], small_g["conv_ln_b"][li] = g[0], gb[0]
        proj3 = s["proj"].reshape(b, lp, d_inp)
        da, dgte, ddw, dcb = _conv_bwd(proj3, conv_dw_full[li], dyconv.reshape(b, lp, c_conv), c_conv=c_conv,
                                       l_valid=l_valid, name="d_conv")
        small_g["conv_dw"][li], small_g["conv_b"][li] = ddw, dcb[0]
        dgo = _mm(dyb, w["w_gqa_out"], mode="nt", b_sharded=True, out_dtype=BF16, name="d_gqa_o")
        big["w_gqa_out"] = _mm(s["go"], dyb, mode="tn", out_dtype=BF16, out_shards=N_CHIPS, name="dw_gqa_out")
        dgq, dgk, dgv = _attn_bwd(s["gq"], s["gk"], s["gv"], dgo, name="d_gqa_attn", **gqa_kw)
        dq_g, dk_g, gq, gk = _gqa_prep_bwd(s["proj"], row("gqa_q_norm_g"), row("gqa_k_norm_g"), cos_g, sin_g,
                                           dgq, dgk, q_off=off_q, k_off=off_k, lp=lp, name="d_gqa_prep")
        small_g["gqa_q_norm_g"][li], small_g["gqa_k_norm_g"][li] = gq[0], gk[0]
        dmo = _mm(dyc, w["w_mla_out"], mode="nt", b_sharded=True, out_dtype=BF16, name="d_mla_o")
        big["w_mla_out"] = _mm(s["mo"], dyc, mode="tn", out_dtype=BF16, out_shards=N_CHIPS, name="dw_mla_out")
        dmq, dmk, dmv = _attn_bwd(s["mq"], s["mk"], s["mv"], dmo, name="d_mla_attn", **mla_kw)
        dq_pre, dk_pre, dkpe = _mla_rope_bwd(dmq, dmk, cos_m, sin_m, lp=lp, name="d_mla_rope")
        dmvb = dmv.astype(BF16)
        g_uq = _mm(s["cqn"], dq_pre, mode="tn", out_dtype=BF16, name="dw_mla_uq")
        g_uk = _mm(s["kvn"], dk_pre, mode="tn", out_dtype=BF16, name="dw_mla_uk")
        g_uv = _mm(s["kvn"], dmvb, mode="tn", out_dtype=BF16, name="dw_mla_uv")
        big["w_mla_uq"] = _cols_to_slots(
            g_uq.reshape(q_rank, MLA_HEADS, MLA_SLOT)[:, :, :HEAD + MLA_ROPE].reshape(q_rank, -1))
        big["w_mla_ukv"] = _cols_to_slots(jnp.concatenate(
            [g_uk.reshape(kv_rank, MLA_HEADS, MLA_SLOT)[:, :, :HEAD], g_uv.reshape(kv_rank, MLA_HEADS, HEAD)],
            axis=-1).reshape(kv_rank, -1))
        dcqn = _mm(dq_pre, w["w_uq"], mode="nt", name="d_mla_cqn")
        dkvn = _mm(dk_pre, w["w_uk"], mode="nt", name="d_mla_kvn_k")
        dkvn = _mm(dmvb, w["w_uv"], mode="nt", add=dkvn, name="d_mla_kvn_v")
        _, dcq, g = _rmsnorm_bwd(s["proj"], row("mla_q_norm_g"), dcqn, width=q_rank, cblk=off_cq // q_rank,
                                 name="d_mla_q_norm")
        small_g["mla_q_norm_g"][li] = g[0]
        _, dckv, g = _rmsnorm_bwd(s["proj"], row("mla_kv_norm_g"), dkvn, width=kv_rank, cblk=off_ckv // kv_rank,
                                  name="d_mla_kv_norm")
        small_g["mla_kv_norm_g"][li] = g[0]
        dproj = jnp.concatenate(
            [da.reshape(t, c_conv), dgte.reshape(t, c_conv), dq_g, dk_g, dgv.astype(BF16), dcq, dckv,
             dl0, dl1, dl2, dkpe], axis=1)
        du = _mm(dproj, w["w_in"], mode="nt", name="d_u")
        g_in = _mm(s["u"], dproj, mode="tn", out_dtype=BF16, name="dw_in")
        big["w_in"] = _cols_to_slots(jnp.concatenate(
            [g_in[:, :off_gate], g_in[:, off_kpe:off_kpe + MLA_ROPE], g_in[:, off_gate:off_kpe]], axis=1))
        dh, dhb, g = _rmsnorm_bwd(s["h_in"], row("mix_norm_g"), du, width=d, cblk=0, res=dh, name="d_mix_norm")
        small_g["mix_norm_g"][li] = g[0]
        glist = [big[nm] for nm in BIG_NAMES]
        from_sib = _grads_to_sibling(glist, name="grads_to_sibling")
        parts = []
        for nm, g, fs in zip(BIG_NAMES, glist, from_sib):
            hk = g.shape[1] // 2
            mine = lax.dynamic_slice_in_dim(g, ci * hk, hk, axis=1)
            rows = (1, N_CHIPS * hk, g.shape[2])
            parts.append(_sum_slots([mine.reshape(rows), fs.reshape(rows)], out_dtype=BF16,
                                    name="sum_sibling_" + nm).reshape(fs.shape))
        from_chips = _grads_to_chips(parts, name="grads_to_chips")
        tots = [_sum_slots([lax.dynamic_index_in_dim(pt, chip, 0, keepdims=True), fc], out_dtype=F32,
                           name="sum_chips_" + nm) for nm, pt, fc in zip(BIG_NAMES, parts, from_chips)]
        others = _share_halves(tots, name="share_halves")
        for nm, tot, oth in zip(BIG_NAMES, tots, others):
            shard_g[nm][li] = jnp.concatenate([jnp.where(ci == 0, tot, oth), jnp.where(ci == 0, oth, tot)], axis=0)

    dh3 = dh.reshape(b, lp, d)
    grad_x = dh3[:, N_META:l_valid]
    g_meta = _meta_grad(dh3, name="meta_grad")

    small_list = [g_meta, g_final[0]] + [jnp.stack(small_g[nm]) for nm in small_names]
    red = _allreduce_small(_pack_small(small_list), name="allreduce_small")
    red = _unpack_small(red.reshape(-1), [a.shape for a in small_list])
    grads = {"meta_tokens": lax.dynamic_slice_in_dim(red[0], chip * (d // N_CHIPS), d // N_CHIPS, axis=1),
             "final_norm_g": red[1]}
    for nm, val in zip(small_names, red[2:]):
        grads[nm] = val
    cs = c_conv // N_CHIPS
    grads["conv_dw"] = lax.dynamic_slice_in_dim(grads["conv_dw"], chip * cs, cs, axis=2).reshape(p["conv_dw"].shape)
    for nm in BIG_NAMES:
        grads[nm] = jnp.stack(shard_g[nm])

    delta, new_m, new_v = {}, {}, {}
    for nm in p:
        delta[nm], new_m[nm], new_v[nm] = _adamw(p[nm], grads[nm], mom_m[nm], mom_v[nm], name="adamw_" + nm)
    return loss, grad_x, grads, delta, new_m, new_v


WEIGHT_NAMES = ("meta_tokens", "mix_norm_g", "w_in", "conv_dw", "conv_b", "conv_ln_g", "conv_ln_b", "w_conv_out",
                "gqa_q_norm_g", "gqa_k_norm_g", "w_gqa_out", "mla_q_norm_g", "w_mla_uq", "mla_kv_norm_g",
                "w_mla_ukv", "w_mla_out", "gate_b", "w_out", "ffn_norm_g", "w_ffn_gate", "w_ffn_up", "w_ffn_down",
                "final_norm_g")


def kernel(x, meta_tokens, mix_norm_g, w_in, conv_dw, conv_b, conv_ln_g, conv_ln_b, w_conv_out, gqa_q_norm_g, gqa_k_norm_g, w_gqa_out, mla_q_norm_g, w_mla_uq, mla_kv_norm_g, w_mla_ukv, w_mla_out, gate_b, w_out, ffn_norm_g, w_ffn_gate, w_ffn_up, w_ffn_down, final_norm_g, loss_target, m_meta_tokens, m_mix_norm_g, m_w_in, m_conv_dw, m_conv_b, m_conv_ln_g, m_conv_ln_b, m_w_conv_out, m_gqa_q_norm_g, m_gqa_k_norm_g, m_w_gqa_out, m_mla_q_norm_g, m_w_mla_uq, m_mla_kv_norm_g, m_w_mla_ukv, m_w_mla_out, m_gate_b, m_w_out, m_ffn_norm_g, m_w_ffn_gate, m_w_ffn_up, m_w_ffn_down, m_final_norm_g, v_meta_tokens, v_mix_norm_g, v_w_in, v_conv_dw, v_conv_b, v_conv_ln_g, v_conv_ln_b, v_w_conv_out, v_gqa_q_norm_g, v_gqa_k_norm_g, v_w_gqa_out, v_mla_q_norm_g, v_w_mla_uq, v_mla_kv_norm_g, v_w_mla_ukv, v_w_mla_out, v_gate_b, v_w_out, v_ffn_norm_g, v_w_ffn_gate, v_w_ffn_up, v_w_ffn_down, v_final_norm_g):
    ws = (meta_tokens, mix_norm_g, w_in, conv_dw, conv_b, conv_ln_g, conv_ln_b, w_conv_out, gqa_q_norm_g,
          gqa_k_norm_g, w_gqa_out, mla_q_norm_g, w_mla_uq, mla_kv_norm_g, w_mla_ukv, w_mla_out, gate_b, w_out,
          ffn_norm_g, w_ffn_gate, w_ffn_up, w_ffn_down, final_norm_g)
    ms = (m_meta_tokens, m_mix_norm_g, m_w_in, m_conv_dw, m_conv_b, m_conv_ln_g, m_conv_ln_b, m_w_conv_out,
          m_gqa_q_norm_g, m_gqa_k_norm_g, m_w_gqa_out, m_mla_q_norm_g, m_w_mla_uq, m_mla_kv_norm_g, m_w_mla_ukv,
          m_w_mla_out, m_gate_b, m_w_out, m_ffn_norm_g, m_w_ffn_gate, m_w_ffn_up, m_w_ffn_down, m_final_norm_g)
    vs = (v_meta_tokens, v_mix_norm_g, v_w_in, v_conv_dw, v_conv_b, v_conv_ln_g, v_conv_ln_b, v_w_conv_out,
          v_gqa_q_norm_g, v_gqa_k_norm_g, v_w_gqa_out, v_mla_q_norm_g, v_w_mla_uq, v_mla_kv_norm_g, v_w_mla_ukv,
          v_w_mla_out, v_gate_b, v_w_out, v_ffn_norm_g, v_w_ffn_gate, v_w_ffn_up, v_w_ffn_down, v_final_norm_g)
    p = dict(zip(WEIGHT_NAMES, ws))
    loss, grad_x, grads, delta, new_m, new_v = _step(p, dict(zip(WEIGHT_NAMES, ms)), dict(zip(WEIGHT_NAMES, vs)),
                                                     x, loss_target)
    return (loss, grad_x, *[grads[n] for n in WEIGHT_NAMES], *[delta[n] for n in WEIGHT_NAMES],
            *[new_m[n] for n in WEIGHT_NAMES], *[new_v[n] for n in WEIGHT_NAMES])
```

```python
import math

import jax
import jax.numpy as jnp
from jax import lax
from jax.experimental import pallas as pl
from jax.experimental.pallas import tpu as pltpu

F32 = jnp.float32
BF16 = jnp.bfloat16
SDS = jax.ShapeDtypeStruct
MESH = pl.DeviceIdType.MESH

N_META = 16
GRID_W = 64
ROPE_THETA = 10000.0
NORM_EPS = 1e-6
CONV_K = 31
CONV_HALO = 16
HEAD = 128
GQA_HEADS = 8
GQA_KV_HEADS = 2
MLA_HEADS = 8
MLA_ROPE = 64
MLA_SLOT = 256
N_BRANCH = 3

ADAM_LR = 0.001
ADAM_B1 = 0.9
ADAM_B2 = 0.999
ADAM_EPS = 1e-08
ADAM_WD = 0.01
ADAM_STEP = 10

LANES = 128
SUBLANES_BF16 = 16
VMEM_LIMIT_BYTES = 56 * 2 ** 20
ROW_BLOCK_BYTES = 1 << 20
N_CHIPS = 4
N_DEV = 8


def _cparams():
    return pltpu.CompilerParams(vmem_limit_bytes=VMEM_LIMIT_BYTES)


def _tile(n, cap, mult):
    best = None
    for d in range(mult, min(n, cap) + 1, mult):
        if n % d == 0:
            best = d
    return n if best is None else best


def _row_tile(rows, width, mult=SUBLANES_BF16):
    return _tile(rows, max(mult, ROW_BLOCK_BYTES // (4 * width)), mult)


def _sigmoid(x):
    return 1.0 / (1.0 + jnp.exp(-x))


def _mm(a, b, *, mode="nn", add=None, out_dtype=F32, b_sharded=False, out_shards=None, name):
    if b_sharded:
        s_b, r_b, c_b = b.shape
        b_rows, b_cols = r_b, s_b * c_b
    else:
        b_rows, b_cols = b.shape
    if mode == "nn":
        (m, k), n = a.shape, b_cols
        assert b_rows == k
    elif mode == "nt":
        (m, k), n = a.shape, b_rows
        assert b_cols == k
    else:
        (k, m), n = a.shape, b_cols
        assert b_rows == k
    n_unit = n
    if b_sharded and mode != "nt":
        n_unit = c_b
    if out_shards is not None:
        assert n % out_shards == 0
        n_unit = math.gcd(n_unit, n // out_shards)
    k_unit = c_b if (b_sharded and mode == "nt") else k
    if mode == "tn":
        tm = _tile(m, 1024, LANES)
        tk = _tile(k_unit, 2176, SUBLANES_BF16)
    else:
        tm = _tile(m, 1088, SUBLANES_BF16)
        tk = _tile(k_unit, 2176, LANES)
    tn = _tile(n_unit, 1408, LANES)
    nk = k // tk
    if mode == "nn":
        a_spec = pl.BlockSpec((tm, tk), lambda i, j, kk: (i, kk))
        dn = (((1,), (0,)), ((), ()))
        if b_sharded:
            per = c_b // tn
            b_spec = pl.BlockSpec((None, tk, tn), lambda i, j, kk: (j // per, kk, j % per))
        else:
            b_spec = pl.BlockSpec((tk, tn), lambda i, j, kk: (kk, j))
    elif mode == "nt":
        a_spec = pl.BlockSpec((tm, tk), lambda i, j, kk: (i, kk))
        dn = (((1,), (1,)), ((), ()))
        if b_sharded:
            per = c_b // tk
            b_spec = pl.BlockSpec((None, tn, tk), lambda i, j, kk: (kk // per, j, kk % per))
        else:
            b_spec = pl.BlockSpec((tn, tk), lambda i, j, kk: (j, kk))
    else:
        a_spec = pl.BlockSpec((tk, tm), lambda i, j, kk: (kk, i))
        dn = (((0,), (0,)), ((), ()))
        if b_sharded:
            per = c_b // tn
            b_spec = pl.BlockSpec((None, tk, tn), lambda i, j, kk: (j // per, kk, j % per))
        else:
            b_spec = pl.BlockSpec((tk, tn), lambda i, j, kk: (kk, j))
    if out_shards is None:
        o_spec = pl.BlockSpec((tm, tn), lambda i, j, kk: (i, j))
        out_shape = SDS((m, n), out_dtype)
    else:
        per_o = (n // out_shards) // tn
        o_spec = pl.BlockSpec((None, tm, tn), lambda i, j, kk: (j // per_o, i, j % per_o))
        out_shape = SDS((out_shards, m, n // out_shards), out_dtype)
    has_add = add is not None
    assert not (has_add and out_shards is not None)

    def body(a_ref, b_ref, *rest):
        if has_add:
            add_ref, o_ref, acc = rest
        else:
            o_ref, acc = rest
        kk = pl.program_id(2)
        part = lax.dot_general(a_ref[...].astype(BF16), b_ref[...].astype(BF16), dn,
                               preferred_element_type=F32)

        @pl.when(kk == 0)
        def _():
            acc[...] = part

        @pl.when(kk > 0)
        def _():
            acc[...] += part

        @pl.when(kk == nk - 1)
        def _():
            r = acc[...]
            if has_add:
                r = r + add_ref[...]
            o_ref[...] = r.astype(o_ref.dtype)

    in_specs = [a_spec, b_spec] + ([o_spec] if has_add else [])
    args = (a, b) + ((add,) if has_add else ())
    return pl.pallas_call(
        body, grid=(m // tm, n // tn, nk), in_specs=in_specs, out_specs=o_spec,
        out_shape=out_shape, scratch_shapes=[pltpu.VMEM((tm, tn), F32)],
        compiler_params=_cparams(), name=name)(*args)


def _rmsnorm_fwd(x, g, *, width, cblk, name):
    t = x.shape[0]
    tr = _row_tile(t, width)

    def body(x_ref, g_ref, o_ref):
        xv = x_ref[...]
        r = lax.rsqrt(jnp.mean(xv * xv, axis=-1, keepdims=True) + NORM_EPS)
        o_ref[...] = (xv * r * g_ref[...]).astype(o_ref.dtype)

    return pl.pallas_call(
        body, grid=(t // tr,),
        in_specs=[pl.BlockSpec((tr, width), lambda i: (i, cblk)), pl.BlockSpec((1, width), lambda i: (0, 0))],
        out_specs=pl.BlockSpec((tr, width), lambda i: (i, 0)),
        out_shape=SDS((t, width), BF16), compiler_params=_cparams(), name=name)(x, g)


def _rmsnorm_bwd(x, g, dy, *, width, cblk, res=None, name):
    t = x.shape[0]
    tr = _row_tile(t, width)
    has_res = res is not None

    def body(x_ref, g_ref, dy_ref, *rest):
        if has_res:
            res_ref, dx_ref, dxb_ref, dg_ref = rest
        else:
            dx_ref, dxb_ref, dg_ref = rest
        xv = x_ref[...]
        dyv = dy_ref[...].astype(F32)
        r = lax.rsqrt(jnp.mean(xv * xv, axis=-1, keepdims=True) + NORM_EPS)
        gy = dyv * g_ref[...]
        dx = r * gy - xv * (r * r * r) * jnp.mean(gy * xv, axis=-1, keepdims=True)
        if has_res:
            dx = dx + res_ref[...]
        dx_ref[...] = dx
        dxb_ref[...] = dx.astype(BF16)
        part = jnp.sum(dyv * xv * r, axis=0, keepdims=True)

        @pl.when(pl.program_id(0) == 0)
        def _():
            dg_ref[...] = part

        @pl.when(pl.program_id(0) > 0)
        def _():
            dg_ref[...] += part

    row = pl.BlockSpec((tr, width), lambda i: (i, 0))
    vec = pl.BlockSpec((1, width), lambda i: (0, 0))
    in_specs = [pl.BlockSpec((tr, width), lambda i: (i, cblk)), vec, row] + ([row] if has_res else [])
    args = (x, g, dy) + ((res,) if has_res else ())
    return pl.pallas_call(
        body, grid=(t // tr,), in_specs=in_specs, out_specs=[row, row, vec],
        out_shape=[SDS((t, width), F32), SDS((t, width), BF16), SDS((1, width), F32)],
        compiler_params=_cparams(), name=name)(*args)


def _valid_rows(tile_index, tr, lp, lo, hi):
    pos = (tile_index % (lp // tr)) * tr + lax.broadcasted_iota(jnp.int32, (tr, 1), 0)
    return (pos >= lo) & (pos < hi)


def _loss_head(h, g, target, *, lp, l_valid, name):
    t, d = h.shape
    tr = _tile(lp, max(SUBLANES_BF16, ROW_BLOCK_BYTES // (4 * d)), SUBLANES_BF16)

    def body(x_ref, g_ref, t_ref, loss_ref, dx_ref, dxb_ref, dg_ref):
        i = pl.program_id(0)
        xv = x_ref[...]
        r = lax.rsqrt(jnp.mean(xv * xv, axis=-1, keepdims=True) + NORM_EPS)
        y = xv * r * g_ref[...]
        valid = _valid_rows(i, tr, lp, N_META, l_valid)
        err = jnp.where(valid, y - t_ref[...], 0.0)
        lpart = 0.5 * jnp.sum(jnp.mean(err * err, axis=-1, keepdims=True), axis=0, keepdims=True)
        dyv = err * (1.0 / d)
        gy = dyv * g_ref[...]
        dx = r * gy - xv * (r * r * r) * jnp.mean(gy * xv, axis=-1, keepdims=True)
        dx_ref[...] = dx
        dxb_ref[...] = dx.astype(BF16)
        gpart = jnp.sum(dyv * xv * r, axis=0, keepdims=True)

        @pl.when(i == 0)
        def _():
            dg_ref[...] = gpart
            loss_ref[...] = jnp.broadcast_to(lpart, loss_ref.shape)

        @pl.when(i > 0)
        def _():
            dg_ref[...] += gpart
            loss_ref[...] += jnp.broadcast_to(lpart, loss_ref.shape)

    row = pl.BlockSpec((tr, d), lambda i: (i, 0))
    vec = pl.BlockSpec((1, d), lambda i: (0, 0))
    return pl.pallas_call(
        body, grid=(t // tr,), in_specs=[row, vec, row],
        out_specs=[pl.BlockSpec((1, LANES), lambda i: (0, 0)), row, row, vec],
        out_shape=[SDS((1, LANES), F32), SDS((t, d), F32), SDS((t, d), BF16), SDS((1, d), F32)],
        compiler_params=_cparams(), name=name)(h, g, target)


def _ln_silu_fwd(yc, g, b, *, name):
    t, c = yc.shape
    tr = _row_tile(t, c)

    def body(x_ref, g_ref, b_ref, o_ref):
        xv = x_ref[...]
        xc = xv - jnp.mean(xv, axis=-1, keepdims=True)
        r = lax.rsqrt(jnp.mean(xc * xc, axis=-1, keepdims=True) + NORM_EPS)
        ln = xc * r * g_ref[...] + b_ref[...]
        o_ref[...] = (ln * _sigmoid(ln)).astype(o_ref.dtype)

    row = pl.BlockSpec((tr, c), lambda i: (i, 0))
    vec = pl.BlockSpec((1, c), lambda i: (0, 0))
    return pl.pallas_call(body, grid=(t // tr,), in_specs=[row, vec, vec], out_specs=row,
                          out_shape=SDS((t, c), BF16), compiler_params=_cparams(), name=name)(yc, g, b)


def _ln_silu_bwd(yc, g, b, ds, *, name):
    t, c = yc.shape
    tr = _row_tile(t, c)

    def body(x_ref, g_ref, b_ref, ds_ref, dx_ref, dg_ref, db_ref):
        xv = x_ref[...]
        xc = xv - jnp.mean(xv, axis=-1, keepdims=True)
        r = lax.rsqrt(jnp.mean(xc * xc, axis=-1, keepdims=True) + NORM_EPS)
        xh = xc * r
        ln = xh * g_ref[...] + b_ref[...]
        sg = _sigmoid(ln)
        dln = ds_ref[...].astype(F32) * (sg * (1.0 + ln * (1.0 - sg)))
        gy = dln * g_ref[...]
        dx_ref[...] = r * (gy - jnp.mean(gy, axis=-1, keepdims=True)
                           - xh * jnp.mean(gy * xh, axis=-1, keepdims=True))
        gpart = jnp.sum(dln * xh, axis=0, keepdims=True)
        bpart = jnp.sum(dln, axis=0, keepdims=True)

        @pl.when(pl.program_id(0) == 0)
        def _():
            dg_ref[...] = gpart
            db_ref[...] = bpart

        @pl.when(pl.program_id(0) > 0)
        def _():
            dg_ref[...] += gpart
            db_ref[...] += bpart

    row = pl.BlockSpec((tr, c), lambda i: (i, 0))
    vec = pl.BlockSpec((1, c), lambda i: (0, 0))
    return pl.pallas_call(
        body, grid=(t // tr,), in_specs=[row, vec, vec, row], out_specs=[row, vec, vec],
        out_shape=[SDS((t, c), F32), SDS((1, c), F32), SDS((1, c), F32)],
        compiler_params=_cparams(), name=name)(yc, g, b, ds)


def _swiglu_fwd(gate, up, *, name):
    t, f = gate.shape
    tr = _row_tile(t, f)

    def body(g_ref, u_ref, o_ref):
        gv = g_ref[...]
        o_ref[...] = (gv * _sigmoid(gv) * u_ref[...]).astype(o_ref.dtype)

    row = pl.BlockSpec((tr, f), lambda i: (i, 0))
    return pl.pallas_call(body, grid=(t // tr,), in_specs=[row, row], out_specs=row,
                          out_shape=SDS((t, f), BF16), compiler_params=_cparams(), name=name)(gate, up)


def _swiglu_bwd(gate, up, dact, *, name):
    t, f = gate.shape
    tr = _row_tile(t, f)

    def body(g_ref, u_ref, d_ref, dg_ref, du_ref):
        gv = g_ref[...]
        sg = _sigmoid(gv)
        dv = d_ref[...]
        dg_ref[...] = (dv * u_ref[...] * (sg * (1.0 + gv * (1.0 - sg)))).astype(dg_ref.dtype)
        du_ref[...] = (dv * (gv * sg)).astype(du_ref.dtype)

    row = pl.BlockSpec((tr, f), lambda i: (i, 0))
    return pl.pallas_call(body, grid=(t // tr,), in_specs=[row, row, row], out_specs=[row, row],
                          out_shape=[SDS((t, f), BF16), SDS((t, f), BF16)],
                          compiler_params=_cparams(), name=name)(gate, up, dact)


def _merge_tiles(t, d):
    tc = _tile(d, 512, LANES)
    return _row_tile(t, tc), tc


def _merge_fwd(proj, gate_b, ya, yb, yc, *, gate_off, name):
    t, d = ya.shape
    tr, tc = _merge_tiles(t, d)
    nc = d // tc
    assert gate_off % tc == 0
    g0 = gate_off // tc

    def body(l0, l1, l2, b0, b1, b2, ya_ref, yb_ref, yc_ref, o_ref):
        acc = _sigmoid(l0[...] + b0[...]) * ya_ref[...]
        acc += _sigmoid(l1[...] + b1[...]) * yb_ref[...]
        acc += _sigmoid(l2[...] + b2[...]) * yc_ref[...]
        o_ref[...] = acc.astype(o_ref.dtype)

    lspec = [pl.BlockSpec((tr, tc), lambda i, j, br=br: (i, g0 + br * nc + j)) for br in range(N_BRANCH)]
    bspec = [pl.BlockSpec((1, tc), lambda i, j, br=br: (0, br * nc + j)) for br in range(N_BRANCH)]
    yspec = pl.BlockSpec((tr, tc), lambda i, j: (i, j))
    return pl.pallas_call(
        body, grid=(t // tr, nc), in_specs=lspec + bspec + [yspec] * 3, out_specs=yspec,
        out_shape=SDS((t, d), BF16), compiler_params=_cparams(), name=name)(
            proj, proj, proj, gate_b, gate_b, gate_b, ya, yb, yc)


def _merge_bwd(proj, gate_b, ya, yb, yc, dm, *, gate_off, name):
    t, d = ya.shape
    tr, tc = _merge_tiles(t, d)
    nc = d // tc
    g0 = gate_off // tc

    def body(l0, l1, l2, b0, b1, b2, ya_ref, yb_ref, yc_ref, dm_ref,
             dya, dyb, dyc, dl0, dl1, dl2, db0, db1, db2):
        i = pl.program_id(1)
        dmv = dm_ref[...]
        for l_ref, b_ref, y_ref, dy_ref, dl_ref, db_ref in (
                (l0, b0, ya_ref, dya, dl0, db0), (l1, b1, yb_ref, dyb, dl1, db1), (l2, b2, yc_ref, dyc, dl2, db2)):
            gt = _sigmoid(l_ref[...] + b_ref[...])
            dy_ref[...] = (dmv * gt).astype(dy_ref.dtype)
            dl = dmv * y_ref[...] * gt * (1.0 - gt)
            dl_ref[...] = dl.astype(dl_ref.dtype)
            part = jnp.sum(dl, axis=0, keepdims=True)

            @pl.when(i == 0)
            def _(db_ref=db_ref, part=part):
                db_ref[...] = part

            @pl.when(i > 0)
            def _(db_ref=db_ref, part=part):
                db_ref[...] += part

    lspec = [pl.BlockSpec((tr, tc), lambda j, i, br=br: (i, g0 + br * nc + j)) for br in range(N_BRANCH)]
    bspec = [pl.BlockSpec((1, tc), lambda j, i, br=br: (0, br * nc + j)) for br in range(N_BRANCH)]
    yspec = pl.BlockSpec((tr, tc), lambda j, i: (i, j))
    vspec = pl.BlockSpec((1, tc), lambda j, i: (0, j))
    return pl.pallas_call(
        body, grid=(nc, t // tr), in_specs=lspec + bspec + [yspec] * 4,
        out_specs=[yspec] * 6 + [vspec] * 3,
        out_shape=[SDS((t, d), BF16)] * 6 + [SDS((1, d), F32)] * 3,
        compiler_params=_cparams(), name=name)(proj, proj, proj, gate_b, gate_b, gate_b, ya, yb, yc, dm)


def _conv_fwd(proj3, dw, cb, *, c_conv, l_valid, name):
    b, lp, _ = proj3.shape
    cw = _tile(c_conv, 256, LANES)
    nc = c_conv // cw

    def body(a_ref, gte_ref, dw_ref, cb_ref, o_ref, zs):
        rows = lax.broadcasted_iota(jnp.int32, (lp, 1), 0)
        z = jnp.where(rows < l_valid, a_ref[...] * _sigmoid(gte_ref[...]), 0.0)
        zs[pl.ds(0, CONV_HALO), :] = jnp.zeros((CONV_HALO, cw), F32)
        zs[pl.ds(CONV_HALO, lp), :] = z
        zs[pl.ds(CONV_HALO + lp, CONV_HALO), :] = jnp.zeros((CONV_HALO, cw), F32)
        acc = jnp.broadcast_to(cb_ref[...], (lp, cw))
        for j in range(CONV_K):
            acc = acc + zs[pl.ds(j + 1, lp), :] * dw_ref[pl.ds(j, 1), :]
        o_ref[...] = acc

    seq = lambda off: pl.BlockSpec((None, lp, cw), lambda bi, ci: (bi, 0, off + ci))
    return pl.pallas_call(
        body, grid=(b, nc),
        in_specs=[seq(0), seq(nc), pl.BlockSpec((CONV_K, cw), lambda bi, ci: (0, ci)),
                  pl.BlockSpec((1, cw), lambda bi, ci: (0, ci))],
        out_specs=seq(0), out_shape=SDS((b, lp, c_conv), F32),
        scratch_shapes=[pltpu.VMEM((lp + 2 * CONV_HALO, cw), F32)],
        compiler_params=_cparams(), name=name)(proj3, proj3, dw, cb)


def _conv_bwd(proj3, dw, dyc3, *, c_conv, l_valid, name):
    b, lp, _ = proj3.shape
    cw = _tile(c_conv, 256, LANES)
    nc = c_conv // cw

    def body(a_ref, gte_ref, dw_ref, dy_ref, da_ref, dgte_ref, ddw_ref, dcb_ref, zs, dys):
        bi = pl.program_id(1)
        rows = lax.broadcasted_iota(jnp.int32, (lp, 1), 0)
        valid = rows < l_valid
        av = a_ref[...]
        sg = _sigmoid(gte_ref[...])
        dyv = dy_ref[...]
        zero = jnp.zeros((CONV_HALO, cw), F32)
        zs[pl.ds(0, CONV_HALO), :] = zero
        zs[pl.ds(CONV_HALO, lp), :] = jnp.where(valid, av * sg, 0.0)
        zs[pl.ds(CONV_HALO + lp, CONV_HALO), :] = zero
        dys[pl.ds(0, CONV_HALO), :] = zero
        dys[pl.ds(CONV_HALO, lp), :] = dyv
        dys[pl.ds(CONV_HALO + lp, CONV_HALO), :] = zero

        @pl.when(bi == 0)
        def _():
            ddw_ref[...] = jnp.zeros_like(ddw_ref)
            dcb_ref[...] = jnp.zeros_like(dcb_ref)

        dz = jnp.zeros((lp, cw), F32)
        for j in range(CONV_K):
            dz = dz + dys[pl.ds(CONV_K - j, lp), :] * dw_ref[pl.ds(j, 1), :]
            ddw_ref[pl.ds(j, 1), :] += jnp.sum(dyv * zs[pl.ds(j + 1, lp), :], axis=0, keepdims=True)
        dcb_ref[...] += jnp.sum(dyv, axis=0, keepdims=True)
        dz = jnp.where(valid, dz, 0.0)
        da_ref[...] = (dz * sg).astype(da_ref.dtype)
        dgte_ref[...] = (dz * av * sg * (1.0 - sg)).astype(dgte_ref.dtype)

    seq = lambda off: pl.BlockSpec((None, lp, cw), lambda ci, bi: (bi, 0, off + ci))
    return pl.pallas_call(
        body, grid=(nc, b),
        in_specs=[seq(0), seq(nc), pl.BlockSpec((CONV_K, cw), lambda ci, bi: (0, ci)), seq(0)],
        out_specs=[seq(0), seq(0), pl.BlockSpec((CONV_K, cw), lambda ci, bi: (0, ci)),
                   pl.BlockSpec((1, cw), lambda ci, bi: (0, ci))],
        out_shape=[SDS((b, lp, c_conv), BF16), SDS((b, lp, c_conv), BF16),
                   SDS((CONV_K, c_conv), F32), SDS((1, c_conv), F32)],
        scratch_shapes=[pltpu.VMEM((lp + 2 * CONV_HALO, cw), F32)] * 2,
        compiler_params=_cparams(), name=name)(proj3, proj3, dw, dyc3)


def _swap_halves(x, group):
    half = group // 2
    lane = lax.broadcasted_iota(jnp.int32, x.shape, 1)
    up = pltpu.roll(x, LANES - half, 1)
    down = pltpu.roll(x, half, 1)
    return jnp.where((lane % group) < half, up, down)


def _rope(x, cos, sin, group):
    return x * cos + _swap_halves(x, group) * sin


def _rope_bwd(dy, cos, sin, group):
    return dy * cos + _swap_halves(dy * sin, group)


def _rope_tables(lp, seq, dim, lanes):
    quarter = dim // 4
    tok = jnp.arange(seq, dtype=jnp.int32)
    zeros = jnp.zeros((N_META,), F32)
    pad = jnp.zeros((lp - N_META - seq,), F32)
    row = jnp.concatenate([zeros, (tok // GRID_W).astype(F32), pad])
    col = jnp.concatenate([zeros, (tok % GRID_W).astype(F32), pad])
    inv = ROPE_THETA ** (-jnp.arange(quarter, dtype=F32) / quarter)
    ar, ac = row[:, None] * inv[None, :], col[:, None] * inv[None, :]
    cos = jnp.concatenate([jnp.cos(ar), jnp.cos(ar), jnp.cos(ac), jnp.cos(ac)], axis=1)
    sin = jnp.concatenate([-jnp.sin(ar), jnp.sin(ar), -jnp.sin(ac), jnp.sin(ac)], axis=1)
    if lanes > dim:
        cos = jnp.concatenate([cos, jnp.ones((lp, lanes - dim), F32)], axis=1)
        sin = jnp.concatenate([sin, jnp.zeros((lp, lanes - dim), F32)], axis=1)
    return cos, sin


def _gqa_prep_fwd(proj, qg, kg, cos, sin, *, q_off, k_off, v_off, lp, name):
    t = proj.shape[0]
    qw, kw = GQA_HEADS * HEAD, GQA_KV_HEADS * HEAD
    tr = _tile(lp, max(SUBLANES_BF16, ROW_BLOCK_BYTES // (4 * qw)), SUBLANES_BF16)
    npos = lp // tr
    assert q_off % qw == 0 and k_off % kw == 0 and v_off % kw == 0

    def body(q_ref, k_ref, v_ref, qg_ref, kg_ref, cos_ref, sin_ref, qo_ref, ko_ref, vo_ref):
        cosv, sinv = cos_ref[...], sin_ref[...]
        for src, g_ref, dst, heads in ((q_ref, qg_ref, qo_ref, GQA_HEADS), (k_ref, kg_ref, ko_ref, GQA_KV_HEADS)):
            for hh in range(heads):
                xh = src[:, hh * HEAD:(hh + 1) * HEAD]
                r = lax.rsqrt(jnp.mean(xh * xh, axis=-1, keepdims=True) + NORM_EPS)
                dst[:, hh * HEAD:(hh + 1) * HEAD] = _rope(xh * r * g_ref[...], cosv, sinv, 64).astype(dst.dtype)
        vo_ref[...] = v_ref[...].astype(vo_ref.dtype)

    tab = pl.BlockSpec((tr, HEAD), lambda i: (i % npos, 0))
    vec = pl.BlockSpec((1, HEAD), lambda i: (0, 0))
    return pl.pallas_call(
        body, grid=(t // tr,),
        in_specs=[pl.BlockSpec((tr, qw), lambda i: (i, q_off // qw)),
                  pl.BlockSpec((tr, kw), lambda i: (i, k_off // kw)),
                  pl.BlockSpec((tr, kw), lambda i: (i, v_off // kw)), vec, vec, tab, tab],
        out_specs=[pl.BlockSpec((tr, qw), lambda i: (i, 0)), pl.BlockSpec((tr, kw), lambda i: (i, 0)),
                   pl.BlockSpec((tr, kw), lambda i: (i, 0))],
        out_shape=[SDS((t, qw), BF16), SDS((t, kw), BF16), SDS((t, kw), BF16)],
        compiler_params=_cparams(), name=name)(proj, proj, proj, qg, kg, cos, sin)


def _gqa_prep_bwd(proj, qg, kg, cos, sin, dqr, dkr, *, q_off, k_off, lp, name):
    t = proj.shape[0]
    qw, kw = GQA_HEADS * HEAD, GQA_KV_HEADS * HEAD
    tr = _tile(lp, max(SUBLANES_BF16, ROW_BLOCK_BYTES // (4 * qw)), SUBLANES_BF16)
    npos = lp // tr

    def body(q_ref, k_ref, qg_ref, kg_ref, cos_ref, sin_ref, dqr_ref, dkr_ref, dq_ref, dk_ref, dqg_ref, dkg_ref):
        cosv, sinv = cos_ref[...], sin_ref[...]
        for src, g_ref, dy_ref, dx_ref, dg_ref, heads in (
                (q_ref, qg_ref, dqr_ref, dq_ref, dqg_ref, GQA_HEADS),
                (k_ref, kg_ref, dkr_ref, dk_ref, dkg_ref, GQA_KV_HEADS)):
            gpart = jnp.zeros((1, HEAD), F32)
            for hh in range(heads):
                sl = slice(hh * HEAD, (hh + 1) * HEAD)
                xh = src[:, sl]
                r = lax.rsqrt(jnp.mean(xh * xh, axis=-1, keepdims=True) + NORM_EPS)
                dxn = _rope_bwd(dy_ref[:, sl], cosv, sinv, 64)
                gy = dxn * g_ref[...]
                dx = r * gy - xh * (r * r * r) * jnp.mean(gy * xh, axis=-1, keepdims=True)
                dx_ref[:, sl] = dx.astype(dx_ref.dtype)
                gpart = gpart + jnp.sum(dxn * xh * r, axis=0, keepdims=True)

            @pl.when(pl.program_id(0) == 0)
            def _(dg_ref=dg_ref, gpart=gpart):
                dg_ref[...] = gpart

            @pl.when(pl.program_id(0) > 0)
            def _(dg_ref=dg_ref, gpart=gpart):
                dg_ref[...] += gpart

    tab = pl.BlockSpec((tr, HEAD), lambda i: (i % npos, 0))
    vec = pl.BlockSpec((1, HEAD), lambda i: (0, 0))
    qrow = pl.BlockSpec((tr, qw), lambda i: (i, 0))
    krow = pl.BlockSpec((tr, kw), lambda i: (i, 0))
    return pl.pallas_call(
        body, grid=(t // tr,),
        in_specs=[pl.BlockSpec((tr, qw), lambda i: (i, q_off // qw)),
                  pl.BlockSpec((tr, kw), lambda i: (i, k_off // kw)), vec, vec, tab, tab, qrow, krow],
        out_specs=[qrow, krow, vec, vec],
        out_shape=[SDS((t, qw), BF16), SDS((t, kw), BF16), SDS((1, HEAD), F32), SDS((1, HEAD), F32)],
        compiler_params=_cparams(), name=name)(proj, proj, qg, kg, cos, sin, dqr, dkr)


def _mla_rope_fwd(q_pre, k_pre, proj, cos, sin, *, kpe_off, lp, name):
    t, w = q_pre.shape
    tr = _tile(lp, max(SUBLANES_BF16, ROW_BLOCK_BYTES // (4 * w)), SUBLANES_BF16)
    npos = lp // tr
    assert kpe_off % LANES == 0

    def body(q_ref, k_ref, kpe_ref, cos_ref, sin_ref, qo_ref, ko_ref):
        cosv, sinv = cos_ref[...], sin_ref[...]
        kr = _rope(kpe_ref[...], cosv, sinv, 32).astype(ko_ref.dtype)
        for hh in range(MLA_HEADS):
            base = hh * MLA_SLOT
            qo_ref[:, base:base + HEAD] = q_ref[:, base:base + HEAD].astype(qo_ref.dtype)
            qo_ref[:, base + HEAD:base + MLA_SLOT] = _rope(
                q_ref[:, base + HEAD:base + MLA_SLOT], cosv, sinv, 32).astype(qo_ref.dtype)
            ko_ref[:, base:base + HEAD] = k_ref[:, base:base + HEAD].astype(ko_ref.dtype)
            ko_ref[:, base + HEAD:base + MLA_SLOT] = kr

    row = pl.BlockSpec((tr, w), lambda i: (i, 0))
    tab = pl.BlockSpec((tr, LANES), lambda i: (i % npos, 0))
    return pl.pallas_call(
        body, grid=(t // tr,),
        in_specs=[row, row, pl.BlockSpec((tr, LANES), lambda i: (i, kpe_off // LANES)), tab, tab],
        out_specs=[row, row], out_shape=[SDS((t, w), BF16), SDS((t, w), BF16)],
        compiler_params=_cparams(), name=name)(q_pre, k_pre, proj, cos, sin)


def _mla_rope_bwd(dq, dk, cos, sin, *, lp, name):
    t, w = dq.shape
    tr = _tile(lp, max(SUBLANES_BF16, ROW_BLOCK_BYTES // (4 * w)), SUBLANES_BF16)
    npos = lp // tr

    def body(dq_ref, dk_ref, cos_ref, sin_ref, dqo_ref, dko_ref, dkpe_ref):
        cosv, sinv = cos_ref[...], sin_ref[...]
        pe = jnp.zeros((tr, LANES), F32)
        for hh in range(MLA_HEADS):
            base = hh * MLA_SLOT
            dqo_ref[:, base:base + HEAD] = dq_ref[:, base:base + HEAD].astype(dqo_ref.dtype)
            dqo_ref[:, base + HEAD:base + MLA_SLOT] = _rope_bwd(
                dq_ref[:, base + HEAD:base + MLA_SLOT], cosv, sinv, 32).astype(dqo_ref.dtype)
            dko_ref[:, base:base + HEAD] = dk_ref[:, base:base + HEAD].astype(dko_ref.dtype)
            dko_ref[:, base + HEAD:base + MLA_SLOT] = jnp.zeros((tr, LANES), dko_ref.dtype)
            pe = pe + dk_ref[:, base + HEAD:base + MLA_SLOT]
        dkpe_ref[...] = _rope_bwd(pe, cosv, sinv, 32).astype(dkpe_ref.dtype)

    row = pl.BlockSpec((tr, w), lambda i: (i, 0))
    tab = pl.BlockSpec((tr, LANES), lambda i: (i % npos, 0))
    return pl.pallas_call(
        body, grid=(t // tr,), in_specs=[row, row, tab, tab],
        out_specs=[row, row, pl.BlockSpec((tr, LANES), lambda i: (i, 0))],
        out_shape=[SDS((t, w), BF16), SDS((t, w), BF16), SDS((t, LANES), BF16)],
        compiler_params=_cparams(), name=name)(dq, dk, cos, sin)


def _softmax_rows(q, k, scale, lp, l_valid):
    s = lax.dot_general(q, k, (((1,), (1,)), ((), ())), preferred_element_type=F32) * scale
    cols = lax.broadcasted_iota(jnp.int32, (1, lp), 1)
    s = jnp.where(cols < l_valid, s, -1e30)
    e = jnp.exp(s - jnp.max(s, axis=-1, keepdims=True))
    return e * (1.0 / jnp.sum(e, axis=-1, keepdims=True))


def _attn_fwd(q, k, v, *, batch, lp, l_valid, heads, kv_heads, dk, dv, scale, name):
    group = heads // kv_heads
    tq = _tile(lp, 512, SUBLANES_BF16)
    nq = lp // tq

    def body(q_ref, k_ref, v_ref, o_ref):
        p = _softmax_rows(q_ref[...], k_ref[...], scale, lp, l_valid)
        o_ref[...] = jnp.dot(p.astype(BF16), v_ref[...], preferred_element_type=F32).astype(o_ref.dtype)

    return pl.pallas_call(
        body, grid=(batch, heads, nq),
        in_specs=[pl.BlockSpec((tq, dk), lambda b, h, i: (b * nq + i, h)),
                  pl.BlockSpec((lp, dk), lambda b, h, i: (b, h // group)),
                  pl.BlockSpec((lp, dv), lambda b, h, i: (b, h // group))],
        out_specs=pl.BlockSpec((tq, dv), lambda b, h, i: (b * nq + i, h)),
        out_shape=SDS((batch * lp, heads * dv), BF16), compiler_params=_cparams(), name=name)(q, k, v)


def _attn_bwd(q, k, v, do, *, batch, lp, l_valid, heads, kv_heads, dk, dv, scale, name):
    group = heads // kv_heads
    tq = _tile(lp, 512, SUBLANES_BF16)
    nq = lp // tq

    def body(q_ref, k_ref, v_ref, do_ref, dq_ref, dk_ref, dv_ref):
        first = (pl.program_id(2) == 0) & (pl.program_id(3) == 0)
        qv, kv, dov = q_ref[...], k_ref[...], do_ref[...]
        p = _softmax_rows(qv, kv, scale, lp, l_valid)
        dp = lax.dot_general(dov, v_ref[...], (((1,), (1,)), ((), ())), preferred_element_type=F32)
        ds = (p * (dp - jnp.sum(p * dp, axis=-1, keepdims=True)) * scale).astype(BF16)
        dq_ref[...] = jnp.dot(ds, kv, preferred_element_type=F32)
        dkp = lax.dot_general(ds, qv, (((0,), (0,)), ((), ())), preferred_element_type=F32)
        dvp = lax.dot_general(p.astype(BF16), dov, (((0,), (0,)), ((), ())), preferred_element_type=F32)

        @pl.when(first)
        def _():
            dk_ref[...] = dkp
            dv_ref[...] = dvp

        @pl.when(jnp.logical_not(first))
        def _():
            dk_ref[...] += dkp
            dv_ref[...] += dvp

    return pl.pallas_call(
        body, grid=(batch, kv_heads, group, nq),
        in_specs=[pl.BlockSpec((tq, dk), lambda b, hk, g, i: (b * nq + i, hk * group + g)),
                  pl.BlockSpec((lp, dk), lambda b, hk, g, i: (b, hk)),
                  pl.BlockSpec((lp, dv), lambda b, hk, g, i: (b, hk)),
                  pl.BlockSpec((tq, dv), lambda b, hk, g, i: (b * nq + i, hk * group + g))],
        out_specs=[pl.BlockSpec((tq, dk), lambda b, hk, g, i: (b * nq + i, hk * group + g)),
                   pl.BlockSpec((lp, dk), lambda b, hk, g, i: (b, hk)),
                   pl.BlockSpec((lp, dv), lambda b, hk, g, i: (b, hk))],
        out_shape=[SDS((batch * lp, heads * dk), F32), SDS((batch * lp, kv_heads * dk), F32),
                   SDS((batch * lp, kv_heads * dv), F32)],
        compiler_params=_cparams(), name=name)(q, k, v, do)


def _meta_grad(dh3, *, name):
    b, _, d = dh3.shape

    def body(x_ref, o_ref):
        @pl.when(pl.program_id(0) == 0)
        def _():
            o_ref[...] = x_ref[...]

        @pl.when(pl.program_id(0) > 0)
        def _():
            o_ref[...] += x_ref[...]

    return pl.pallas_call(
        body, grid=(b,), in_specs=[pl.BlockSpec((None, N_META, d), lambda bi: (bi, 0, 0))],
        out_specs=pl.BlockSpec((N_META, d), lambda bi: (0, 0)), out_shape=SDS((N_META, d), F32),
        compiler_params=_cparams(), name=name)(dh3)


def _sum_slots(parts, *, out_dtype, name):
    _, r, c = parts[0].shape
    tr = _row_tile(r, c)
    counts = [p.shape[0] for p in parts]

    def body(*refs):
        o_ref = refs[-1]
        acc = None
        for ref, cnt in zip(refs[:-1], counts):
            for s in range(cnt):
                term = ref[s].astype(F32)
                acc = term if acc is None else acc + term
        o_ref[...] = acc.astype(o_ref.dtype)

    return pl.pallas_call(
        body, grid=(r // tr,),
        in_specs=[pl.BlockSpec((cnt, tr, c), lambda i: (0, i, 0)) for cnt in counts],
        out_specs=pl.BlockSpec((tr, c), lambda i: (i, 0)), out_shape=SDS((r, c), out_dtype),
        compiler_params=_cparams(), name=name)(*parts)


def _adamw(w, g, m, v, *, name):
    shape = w.shape
    cols = shape[-1]
    rows = w.size // cols
    tr = _tile(rows, max(8, ROW_BLOCK_BYTES // (4 * cols)), 8)
    bc1 = 1.0 - ADAM_B1 ** ADAM_STEP
    bc2 = 1.0 - ADAM_B2 ** ADAM_STEP

    def body(w_ref, g_ref, m_ref, v_ref, d_ref, mo_ref, vo_ref):
        gv = g_ref[...]
        mn = ADAM_B1 * m_ref[...] + (1.0 - ADAM_B1) * gv
        vn = ADAM_B2 * v_ref[...] + (1.0 - ADAM_B2) * (gv * gv)
        mo_ref[...] = mn
        vo_ref[...] = vn
        d_ref[...] = -ADAM_LR * ((mn / bc1) / (jnp.sqrt(vn / bc2) + ADAM_EPS) + ADAM_WD * w_ref[...])

    spec = pl.BlockSpec((tr, cols), lambda i: (i, 0))
    outs = pl.pallas_call(
        body, grid=(rows // tr,), in_specs=[spec] * 4, out_specs=[spec] * 3,
        out_shape=[SDS((rows, cols), F32)] * 3, compiler_params=_cparams(), name=name)(
            *[a.reshape(rows, cols) for a in (w, g, m, v)])
    return tuple(o.reshape(shape) for o in outs)


def _mesh_pos():
    return lax.axis_index("x"), lax.axis_index("y"), lax.axis_index("c")


def _other_chips(x, y):
    return [(1 - x, y), (x, 1 - y), (1 - x, 1 - y)]


def _chip_allgather_small(vec, *, name):
    r, c = vec.shape

    def body(x_ref, out_ref, send_sems, recv_sems):
        x, y, cc = _mesh_pos()
        mine = 2 * x + y
        out_ref[mine] = x_ref[...]
        chips = _other_chips(x, y)
        sends = [pltpu.make_async_remote_copy(
            src_ref=x_ref, dst_ref=out_ref.at[mine], send_sem=send_sems.at[j], recv_sem=recv_sems.at[j],
            device_id=(px, py, cc), device_id_type=MESH) for j, (px, py) in enumerate(chips)]
        for cp in sends:
            cp.start()
        for j, (px, py) in enumerate(chips):
            pltpu.make_async_remote_copy(
                src_ref=x_ref, dst_ref=out_ref.at[2 * px + py], send_sem=send_sems.at[j],
                recv_sem=recv_sems.at[j], device_id=(px, py, cc), device_id_type=MESH).wait_recv()
        for cp in sends:
            cp.wait_send()

    return pl.pallas_call(
        body, out_shape=SDS((N_CHIPS, r, c), F32),
        in_specs=[pl.BlockSpec(memory_space=pltpu.VMEM)], out_specs=pl.BlockSpec(memory_space=pltpu.VMEM),
        scratch_shapes=[pltpu.SemaphoreType.DMA((3,)), pltpu.SemaphoreType.DMA((3,))],
        compiler_params=_cparams(), name=name)(vec)


def _allreduce_small(vec, *, name):
    r, c = vec.shape

    def body(x_ref, out_ref, buf, send_sems, recv_sems):
        x, y, cc = _mesh_pos()
        me = 4 * x + 2 * y + cc
        buf[me] = x_ref[...]

        def peer(kk):
            fx, fy, fc = (kk >> 2) & 1, (kk >> 1) & 1, kk & 1
            px = x if fx == 0 else 1 - x
            py = y if fy == 0 else 1 - y
            pc = cc if fc == 0 else 1 - cc
            return (px, py, pc), 4 * px + 2 * py + pc

        sends = []
        for kk in range(1, N_DEV):
            dev, _ = peer(kk)
            sends.append(pltpu.make_async_remote_copy(
                src_ref=x_ref, dst_ref=buf.at[me], send_sem=send_sems.at[kk - 1], recv_sem=recv_sems.at[kk - 1],
                device_id=dev, device_id_type=MESH))
        for cp in sends:
            cp.start()
        for kk in range(1, N_DEV):
            dev, slot = peer(kk)
            pltpu.make_async_remote_copy(
                src_ref=x_ref, dst_ref=buf.at[slot], send_sem=send_sems.at[kk - 1], recv_sem=recv_sems.at[kk - 1],
                device_id=dev, device_id_type=MESH).wait_recv()
        for cp in sends:
            cp.wait_send()
        acc = buf[0]
        for s in range(1, N_DEV):
            acc = acc + buf[s]
        out_ref[...] = acc

    return pl.pallas_call(
        body, out_shape=SDS((r, c), F32),
        in_specs=[pl.BlockSpec(memory_space=pltpu.VMEM)], out_specs=pl.BlockSpec(memory_space=pltpu.VMEM),
        scratch_shapes=[pltpu.VMEM((N_DEV, r, c), F32), pltpu.SemaphoreType.DMA((N_DEV - 1,)),
                        pltpu.SemaphoreType.DMA((N_DEV - 1,))],
        compiler_params=_cparams(), name=name)(vec)


_HBM = pl.BlockSpec(memory_space=pltpu.HBM)


def _half(rows, half):
    hk = rows // 2
    assert hk % SUBLANES_BF16 == 0
    return pl.ds(pl.multiple_of(half * hk, SUBLANES_BF16), hk)


_SEM = pl.BlockSpec(memory_space=pltpu.SEMAPHORE)
_ANY = pl.BlockSpec(memory_space=pl.ANY)
_DATAFLOW = pltpu.SideEffectType.DATAFLOW_SIDE_EFFECTING


def _in_hbm(a):
    return pltpu.with_memory_space_constraint(a, pltpu.HBM)


def _split_start(srcs, lands, prev, make_copy, *, name):
    nw = len(srcs)

    def body(*refs):
        src_refs, land_refs = refs[:nw], refs[nw:2 * nw]
        send_sems, recv_sems = refs[2 * nw + 1], refs[2 * nw + 2]
        token = refs[-1]
        for j in range(3):
            for i in range(nw):
                make_copy(src_refs, land_refs, send_sems, recv_sems, j, i, True).start()
        token[...] = jnp.zeros_like(token)

    outs = pl.pallas_call(
        body, name=name,
        out_shape=(pltpu.SemaphoreType.DMA((3 * nw,)), pltpu.SemaphoreType.DMA((3 * nw,)),
                   *[pltpu.HBM(a.shape, a.dtype) for a in srcs], *[pltpu.HBM(a.shape, a.dtype) for a in lands],
                   SDS((8, LANES), F32)),
        in_specs=[_HBM] * (2 * nw) + [_ANY],
        out_specs=(_SEM, _SEM, *[_HBM] * (2 * nw), pl.BlockSpec(memory_space=pltpu.VMEM)),
        input_output_aliases={i: 2 + i for i in range(2 * nw)},
        compiler_params=pltpu.CompilerParams(has_side_effects=_DATAFLOW),
    )(*[_in_hbm(a) for a in srcs], *[_in_hbm(a) for a in lands], prev)
    return outs[0], outs[1], list(outs[2:2 + nw]), list(outs[2 + nw:2 + 2 * nw]), outs[-1]


def _split_wait(pending, after, make_copy, *, name):
    send_sems, recv_sems, srcs, lands, _ = pending
    nw = len(srcs)

    def body(*refs):
        src_refs, land_refs = refs[:nw], refs[nw:2 * nw]
        send_refs, recv_refs = refs[2 * nw], refs[2 * nw + 1]
        for j in range(3):
            for i in range(nw):
                cp = make_copy(src_refs, land_refs, send_refs, recv_refs, j, i, False)
                cp.wait_send()
                cp.wait_recv()

    outs = pl.pallas_call(
        body, name=name,
        out_shape=(*[pltpu.HBM(a.shape, a.dtype) for a in srcs], *[pltpu.HBM(a.shape, a.dtype) for a in lands]),
        in_specs=[_HBM] * (2 * nw) + [_SEM, _SEM, _ANY], out_specs=tuple([_HBM] * (2 * nw)),
        input_output_aliases={i: i for i in range(2 * nw)},
        compiler_params=pltpu.CompilerParams(has_side_effects=_DATAFLOW),
    )(*srcs, *lands, send_sems, recv_sems, after)
    return list(outs[:nw]), list(outs[nw:])


def _gather_copy(shapes):
    def make(src_refs, land_refs, send_sems, recv_sems, j, i, outgoing):
        x, y, cc = _mesh_pos()
        px, py = _other_chips(x, y)[j]
        rows = _half(shapes[i][0], cc)
        slot = 2 * x + y if outgoing else 2 * px + py
        return pltpu.make_async_remote_copy(
            src_ref=src_refs[i].at[rows], dst_ref=land_refs[i].at[slot, rows],
            send_sem=send_sems.at[j * len(shapes) + i], recv_sem=recv_sems.at[j * len(shapes) + i],
            device_id=(px, py, cc), device_id_type=MESH)
    return make


def _gather_forward(lands, *, name):
    nw = len(lands)

    def body(*refs):
        in_refs, out_refs = refs[:nw], refs[nw:2 * nw]
        send_sems, recv_sems = refs[2 * nw:]
        x, y, cc = _mesh_pos()
        cps = []
        for j, (px, py) in enumerate(_other_chips(x, y)):
            for i in range(nw):
                rows = _half(lands[i].shape[1], cc)
                cps.append(pltpu.make_async_remote_copy(
                    src_ref=in_refs[i].at[2 * px + py, rows], dst_ref=out_refs[i].at[2 * px + py, rows],
                    send_sem=send_sems.at[j, i], recv_sem=recv_sems.at[j, i],
                    device_id=(x, y, 1 - cc), device_id_type=MESH))
        for cp in cps:
            cp.start()
        for cp in cps:
            cp.wait()

    return pl.pallas_call(
        body, out_shape=[SDS(a.shape, a.dtype) for a in lands], in_specs=[_HBM] * nw, out_specs=[_HBM] * nw,
        input_output_aliases={i: i for i in range(nw)},
        scratch_shapes=[pltpu.SemaphoreType.DMA((3, nw)), pltpu.SemaphoreType.DMA((3, nw))], name=name)(*lands)


def _grads_to_sibling(gs, *, name):
    nw = len(gs)

    def body(*refs):
        g_refs, out_refs = refs[:nw], refs[nw:2 * nw]
        send_sems, recv_sems = refs[2 * nw:]
        x, y, cc = _mesh_pos()
        cps = [pltpu.make_async_remote_copy(
            src_ref=g_refs[i].at[pl.ds(0, N_CHIPS), _half(gs[i].shape[1], 1 - cc)], dst_ref=out_refs[i],
            send_sem=send_sems.at[i], recv_sem=recv_sems.at[i], device_id=(x, y, 1 - cc), device_id_type=MESH)
            for i in range(nw)]
        for cp in cps:
            cp.start()
        for cp in cps:
            cp.wait()

    return pl.pallas_call(
        body, out_shape=[SDS((N_CHIPS, g.shape[1] // 2, g.shape[2]), g.dtype) for g in gs],
        in_specs=[_HBM] * nw, out_specs=[_HBM] * nw,
        scratch_shapes=[pltpu.SemaphoreType.DMA((nw,)), pltpu.SemaphoreType.DMA((nw,))], name=name)(*gs)


def _chips_copy(nw):
    def make(src_refs, land_refs, send_sems, recv_sems, j, i, outgoing):
        x, y, cc = _mesh_pos()
        px, py = _other_chips(x, y)[j]
        return pltpu.make_async_remote_copy(
            src_ref=src_refs[i].at[2 * px + py], dst_ref=land_refs[i].at[j],
            send_sem=send_sems.at[j * nw + i], recv_sem=recv_sems.at[j * nw + i],
            device_id=(px, py, cc), device_id_type=MESH)
    return make


def _share_halves(tots, *, name):
    nw = len(tots)

    def body(*refs):
        t_refs, out_refs = refs[:nw], refs[nw:2 * nw]
        send_sems, recv_sems = refs[2 * nw:]
        x, y, cc = _mesh_pos()
        cps = [pltpu.make_async_remote_copy(
            src_ref=t_refs[i], dst_ref=out_refs[i], send_sem=send_sems.at[i], recv_sem=recv_sems.at[i],
            device_id=(x, y, 1 - cc), device_id_type=MESH) for i in range(nw)]
        for cp in cps:
            cp.start()
        for cp in cps:
            cp.wait()

    return pl.pallas_call(
        body, out_shape=[SDS(tt.shape, tt.dtype) for tt in tots],
        in_specs=[_HBM] * nw, out_specs=[_HBM] * nw,
        scratch_shapes=[pltpu.SemaphoreType.DMA((nw,)), pltpu.SemaphoreType.DMA((nw,))], name=name)(*tots)


def _pack_small(arrays):
    flat = jnp.concatenate([a.astype(F32).reshape(-1) for a in arrays])
    n = flat.shape[0]
    padded = -(-n // (8 * LANES)) * (8 * LANES)
    return jnp.pad(flat, (0, padded - n)).reshape(padded // LANES, LANES)


def _unpack_small(flat, shapes):
    out, off = [], 0
    for shp in shapes:
        n = math.prod(shp)
        out.append(flat[off:off + n].reshape(shp))
        off += n
    return out


BIG = (("w_in", 1), ("w_conv_out", 1), ("w_gqa_out", 1), ("w_mla_uq", 1), ("w_mla_ukv", 1), ("w_mla_out", 1),
       ("w_out", 0), ("w_ffn_gate", 1), ("w_ffn_up", 1), ("w_ffn_down", 0))
BIG_NAMES = tuple(nm for nm, _ in BIG)


def _cols_to_slots(full):
    k, n = full.shape
    return full.reshape(k, N_CHIPS, n // N_CHIPS).transpose(1, 0, 2)


def _slots_to_cols(slots):
    s, k, n = slots.shape
    return slots.transpose(1, 0, 2).reshape(k, s * n)


def _step(p, mom_m, mom_v, x, loss_target):
    b, seq, d = x.shape
    depth = p["w_in"].shape[0]
    l_valid = N_META + seq
    lp = -(-l_valid // LANES) * LANES
    t = b * lp
    c_conv = d // 2
    q_rank = p["mla_q_norm_g"].shape[1]
    kv_rank = p["mla_kv_norm_g"].shape[1]
    qw, kw = GQA_HEADS * HEAD, GQA_KV_HEADS * HEAD
    off_q = 2 * c_conv
    off_k = off_q + qw
    off_v = off_k + kw
    off_cq = off_v + kw
    off_ckv = off_cq + q_rank
    off_gate = off_ckv + kv_rank
    off_kpe = off_gate + N_BRANCH * d
    d_inp = off_kpe + LANES
    assert off_cq % q_rank == 0 and off_ckv % kv_rank == 0
    xi, yi, ci = _mesh_pos()
    chip = 2 * xi + yi

    small_sh = [p["meta_tokens"], p["conv_dw"]]
    got = _chip_allgather_small(_pack_small(small_sh), name="gather_small")
    got = [_unpack_small(got[s].reshape(-1), [a.shape for a in small_sh]) for s in range(N_CHIPS)]
    meta_full = jnp.concatenate([g[0] for g in got], axis=1)
    conv_dw_full = jnp.concatenate([g[1] for g in got], axis=-1).reshape(depth, CONV_K, c_conv)

    gather_copy = _gather_copy([p[nm].shape[1:] for nm in BIG_NAMES])
    gathers = []
    order = jnp.zeros((8, LANES), F32)
    for li in range(depth):
        own = [p[nm][li].astype(BF16) for nm in BIG_NAMES]
        lands = [lax.empty((N_CHIPS,) + o.shape, BF16) for o in own]
        gathers.append(_split_start(own, lands, order, gather_copy, name="gather_start_%d" % li))
        order = gathers[-1][4]
    meta_full = meta_full + order[0, 0]

    def layer_weights(li, after):
        own, lands = _split_wait(gathers[li], after, gather_copy, name="gather_wait_%d" % li)
        got = _gather_forward(lands, name="gather_forward")
        full = {nm: lax.dynamic_update_slice(g, o[None], (chip, 0, 0)) for nm, g, o in zip(BIG_NAMES, got, own)}
        w_in = _slots_to_cols(full["w_in"])
        full["w_in"] = jnp.concatenate(
            [w_in[:, :off_gate], w_in[:, off_gate + MLA_ROPE:], w_in[:, off_gate:off_gate + MLA_ROPE],
             jnp.zeros((d, LANES - MLA_ROPE), BF16)], axis=1)
        full["w_out"] = full["w_out"].reshape(-1, d)
        full["w_ffn_down"] = full["w_ffn_down"].reshape(-1, d)
        uq = _slots_to_cols(full["w_mla_uq"]).reshape(q_rank, MLA_HEADS, HEAD + MLA_ROPE)
        full["w_uq"] = jnp.pad(uq, ((0, 0), (0, 0), (0, MLA_SLOT - HEAD - MLA_ROPE))).reshape(
            q_rank, MLA_HEADS * MLA_SLOT)
        ukv = _slots_to_cols(full["w_mla_ukv"]).reshape(kv_rank, MLA_HEADS, 2 * HEAD)
        full["w_uk"] = jnp.pad(ukv[:, :, :HEAD], ((0, 0), (0, 0), (0, MLA_SLOT - HEAD))).reshape(
            kv_rank, MLA_HEADS * MLA_SLOT)
        full["w_uv"] = ukv[:, :, HEAD:].reshape(kv_rank, MLA_HEADS * HEAD)
        return full

    cos_g, sin_g = _rope_tables(lp, seq, HEAD, HEAD)
    cos_m, sin_m = _rope_tables(lp, seq, MLA_ROPE, LANES)
    gqa_scale = 1.0 / math.sqrt(HEAD)
    mla_scale = 1.0 / math.sqrt(HEAD + MLA_ROPE)
    attn_kw = dict(batch=b, lp=lp, l_valid=l_valid)
    gqa_kw = dict(heads=GQA_HEADS, kv_heads=GQA_KV_HEADS, dk=HEAD, dv=HEAD, scale=gqa_scale, **attn_kw)
    mla_kw = dict(heads=MLA_HEADS, kv_heads=MLA_HEADS, dk=MLA_SLOT, dv=HEAD, scale=mla_scale, **attn_kw)

    h = jnp.concatenate([jnp.broadcast_to(meta_full[None], (b, N_META, d)), x,
                         jnp.zeros((b, lp - l_valid, d), F32)], axis=1).reshape(t, d)
    saved, weights = [], []
    for li in range(depth):
        w = layer_weights(li, h)
        weights.append(w)
        row = lambda nm: p[nm][li][None, :]
        s = {"h_in": h}
        s["u"] = _rmsnorm_fwd(h, row("mix_norm_g"), width=d, cblk=0, name="mix_norm")
        proj = _mm(s["u"], w["w_in"], name="in_proj")
        s["proj"] = proj
        proj3 = proj.reshape(b, lp, d_inp)
        s["yconv"] = _conv_fwd(proj3, conv_dw_full[li], row("conv_b"), c_conv=c_conv, l_valid=l_valid,
                               name="conv").reshape(t, c_conv)
        s["sconv"] = _ln_silu_fwd(s["yconv"], row("conv_ln_g"), row("conv_ln_b"), name="conv_ln_silu")
        s["ya"] = _mm(s["sconv"], w["w_conv_out"], b_sharded=True, name="conv_out")
        s["gq"], s["gk"], s["gv"] = _gqa_prep_fwd(
            proj, row("gqa_q_norm_g"), row("gqa_k_norm_g"), cos_g, sin_g,
            q_off=off_q, k_off=off_k, v_off=off_v, lp=lp, name="gqa_prep")
        s["go"] = _attn_fwd(s["gq"], s["gk"], s["gv"], name="gqa_attn", **gqa_kw)
        s["yb"] = _mm(s["go"], w["w_gqa_out"], b_sharded=True, name="gqa_out")
        s["cqn"] = _rmsnorm_fwd(proj, row("mla_q_norm_g"), width=q_rank, cblk=off_cq // q_rank, name="mla_q_norm")
        s["kvn"] = _rmsnorm_fwd(proj, row("mla_kv_norm_g"), width=kv_rank, cblk=off_ckv // kv_rank,
                                name="mla_kv_norm")
        q_pre = _mm(s["cqn"], w["w_uq"], name="mla_uq")
        k_pre = _mm(s["kvn"], w["w_uk"], name="mla_uk")
        s["mv"] = _mm(s["kvn"], w["w_uv"], out_dtype=BF16, name="mla_uv")
        s["mq"], s["mk"] = _mla_rope_fwd(q_pre, k_pre, proj, cos_m, sin_m, kpe_off=off_kpe, lp=lp, name="mla_rope")
        s["mo"] = _attn_fwd(s["mq"], s["mk"], s["mv"], name="mla_attn", **mla_kw)
        s["yc"] = _mm(s["mo"], w["w_mla_out"], b_sharded=True, name="mla_out")
        s["merged"] = _merge_fwd(proj, row("gate_b"), s["ya"], s["yb"], s["yc"], gate_off=off_gate, name="merge")
        h = _mm(s["merged"], w["w_out"], add=h, name="mix_out")
        s["h_mid"] = h
        s["v"] = _rmsnorm_fwd(h, row("ffn_norm_g"), width=d, cblk=0, name="ffn_norm")
        s["gate"] = _mm(s["v"], w["w_ffn_gate"], b_sharded=True, name="ffn_gate")
        s["up"] = _mm(s["v"], w["w_ffn_up"], b_sharded=True, name="ffn_up")
        s["act"] = _swiglu_fwd(s["gate"], s["up"], name="swiglu")
        h = _mm(s["act"], w["w_ffn_down"], add=h, name="ffn_down")
        saved.append(s)

    target = jnp.pad(loss_target, ((0, 0), (N_META, lp - l_valid), (0, 0))).reshape(t, d)
    loss_part, dh, dhb, g_final = _loss_head(h, p["final_norm_g"][None, :], target, lp=lp, l_valid=l_valid,
                                             name="loss_head")
    loss = lax.psum(loss_part[0, 0], ("x", "y", "c"))

    small_names = ["mix_norm_g", "conv_dw", "conv_b", "conv_ln_g", "conv_ln_b", "gqa_q_norm_g", "gqa_k_norm_g",
                   "mla_q_norm_g", "mla_kv_norm_g", "gate_b", "ffn_norm_g"]
    small_g = {nm: [None] * depth for nm in small_names}
    shard_g = {nm: [None] * depth for nm in BIG_NAMES}
    chips_copy = _chips_copy(len(BIG_NAMES))

    def finish_reduce(item, after):
        lj, pending = item
        parts, from_chips = _split_wait(pending, after, chips_copy, name="chips_wait_%d" % lj)
        tots = [_sum_slots([lax.dynamic_index_in_dim(pt, chip, 0, keepdims=True), fc], out_dtype=F32,
                           name="sum_chips_" + nm) for nm, pt, fc in zip(BIG_NAMES, parts, from_chips)]
        others = _share_halves(tots, name="share_halves")
        for nm, tot, oth in zip(BIG_NAMES, tots, others):
            shard_g[nm][lj] = jnp.concatenate([jnp.where(ci == 0, tot, oth), jnp.where(ci == 0, oth, tot)], axis=0)

    in_flight = None
    for li in reversed(range(depth)):
        w, s = weights[li], saved[li]
        row = lambda nm: p[nm][li][None, :]
        big = {}
        if in_flight is not None:
            dhb = dhb + in_flight[1][4][0, 0].astype(BF16)
        dact = _mm(dhb, w["w_ffn_down"], mode="nt", name="d_ffn_act")
        big["w_ffn_down"] = _mm(s["act"], dhb, mode="tn", out_dtype=BF16, name="dw_ffn_down").reshape(N_CHIPS, -1, d)
        dgate, dup = _swiglu_bwd(s["gate"], s["up"], dact, name="d_swiglu")
        dv = _mm(dgate, w["w_ffn_gate"], mode="nt", b_sharded=True, name="d_ffn_v_gate")
        dv = _mm(dup, w["w_ffn_up"], mode="nt", b_sharded=True, add=dv, name="d_ffn_v_up")
        big["w_ffn_gate"] = _mm(s["v"], dgate, mode="tn", out_dtype=BF16, out_shards=N_CHIPS, name="dw_ffn_gate")
        big["w_ffn_up"] = _mm(s["v"], dup, mode="tn", out_dtype=BF16, out_shards=N_CHIPS, name="dw_ffn_up")
        dh, dhb, g = _rmsnorm_bwd(s["h_mid"], row("ffn_norm_g"), dv, width=d, cblk=0, res=dh, name="d_ffn_norm")
        small_g["ffn_norm_g"][li] = g[0]
        dm = _mm(dhb, w["w_out"], mode="nt", name="d_merged")
        big["w_out"] = _mm(s["merged"], dhb, mode="tn", out_dtype=BF16, name="dw_out").reshape(N_CHIPS, -1, d)
        dya, dyb, dyc, dl0, dl1, dl2, db0, db1, db2 = _merge_bwd(
            s["proj"], row("gate_b"), s["ya"], s["yb"], s["yc"], dm, gate_off=off_gate, name="d_merge")
        small_g["gate_b"][li] = jnp.concatenate([db0[0], db1[0], db2[0]])
        ds = _mm(dya, w["w_conv_out"], mode="nt", b_sharded=True, name="d_conv_s")
        big["w_conv_out"] = _mm(s["sconv"], dya, mode="tn", out_dtype=BF16, out_shards=N_CHIPS, name="dw_conv_out")
        dyconv, g, gb = _ln_silu_bwd(s["yconv"], row("conv_ln_g"), row("conv_ln_b"), ds, name="d_conv_ln_silu")
        small_g["conv_ln_g"][li], small_g["conv_ln_b"][li] = g[0], gb[0]
        proj3 = s["proj"].reshape(b, lp, d_inp)
        da, dgte, ddw, dcb = _conv_bwd(proj3, conv_dw_full[li], dyconv.reshape(b, lp, c_conv), c_conv=c_conv,
                                       l_valid=l_valid, name="d_conv")
        small_g["conv_dw"][li], small_g["conv_b"][li] = ddw, dcb[0]
        dgo = _mm(dyb, w["w_gqa_out"], mode="nt", b_sharded=True, out_dtype=BF16, name="d_gqa_o")
        big["w_gqa_out"] = _mm(s["go"], dyb, mode="tn", out_dtype=BF16, out_shards=N_CHIPS, name="dw_gqa_out")
        dgq, dgk, dgv = _attn_bwd(s["gq"], s["gk"], s["gv"], dgo, name="d_gqa_attn", **gqa_kw)
        dq_g, dk_g, gq, gk = _gqa_prep_bwd(s["proj"], row("gqa_q_norm_g"), row("gqa_k_norm_g"), cos_g, sin_g,
                                           dgq, dgk, q_off=off_q, k_off=off_k, lp=lp, name="d_gqa_prep")
        small_g["gqa_q_norm_g"][li], small_g["gqa_k_norm_g"][li] = gq[0], gk[0]
        dmo = _mm(dyc, w["w_mla_out"], mode="nt", b_sharded=True, out_dtype=BF16, name="d_mla_o")
        big["w_mla_out"] = _mm(s["mo"], dyc, mode="tn", out_dtype=BF16, out_shards=N_CHIPS, name="dw_mla_out")
        dmq, dmk, dmv = _attn_bwd(s["mq"], s["mk"], s["mv"], dmo, name="d_mla_attn", **mla_kw)
        dq_pre, dk_pre, dkpe = _mla_rope_bwd(dmq, dmk, cos_m, sin_m, lp=lp, name="d_mla_rope")
        dmvb = dmv.astype(BF16)
        g_uq = _mm(s["cqn"], dq_pre, mode="tn", out_dtype=BF16, name="dw_mla_uq")
        g_uk = _mm(s["kvn"], dk_pre, mode="tn", out_dtype=BF16, name="dw_mla_uk")
        g_uv = _mm(s["kvn"], dmvb, mode="tn", out_dtype=BF16, name="dw_mla_uv")
        big["w_mla_uq"] = _cols_to_slots(
            g_uq.reshape(q_rank, MLA_HEADS, MLA_SLOT)[:, :, :HEAD + MLA_ROPE].reshape(q_rank, -1))
        big["w_mla_ukv"] = _cols_to_slots(jnp.concatenate(
            [g_uk.reshape(kv_rank, MLA_HEADS, MLA_SLOT)[:, :, :HEAD], g_uv.reshape(kv_rank, MLA_HEADS, HEAD)],
            axis=-1).reshape(kv_rank, -1))
        dcqn = _mm(dq_pre, w["w_uq"], mode="nt", name="d_mla_cqn")
        dkvn = _mm(dk_pre, w["w_uk"], mode="nt", name="d_mla_kvn_k")
        dkvn = _mm(dmvb, w["w_uv"], mode="nt", add=dkvn, name="d_mla_kvn_v")
        _, dcq, g = _rmsnorm_bwd(s["proj"], row("mla_q_norm_g"), dcqn, width=q_rank, cblk=off_cq // q_rank,
                                 name="d_mla_q_norm")
        small_g["mla_q_norm_g"][li] = g[0]
        _, dckv, g = _rmsnorm_bwd(s["proj"], row("mla_kv_norm_g"), dkvn, width=kv_rank, cblk=off_ckv // kv_rank,
                                  name="d_mla_kv_norm")
        small_g["mla_kv_norm_g"][li] = g[0]
        dproj = jnp.concatenate(
            [da.reshape(t, c_conv), dgte.reshape(t, c_conv), dq_g, dk_g, dgv.astype(BF16), dcq, dckv,
             dl0, dl1, dl2, dkpe], axis=1)
        du = _mm(dproj, w["w_in"], mode="nt", name="d_u")
        g_in = _mm(s["u"], dproj, mode="tn", out_dtype=BF16, name="dw_in")
        big["w_in"] = _cols_to_slots(jnp.concatenate(
            [g_in[:, :off_gate], g_in[:, off_kpe:off_kpe + MLA_ROPE], g_in[:, off_gate:off_kpe]], axis=1))
        dh, dhb, g = _rmsnorm_bwd(s["h_in"], row("mix_norm_g"), du, width=d, cblk=0, res=dh, name="d_mix_norm")
        small_g["mix_norm_g"][li] = g[0]
        if in_flight is not None:
            finish_reduce(in_flight, dh)
        glist = [big[nm] for nm in BIG_NAMES]
        from_sib = _grads_to_sibling(glist, name="grads_to_sibling")
        parts = []
        for nm, g, fs in zip(BIG_NAMES, glist, from_sib):
            hk = g.shape[1] // 2
            mine = lax.dynamic_slice_in_dim(g, ci * hk, hk, axis=1)
            rows = (1, N_CHIPS * hk, g.shape[2])
            parts.append(_sum_slots([mine.reshape(rows), fs.reshape(rows)], out_dtype=BF16,
                                    name="sum_sibling_" + nm).reshape(fs.shape))
        lands = [lax.empty((3,) + pt.shape[1:], BF16) for pt in parts]
        in_flight = (li, _split_start(parts, lands, order, chips_copy, name="chips_start_%d" % li))

    dh3 = dh.reshape(b, lp, d)
    grad_x = dh3[:, N_META:l_valid]
    g_meta = _meta_grad(dh3, name="meta_grad")
    finish_reduce(in_flight, g_meta)

    small_list = [g_meta, g_final[0]] + [jnp.stack(small_g[nm]) for nm in small_names]
    red = _allreduce_small(_pack_small(small_list), name="allreduce_small")
    red = _unpack_small(red.reshape(-1), [a.shape for a in small_list])
    grads = {"meta_tokens": lax.dynamic_slice_in_dim(red[0], chip * (d // N_CHIPS), d // N_CHIPS, axis=1),
             "final_norm_g": red[1]}
    for nm, val in zip(small_names, red[2:]):
        grads[nm] = val
    cs = c_conv // N_CHIPS
    grads["conv_dw"] = lax.dynamic_slice_in_dim(grads["conv_dw"], chip * cs, cs, axis=2).reshape(p["conv_dw"].shape)
    for nm in BIG_NAMES:
        grads[nm] = jnp.stack(shard_g[nm])

    delta, new_m, new_v = {}, {}, {}
    for nm in p:
        delta[nm], new_m[nm], new_v[nm] = _adamw(p[nm], grads[nm], mom_m[nm], mom_v[nm], name="adamw_" + nm)
    return loss, grad_x, grads, delta, new_m, new_v


WEIGHT_NAMES = ("meta_tokens", "mix_norm_g", "w_in", "conv_dw", "conv_b", "conv_ln_g", "conv_ln_b", "w_conv_out",
                "gqa_q_norm_g", "gqa_k_norm_g", "w_gqa_out", "mla_q_norm_g", "w_mla_uq", "mla_kv_norm_g",
                "w_mla_ukv", "w_mla_out", "gate_b", "w_out", "ffn_norm_g", "w_ffn_gate", "w_ffn_up", "w_ffn_down",
                "final_norm_g")


def kernel(x, meta_tokens, mix_norm_g, w_in, conv_dw, conv_b, conv_ln_g, conv_ln_b, w_conv_out, gqa_q_norm_g, gqa_k_norm_g, w_gqa_out, mla_q_norm_g, w_mla_uq, mla_kv_norm_g, w_mla_ukv, w_mla_out, gate_b, w_out, ffn_norm_g, w_ffn_gate, w_ffn_up, w_ffn_down, final_norm_g, loss_target, m_meta_tokens, m_mix_norm_g, m_w_in, m_conv_dw, m_conv_b, m_conv_ln_g, m_conv_ln_b, m_w_conv_out, m_gqa_q_norm_g, m_gqa_k_norm_g, m_w_gqa_out, m_mla_q_norm_g, m_w_mla_uq, m_mla_kv_norm_g, m_w_mla_ukv, m_w_mla_out, m_gate_b, m_w_out, m_ffn_norm_g, m_w_ffn_gate, m_w_ffn_up, m_w_ffn_down, m_final_norm_g, v_meta_tokens, v_mix_norm_g, v_w_in, v_conv_dw, v_conv_b, v_conv_ln_g, v_conv_ln_b, v_w_conv_out, v_gqa_q_norm_g, v_gqa_k_norm_g, v_w_gqa_out, v_mla_q_norm_g, v_w_mla_uq, v_mla_kv_norm_g, v_w_mla_ukv, v_w_mla_out, v_gate_b, v_w_out, v_ffn_norm_g, v_w_ffn_gate, v_w_ffn_up, v_w_ffn_down, v_final_norm_g):
    ws = (meta_tokens, mix_norm_g, w_in, conv_dw, conv_b, conv_ln_g, conv_ln_b, w_conv_out, gqa_q_norm_g,
          gqa_k_norm_g, w_gqa_out, mla_q_norm_g, w_mla_uq, mla_kv_norm_g, w_mla_ukv, w_mla_out, gate_b, w_out,
          ffn_norm_g, w_ffn_gate, w_ffn_up, w_ffn_down, final_norm_g)
    ms = (m_meta_tokens, m_mix_norm_g, m_w_in, m_conv_dw, m_conv_b, m_conv_ln_g, m_conv_ln_b, m_w_conv_out,
          m_gqa_q_norm_g, m_gqa_k_norm_g, m_w_gqa_out, m_mla_q_norm_g, m_w_mla_uq, m_mla_kv_norm_g, m_w_mla_ukv,
          m_w_mla_out, m_gate_b, m_w_out, m_ffn_norm_g, m_w_ffn_gate, m_w_ffn_up, m_w_ffn_down, m_final_norm_g)
    vs = (v_meta_tokens, v_mix_norm_g, v_w_in, v_conv_dw, v_conv_b, v_conv_ln_g, v_conv_ln_b, v_w_conv_out,
          v_gqa_q_norm_g, v_gqa_k_norm_g, v_w_gqa_out, v_mla_q_norm_g, v_w_mla_uq, v_mla_kv_norm_g, v_w_mla_ukv,
          v_w_mla_out, v_gate_b, v_w_out, v_ffn_norm_g, v_w_ffn_gate, v_w_ffn_up, v_w_ffn_down, v_final_norm_g)
    p = dict(zip(WEIGHT_NAMES, ws))
    loss, grad_x, grads, delta, new_m, new_v = _step(p, dict(zip(WEIGHT_NAMES, ms)), dict(zip(WEIGHT_NAMES, vs)),
                                                     x, loss_target)
    return (loss, grad_x, *[grads[n] for n in WEIGHT_NAMES], *[delta[n] for n in WEIGHT_NAMES],
            *[new_m[n] for n in WEIGHT_NAMES], *[new_v[n] for n in WEIGHT_NAMES])
```

```python
import math

import jax
import jax.numpy as jnp
from jax import lax
from jax.experimental import pallas as pl
from jax.experimental.pallas import tpu as pltpu

F32 = jnp.float32
BF16 = jnp.bfloat16
SDS = jax.ShapeDtypeStruct
MESH = pl.DeviceIdType.MESH

N_META = 16
GRID_W = 64
ROPE_THETA = 10000.0
NORM_EPS = 1e-6
CONV_K = 31
CONV_HALO = 16
HEAD = 128
GQA_HEADS = 8
GQA_KV_HEADS = 2
MLA_HEADS = 8
MLA_ROPE = 64
MLA_SLOT = 256
N_BRANCH = 3

ADAM_LR = 0.001
ADAM_B1 = 0.9
ADAM_B2 = 0.999
ADAM_EPS = 1e-08
ADAM_WD = 0.01
ADAM_STEP = 10

LANES = 128
SUBLANES_BF16 = 16
VMEM_LIMIT_BYTES = 56 * 2 ** 20
ROW_BLOCK_BYTES = 1 << 20
N_CHIPS = 4
N_DEV = 8


def _cparams():
    return pltpu.CompilerParams(vmem_limit_bytes=VMEM_LIMIT_BYTES)


def _tile(n, cap, mult):
    best = None
    for d in range(mult, min(n, cap) + 1, mult):
        if n % d == 0:
            best = d
    return n if best is None else best


def _row_tile(rows, width, mult=SUBLANES_BF16):
    return _tile(rows, max(mult, ROW_BLOCK_BYTES // (4 * width)), mult)


def _sigmoid(x):
    return 1.0 / (1.0 + jnp.exp(-x))


def _mm(a, b, *, mode="nn", add=None, out_dtype=F32, b_sharded=False, out_shards=None, name):
    if b_sharded:
        s_b, r_b, c_b = b.shape
        b_rows, b_cols = r_b, s_b * c_b
    else:
        b_rows, b_cols = b.shape
    if mode == "nn":
        (m, k), n = a.shape, b_cols
        assert b_rows == k
    elif mode == "nt":
        (m, k), n = a.shape, b_rows
        assert b_cols == k
    else:
        (k, m), n = a.shape, b_cols
        assert b_rows == k
    n_unit = n
    if b_sharded and mode != "nt":
        n_unit = c_b
    if out_shards is not None:
        assert n % out_shards == 0
        n_unit = math.gcd(n_unit, n // out_shards)
    k_unit = c_b if (b_sharded and mode == "nt") else k
    if mode == "tn":
        tm = _tile(m, 1024, LANES)
        tk = _tile(k_unit, 2176, SUBLANES_BF16)
    else:
        tm = _tile(m, 1088, SUBLANES_BF16)
        tk = _tile(k_unit, 2176, LANES)
    tn = _tile(n_unit, 1408, LANES)
    nk = k // tk
    if mode == "nn":
        a_spec = pl.BlockSpec((tm, tk), lambda i, j, kk: (i, kk))
        dn = (((1,), (0,)), ((), ()))
        if b_sharded:
            per = c_b // tn
            b_spec = pl.BlockSpec((None, tk, tn), lambda i, j, kk: (j // per, kk, j % per))
        else:
            b_spec = pl.BlockSpec((tk, tn), lambda i, j, kk: (kk, j))
    elif mode == "nt":
        a_spec = pl.BlockSpec((tm, tk), lambda i, j, kk: (i, kk))
        dn = (((1,), (1,)), ((), ()))
        if b_sharded:
            per = c_b // tk
            b_spec = pl.BlockSpec((None, tn, tk), lambda i, j, kk: (kk // per, j, kk % per))
        else:
            b_spec = pl.BlockSpec((tn, tk), lambda i, j, kk: (j, kk))
    else:
        a_spec = pl.BlockSpec((tk, tm), lambda i, j, kk: (kk, i))
        dn = (((0,), (0,)), ((), ()))
        if b_sharded:
            per = c_b // tn
            b_spec = pl.BlockSpec((None, tk, tn), lambda i, j, kk: (j // per, kk, j % per))
        else:
            b_spec = pl.BlockSpec((tk, tn), lambda i, j, kk: (kk, j))
    if out_shards is None:
        o_spec = pl.BlockSpec((tm, tn), lambda i, j, kk: (i, j))
        out_shape = SDS((m, n), out_dtype)
    else:
        per_o = (n // out_shards) // tn
        o_spec = pl.BlockSpec((None, tm, tn), lambda i, j, kk: (j // per_o, i, j % per_o))
        out_shape = SDS((out_shards, m, n // out_shards), out_dtype)
    has_add = add is not None
    assert not (has_add and out_shards is not None)

    def body(a_ref, b_ref, *rest):
        if has_add:
            add_ref, o_ref, acc = rest
        else:
            o_ref, acc = rest
        kk = pl.program_id(2)
        part = lax.dot_general(a_ref[...].astype(BF16), b_ref[...].astype(BF16), dn,
                               preferred_element_type=F32)

        @pl.when(kk == 0)
        def _():
            acc[...] = part

        @pl.when(kk > 0)
        def _():
            acc[...] += part

        @pl.when(kk == nk - 1)
        def _():
            r = acc[...]
            if has_add:
                r = r + add_ref[...]
            o_ref[...] = r.astype(o_ref.dtype)

    in_specs = [a_spec, b_spec] + ([o_spec] if has_add else [])
    args = (a, b) + ((add,) if has_add else ())
    return pl.pallas_call(
        body, grid=(m // tm, n // tn, nk), in_specs=in_specs, out_specs=o_spec,
        out_shape=out_shape, scratch_shapes=[pltpu.VMEM((tm, tn), F32)],
        compiler_params=_cparams(), name=name)(*args)


def _rmsnorm_fwd(x, g, *, width, cblk, name):
    t = x.shape[0]
    tr = _row_tile(t, width)

    def body(x_ref, g_ref, o_ref):
        xv = x_ref[...]
        r = lax.rsqrt(jnp.mean(xv * xv, axis=-1, keepdims=True) + NORM_EPS)
        o_ref[...] = (xv * r * g_ref[...]).astype(o_ref.dtype)

    return pl.pallas_call(
        body, grid=(t // tr,),
        in_specs=[pl.BlockSpec((tr, width), lambda i: (i, cblk)), pl.BlockSpec((1, width), lambda i: (0, 0))],
        out_specs=pl.BlockSpec((tr, width), lambda i: (i, 0)),
        out_shape=SDS((t, width), BF16), compiler_params=_cparams(), name=name)(x, g)


def _rmsnorm_bwd(x, g, dy, *, width, cblk, res=None, name):
    t = x.shape[0]
    tr = _row_tile(t, width)
    has_res = res is not None

    def body(x_ref, g_ref, dy_ref, *rest):
        if has_res:
            res_ref, dx_ref, dxb_ref, dg_ref = rest
        else:
            dx_ref, dxb_ref, dg_ref = rest
        xv = x_ref[...]
        dyv = dy_ref[...].astype(F32)
        r = lax.rsqrt(jnp.mean(xv * xv, axis=-1, keepdims=True) + NORM_EPS)
        gy = dyv * g_ref[...]
        dx = r * gy - xv * (r * r * r) * jnp.mean(gy * xv, axis=-1, keepdims=True)
        if has_res:
            dx = dx + res_ref[...]
        dx_ref[...] = dx
        dxb_ref[...] = dx.astype(BF16)
        part = jnp.sum(dyv * xv * r, axis=0, keepdims=True)

        @pl.when(pl.program_id(0) == 0)
        def _():
            dg_ref[...] = part

        @pl.when(pl.program_id(0) > 0)
        def _():
            dg_ref[...] += part

    row = pl.BlockSpec((tr, width), lambda i: (i, 0))
    vec = pl.BlockSpec((1, width), lambda i: (0, 0))
    in_specs = [pl.BlockSpec((tr, width), lambda i: (i, cblk)), vec, row] + ([row] if has_res else [])
    args = (x, g, dy) + ((res,) if has_res else ())
    return pl.pallas_call(
        body, grid=(t // tr,), in_specs=in_specs, out_specs=[row, row, vec],
        out_shape=[SDS((t, width), F32), SDS((t, width), BF16), SDS((1, width), F32)],
        compiler_params=_cparams(), name=name)(*args)


def _valid_rows(tile_index, tr, lp, lo, hi):
    pos = (tile_index % (lp // tr)) * tr + lax.broadcasted_iota(jnp.int32, (tr, 1), 0)
    return (pos >= lo) & (pos < hi)


def _loss_head(h, g, target, *, lp, l_valid, name):
    t, d = h.shape
    tr = _tile(lp, max(SUBLANES_BF16, ROW_BLOCK_BYTES // (4 * d)), SUBLANES_BF16)

    def body(x_ref, g_ref, t_ref, loss_ref, dx_ref, dxb_ref, dg_ref):
        i = pl.program_id(0)
        xv = x_ref[...]
        r = lax.rsqrt(jnp.mean(xv * xv, axis=-1, keepdims=True) + NORM_EPS)
        y = xv * r * g_ref[...]
        valid = _valid_rows(i, tr, lp, N_META, l_valid)
        err = jnp.where(valid, y - t_ref[...], 0.0)
        lpart = 0.5 * jnp.sum(jnp.mean(err * err, axis=-1, keepdims=True), axis=0, keepdims=True)
        dyv = err * (1.0 / d)
        gy = dyv * g_ref[...]
        dx = r * gy - xv * (r * r * r) * jnp.mean(gy * xv, axis=-1, keepdims=True)
        dx_ref[...] = dx
        dxb_ref[...] = dx.astype(BF16)
        gpart = jnp.sum(dyv * xv * r, axis=0, keepdims=True)

        @pl.when(i == 0)
        def _():
            dg_ref[...] = gpart
            loss_ref[...] = jnp.broadcast_to(lpart, loss_ref.shape)

        @pl.when(i > 0)
        def _():
            dg_ref[...] += gpart
            loss_ref[...] += jnp.broadcast_to(lpart, loss_ref.shape)

    row = pl.BlockSpec((tr, d), lambda i: (i, 0))
    vec = pl.BlockSpec((1, d), lambda i: (0, 0))
    return pl.pallas_call(
        body, grid=(t // tr,), in_specs=[row, vec, row],
        out_specs=[pl.BlockSpec((1, LANES), lambda i: (0, 0)), row, row, vec],
        out_shape=[SDS((1, LANES), F32), SDS((t, d), F32), SDS((t, d), BF16), SDS((1, d), F32)],
        compiler_params=_cparams(), name=name)(h, g, target)


def _ln_silu_fwd(yc, g, b, *, name):
    t, c = yc.shape
    tr = _row_tile(t, c)

    def body(x_ref, g_ref, b_ref, o_ref):
        xv = x_ref[...]
        xc = xv - jnp.mean(xv, axis=-1, keepdims=True)
        r = lax.rsqrt(jnp.mean(xc * xc, axis=-1, keepdims=True) + NORM_EPS)
        ln = xc * r * g_ref[...] + b_ref[...]
        o_ref[...] = (ln * _sigmoid(ln)).astype(o_ref.dtype)

    row = pl.BlockSpec((tr, c), lambda i: (i, 0))
    vec = pl.BlockSpec((1, c), lambda i: (0, 0))
    return pl.pallas_call(body, grid=(t // tr,), in_specs=[row, vec, vec], out_specs=row,
                          out_shape=SDS((t, c), BF16), compiler_params=_cparams(), name=name)(yc, g, b)


def _ln_silu_bwd(yc, g, b, ds, *, name):
    t, c = yc.shape
    tr = _row_tile(t, c)

    def body(x_ref, g_ref, b_ref, ds_ref, dx_ref, dg_ref, db_ref):
        xv = x_ref[...]
        xc = xv - jnp.mean(xv, axis=-1, keepdims=True)
        r = lax.rsqrt(jnp.mean(xc * xc, axis=-1, keepdims=True) + NORM_EPS)
        xh = xc * r
        ln = xh * g_ref[...] + b_ref[...]
        sg = _sigmoid(ln)
        dln = ds_ref[...].astype(F32) * (sg * (1.0 + ln * (1.0 - sg)))
        gy = dln * g_ref[...]
        dx_ref[...] = r * (gy - jnp.mean(gy, axis=-1, keepdims=True)
                           - xh * jnp.mean(gy * xh, axis=-1, keepdims=True))
        gpart = jnp.sum(dln * xh, axis=0, keepdims=True)
        bpart = jnp.sum(dln, axis=0, keepdims=True)

        @pl.when(pl.program_id(0) == 0)
        def _():
            dg_ref[...] = gpart
            db_ref[...] = bpart

        @pl.when(pl.program_id(0) > 0)
        def _():
            dg_ref[...] += gpart
            db_ref[...] += bpart

    row = pl.BlockSpec((tr, c), lambda i: (i, 0))
    vec = pl.BlockSpec((1, c), lambda i: (0, 0))
    return pl.pallas_call(
        body, grid=(t // tr,), in_specs=[row, vec, vec, row], out_specs=[row, vec, vec],
        out_shape=[SDS((t, c), F32), SDS((1, c), F32), SDS((1, c), F32)],
        compiler_params=_cparams(), name=name)(yc, g, b, ds)


def _swiglu_fwd(gate, up, *, name):
    t, f = gate.shape
    tr = _row_tile(t, f)

    def body(g_ref, u_ref, o_ref):
        gv = g_ref[...]
        o_ref[...] = (gv * _sigmoid(gv) * u_ref[...]).astype(o_ref.dtype)

    row = pl.BlockSpec((tr, f), lambda i: (i, 0))
    return pl.pallas_call(body, grid=(t // tr,), in_specs=[row, row], out_specs=row,
                          out_shape=SDS((t, f), BF16), compiler_params=_cparams(), name=name)(gate, up)


def _swiglu_bwd(gate, up, dact, *, name):
    t, f = gate.shape
    tr = _row_tile(t, f)

    def body(g_ref, u_ref, d_ref, dg_ref, du_ref):
        gv = g_ref[...]
        sg = _sigmoid(gv)
        dv = d_ref[...]
        dg_ref[...] = (dv * u_ref[...] * (sg * (1.0 + gv * (1.0 - sg)))).astype(dg_ref.dtype)
        du_ref[...] = (dv * (gv * sg)).astype(du_ref.dtype)

    row = pl.BlockSpec((tr, f), lambda i: (i, 0))
    return pl.pallas_call(body, grid=(t // tr,), in_specs=[row, row, row], out_specs=[row, row],
                          out_shape=[SDS((t, f), BF16), SDS((t, f), BF16)],
                          compiler_params=_cparams(), name=name)(gate, up, dact)


def _merge_tiles(t, d):
    tc = _tile(d, 512, LANES)
    return _row_tile(t, tc), tc


def _merge_fwd(proj, gate_b, ya, yb, yc, *, gate_off, name):
    t, d = ya.shape
    tr, tc = _merge_tiles(t, d)
    nc = d // tc
    assert gate_off % tc == 0
    g0 = gate_off // tc

    def body(l0, l1, l2, b0, b1, b2, ya_ref, yb_ref, yc_ref, o_ref):
        acc = _sigmoid(l0[...] + b0[...]) * ya_ref[...]
        acc += _sigmoid(l1[...] + b1[...]) * yb_ref[...]
        acc += _sigmoid(l2[...] + b2[...]) * yc_ref[...]
        o_ref[...] = acc.astype(o_ref.dtype)

    lspec = [pl.BlockSpec((tr, tc), lambda i, j, br=br: (i, g0 + br * nc + j)) for br in range(N_BRANCH)]
    bspec = [pl.BlockSpec((1, tc), lambda i, j, br=br: (0, br * nc + j)) for br in range(N_BRANCH)]
    yspec = pl.BlockSpec((tr, tc), lambda i, j: (i, j))
    return pl.pallas_call(
        body, grid=(t // tr, nc), in_specs=lspec + bspec + [yspec] * 3, out_specs=yspec,
        out_shape=SDS((t, d), BF16), compiler_params=_cparams(), name=name)(
            proj, proj, proj, gate_b, gate_b, gate_b, ya, yb, yc)


def _merge_bwd(proj, gate_b, ya, yb, yc, dm, *, gate_off, name):
    t, d = ya.shape
    tr, tc = _merge_tiles(t, d)
    nc = d // tc
    g0 = gate_off // tc

    def body(l0, l1, l2, b0, b1, b2, ya_ref, yb_ref, yc_ref, dm_ref,
             dya, dyb, dyc, dl0, dl1, dl2, db0, db1, db2):
        i = pl.program_id(1)
        dmv = dm_ref[...]
        for l_ref, b_ref, y_ref, dy_ref, dl_ref, db_ref in (
                (l0, b0, ya_ref, dya, dl0, db0), (l1, b1, yb_ref, dyb, dl1, db1), (l2, b2, yc_ref, dyc, dl2, db2)):
            gt = _sigmoid(l_ref[...] + b_ref[...])
            dy_ref[...] = (dmv * gt).astype(dy_ref.dtype)
            dl = dmv * y_ref[...] * gt * (1.0 - gt)
            dl_ref[...] = dl.astype(dl_ref.dtype)
            part = jnp.sum(dl, axis=0, keepdims=True)

            @pl.when(i == 0)
            def _(db_ref=db_ref, part=part):
                db_ref[...] = part

            @pl.when(i > 0)
            def _(db_ref=db_ref, part=part):
                db_ref[...] += part

    lspec = [pl.BlockSpec((tr, tc), lambda j, i, br=br: (i, g0 + br * nc + j)) for br in range(N_BRANCH)]
    bspec = [pl.BlockSpec((1, tc), lambda j, i, br=br: (0, br * nc + j)) for br in range(N_BRANCH)]
    yspec = pl.BlockSpec((tr, tc), lambda j, i: (i, j))
    vspec = pl.BlockSpec((1, tc), lambda j, i: (0, j))
    return pl.pallas_call(
        body, grid=(nc, t // tr), in_specs=lspec + bspec + [yspec] * 4,
        out_specs=[yspec] * 6 + [vspec] * 3,
        out_shape=[SDS((t, d), BF16)] * 6 + [SDS((1, d), F32)] * 3,
        compiler_params=_cparams(), name=name)(proj, proj, proj, gate_b, gate_b, gate_b, ya, yb, yc, dm)


def _conv_fwd(proj3, dw, cb, *, c_conv, l_valid, name):
    b, lp, _ = proj3.shape
    cw = _tile(c_conv, 256, LANES)
    nc = c_conv // cw

    def body(a_ref, gte_ref, dw_ref, cb_ref, o_ref, zs):
        rows = lax.broadcasted_iota(jnp.int32, (lp, 1), 0)
        z = jnp.where(rows < l_valid, a_ref[...] * _sigmoid(gte_ref[...]), 0.0)
        zs[pl.ds(0, CONV_HALO), :] = jnp.zeros((CONV_HALO, cw), F32)
        zs[pl.ds(CONV_HALO, lp), :] = z
        zs[pl.ds(CONV_HALO + lp, CONV_HALO), :] = jnp.zeros((CONV_HALO, cw), F32)
        acc = jnp.broadcast_to(cb_ref[...], (lp, cw))
        for j in range(CONV_K):
            acc = acc + zs[pl.ds(j + 1, lp), :] * dw_ref[pl.ds(j, 1), :]
        o_ref[...] = acc

    seq = lambda off: pl.BlockSpec((None, lp, cw), lambda bi, ci: (bi, 0, off + ci))
    return pl.pallas_call(
        body, grid=(b, nc),
        in_specs=[seq(0), seq(nc), pl.BlockSpec((CONV_K, cw), lambda bi, ci: (0, ci)),
                  pl.BlockSpec((1, cw), lambda bi, ci: (0, ci))],
        out_specs=seq(0), out_shape=SDS((b, lp, c_conv), F32),
        scratch_shapes=[pltpu.VMEM((lp + 2 * CONV_HALO, cw), F32)],
        compiler_params=_cparams(), name=name)(proj3, proj3, dw, cb)


def _conv_bwd(proj3, dw, dyc3, *, c_conv, l_valid, name):
    b, lp, _ = proj3.shape
    cw = _tile(c_conv, 256, LANES)
    nc = c_conv // cw

    def body(a_ref, gte_ref, dw_ref, dy_ref, da_ref, dgte_ref, ddw_ref, dcb_ref, zs, dys):
        bi = pl.program_id(1)
        rows = lax.broadcasted_iota(jnp.int32, (lp, 1), 0)
        valid = rows < l_valid
        av = a_ref[...]
        sg = _sigmoid(gte_ref[...])
        dyv = dy_ref[...]
        zero = jnp.zeros((CONV_HALO, cw), F32)
        zs[pl.ds(0, CONV_HALO), :] = zero
        zs[pl.ds(CONV_HALO, lp), :] = jnp.where(valid, av * sg, 0.0)
        zs[pl.ds(CONV_HALO + lp, CONV_HALO), :] = zero
        dys[pl.ds(0, CONV_HALO), :] = zero
        dys[pl.ds(CONV_HALO, lp), :] = dyv
        dys[pl.ds(CONV_HALO + lp, CONV_HALO), :] = zero

        @pl.when(bi == 0)
        def _():
            ddw_ref[...] = jnp.zeros_like(ddw_ref)
            dcb_ref[...] = jnp.zeros_like(dcb_ref)

        dz = jnp.zeros((lp, cw), F32)
        for j in range(CONV_K):
            dz = dz + dys[pl.ds(CONV_K - j, lp), :] * dw_ref[pl.ds(j, 1), :]
            ddw_ref[pl.ds(j, 1), :] += jnp.sum(dyv * zs[pl.ds(j + 1, lp), :], axis=0, keepdims=True)
        dcb_ref[...] += jnp.sum(dyv, axis=0, keepdims=True)
        dz = jnp.where(valid, dz, 0.0)
        da_ref[...] = (dz * sg).astype(da_ref.dtype)
        dgte_ref[...] = (dz * av * sg * (1.0 - sg)).astype(dgte_ref.dtype)

    seq = lambda off: pl.BlockSpec((None, lp, cw), lambda ci, bi: (bi, 0, off + ci))
    return pl.pallas_call(
        body, grid=(nc, b),
        in_specs=[seq(0), seq(nc), pl.BlockSpec((CONV_K, cw), lambda ci, bi: (0, ci)), seq(0)],
        out_specs=[seq(0), seq(0), pl.BlockSpec((CONV_K, cw), lambda ci, bi: (0, ci)),
                   pl.BlockSpec((1, cw), lambda ci, bi: (0, ci))],
        out_shape=[SDS((b, lp, c_conv), BF16), SDS((b, lp, c_conv), BF16),
                   SDS((CONV_K, c_conv), F32), SDS((1, c_conv), F32)],
        scratch_shapes=[pltpu.VMEM((lp + 2 * CONV_HALO, cw), F32)] * 2,
        compiler_params=_cparams(), name=name)(proj3, proj3, dw, dyc3)


def _swap_halves(x, group):
    half = group // 2
    lane = lax.broadcasted_iota(jnp.int32, x.shape, 1)
    up = pltpu.roll(x, LANES - half, 1)
    down = pltpu.roll(x, half, 1)
    return jnp.where((lane % group) < half, up, down)


def _rope(x, cos, sin, group):
    return x * cos + _swap_halves(x, group) * sin


def _rope_bwd(dy, cos, sin, group):
    return dy * cos + _swap_halves(dy * sin, group)


def _rope_tables(lp, seq, dim, lanes):
    quarter = dim // 4
    tok = jnp.arange(seq, dtype=jnp.int32)
    zeros = jnp.zeros((N_META,), F32)
    pad = jnp.zeros((lp - N_META - seq,), F32)
    row = jnp.concatenate([zeros, (tok // GRID_W).astype(F32), pad])
    col = jnp.concatenate([zeros, (tok % GRID_W).astype(F32), pad])
    inv = ROPE_THETA ** (-jnp.arange(quarter, dtype=F32) / quarter)
    ar, ac = row[:, None] * inv[None, :], col[:, None] * inv[None, :]
    cos = jnp.concatenate([jnp.cos(ar), jnp.cos(ar), jnp.cos(ac), jnp.cos(ac)], axis=1)
    sin = jnp.concatenate([-jnp.sin(ar), jnp.sin(ar), -jnp.sin(ac), jnp.sin(ac)], axis=1)
    if lanes > dim:
        cos = jnp.concatenate([cos, jnp.ones((lp, lanes - dim), F32)], axis=1)
        sin = jnp.concatenate([sin, jnp.zeros((lp, lanes - dim), F32)], axis=1)
    return cos, sin


def _gqa_prep_fwd(proj, qg, kg, cos, sin, *, q_off, k_off, v_off, lp, name):
    t = proj.shape[0]
    qw, kw = GQA_HEADS * HEAD, GQA_KV_HEADS * HEAD
    tr = _tile(lp, max(SUBLANES_BF16, ROW_BLOCK_BYTES // (4 * qw)), SUBLANES_BF16)
    npos = lp // tr
    assert q_off % qw == 0 and k_off % kw == 0 and v_off % kw == 0

    def body(q_ref, k_ref, v_ref, qg_ref, kg_ref, cos_ref, sin_ref, qo_ref, ko_ref, vo_ref):
        cosv, sinv = cos_ref[...], sin_ref[...]
        for src, g_ref, dst, heads in ((q_ref, qg_ref, qo_ref, GQA_HEADS), (k_ref, kg_ref, ko_ref, GQA_KV_HEADS)):
            for hh in range(heads):
                xh = src[:, hh * HEAD:(hh + 1) * HEAD]
                r = lax.rsqrt(jnp.mean(xh * xh, axis=-1, keepdims=True) + NORM_EPS)
                dst[:, hh * HEAD:(hh + 1) * HEAD] = _rope(xh * r * g_ref[...], cosv, sinv, 64).astype(dst.dtype)
        vo_ref[...] = v_ref[...].astype(vo_ref.dtype)

    tab = pl.BlockSpec((tr, HEAD), lambda i: (i % npos, 0))
    vec = pl.BlockSpec((1, HEAD), lambda i: (0, 0))
    return pl.pallas_call(
        body, grid=(t // tr,),
        in_specs=[pl.BlockSpec((tr, qw), lambda i: (i, q_off // qw)),
                  pl.BlockSpec((tr, kw), lambda i: (i, k_off // kw)),
                  pl.BlockSpec((tr, kw), lambda i: (i, v_off // kw)), vec, vec, tab, tab],
        out_specs=[pl.BlockSpec((tr, qw), lambda i: (i, 0)), pl.BlockSpec((tr, kw), lambda i: (i, 0)),
                   pl.BlockSpec((tr, kw), lambda i: (i, 0))],
        out_shape=[SDS((t, qw), BF16), SDS((t, kw), BF16), SDS((t, kw), BF16)],
        compiler_params=_cparams(), name=name)(proj, proj, proj, qg, kg, cos, sin)


def _gqa_prep_bwd(proj, qg, kg, cos, sin, dqr, dkr, *, q_off, k_off, lp, name):
    t = proj.shape[0]
    qw, kw = GQA_HEADS * HEAD, GQA_KV_HEADS * HEAD
    tr = _tile(lp, max(SUBLANES_BF16, ROW_BLOCK_BYTES // (4 * qw)), SUBLANES_BF16)
    npos = lp // tr

    def body(q_ref, k_ref, qg_ref, kg_ref, cos_ref, sin_ref, dqr_ref, dkr_ref, dq_ref, dk_ref, dqg_ref, dkg_ref):
        cosv, sinv = cos_ref[...], sin_ref[...]
        for src, g_ref, dy_ref, dx_ref, dg_ref, heads in (
                (q_ref, qg_ref, dqr_ref, dq_ref, dqg_ref, GQA_HEADS),
                (k_ref, kg_ref, dkr_ref, dk_ref, dkg_ref, GQA_KV_HEADS)):
            gpart = jnp.zeros((1, HEAD), F32)
            for hh in range(heads):
                sl = slice(hh * HEAD, (hh + 1) * HEAD)
                xh = src[:, sl]
                r = lax.rsqrt(jnp.mean(xh * xh, axis=-1, keepdims=True) + NORM_EPS)
                dxn = _rope_bwd(dy_ref[:, sl], cosv, sinv, 64)
                gy = dxn * g_ref[...]
                dx = r * gy - xh * (r * r * r) * jnp.mean(gy * xh, axis=-1, keepdims=True)
                dx_ref[:, sl] = dx.astype(dx_ref.dtype)
                gpart = gpart + jnp.sum(dxn * xh * r, axis=0, keepdims=True)

            @pl.when(pl.program_id(0) == 0)
            def _(dg_ref=dg_ref, gpart=gpart):
                dg_ref[...] = gpart

            @pl.when(pl.program_id(0) > 0)
            def _(dg_ref=dg_ref, gpart=gpart):
                dg_ref[...] += gpart

    tab = pl.BlockSpec((tr, HEAD), lambda i: (i % npos, 0))
    vec = pl.BlockSpec((1, HEAD), lambda i: (0, 0))
    qrow = pl.BlockSpec((tr, qw), lambda i: (i, 0))
    krow = pl.BlockSpec((tr, kw), lambda i: (i, 0))
    return pl.pallas_call(
        body, grid=(t // tr,),
        in_specs=[pl.BlockSpec((tr, qw), lambda i: (i, q_off // qw)),
                  pl.BlockSpec((tr, kw), lambda i: (i, k_off // kw)), vec, vec, tab, tab, qrow, krow],
        out_specs=[qrow, krow, vec, vec],
        out_shape=[SDS((t, qw), BF16), SDS((t, kw), BF16), SDS((1, HEAD), F32), SDS((1, HEAD), F32)],
        compiler_params=_cparams(), name=name)(proj, proj, qg, kg, cos, sin, dqr, dkr)


def _mla_rope_fwd(q_pre, k_pre, proj, cos, sin, *, kpe_off, lp, name):
    t, w = q_pre.shape
    tr = _tile(lp, max(SUBLANES_BF16, ROW_BLOCK_BYTES // (4 * w)), SUBLANES_BF16)
    npos = lp // tr
    assert kpe_off % LANES == 0

    def body(q_ref, k_ref, kpe_ref, cos_ref, sin_ref, qo_ref, ko_ref):
        cosv, sinv = cos_ref[...], sin_ref[...]
        kr = _rope(kpe_ref[...], cosv, sinv, 32).astype(ko_ref.dtype)
        for hh in range(MLA_HEADS):
            base = hh * MLA_SLOT
            qo_ref[:, base:base + HEAD] = q_ref[:, base:base + HEAD].astype(qo_ref.dtype)
            qo_ref[:, base + HEAD:base + MLA_SLOT] = _rope(
                q_ref[:, base + HEAD:base + MLA_SLOT], cosv, sinv, 32).astype(qo_ref.dtype)
            ko_ref[:, base:base + HEAD] = k_ref[:, base:base + HEAD].astype(ko_ref.dtype)
            ko_ref[:, base + HEAD:base + MLA_SLOT] = kr

    row = pl.BlockSpec((tr, w), lambda i: (i, 0))
    tab = pl.BlockSpec((tr, LANES), lambda i: (i % npos, 0))
    return pl.pallas_call(
        body, grid=(t // tr,),
        in_specs=[row, row, pl.BlockSpec((tr, LANES), lambda i: (i, kpe_off // LANES)), tab, tab],
        out_specs=[row, row], out_shape=[SDS((t, w), BF16), SDS((t, w), BF16)],
        compiler_params=_cparams(), name=name)(q_pre, k_pre, proj, cos, sin)


def _mla_rope_bwd(dq, dk, cos, sin, *, lp, name):
    t, w = dq.shape
    tr = _tile(lp, max(SUBLANES_BF16, ROW_BLOCK_BYTES // (4 * w)), SUBLANES_BF16)
    npos = lp // tr

    def body(dq_ref, dk_ref, cos_ref, sin_ref, dqo_ref, dko_ref, dkpe_ref):
        cosv, sinv = cos_ref[...], sin_ref[...]
        pe = jnp.zeros((tr, LANES), F32)
        for hh in range(MLA_HEADS):
            base = hh * MLA_SLOT
            dqo_ref[:, base:base + HEAD] = dq_ref[:, base:base + HEAD].astype(dqo_ref.dtype)
            dqo_ref[:, base + HEAD:base + MLA_SLOT] = _rope_bwd(
                dq_ref[:, base + HEAD:base + MLA_SLOT], cosv, sinv, 32).astype(dqo_ref.dtype)
            dko_ref[:, base:base + HEAD] = dk_ref[:, base:base + HEAD].astype(dko_ref.dtype)
            dko_ref[:, base + HEAD:base + MLA_SLOT] = jnp.zeros((tr, LANES), dko_ref.dtype)
            pe = pe + dk_ref[:, base + HEAD:base + MLA_SLOT]
        dkpe_ref[...] = _rope_bwd(pe, cosv, sinv, 32).astype(dkpe_ref.dtype)

    row = pl.BlockSpec((tr, w), lambda i: (i, 0))
    tab = pl.BlockSpec((tr, LANES), lambda i: (i % npos, 0))
    return pl.pallas_call(
        body, grid=(t // tr,), in_specs=[row, row, tab, tab],
        out_specs=[row, row, pl.BlockSpec((tr, LANES), lambda i: (i, 0))],
        out_shape=[SDS((t, w), BF16), SDS((t, w), BF16), SDS((t, LANES), BF16)],
        compiler_params=_cparams(), name=name)(dq, dk, cos, sin)


def _softmax_rows(q, k, scale, lp, l_valid):
    s = lax.dot_general(q, k, (((1,), (1,)), ((), ())), preferred_element_type=F32) * scale
    cols = lax.broadcasted_iota(jnp.int32, (1, lp), 1)
    s = jnp.where(cols < l_valid, s, -1e30)
    e = jnp.exp(s - jnp.max(s, axis=-1, keepdims=True))
    return e, 1.0 / jnp.sum(e, axis=-1, keepdims=True)


def _attn_fwd(q, k, v, *, batch, lp, l_valid, heads, kv_heads, dk, dv, scale, name):
    group = heads // kv_heads
    tq = _tile(lp, 512, SUBLANES_BF16)
    nq = lp // tq

    def body(q_ref, k_ref, v_ref, o_ref):
        e, inv = _softmax_rows(q_ref[...], k_ref[...], scale, lp, l_valid)
        o_ref[...] = (jnp.dot(e.astype(BF16), v_ref[...], preferred_element_type=F32) * inv).astype(o_ref.dtype)

    return pl.pallas_call(
        body, grid=(batch, heads, nq),
        in_specs=[pl.BlockSpec((tq, dk), lambda b, h, i: (b * nq + i, h)),
                  pl.BlockSpec((lp, dk), lambda b, h, i: (b, h // group)),
                  pl.BlockSpec((lp, dv), lambda b, h, i: (b, h // group))],
        out_specs=pl.BlockSpec((tq, dv), lambda b, h, i: (b * nq + i, h)),
        out_shape=SDS((batch * lp, heads * dv), BF16), compiler_params=_cparams(), name=name)(q, k, v)


def _attn_bwd(q, k, v, do, *, batch, lp, l_valid, heads, kv_heads, dk, dv, scale, name):
    group = heads // kv_heads
    tq = _tile(lp, 512, SUBLANES_BF16)
    nq = lp // tq

    def body(q_ref, k_ref, v_ref, do_ref, dq_ref, dk_ref, dv_ref):
        first = (pl.program_id(2) == 0) & (pl.program_id(3) == 0)
        qv, kv, dov = q_ref[...], k_ref[...], do_ref[...]
        e, inv = _softmax_rows(qv, kv, scale, lp, l_valid)
        dp = lax.dot_general(dov, v_ref[...], (((1,), (1,)), ((), ())), preferred_element_type=F32)
        delta = inv * jnp.sum(e * dp, axis=-1, keepdims=True)
        ds = (e * ((dp - delta) * (inv * scale))).astype(BF16)
        dq_ref[...] = jnp.dot(ds, kv, preferred_element_type=F32)
        dkp = lax.dot_general(ds, qv, (((0,), (0,)), ((), ())), preferred_element_type=F32)
        dvp = lax.dot_general(e.astype(BF16), (dov * inv).astype(BF16), (((0,), (0,)), ((), ())),
                              preferred_element_type=F32)

        @pl.when(first)
        def _():
            dk_ref[...] = dkp
            dv_ref[...] = dvp

        @pl.when(jnp.logical_not(first))
        def _():
            dk_ref[...] += dkp
            dv_ref[...] += dvp

    return pl.pallas_call(
        body, grid=(batch, kv_heads, group, nq),
        in_specs=[pl.BlockSpec((tq, dk), lambda b, hk, g, i: (b * nq + i, hk * group + g)),
                  pl.BlockSpec((lp, dk), lambda b, hk, g, i: (b, hk)),
                  pl.BlockSpec((lp, dv), lambda b, hk, g, i: (b, hk)),
                  pl.BlockSpec((tq, dv), lambda b, hk, g, i: (b * nq + i, hk * group + g))],
        out_specs=[pl.BlockSpec((tq, dk), lambda b, hk, g, i: (b * nq + i, hk * group + g)),
                   pl.BlockSpec((lp, dk), lambda b, hk, g, i: (b, hk)),
                   pl.BlockSpec((lp, dv), lambda b, hk, g, i: (b, hk))],
        out_shape=[SDS((batch * lp, heads * dk), F32), SDS((batch * lp, kv_heads * dk), F32),
                   SDS((batch * lp, kv_heads * dv), F32)],
        compiler_params=_cparams(), name=name)(q, k, v, do)


def _meta_grad(dh3, *, name):
    b, _, d = dh3.shape

    def body(x_ref, o_ref):
        @pl.when(pl.program_id(0) == 0)
        def _():
            o_ref[...] = x_ref[...]

        @pl.when(pl.program_id(0) > 0)
        def _():
            o_ref[...] += x_ref[...]

    return pl.pallas_call(
        body, grid=(b,), in_specs=[pl.BlockSpec((None, N_META, d), lambda bi: (bi, 0, 0))],
        out_specs=pl.BlockSpec((N_META, d), lambda bi: (0, 0)), out_shape=SDS((N_META, d), F32),
        compiler_params=_cparams(), name=name)(dh3)


def _sum_slots(parts, *, out_dtype, name):
    _, r, c = parts[0].shape
    tr = _row_tile(r, c)
    counts = [p.shape[0] for p in parts]

    def body(*refs):
        o_ref = refs[-1]
        acc = None
        for ref, cnt in zip(refs[:-1], counts):
            for s in range(cnt):
                term = ref[s].astype(F32)
                acc = term if acc is None else acc + term
        o_ref[...] = acc.astype(o_ref.dtype)

    return pl.pallas_call(
        body, grid=(r // tr,),
        in_specs=[pl.BlockSpec((cnt, tr, c), lambda i: (0, i, 0)) for cnt in counts],
        out_specs=pl.BlockSpec((tr, c), lambda i: (i, 0)), out_shape=SDS((r, c), out_dtype),
        compiler_params=_cparams(), name=name)(*parts)


def _adamw(w, g, m, v, *, name):
    shape = w.shape
    cols = shape[-1]
    rows = w.size // cols
    tr = _tile(rows, max(8, ROW_BLOCK_BYTES // (4 * cols)), 8)
    bc1 = 1.0 - ADAM_B1 ** ADAM_STEP
    bc2 = 1.0 - ADAM_B2 ** ADAM_STEP

    def body(w_ref, g_ref, m_ref, v_ref, d_ref, mo_ref, vo_ref):
        gv = g_ref[...]
        mn = ADAM_B1 * m_ref[...] + (1.0 - ADAM_B1) * gv
        vn = ADAM_B2 * v_ref[...] + (1.0 - ADAM_B2) * (gv * gv)
        mo_ref[...] = mn
        vo_ref[...] = vn
        d_ref[...] = -ADAM_LR * ((mn / bc1) / (jnp.sqrt(vn / bc2) + ADAM_EPS) + ADAM_WD * w_ref[...])

    spec = pl.BlockSpec((tr, cols), lambda i: (i, 0))
    outs = pl.pallas_call(
        body, grid=(rows // tr,), in_specs=[spec] * 4, out_specs=[spec] * 3,
        out_shape=[SDS((rows, cols), F32)] * 3, compiler_params=_cparams(), name=name)(
            *[a.reshape(rows, cols) for a in (w, g, m, v)])
    return tuple(o.reshape(shape) for o in outs)


def _mesh_pos():
    return lax.axis_index("x"), lax.axis_index("y"), lax.axis_index("c")


def _other_chips(x, y):
    return [(1 - x, y), (x, 1 - y), (1 - x, 1 - y)]


def _chip_allgather_small(vec, *, name):
    r, c = vec.shape

    def body(x_ref, out_ref, send_sems, recv_sems):
        x, y, cc = _mesh_pos()
        mine = 2 * x + y
        out_ref[mine] = x_ref[...]
        chips = _other_chips(x, y)
        sends = [pltpu.make_async_remote_copy(
            src_ref=x_ref, dst_ref=out_ref.at[mine], send_sem=send_sems.at[j], recv_sem=recv_sems.at[j],
            device_id=(px, py, cc), device_id_type=MESH) for j, (px, py) in enumerate(chips)]
        for cp in sends:
            cp.start()
        for j, (px, py) in enumerate(chips):
            pltpu.make_async_remote_copy(
                src_ref=x_ref, dst_ref=out_ref.at[2 * px + py], send_sem=send_sems.at[j],
                recv_sem=recv_sems.at[j], device_id=(px, py, cc), device_id_type=MESH).wait_recv()
        for cp in sends:
            cp.wait_send()

    return pl.pallas_call(
        body, out_shape=SDS((N_CHIPS, r, c), F32),
        in_specs=[pl.BlockSpec(memory_space=pltpu.VMEM)], out_specs=pl.BlockSpec(memory_space=pltpu.VMEM),
        scratch_shapes=[pltpu.SemaphoreType.DMA((3,)), pltpu.SemaphoreType.DMA((3,))],
        compiler_params=_cparams(), name=name)(vec)


def _allreduce_small(vec, *, name):
    r, c = vec.shape

    def body(x_ref, out_ref, buf, send_sems, recv_sems):
        x, y, cc = _mesh_pos()
        me = 4 * x + 2 * y + cc
        buf[me] = x_ref[...]

        def peer(kk):
            fx, fy, fc = (kk >> 2) & 1, (kk >> 1) & 1, kk & 1
            px = x if fx == 0 else 1 - x
            py = y if fy == 0 else 1 - y
            pc = cc if fc == 0 else 1 - cc
            return (px, py, pc), 4 * px + 2 * py + pc

        sends = []
        for kk in range(1, N_DEV):
            dev, _ = peer(kk)
            sends.append(pltpu.make_async_remote_copy(
                src_ref=x_ref, dst_ref=buf.at[me], send_sem=send_sems.at[kk - 1], recv_sem=recv_sems.at[kk - 1],
                device_id=dev, device_id_type=MESH))
        for cp in sends:
            cp.start()
        for kk in range(1, N_DEV):
            dev, slot = peer(kk)
            pltpu.make_async_remote_copy(
                src_ref=x_ref, dst_ref=buf.at[slot], send_sem=send_sems.at[kk - 1], recv_sem=recv_sems.at[kk - 1],
                device_id=dev, device_id_type=MESH).wait_recv()
        for cp in sends:
            cp.wait_send()
        acc = buf[0]
        for s in range(1, N_DEV):
            acc = acc + buf[s]
        out_ref[...] = acc

    return pl.pallas_call(
        body, out_shape=SDS((r, c), F32),
        in_specs=[pl.BlockSpec(memory_space=pltpu.VMEM)], out_specs=pl.BlockSpec(memory_space=pltpu.VMEM),
        scratch_shapes=[pltpu.VMEM((N_DEV, r, c), F32), pltpu.SemaphoreType.DMA((N_DEV - 1,)),
                        pltpu.SemaphoreType.DMA((N_DEV - 1,))],
        compiler_params=_cparams(), name=name)(vec)


_HBM = pl.BlockSpec(memory_space=pltpu.HBM)


def _half(rows, half):
    hk = rows // 2
    assert hk % SUBLANES_BF16 == 0
    return pl.ds(pl.multiple_of(half * hk, SUBLANES_BF16), hk)


_SEM = pl.BlockSpec(memory_space=pltpu.SEMAPHORE)
_ANY = pl.BlockSpec(memory_space=pl.ANY)
_DATAFLOW = pltpu.SideEffectType.DATAFLOW_SIDE_EFFECTING


def _in_hbm(a):
    return pltpu.with_memory_space_constraint(a, pltpu.HBM)


def _split_start(srcs, lands, prev, make_copy, *, name):
    nw = len(srcs)

    def body(*refs):
        src_refs, land_refs = refs[:nw], refs[nw:2 * nw]
        send_sems, recv_sems = refs[2 * nw + 1], refs[2 * nw + 2]
        token = refs[-1]
        for j in range(3):
            for i in range(nw):
                make_copy(src_refs, land_refs, send_sems, recv_sems, j, i, True).start()
        token[...] = jnp.zeros_like(token)

    outs = pl.pallas_call(
        body, name=name,
        out_shape=(pltpu.SemaphoreType.DMA((3 * nw,)), pltpu.SemaphoreType.DMA((3 * nw,)),
                   *[pltpu.HBM(a.shape, a.dtype) for a in srcs], *[pltpu.HBM(a.shape, a.dtype) for a in lands],
                   SDS((8, LANES), F32)),
        in_specs=[_HBM] * (2 * nw) + [_ANY],
        out_specs=(_SEM, _SEM, *[_HBM] * (2 * nw), pl.BlockSpec(memory_space=pltpu.VMEM)),
        input_output_aliases={i: 2 + i for i in range(2 * nw)},
        compiler_params=pltpu.CompilerParams(has_side_effects=_DATAFLOW),
    )(*[_in_hbm(a) for a in srcs], *[_in_hbm(a) for a in lands], prev)
    return outs[0], outs[1], list(outs[2:2 + nw]), list(outs[2 + nw:2 + 2 * nw]), outs[-1]


def _split_wait(pending, after, make_copy, *, name):
    send_sems, recv_sems, srcs, lands, _ = pending
    nw = len(srcs)

    def body(*refs):
        src_refs, land_refs = refs[:nw], refs[nw:2 * nw]
        send_refs, recv_refs = refs[2 * nw], refs[2 * nw + 1]
        for j in range(3):
            for i in range(nw):
                cp = make_copy(src_refs, land_refs, send_refs, recv_refs, j, i, False)
                cp.wait_send()
                cp.wait_recv()

    outs = pl.pallas_call(
        body, name=name,
        out_shape=(*[pltpu.HBM(a.shape, a.dtype) for a in srcs], *[pltpu.HBM(a.shape, a.dtype) for a in lands]),
        in_specs=[_HBM] * (2 * nw) + [_SEM, _SEM, _ANY], out_specs=tuple([_HBM] * (2 * nw)),
        input_output_aliases={i: i for i in range(2 * nw)},
        compiler_params=pltpu.CompilerParams(has_side_effects=_DATAFLOW),
    )(*srcs, *lands, send_sems, recv_sems, after)
    return list(outs[:nw]), list(outs[nw:])


def _gather_copy(shapes):
    def make(src_refs, land_refs, send_sems, recv_sems, j, i, outgoing):
        x, y, cc = _mesh_pos()
        px, py = _other_chips(x, y)[j]
        rows = _half(shapes[i][0], cc)
        slot = 2 * x + y if outgoing else 2 * px + py
        return pltpu.make_async_remote_copy(
            src_ref=src_refs[i].at[rows], dst_ref=land_refs[i].at[slot, rows],
            send_sem=send_sems.at[j * len(shapes) + i], recv_sem=recv_sems.at[j * len(shapes) + i],
            device_id=(px, py, cc), device_id_type=MESH)
    return make


def _gather_forward(lands, *, name):
    nw = len(lands)

    def body(*refs):
        in_refs, out_refs = refs[:nw], refs[nw:2 * nw]
        send_sems, recv_sems = refs[2 * nw:]
        x, y, cc = _mesh_pos()
        cps = []
        for j, (px, py) in enumerate(_other_chips(x, y)):
            for i in range(nw):
                rows = _half(lands[i].shape[1], cc)
                cps.append(pltpu.make_async_remote_copy(
                    src_ref=in_refs[i].at[2 * px + py, rows], dst_ref=out_refs[i].at[2 * px + py, rows],
                    send_sem=send_sems.at[j, i], recv_sem=recv_sems.at[j, i],
                    device_id=(x, y, 1 - cc), device_id_type=MESH))
        for cp in cps:
            cp.start()
        for cp in cps:
            cp.wait()

    return pl.pallas_call(
        body, out_shape=[SDS(a.shape, a.dtype) for a in lands], in_specs=[_HBM] * nw, out_specs=[_HBM] * nw,
        input_output_aliases={i: i for i in range(nw)},
        scratch_shapes=[pltpu.SemaphoreType.DMA((3, nw)), pltpu.SemaphoreType.DMA((3, nw))], name=name)(*lands)


def _grads_to_sibling(gs, *, name):
    nw = len(gs)

    def body(*refs):
        g_refs, out_refs = refs[:nw], refs[nw:2 * nw]
        send_sems, recv_sems = refs[2 * nw:]
        x, y, cc = _mesh_pos()
        cps = [pltpu.make_async_remote_copy(
            src_ref=g_refs[i].at[pl.ds(0, N_CHIPS), _half(gs[i].shape[1], 1 - cc)], dst_ref=out_refs[i],
            send_sem=send_sems.at[i], recv_sem=recv_sems.at[i], device_id=(x, y, 1 - cc), device_id_type=MESH)
            for i in range(nw)]
        for cp in cps:
            cp.start()
        for cp in cps:
            cp.wait()

    return pl.pallas_call(
        body, out_shape=[SDS((N_CHIPS, g.shape[1] // 2, g.shape[2]), g.dtype) for g in gs],
        in_specs=[_HBM] * nw, out_specs=[_HBM] * nw,
        scratch_shapes=[pltpu.SemaphoreType.DMA((nw,)), pltpu.SemaphoreType.DMA((nw,))], name=name)(*gs)


def _chips_copy(nw):
    def make(src_refs, land_refs, send_sems, recv_sems, j, i, outgoing):
        x, y, cc = _mesh_pos()
        px, py = _other_chips(x, y)[j]
        return pltpu.make_async_remote_copy(
            src_ref=src_refs[i].at[2 * px + py], dst_ref=land_refs[i].at[j],
            send_sem=send_sems.at[j * nw + i], recv_sem=recv_sems.at[j * nw + i],
            device_id=(px, py, cc), device_id_type=MESH)
    return make


def _share_halves(tots, *, name):
    nw = len(tots)

    def body(*refs):
        t_refs, out_refs = refs[:nw], refs[nw:2 * nw]
        send_sems, recv_sems = refs[2 * nw:]
        x, y, cc = _mesh_pos()
        cps = [pltpu.make_async_remote_copy(
            src_ref=t_refs[i], dst_ref=out_refs[i], send_sem=send_sems.at[i], recv_sem=recv_sems.at[i],
            device_id=(x, y, 1 - cc), device_id_type=MESH) for i in range(nw)]
        for cp in cps:
            cp.start()
        for cp in cps:
            cp.wait()

    return pl.pallas_call(
        body, out_shape=[SDS(tt.shape, tt.dtype) for tt in tots],
        in_specs=[_HBM] * nw, out_specs=[_HBM] * nw,
        scratch_shapes=[pltpu.SemaphoreType.DMA((nw,)), pltpu.SemaphoreType.DMA((nw,))], name=name)(*tots)


def _pack_small(arrays):
    flat = jnp.concatenate([a.astype(F32).reshape(-1) for a in arrays])
    n = flat.shape[0]
    padded = -(-n // (8 * LANES)) * (8 * LANES)
    return jnp.pad(flat, (0, padded - n)).reshape(padded // LANES, LANES)


def _unpack_small(flat, shapes):
    out, off = [], 0
    for shp in shapes:
        n = math.prod(shp)
        out.append(flat[off:off + n].reshape(shp))
        off += n
    return out


BIG = (("w_in", 1), ("w_conv_out", 1), ("w_gqa_out", 1), ("w_mla_uq", 1), ("w_mla_ukv", 1), ("w_mla_out", 1),
       ("w_out", 0), ("w_ffn_gate", 1), ("w_ffn_up", 1), ("w_ffn_down", 0))
BIG_NAMES = tuple(nm for nm, _ in BIG)
BIG_GROUPS = (("w_in",), tuple(nm for nm in BIG_NAMES if nm != "w_in"))


def _cols_to_slots(full):
    k, n = full.shape
    return full.reshape(k, N_CHIPS, n // N_CHIPS).transpose(1, 0, 2)


def _slots_to_cols(slots):
    s, k, n = slots.shape
    return slots.transpose(1, 0, 2).reshape(k, s * n)


def _step(p, mom_m, mom_v, x, loss_target):
    b, seq, d = x.shape
    depth = p["w_in"].shape[0]
    l_valid = N_META + seq
    lp = -(-l_valid // LANES) * LANES
    t = b * lp
    c_conv = d // 2
    q_rank = p["mla_q_norm_g"].shape[1]
    kv_rank = p["mla_kv_norm_g"].shape[1]
    qw, kw = GQA_HEADS * HEAD, GQA_KV_HEADS * HEAD
    off_q = 2 * c_conv
    off_k = off_q + qw
    off_v = off_k + kw
    off_cq = off_v + kw
    off_ckv = off_cq + q_rank
    off_gate = off_ckv + kv_rank
    off_kpe = off_gate + N_BRANCH * d
    d_inp = off_kpe + LANES
    assert off_cq % q_rank == 0 and off_ckv % kv_rank == 0
    xi, yi, ci = _mesh_pos()
    chip = 2 * xi + yi

    small_sh = [p["meta_tokens"], p["conv_dw"]]
    small_all = _chip_allgather_small(_pack_small(small_sh), name="gather_small")
    got = [_unpack_small(small_all[s].reshape(-1), [a.shape for a in small_sh]) for s in range(N_CHIPS)]
    meta_full = jnp.concatenate([g[0] for g in got], axis=1)
    conv_dw_full = jnp.concatenate([g[1] for g in got], axis=-1).reshape(depth, CONV_K, c_conv)

    gather_copies = [_gather_copy([p[nm].shape[1:] for nm in grp]) for grp in BIG_GROUPS]
    gathers = []
    order = small_all
    for li in range(depth):
        gathers.append([])
        for gi, grp in enumerate(BIG_GROUPS):
            own = [p[nm][li].astype(BF16) for nm in grp]
            lands = [lax.empty((N_CHIPS,) + o.shape, BF16) for o in own]
            gathers[li].append(_split_start(own, lands, order, gather_copies[gi],
                                            name="gather_start_%d_%d" % (li, gi)))
            order = gathers[li][gi][4]
    meta_full = meta_full + order[0, 0]

    def gathered(li, gi, after):
        own, lands = _split_wait(gathers[li][gi], after, gather_copies[gi], name="gather_wait_%d_%d" % (li, gi))
        got = _gather_forward(lands, name="gather_forward_%d" % gi)
        return {nm: lax.dynamic_update_slice(g, o[None], (chip, 0, 0)) for nm, g, o in zip(BIG_GROUPS[gi], got, own)}

    def in_proj_weight(slots):
        w_in = _slots_to_cols(slots)
        return jnp.concatenate(
            [w_in[:, :off_gate], w_in[:, off_gate + MLA_ROPE:], w_in[:, off_gate:off_gate + MLA_ROPE],
             jnp.zeros((d, LANES - MLA_ROPE), BF16)], axis=1)

    def other_weights(full):
        full["w_out"] = full["w_out"].reshape(-1, d)
        full["w_ffn_down"] = full["w_ffn_down"].reshape(-1, d)
        uq = _slots_to_cols(full["w_mla_uq"]).reshape(q_rank, MLA_HEADS, HEAD + MLA_ROPE)
        full["w_uq"] = jnp.pad(uq, ((0, 0), (0, 0), (0, MLA_SLOT - HEAD - MLA_ROPE))).reshape(
            q_rank, MLA_HEADS * MLA_SLOT)
        ukv = _slots_to_cols(full["w_mla_ukv"]).reshape(kv_rank, MLA_HEADS, 2 * HEAD)
        full["w_uk"] = jnp.pad(ukv[:, :, :HEAD], ((0, 0), (0, 0), (0, MLA_SLOT - HEAD))).reshape(
            kv_rank, MLA_HEADS * MLA_SLOT)
        full["w_uv"] = ukv[:, :, HEAD:].reshape(kv_rank, MLA_HEADS * HEAD)
        return full

    cos_g, sin_g = _rope_tables(lp, seq, HEAD, HEAD)
    cos_m, sin_m = _rope_tables(lp, seq, MLA_ROPE, LANES)
    gqa_scale = 1.0 / math.sqrt(HEAD)
    mla_scale = 1.0 / math.sqrt(HEAD + MLA_ROPE)
    attn_kw = dict(batch=b, lp=lp, l_valid=l_valid)
    gqa_kw = dict(heads=GQA_HEADS, kv_heads=GQA_KV_HEADS, dk=HEAD, dv=HEAD, scale=gqa_scale, **attn_kw)
    mla_kw = dict(heads=MLA_HEADS, kv_heads=MLA_HEADS, dk=MLA_SLOT, dv=HEAD, scale=mla_scale, **attn_kw)

    h = jnp.concatenate([jnp.broadcast_to(meta_full[None], (b, N_META, d)), x,
                         jnp.zeros((b, lp - l_valid, d), F32)], axis=1).reshape(t, d)
    saved, weights = [], []
    for li in range(depth):
        w = {"w_in": in_proj_weight(gathered(li, 0, h)["w_in"])}
        weights.append(w)
        row = lambda nm: p[nm][li][None, :]
        s = {"h_in": h}
        s["u"] = _rmsnorm_fwd(h, row("mix_norm_g"), width=d, cblk=0, name="mix_norm")
        proj = _mm(s["u"], w["w_in"], name="in_proj")
        s["proj"] = proj
        proj3 = proj.reshape(b, lp, d_inp)
        s["yconv"] = _conv_fwd(proj3, conv_dw_full[li], row("conv_b"), c_conv=c_conv, l_valid=l_valid,
                               name="conv").reshape(t, c_conv)
        s["sconv"] = _ln_silu_fwd(s["yconv"], row("conv_ln_g"), row("conv_ln_b"), name="conv_ln_silu")
        w.update(other_weights(gathered(li, 1, s["sconv"])))
        s["ya"] = _mm(s["sconv"], w["w_conv_out"], b_sharded=True, name="conv_out")
        s["gq"], s["gk"], s["gv"] = _gqa_prep_fwd(
            proj, row("gqa_q_norm_g"), row("gqa_k_norm_g"), cos_g, sin_g,
            q_off=off_q, k_off=off_k, v_off=off_v, lp=lp, name="gqa_prep")
        s["go"] = _attn_fwd(s["gq"], s["gk"], s["gv"], name="gqa_attn", **gqa_kw)
        s["yb"] = _mm(s["go"], w["w_gqa_out"], b_sharded=True, name="gqa_out")
        s["cqn"] = _rmsnorm_fwd(proj, row("mla_q_norm_g"), width=q_rank, cblk=off_cq // q_rank, name="mla_q_norm")
        s["kvn"] = _rmsnorm_fwd(proj, row("mla_kv_norm_g"), width=kv_rank, cblk=off_ckv // kv_rank,
                                name="mla_kv_norm")
        q_pre = _mm(s["cqn"], w["w_uq"], name="mla_uq")
        k_pre = _mm(s["kvn"], w["w_uk"], name="mla_uk")
        s["mv"] = _mm(s["kvn"], w["w_uv"], out_dtype=BF16, name="mla_uv")
        s["mq"], s["mk"] = _mla_rope_fwd(q_pre, k_pre, proj, cos_m, sin_m, kpe_off=off_kpe, lp=lp, name="mla_rope")
        s["mo"] = _attn_fwd(s["mq"], s["mk"], s["mv"], name="mla_attn", **mla_kw)
        s["yc"] = _mm(s["mo"], w["w_mla_out"], b_sharded=True, name="mla_out")
        s["merged"] = _merge_fwd(proj, row("gate_b"), s["ya"], s["yb"], s["yc"], gate_off=off_gate, name="merge")
        h = _mm(s["merged"], w["w_out"], add=h, name="mix_out")
        s["h_mid"] = h
        s["v"] = _rmsnorm_fwd(h, row("ffn_norm_g"), width=d, cblk=0, name="ffn_norm")
        s["gate"] = _mm(s["v"], w["w_ffn_gate"], b_sharded=True, name="ffn_gate")
        s["up"] = _mm(s["v"], w["w_ffn_up"], b_sharded=True, name="ffn_up")
        s["act"] = _swiglu_fwd(s["gate"], s["up"], name="swiglu")
        h = _mm(s["act"], w["w_ffn_down"], add=h, name="ffn_down")
        saved.append(s)

    target = jnp.pad(loss_target, ((0, 0), (N_META, lp - l_valid), (0, 0))).reshape(t, d)
    loss_part, dh, dhb, g_final = _loss_head(h, p["final_norm_g"][None, :], target, lp=lp, l_valid=l_valid,
                                             name="loss_head")
    loss = lax.psum(loss_part[0, 0], ("x", "y", "c"))

    small_names = ["mix_norm_g", "conv_dw", "conv_b", "conv_ln_g", "conv_ln_b", "gqa_q_norm_g", "gqa_k_norm_g",
                   "mla_q_norm_g", "mla_kv_norm_g", "gate_b", "ffn_norm_g"]
    small_g = {nm: [None] * depth for nm in small_names}
    shard_g = {nm: [None] * depth for nm in BIG_NAMES}
    chips_copies = [_chips_copy(len(grp)) for grp in BIG_GROUPS]

    def start_reduce(lj, gi, big):
        grp = BIG_GROUPS[gi]
        glist = [big[nm] for nm in grp]
        from_sib = _grads_to_sibling(glist, name="grads_to_sibling_%d" % gi)
        parts = []
        for nm, g, fs in zip(grp, glist, from_sib):
            hk = g.shape[1] // 2
            mine = lax.dynamic_slice_in_dim(g, ci * hk, hk, axis=1)
            rows = (1, N_CHIPS * hk, g.shape[2])
            parts.append(_sum_slots([mine.reshape(rows), fs.reshape(rows)], out_dtype=BF16,
                                    name="sum_sibling_" + nm).reshape(fs.shape))
        lands = [lax.empty((3,) + pt.shape[1:], BF16) for pt in parts]
        return lj, gi, _split_start(parts, lands, order, chips_copies[gi], name="chips_start_%d_%d" % (lj, gi))

    def finish_reduce(item, after):
        lj, gi, pending = item
        grp = BIG_GROUPS[gi]
        parts, from_chips = _split_wait(pending, after, chips_copies[gi], name="chips_wait_%d_%d" % (lj, gi))
        tots = [_sum_slots([lax.dynamic_index_in_dim(pt, chip, 0, keepdims=True), fc], out_dtype=F32,
                           name="sum_chips_" + nm) for nm, pt, fc in zip(grp, parts, from_chips)]
        others = _share_halves(tots, name="share_halves_%d" % gi)
        for nm, tot, oth in zip(grp, tots, others):
            shard_g[nm][lj] = jnp.concatenate([jnp.where(ci == 0, tot, oth), jnp.where(ci == 0, oth, tot)], axis=0)

    def behind(item):
        return item[2][4][0, 0].astype(BF16)

    in_flight = []
    for li in reversed(range(depth)):
        w, s = weights[li], saved[li]
        row = lambda nm: p[nm][li][None, :]
        big = {}
        for item in in_flight:
            dhb = dhb + behind(item)
        dact = _mm(dhb, w["w_ffn_down"], mode="nt", name="d_ffn_act")
        big["w_ffn_down"] = _mm(s["act"], dhb, mode="tn", out_dtype=BF16, name="dw_ffn_down").reshape(N_CHIPS, -1, d)
        dgate, dup = _swiglu_bwd(s["gate"], s["up"], dact, name="d_swiglu")
        dv = _mm(dgate, w["w_ffn_gate"], mode="nt", b_sharded=True, name="d_ffn_v_gate")
        dv = _mm(dup, w["w_ffn_up"], mode="nt", b_sharded=True, add=dv, name="d_ffn_v_up")
        big["w_ffn_gate"] = _mm(s["v"], dgate, mode="tn", out_dtype=BF16, out_shards=N_CHIPS, name="dw_ffn_gate")
        big["w_ffn_up"] = _mm(s["v"], dup, mode="tn", out_dtype=BF16, out_shards=N_CHIPS, name="dw_ffn_up")
        dh, dhb, g = _rmsnorm_bwd(s["h_mid"], row("ffn_norm_g"), dv, width=d, cblk=0, res=dh, name="d_ffn_norm")
        small_g["ffn_norm_g"][li] = g[0]
        dm = _mm(dhb, w["w_out"], mode="nt", name="d_merged")
        big["w_out"] = _mm(s["merged"], dhb, mode="tn", out_dtype=BF16, name="dw_out").reshape(N_CHIPS, -1, d)
        dya, dyb, dyc, dl0, dl1, dl2, db0, db1, db2 = _merge_bwd(
            s["proj"], row("gate_b"), s["ya"], s["yb"], s["yc"], dm, gate_off=off_gate, name="d_merge")
        small_g["gate_b"][li] = jnp.concatenate([db0[0], db1[0], db2[0]])
        ds = _mm(dya, w["w_conv_out"], mode="nt", b_sharded=True, name="d_conv_s")
        big["w_conv_out"] = _mm(s["sconv"], dya, mode="tn", out_dtype=BF16, out_shards=N_CHIPS, name="dw_conv_out")
        dyconv, g, gb = _ln_silu_bwd(s["yconv"], row("conv_ln_g"), row("conv_ln_b"), ds, name="d_conv_ln_silu")
        small_g["conv_ln_g"][li], small_g["conv_ln_b"][li] = g[0], gb[0]
        proj3 = s["proj"].reshape(b, lp, d_inp)
        da, dgte, ddw, dcb = _conv_bwd(proj3, conv_dw_full[li], dyconv.reshape(b, lp, c_conv), c_conv=c_conv,
                                       l_valid=l_valid, name="d_conv")
        small_g["conv_dw"][li], small_g["conv_b"][li] = ddw, dcb[0]
        dgo = _mm(dyb, w["w_gqa_out"], mode="nt", b_sharded=True, out_dtype=BF16, name="d_gqa_o")
        big["w_gqa_out"] = _mm(s["go"], dyb, mode="tn", out_dtype=BF16, out_shards=N_CHIPS, name="dw_gqa_out")
        dgq, dgk, dgv = _attn_bwd(s["gq"], s["gk"], s["gv"], dgo, name="d_gqa_attn", **gqa_kw)
        dq_g, dk_g, gq, gk = _gqa_prep_bwd(s["proj"], row("gqa_q_norm_g"), row("gqa_k_norm_g"), cos_g, sin_g,
                                           dgq, dgk, q_off=off_q, k_off=off_k, lp=lp, name="d_gqa_prep")
        small_g["gqa_q_norm_g"][li], small_g["gqa_k_norm_g"][li] = gq[0], gk[0]
        dmo = _mm(dyc, w["w_mla_out"], mode="nt", b_sharded=True, out_dtype=BF16, name="d_mla_o")
        big["w_mla_out"] = _mm(s["mo"], dyc, mode="tn", out_dtype=BF16, out_shards=N_CHIPS, name="dw_mla_out")
        dmq, dmk, dmv = _attn_bwd(s["mq"], s["mk"], s["mv"], dmo, name="d_mla_attn", **mla_kw)
        dq_pre, dk_pre, dkpe = _mla_rope_bwd(dmq, dmk, cos_m, sin_m, lp=lp, name="d_mla_rope")
        dmvb = dmv.astype(BF16)
        g_uq = _mm(s["cqn"], dq_pre, mode="tn", out_dtype=BF16, name="dw_mla_uq")
        g_uk = _mm(s["kvn"], dk_pre, mode="tn", out_dtype=BF16, name="dw_mla_uk")
        g_uv = _mm(s["kvn"], dmvb, mode="tn", out_dtype=BF16, name="dw_mla_uv")
        big["w_mla_uq"] = _cols_to_slots(
            g_uq.reshape(q_rank, MLA_HEADS, MLA_SLOT)[:, :, :HEAD + MLA_ROPE].reshape(q_rank, -1))
        big["w_mla_ukv"] = _cols_to_slots(jnp.concatenate(
            [g_uk.reshape(kv_rank, MLA_HEADS, MLA_SLOT)[:, :, :HEAD], g_uv.reshape(kv_rank, MLA_HEADS, HEAD)],
            axis=-1).reshape(kv_rank, -1))
        dcqn = _mm(dq_pre, w["w_uq"], mode="nt", name="d_mla_cqn")
        dkvn = _mm(dk_pre, w["w_uk"], mode="nt", name="d_mla_kvn_k")
        dkvn = _mm(dmvb, w["w_uv"], mode="nt", add=dkvn, name="d_mla_kvn_v")
        _, dcq, g = _rmsnorm_bwd(s["proj"], row("mla_q_norm_g"), dcqn, width=q_rank, cblk=off_cq // q_rank,
                                 name="d_mla_q_norm")
        small_g["mla_q_norm_g"][li] = g[0]
        _, dckv, g = _rmsnorm_bwd(s["proj"], row("mla_kv_norm_g"), dkvn, width=kv_rank, cblk=off_ckv // kv_rank,
                                  name="d_mla_kv_norm")
        small_g["mla_kv_norm_g"][li] = g[0]
        for item in in_flight:
            finish_reduce(item, dckv)
        in_flight = [start_reduce(li, 1, big)]
        dproj = jnp.concatenate(
            [da.reshape(t, c_conv), dgte.reshape(t, c_conv), dq_g, dk_g, dgv.astype(BF16), dcq, dckv,
             dl0, dl1, dl2, dkpe], axis=1) + behind(in_flight[0])
        du = _mm(dproj, w["w_in"], mode="nt", name="d_u")
        g_in = _mm(s["u"], dproj, mode="tn", out_dtype=BF16, name="dw_in")
        big["w_in"] = _cols_to_slots(jnp.concatenate(
            [g_in[:, :off_gate], g_in[:, off_kpe:off_kpe + MLA_ROPE], g_in[:, off_gate:off_kpe]], axis=1))
        dh, dhb, g = _rmsnorm_bwd(s["h_in"], row("mix_norm_g"), du, width=d, cblk=0, res=dh, name="d_mix_norm")
        small_g["mix_norm_g"][li] = g[0]
        in_flight.append(start_reduce(li, 0, big))

    dh3 = dh.reshape(b, lp, d)
    grad_x = dh3[:, N_META:l_valid]
    g_meta = _meta_grad(dh3, name="meta_grad")
    for item in in_flight:
        finish_reduce(item, g_meta)

    small_list = [g_meta, g_final[0]] + [jnp.stack(small_g[nm]) for nm in small_names]
    red = _allreduce_small(_pack_small(small_list), name="allreduce_small")
    red = _unpack_small(red.reshape(-1), [a.shape for a in small_list])
    grads = {"meta_tokens": lax.dynamic_slice_in_dim(red[0], chip * (d // N_CHIPS), d // N_CHIPS, axis=1),
             "final_norm_g": red[1]}
    for nm, val in zip(small_names, red[2:]):
        grads[nm] = val
    cs = c_conv // N_CHIPS
    grads["conv_dw"] = lax.dynamic_slice_in_dim(grads["conv_dw"], chip * cs, cs, axis=2).reshape(p["conv_dw"].shape)
    for nm in BIG_NAMES:
        grads[nm] = jnp.stack(shard_g[nm])

    delta, new_m, new_v = {}, {}, {}
    for nm in p:
        delta[nm], new_m[nm], new_v[nm] = _adamw(p[nm], grads[nm], mom_m[nm], mom_v[nm], name="adamw_" + nm)
    return loss, grad_x, grads, delta, new_m, new_v


WEIGHT_NAMES = ("meta_tokens", "mix_norm_g", "w_in", "conv_dw", "conv_b", "conv_ln_g", "conv_ln_b", "w_conv_out",
                "gqa_q_norm_g", "gqa_k_norm_g", "w_gqa_out", "mla_q_norm_g", "w_mla_uq", "mla_kv_norm_g",
                "w_mla_ukv", "w_mla_out", "gate_b", "w_out", "ffn_norm_g", "w_ffn_gate", "w_ffn_up", "w_ffn_down",
                "final_norm_g")


def kernel(x, meta_tokens, mix_norm_g, w_in, conv_dw, conv_b, conv_ln_g, conv_ln_b, w_conv_out, gqa_q_norm_g, gqa_k_norm_g, w_gqa_out, mla_q_norm_g, w_mla_uq, mla_kv_norm_g, w_mla_ukv, w_mla_out, gate_b, w_out, ffn_norm_g, w_ffn_gate, w_ffn_up, w_ffn_down, final_norm_g, loss_target, m_meta_tokens, m_mix_norm_g, m_w_in, m_conv_dw, m_conv_b, m_conv_ln_g, m_conv_ln_b, m_w_conv_out, m_gqa_q_norm_g, m_gqa_k_norm_g, m_w_gqa_out, m_mla_q_norm_g, m_w_mla_uq, m_mla_kv_norm_g, m_w_mla_ukv, m_w_mla_out, m_gate_b, m_w_out, m_ffn_norm_g, m_w_ffn_gate, m_w_ffn_up, m_w_ffn_down, m_final_norm_g, v_meta_tokens, v_mix_norm_g, v_w_in, v_conv_dw, v_conv_b, v_conv_ln_g, v_conv_ln_b, v_w_conv_out, v_gqa_q_norm_g, v_gqa_k_norm_g, v_w_gqa_out, v_mla_q_norm_g, v_w_mla_uq, v_mla_kv_norm_g, v_w_mla_ukv, v_w_mla_out, v_gate_b, v_w_out, v_ffn_norm_g, v_w_ffn_gate, v_w_ffn_up, v_w_ffn_down, v_final_norm_g):
    ws = (meta_tokens, mix_norm_g, w_in, conv_dw, conv_b, conv_ln_g, conv_ln_b, w_conv_out, gqa_q_norm_g,
          gqa_k_norm_g, w_gqa_out, mla_q_norm_g, w_mla_uq, mla_kv_norm_g, w_mla_ukv, w_mla_out, gate_b, w_out,
          ffn_norm_g, w_ffn_gate, w_ffn_up, w_ffn_down, final_norm_g)
    ms = (m_meta_tokens, m_mix_norm_g, m_w_in, m_conv_dw, m_conv_b, m_conv_ln_g, m_conv_ln_b, m_w_conv_out,
          m_gqa_q_norm_g, m_gqa_k_norm_g, m_w_gqa_out, m_mla_q_norm_g, m_w_mla_uq, m_mla_kv_norm_g, m_w_mla_ukv,
          m_w_mla_out, m_gate_b, m_w_out, m_ffn_norm_g, m_w_ffn_gate, m_w_ffn_up, m_w_ffn_down, m_final_norm_g)
    vs = (v_meta_tokens, v_mix_norm_g, v_w_in, v_conv_dw, v_conv_b, v_conv_ln_g, v_conv_ln_b, v_w_conv_out,
          v_gqa_q_norm_g, v_gqa_k_norm_g, v_w_gqa_out, v_mla_q_norm_g, v_w_mla_uq, v_mla_kv_norm_g, v_w_mla_ukv,
          v_w_mla_out, v_gate_b, v_w_out, v_ffn_norm_g, v_w_ffn_gate, v_w_ffn_up, v_w_ffn_down, v_final_norm_g)
    p = dict(zip(WEIGHT_NAMES, ws))
    loss, grad_x, grads, delta, new_m, new_v = _step(p, dict(zip(WEIGHT_NAMES, ms)), dict(zip(WEIGHT_NAMES, vs)),
                                                     x, loss_target)
    return (loss, grad_x, *[grads[n] for n in WEIGHT_NAMES], *[delta[n] for n in WEIGHT_NAMES],
            *[new_m[n] for n in WEIGHT_NAMES], *[new_v[n] for n in WEIGHT_NAMES])
```

```python
import math

import jax
import jax.numpy as jnp
from jax import lax
from jax.experimental import pallas as pl
from jax.experimental.pallas import tpu as pltpu

F32 = jnp.float32
BF16 = jnp.bfloat16
SDS = jax.ShapeDtypeStruct
MESH = pl.DeviceIdType.MESH

N_META = 16
GRID_W = 64
ROPE_THETA = 10000.0
NORM_EPS = 1e-6
CONV_K = 31
CONV_HALO = 16
HEAD = 128
GQA_HEADS = 8
GQA_KV_HEADS = 2
MLA_HEADS = 8
MLA_ROPE = 64
MLA_SLOT = 256
N_BRANCH = 3

ADAM_LR = 0.001
ADAM_B1 = 0.9
ADAM_B2 = 0.999
ADAM_EPS = 1e-08
ADAM_WD = 0.01
ADAM_STEP = 10

LANES = 128
SUBLANES_BF16 = 16
VMEM_LIMIT_BYTES = 56 * 2 ** 20
ROW_BLOCK_BYTES = 2 << 20
ATTN_Q_ROWS = 1088
N_CHIPS = 4
N_DEV = 8


def _cparams():
    return pltpu.CompilerParams(vmem_limit_bytes=VMEM_LIMIT_BYTES)


def _tile(n, cap, mult):
    best = None
    for d in range(mult, min(n, cap) + 1, mult):
        if n % d == 0:
            best = d
    return n if best is None else best


def _row_tile(rows, width, mult=SUBLANES_BF16):
    return _tile(rows, max(mult, ROW_BLOCK_BYTES // (4 * width)), mult)


def _sigmoid(x):
    return 1.0 / (1.0 + jnp.exp(-x))


def _mm(a, b, *, mode="nn", add=None, out_dtype=F32, b_sharded=False, out_shards=None, deps=(), name):
    if b_sharded:
        s_b, r_b, c_b = b.shape
        b_rows, b_cols = r_b, s_b * c_b
    else:
        b_rows, b_cols = b.shape
    if mode == "nn":
        (m, k), n = a.shape, b_cols
        assert b_rows == k
    elif mode == "nt":
        (m, k), n = a.shape, b_rows
        assert b_cols == k
    else:
        (k, m), n = a.shape, b_cols
        assert b_rows == k
    n_unit = n
    if b_sharded and mode != "nt":
        n_unit = c_b
    if out_shards is not None:
        assert n % out_shards == 0
        n_unit = math.gcd(n_unit, n // out_shards)
    k_unit = c_b if (b_sharded and mode == "nt") else k
    if mode == "tn":
        tm = _tile(m, 1024, LANES)
        tk = _tile(k_unit, 2176, SUBLANES_BF16)
    else:
        tm = _tile(m, 1088, SUBLANES_BF16)
        tk = _tile(k_unit, 2176, LANES)
    tn = _tile(n_unit, 1408, LANES)
    nk = k // tk
    if mode == "nn":
        a_spec = pl.BlockSpec((tm, tk), lambda i, j, kk: (i, kk))
        dn = (((1,), (0,)), ((), ()))
        if b_sharded:
            per = c_b // tn
            b_spec = pl.BlockSpec((None, tk, tn), lambda i, j, kk: (j // per, kk, j % per))
        else:
            b_spec = pl.BlockSpec((tk, tn), lambda i, j, kk: (kk, j))
    elif mode == "nt":
        a_spec = pl.BlockSpec((tm, tk), lambda i, j, kk: (i, kk))
        dn = (((1,), (1,)), ((), ()))
        if b_sharded:
            per = c_b // tk
            b_spec = pl.BlockSpec((None, tn, tk), lambda i, j, kk: (kk // per, j, kk % per))
        else:
            b_spec = pl.BlockSpec((tn, tk), lambda i, j, kk: (j, kk))
    else:
        a_spec = pl.BlockSpec((tk, tm), lambda i, j, kk: (kk, i))
        dn = (((0,), (0,)), ((), ()))
        if b_sharded:
            per = c_b // tn
            b_spec = pl.BlockSpec((None, tk, tn), lambda i, j, kk: (j // per, kk, j % per))
        else:
            b_spec = pl.BlockSpec((tk, tn), lambda i, j, kk: (kk, j))
    if out_shards is None:
        o_spec = pl.BlockSpec((tm, tn), lambda i, j, kk: (i, j))
        out_shape = SDS((m, n), out_dtype)
    else:
        per_o = (n // out_shards) // tn
        o_spec = pl.BlockSpec((None, tm, tn), lambda i, j, kk: (j // per_o, i, j % per_o))
        out_shape = SDS((out_shards, m, n // out_shards), out_dtype)
    has_add = add is not None
    assert not (has_add and out_shards is not None)
    deps = list(deps)
    use_acc = nk > 1 and out_dtype != F32

    def body(a_ref, b_ref, *rest):
        add_ref = rest[0] if has_add else None
        o_ref = rest[int(has_add) + len(deps)]
        kk = pl.program_id(2)
        part = lax.dot_general(a_ref[...].astype(BF16), b_ref[...].astype(BF16), dn,
                               preferred_element_type=F32)
        if nk == 1:
            o_ref[...] = (part + add_ref[...] if has_add else part).astype(o_ref.dtype)
            return
        acc = rest[-1] if use_acc else o_ref

        @pl.when(kk == 0)
        def _():
            acc[...] = part

        @pl.when((kk > 0) & (kk < nk - 1))
        def _():
            acc[...] += part

        @pl.when(kk == nk - 1)
        def _():
            r = acc[...] + part
            if has_add:
                r = r + add_ref[...]
            o_ref[...] = r.astype(o_ref.dtype)

    in_specs = [a_spec, b_spec] + ([o_spec] if has_add else []) + [_ANY] * len(deps)
    args = (a, b) + ((add,) if has_add else ()) + tuple(deps)
    return pl.pallas_call(
        body, grid=(m // tm, n // tn, nk), in_specs=in_specs, out_specs=o_spec,
        out_shape=out_shape, scratch_shapes=[pltpu.VMEM((tm, tn), F32)] if use_acc else [],
        compiler_params=_cparams(), name=name)(*args)


def _rmsnorm_fwd(x, g, *, width, cblk, name):
    t = x.shape[0]
    tr = _row_tile(t, width)

    def body(x_ref, g_ref, o_ref):
        xv = x_ref[...]
        r = lax.rsqrt(jnp.mean(xv * xv, axis=-1, keepdims=True) + NORM_EPS)
        o_ref[...] = (xv * r * g_ref[...]).astype(o_ref.dtype)

    return pl.pallas_call(
        body, grid=(t // tr,),
        in_specs=[pl.BlockSpec((tr, width), lambda i: (i, cblk)), pl.BlockSpec((1, width), lambda i: (0, 0))],
        out_specs=pl.BlockSpec((tr, width), lambda i: (i, 0)),
        out_shape=SDS((t, width), BF16), compiler_params=_cparams(), name=name)(x, g)


def _rmsnorm_bwd(x, g, dy, *, width, cblk, res=None, name):
    t = x.shape[0]
    tr = _row_tile(t, width)
    has_res = res is not None

    def body(x_ref, g_ref, dy_ref, *rest):
        if has_res:
            res_ref, dx_ref, dxb_ref, dg_ref = rest
        else:
            dx_ref, dxb_ref, dg_ref = rest
        xv = x_ref[...]
        dyv = dy_ref[...].astype(F32)
        r = lax.rsqrt(jnp.mean(xv * xv, axis=-1, keepdims=True) + NORM_EPS)
        gy = dyv * g_ref[...]
        dx = r * gy - xv * (r * r * r) * jnp.mean(gy * xv, axis=-1, keepdims=True)
        if has_res:
            dx = dx + res_ref[...]
        dx_ref[...] = dx
        dxb_ref[...] = dx.astype(BF16)
        part = jnp.sum(dyv * xv * r, axis=0, keepdims=True)

        @pl.when(pl.program_id(0) == 0)
        def _():
            dg_ref[...] = part

        @pl.when(pl.program_id(0) > 0)
        def _():
            dg_ref[...] += part

    row = pl.BlockSpec((tr, width), lambda i: (i, 0))
    vec = pl.BlockSpec((1, width), lambda i: (0, 0))
    in_specs = [pl.BlockSpec((tr, width), lambda i: (i, cblk)), vec, row] + ([row] if has_res else [])
    args = (x, g, dy) + ((res,) if has_res else ())
    return pl.pallas_call(
        body, grid=(t // tr,), in_specs=in_specs, out_specs=[row, row, vec],
        out_shape=[SDS((t, width), F32), SDS((t, width), BF16), SDS((1, width), F32)],
        compiler_params=_cparams(), name=name)(*args)


def _valid_rows(tile_index, tr, lp, lo, hi):
    pos = (tile_index % (lp // tr)) * tr + lax.broadcasted_iota(jnp.int32, (tr, 1), 0)
    return (pos >= lo) & (pos < hi)


def _loss_head(h, g, target, *, lp, l_valid, name):
    t, d = h.shape
    tr = _tile(lp, max(SUBLANES_BF16, ROW_BLOCK_BYTES // (4 * d)), SUBLANES_BF16)

    def body(x_ref, g_ref, t_ref, loss_ref, dx_ref, dxb_ref, dg_ref):
        i = pl.program_id(0)
        xv = x_ref[...]
        r = lax.rsqrt(jnp.mean(xv * xv, axis=-1, keepdims=True) + NORM_EPS)
        y = xv * r * g_ref[...]
        valid = _valid_rows(i, tr, lp, N_META, l_valid)
        err = jnp.where(valid, y - t_ref[...], 0.0)
        lpart = 0.5 * jnp.sum(jnp.mean(err * err, axis=-1, keepdims=True), axis=0, keepdims=True)
        dyv = err * (1.0 / d)
        gy = dyv * g_ref[...]
        dx = r * gy - xv * (r * r * r) * jnp.mean(gy * xv, axis=-1, keepdims=True)
        dx_ref[...] = dx
        dxb_ref[...] = dx.astype(BF16)
        gpart = jnp.sum(dyv * xv * r, axis=0, keepdims=True)

        @pl.when(i == 0)
        def _():
            dg_ref[...] = gpart
            loss_ref[...] = jnp.broadcast_to(lpart, loss_ref.shape)

        @pl.when(i > 0)
        def _():
            dg_ref[...] += gpart
            loss_ref[...] += jnp.broadcast_to(lpart, loss_ref.shape)

    row = pl.BlockSpec((tr, d), lambda i: (i, 0))
    vec = pl.BlockSpec((1, d), lambda i: (0, 0))
    return pl.pallas_call(
        body, grid=(t // tr,), in_specs=[row, vec, row],
        out_specs=[pl.BlockSpec((1, LANES), lambda i: (0, 0)), row, row, vec],
        out_shape=[SDS((1, LANES), F32), SDS((t, d), F32), SDS((t, d), BF16), SDS((1, d), F32)],
        compiler_params=_cparams(), name=name)(h, g, target)


def _ln_silu_fwd(yc, g, b, *, name):
    t, c = yc.shape
    tr = _row_tile(t, c)

    def body(x_ref, g_ref, b_ref, o_ref):
        xv = x_ref[...]
        xc = xv - jnp.mean(xv, axis=-1, keepdims=True)
        r = lax.rsqrt(jnp.mean(xc * xc, axis=-1, keepdims=True) + NORM_EPS)
        ln = xc * r * g_ref[...] + b_ref[...]
        o_ref[...] = (ln * _sigmoid(ln)).astype(o_ref.dtype)

    row = pl.BlockSpec((tr, c), lambda i: (i, 0))
    vec = pl.BlockSpec((1, c), lambda i: (0, 0))
    return pl.pallas_call(body, grid=(t // tr,), in_specs=[row, vec, vec], out_specs=row,
                          out_shape=SDS((t, c), BF16), compiler_params=_cparams(), name=name)(yc, g, b)


def _ln_silu_bwd(yc, g, b, ds, *, name):
    t, c = yc.shape
    tr = _row_tile(t, c)

    def body(x_ref, g_ref, b_ref, ds_ref, dx_ref, dg_ref, db_ref):
        xv = x_ref[...]
        xc = xv - jnp.mean(xv, axis=-1, keepdims=True)
        r = lax.rsqrt(jnp.mean(xc * xc, axis=-1, keepdims=True) + NORM_EPS)
        xh = xc * r
        ln = xh * g_ref[...] + b_ref[...]
        sg = _sigmoid(ln)
        dln = ds_ref[...].astype(F32) * (sg * (1.0 + ln * (1.0 - sg)))
        gy = dln * g_ref[...]
        dx_ref[...] = r * (gy - jnp.mean(gy, axis=-1, keepdims=True)
                           - xh * jnp.mean(gy * xh, axis=-1, keepdims=True))
        gpart = jnp.sum(dln * xh, axis=0, keepdims=True)
        bpart = jnp.sum(dln, axis=0, keepdims=True)

        @pl.when(pl.program_id(0) == 0)
        def _():
            dg_ref[...] = gpart
            db_ref[...] = bpart

        @pl.when(pl.program_id(0) > 0)
        def _():
            dg_ref[...] += gpart
            db_ref[...] += bpart

    row = pl.BlockSpec((tr, c), lambda i: (i, 0))
    vec = pl.BlockSpec((1, c), lambda i: (0, 0))
    return pl.pallas_call(
        body, grid=(t // tr,), in_specs=[row, vec, vec, row], out_specs=[row, vec, vec],
        out_shape=[SDS((t, c), F32), SDS((1, c), F32), SDS((1, c), F32)],
        compiler_params=_cparams(), name=name)(yc, g, b, ds)


def _swiglu_fwd(gate, up, *, name):
    t, f = gate.shape
    tr = _row_tile(t, f)

    def body(g_ref, u_ref, o_ref):
        gv = g_ref[...].astype(F32)
        o_ref[...] = (gv * _sigmoid(gv) * u_ref[...].astype(F32)).astype(o_ref.dtype)

    row = pl.BlockSpec((tr, f), lambda i: (i, 0))
    return pl.pallas_call(body, grid=(t // tr,), in_specs=[row, row], out_specs=row,
                          out_shape=SDS((t, f), BF16), compiler_params=_cparams(), name=name)(gate, up)


def _swiglu_bwd(gate, up, dact, *, name):
    t, f = gate.shape
    tr = _row_tile(t, f)

    def body(g_ref, u_ref, d_ref, dg_ref, du_ref):
        gv = g_ref[...].astype(F32)
        sg = _sigmoid(gv)
        dv = d_ref[...]
        dg_ref[...] = (dv * u_ref[...].astype(F32) * (sg * (1.0 + gv * (1.0 - sg)))).astype(dg_ref.dtype)
        du_ref[...] = (dv * (gv * sg)).astype(du_ref.dtype)

    row = pl.BlockSpec((tr, f), lambda i: (i, 0))
    return pl.pallas_call(body, grid=(t // tr,), in_specs=[row, row, row], out_specs=[row, row],
                          out_shape=[SDS((t, f), BF16), SDS((t, f), BF16)],
                          compiler_params=_cparams(), name=name)(gate, up, dact)


def _merge_tiles(t, d):
    tc = _tile(d, 512, LANES)
    return _row_tile(t, tc), tc


def _merge_fwd(proj, gate_b, ya, yb, yc, *, gate_off, name):
    t, d = ya.shape
    tr, tc = _merge_tiles(t, d)
    nc = d // tc
    assert gate_off % tc == 0
    g0 = gate_off // tc

    def body(l0, l1, l2, b0, b1, b2, ya_ref, yb_ref, yc_ref, o_ref):
        acc = _sigmoid(l0[...] + b0[...]) * ya_ref[...]
        acc += _sigmoid(l1[...] + b1[...]) * yb_ref[...]
        acc += _sigmoid(l2[...] + b2[...]) * yc_ref[...]
        o_ref[...] = acc.astype(o_ref.dtype)

    lspec = [pl.BlockSpec((tr, tc), lambda i, j, br=br: (i, g0 + br * nc + j)) for br in range(N_BRANCH)]
    bspec = [pl.BlockSpec((1, tc), lambda i, j, br=br: (0, br * nc + j)) for br in range(N_BRANCH)]
    yspec = pl.BlockSpec((tr, tc), lambda i, j: (i, j))
    return pl.pallas_call(
        body, grid=(t // tr, nc), in_specs=lspec + bspec + [yspec] * 3, out_specs=yspec,
        out_shape=SDS((t, d), BF16), compiler_params=_cparams(), name=name)(
            proj, proj, proj, gate_b, gate_b, gate_b, ya, yb, yc)


def _merge_bwd(proj, gate_b, ya, yb, yc, dm, *, gate_off, name):
    t, d = ya.shape
    tr, tc = _merge_tiles(t, d)
    nc = d // tc
    g0 = gate_off // tc

    def body(l0, l1, l2, b0, b1, b2, ya_ref, yb_ref, yc_ref, dm_ref,
             dya, dyb, dyc, dl0, dl1, dl2, db0, db1, db2):
        i = pl.program_id(1)
        dmv = dm_ref[...]
        for l_ref, b_ref, y_ref, dy_ref, dl_ref, db_ref in (
                (l0, b0, ya_ref, dya, dl0, db0), (l1, b1, yb_ref, dyb, dl1, db1), (l2, b2, yc_ref, dyc, dl2, db2)):
            gt = _sigmoid(l_ref[...] + b_ref[...])
            dy_ref[...] = (dmv * gt).astype(dy_ref.dtype)
            dl = dmv * y_ref[...] * gt * (1.0 - gt)
            dl_ref[...] = dl.astype(dl_ref.dtype)
            part = jnp.sum(dl, axis=0, keepdims=True)

            @pl.when(i == 0)
            def _(db_ref=db_ref, part=part):
                db_ref[...] = part

            @pl.when(i > 0)
            def _(db_ref=db_ref, part=part):
                db_ref[...] += part

    lspec = [pl.BlockSpec((tr, tc), lambda j, i, br=br: (i, g0 + br * nc + j)) for br in range(N_BRANCH)]
    bspec = [pl.BlockSpec((1, tc), lambda j, i, br=br: (0, br * nc + j)) for br in range(N_BRANCH)]
    yspec = pl.BlockSpec((tr, tc), lambda j, i: (i, j))
    vspec = pl.BlockSpec((1, tc), lambda j, i: (0, j))
    return pl.pallas_call(
        body, grid=(nc, t // tr), in_specs=lspec + bspec + [yspec] * 4,
        out_specs=[yspec] * 6 + [vspec] * 3,
        out_shape=[SDS((t, d), BF16)] * 6 + [SDS((1, d), F32)] * 3,
        compiler_params=_cparams(), name=name)(proj, proj, proj, gate_b, gate_b, gate_b, ya, yb, yc, dm)


def _conv_fwd(proj3, dw, cb, *, c_conv, l_valid, name):
    b, lp, _ = proj3.shape
    cw = _tile(c_conv, 256, LANES)
    nc = c_conv // cw

    def body(a_ref, gte_ref, dw_ref, cb_ref, o_ref, zs):
        rows = lax.broadcasted_iota(jnp.int32, (lp, 1), 0)
        z = jnp.where(rows < l_valid, a_ref[...] * _sigmoid(gte_ref[...]), 0.0)
        zs[pl.ds(0, CONV_HALO), :] = jnp.zeros((CONV_HALO, cw), F32)
        zs[pl.ds(CONV_HALO, lp), :] = z
        zs[pl.ds(CONV_HALO + lp, CONV_HALO), :] = jnp.zeros((CONV_HALO, cw), F32)
        acc = jnp.broadcast_to(cb_ref[...], (lp, cw))
        for j in range(CONV_K):
            acc = acc + zs[pl.ds(j + 1, lp), :] * dw_ref[pl.ds(j, 1), :]
        o_ref[...] = acc

    seq = lambda off: pl.BlockSpec((None, lp, cw), lambda bi, ci: (bi, 0, off + ci))
    return pl.pallas_call(
        body, grid=(b, nc),
        in_specs=[seq(0), seq(nc), pl.BlockSpec((CONV_K, cw), lambda bi, ci: (0, ci)),
                  pl.BlockSpec((1, cw), lambda bi, ci: (0, ci))],
        out_specs=seq(0), out_shape=SDS((b, lp, c_conv), F32),
        scratch_shapes=[pltpu.VMEM((lp + 2 * CONV_HALO, cw), F32)],
        compiler_params=_cparams(), name=name)(proj3, proj3, dw, cb)


def _conv_bwd(proj3, dw, dyc3, *, c_conv, l_valid, name):
    b, lp, _ = proj3.shape
    cw = _tile(c_conv, 256, LANES)
    nc = c_conv // cw

    def body(a_ref, gte_ref, dw_ref, dy_ref, da_ref, dgte_ref, ddw_ref, dcb_ref, zs, dys):
        bi = pl.program_id(1)
        rows = lax.broadcasted_iota(jnp.int32, (lp, 1), 0)
        valid = rows < l_valid
        av = a_ref[...]
        sg = _sigmoid(gte_ref[...])
        dyv = dy_ref[...]
        zero = jnp.zeros((CONV_HALO, cw), F32)
        zs[pl.ds(0, CONV_HALO), :] = zero
        zs[pl.ds(CONV_HALO, lp), :] = jnp.where(valid, av * sg, 0.0)
        zs[pl.ds(CONV_HALO + lp, CONV_HALO), :] = zero
        dys[pl.ds(0, CONV_HALO), :] = zero
        dys[pl.ds(CONV_HALO, lp), :] = dyv
        dys[pl.ds(CONV_HALO + lp, CONV_HALO), :] = zero

        @pl.when(bi == 0)
        def _():
            ddw_ref[...] = jnp.zeros_like(ddw_ref)
            dcb_ref[...] = jnp.zeros_like(dcb_ref)

        dz = jnp.zeros((lp, cw), F32)
        for j in range(CONV_K):
            dz = dz + dys[pl.ds(CONV_K - j, lp), :] * dw_ref[pl.ds(j, 1), :]
            ddw_ref[pl.ds(j, 1), :] += jnp.sum(dyv * zs[pl.ds(j + 1, lp), :], axis=0, keepdims=True)
        dcb_ref[...] += jnp.sum(dyv, axis=0, keepdims=True)
        dz = jnp.where(valid, dz, 0.0)
        da_ref[...] = (dz * sg).astype(da_ref.dtype)
        dgte_ref[...] = (dz * av * sg * (1.0 - sg)).astype(dgte_ref.dtype)

    seq = lambda off: pl.BlockSpec((None, lp, cw), lambda ci, bi: (bi, 0, off + ci))
    return pl.pallas_call(
        body, grid=(nc, b),
        in_specs=[seq(0), seq(nc), pl.BlockSpec((CONV_K, cw), lambda ci, bi: (0, ci)), seq(0)],
        out_specs=[seq(0), seq(0), pl.BlockSpec((CONV_K, cw), lambda ci, bi: (0, ci)),
                   pl.BlockSpec((1, cw), lambda ci, bi: (0, ci))],
        out_shape=[SDS((b, lp, c_conv), BF16), SDS((b, lp, c_conv), BF16),
                   SDS((CONV_K, c_conv), F32), SDS((1, c_conv), F32)],
        scratch_shapes=[pltpu.VMEM((lp + 2 * CONV_HALO, cw), F32)] * 2,
        compiler_params=_cparams(), name=name)(proj3, proj3, dw, dyc3)


def _swap_halves(x, group):
    half = group // 2
    lane = lax.broadcasted_iota(jnp.int32, x.shape, 1)
    up = pltpu.roll(x, LANES - half, 1)
    down = pltpu.roll(x, half, 1)
    return jnp.where((lane % group) < half, up, down)


def _rope(x, cos, sin, group):
    return x * cos + _swap_halves(x, group) * sin


def _rope_bwd(dy, cos, sin, group):
    return dy * cos + _swap_halves(dy * sin, group)


def _rope_tables(lp, seq, dim, lanes):
    quarter = dim // 4
    tok = jnp.arange(seq, dtype=jnp.int32)
    zeros = jnp.zeros((N_META,), F32)
    pad = jnp.zeros((lp - N_META - seq,), F32)
    row = jnp.concatenate([zeros, (tok // GRID_W).astype(F32), pad])
    col = jnp.concatenate([zeros, (tok % GRID_W).astype(F32), pad])
    inv = ROPE_THETA ** (-jnp.arange(quarter, dtype=F32) / quarter)
    ar, ac = row[:, None] * inv[None, :], col[:, None] * inv[None, :]
    cos = jnp.concatenate([jnp.cos(ar), jnp.cos(ar), jnp.cos(ac), jnp.cos(ac)], axis=1)
    sin = jnp.concatenate([-jnp.sin(ar), jnp.sin(ar), -jnp.sin(ac), jnp.sin(ac)], axis=1)
    if lanes > dim:
        cos = jnp.concatenate([cos, jnp.ones((lp, lanes - dim), F32)], axis=1)
        sin = jnp.concatenate([sin, jnp.zeros((lp, lanes - dim), F32)], axis=1)
    return cos, sin


def _gqa_prep_fwd(proj, qg, kg, cos, sin, *, q_off, k_off, v_off, lp, name):
    t = proj.shape[0]
    qw, kw = GQA_HEADS * HEAD, GQA_KV_HEADS * HEAD
    tr = _tile(lp, max(SUBLANES_BF16, ROW_BLOCK_BYTES // (4 * qw)), SUBLANES_BF16)
    npos = lp // tr
    assert q_off % qw == 0 and k_off % kw == 0 and v_off % kw == 0

    def body(q_ref, k_ref, v_ref, qg_ref, kg_ref, cos_ref, sin_ref, qo_ref, ko_ref, vo_ref):
        cosv, sinv = cos_ref[...], sin_ref[...]
        for src, g_ref, dst, heads in ((q_ref, qg_ref, qo_ref, GQA_HEADS), (k_ref, kg_ref, ko_ref, GQA_KV_HEADS)):
            for hh in range(heads):
                xh = src[:, hh * HEAD:(hh + 1) * HEAD]
                r = lax.rsqrt(jnp.mean(xh * xh, axis=-1, keepdims=True) + NORM_EPS)
                dst[:, hh * HEAD:(hh + 1) * HEAD] = _rope(xh * r * g_ref[...], cosv, sinv, 64).astype(dst.dtype)
        vo_ref[...] = v_ref[...].astype(vo_ref.dtype)

    tab = pl.BlockSpec((tr, HEAD), lambda i: (i % npos, 0))
    vec = pl.BlockSpec((1, HEAD), lambda i: (0, 0))
    return pl.pallas_call(
        body, grid=(t // tr,),
        in_specs=[pl.BlockSpec((tr, qw), lambda i: (i, q_off // qw)),
                  pl.BlockSpec((tr, kw), lambda i: (i, k_off // kw)),
                  pl.BlockSpec((tr, kw), lambda i: (i, v_off // kw)), vec, vec, tab, tab],
        out_specs=[pl.BlockSpec((tr, qw), lambda i: (i, 0)), pl.BlockSpec((tr, kw), lambda i: (i, 0)),
                   pl.BlockSpec((tr, kw), lambda i: (i, 0))],
        out_shape=[SDS((t, qw), BF16), SDS((t, kw), BF16), SDS((t, kw), BF16)],
        compiler_params=_cparams(), name=name)(proj, proj, proj, qg, kg, cos, sin)


def _gqa_prep_bwd(proj, qg, kg, cos, sin, dqr, dkr, *, q_off, k_off, lp, name):
    t = proj.shape[0]
    qw, kw = GQA_HEADS * HEAD, GQA_KV_HEADS * HEAD
    tr = _tile(lp, max(SUBLANES_BF16, ROW_BLOCK_BYTES // (4 * qw)), SUBLANES_BF16)
    npos = lp // tr

    def body(q_ref, k_ref, qg_ref, kg_ref, cos_ref, sin_ref, dqr_ref, dkr_ref, dq_ref, dk_ref, dqg_ref, dkg_ref):
        cosv, sinv = cos_ref[...], sin_ref[...]
        for src, g_ref, dy_ref, dx_ref, dg_ref, heads in (
                (q_ref, qg_ref, dqr_ref, dq_ref, dqg_ref, GQA_HEADS),
                (k_ref, kg_ref, dkr_ref, dk_ref, dkg_ref, GQA_KV_HEADS)):
            gpart = jnp.zeros((1, HEAD), F32)
            for hh in range(heads):
                sl = slice(hh * HEAD, (hh + 1) * HEAD)
                xh = src[:, sl]
                r = lax.rsqrt(jnp.mean(xh * xh, axis=-1, keepdims=True) + NORM_EPS)
                dxn = _rope_bwd(dy_ref[:, sl], cosv, sinv, 64)
                gy = dxn * g_ref[...]
                dx = r * gy - xh * (r * r * r) * jnp.mean(gy * xh, axis=-1, keepdims=True)
                dx_ref[:, sl] = dx.astype(dx_ref.dtype)
                gpart = gpart + jnp.sum(dxn * xh * r, axis=0, keepdims=True)

            @pl.when(pl.program_id(0) == 0)
            def _(dg_ref=dg_ref, gpart=gpart):
                dg_ref[...] = gpart

            @pl.when(pl.program_id(0) > 0)
            def _(dg_ref=dg_ref, gpart=gpart):
                dg_ref[...] += gpart

    tab = pl.BlockSpec((tr, HEAD), lambda i: (i % npos, 0))
    vec = pl.BlockSpec((1, HEAD), lambda i: (0, 0))
    qrow = pl.BlockSpec((tr, qw), lambda i: (i, 0))
    krow = pl.BlockSpec((tr, kw), lambda i: (i, 0))
    return pl.pallas_call(
        body, grid=(t // tr,),
        in_specs=[pl.BlockSpec((tr, qw), lambda i: (i, q_off // qw)),
                  pl.BlockSpec((tr, kw), lambda i: (i, k_off // kw)), vec, vec, tab, tab, qrow, krow],
        out_specs=[qrow, krow, vec, vec],
        out_shape=[SDS((t, qw), BF16), SDS((t, kw), BF16), SDS((1, HEAD), F32), SDS((1, HEAD), F32)],
        compiler_params=_cparams(), name=name)(proj, proj, qg, kg, cos, sin, dqr, dkr)


def _mla_rope_fwd(q_pre, k_pre, proj, cos, sin, *, kpe_off, lp, name):
    t, w = q_pre.shape
    tr = _tile(lp, max(SUBLANES_BF16, ROW_BLOCK_BYTES // (4 * w)), SUBLANES_BF16)
    npos = lp // tr
    assert kpe_off % LANES == 0

    def body(q_ref, k_ref, kpe_ref, cos_ref, sin_ref, qo_ref, ko_ref):
        cosv, sinv = cos_ref[...], sin_ref[...]
        kr = _rope(kpe_ref[...], cosv, sinv, 32).astype(ko_ref.dtype)
        for hh in range(MLA_HEADS):
            base = hh * MLA_SLOT
            qo_ref[:, base:base + HEAD] = q_ref[:, base:base + HEAD].astype(qo_ref.dtype)
            qo_ref[:, base + HEAD:base + MLA_SLOT] = _rope(
                q_ref[:, base + HEAD:base + MLA_SLOT], cosv, sinv, 32).astype(qo_ref.dtype)
            ko_ref[:, base:base + HEAD] = k_ref[:, base:base + HEAD].astype(ko_ref.dtype)
            ko_ref[:, base + HEAD:base + MLA_SLOT] = kr

    row = pl.BlockSpec((tr, w), lambda i: (i, 0))
    tab = pl.BlockSpec((tr, LANES), lambda i: (i % npos, 0))
    return pl.pallas_call(
        body, grid=(t // tr,),
        in_specs=[row, row, pl.BlockSpec((tr, LANES), lambda i: (i, kpe_off // LANES)), tab, tab],
        out_specs=[row, row], out_shape=[SDS((t, w), BF16), SDS((t, w), BF16)],
        compiler_params=_cparams(), name=name)(q_pre, k_pre, proj, cos, sin)


def _mla_rope_bwd(dq, dk, cos, sin, *, lp, name):
    t, w = dq.shape
    tr = _tile(lp, max(SUBLANES_BF16, ROW_BLOCK_BYTES // (4 * w)), SUBLANES_BF16)
    npos = lp // tr

    def body(dq_ref, dk_ref, cos_ref, sin_ref, dqo_ref, dko_ref, dkpe_ref):
        cosv, sinv = cos_ref[...], sin_ref[...]
        pe = jnp.zeros((tr, LANES), F32)
        for hh in range(MLA_HEADS):
            base = hh * MLA_SLOT
            dqo_ref[:, base:base + HEAD] = dq_ref[:, base:base + HEAD].astype(dqo_ref.dtype)
            dqo_ref[:, base + HEAD:base + MLA_SLOT] = _rope_bwd(
                dq_ref[:, base + HEAD:base + MLA_SLOT], cosv, sinv, 32).astype(dqo_ref.dtype)
            dko_ref[:, base:base + HEAD] = dk_ref[:, base:base + HEAD].astype(dko_ref.dtype)
            dko_ref[:, base + HEAD:base + MLA_SLOT] = jnp.zeros((tr, LANES), dko_ref.dtype)
            pe = pe + dk_ref[:, base + HEAD:base + MLA_SLOT]
        dkpe_ref[...] = _rope_bwd(pe, cosv, sinv, 32).astype(dkpe_ref.dtype)

    row = pl.BlockSpec((tr, w), lambda i: (i, 0))
    tab = pl.BlockSpec((tr, LANES), lambda i: (i % npos, 0))
    return pl.pallas_call(
        body, grid=(t // tr,), in_specs=[row, row, tab, tab],
        out_specs=[row, row, pl.BlockSpec((tr, LANES), lambda i: (i, 0))],
        out_shape=[SDS((t, w), BF16), SDS((t, w), BF16), SDS((t, LANES), BF16)],
        compiler_params=_cparams(), name=name)(dq, dk, cos, sin)


def _softmax_rows(q, k, scale, lp, l_valid):
    s = lax.dot_general(q, k, (((1,), (1,)), ((), ())), preferred_element_type=F32) * scale
    cols = lax.broadcasted_iota(jnp.int32, (1, lp), 1)
    s = jnp.where(cols < l_valid, s, -1e30)
    e = jnp.exp(s - jnp.max(s, axis=-1, keepdims=True))
    return e, 1.0 / jnp.sum(e, axis=-1, keepdims=True)


def _attn_fwd(q, k, v, *, batch, lp, l_valid, heads, kv_heads, dk, dv, scale, name):
    group = heads // kv_heads
    tq = _tile(lp, ATTN_Q_ROWS, SUBLANES_BF16)
    nq = lp // tq

    def body(q_ref, k_ref, v_ref, o_ref):
        e, inv = _softmax_rows(q_ref[...], k_ref[...], scale, lp, l_valid)
        o_ref[...] = (jnp.dot(e.astype(BF16), v_ref[...], preferred_element_type=F32) * inv).astype(o_ref.dtype)

    return pl.pallas_call(
        body, grid=(batch, heads, nq),
        in_specs=[pl.BlockSpec((tq, dk), lambda b, h, i: (b * nq + i, h)),
                  pl.BlockSpec((lp, dk), lambda b, h, i: (b, h // group)),
                  pl.BlockSpec((lp, dv), lambda b, h, i: (b, h // group))],
        out_specs=pl.BlockSpec((tq, dv), lambda b, h, i: (b * nq + i, h)),
        out_shape=SDS((batch * lp, heads * dv), BF16), compiler_params=_cparams(), name=name)(q, k, v)


def _attn_bwd(q, k, v, do, *, batch, lp, l_valid, heads, kv_heads, dk, dv, scale, name):
    group = heads // kv_heads
    tq = _tile(lp, ATTN_Q_ROWS, SUBLANES_BF16)
    nq = lp // tq

    def body(q_ref, k_ref, v_ref, do_ref, dq_ref, dk_ref, dv_ref):
        first = (pl.program_id(2) == 0) & (pl.program_id(3) == 0)
        qv, kv, dov = q_ref[...], k_ref[...], do_ref[...]
        e, inv = _softmax_rows(qv, kv, scale, lp, l_valid)
        dp = lax.dot_general(dov, v_ref[...], (((1,), (1,)), ((), ())), preferred_element_type=F32)
        delta = inv * jnp.sum(e * dp, axis=-1, keepdims=True)
        ds = (e * ((dp - delta) * (inv * scale))).astype(BF16)
        dq_ref[...] = jnp.dot(ds, kv, preferred_element_type=F32)
        dkp = lax.dot_general(ds, qv, (((0,), (0,)), ((), ())), preferred_element_type=F32)
        dvp = lax.dot_general(e.astype(BF16), (dov * inv).astype(BF16), (((0,), (0,)), ((), ())),
                              preferred_element_type=F32)

        @pl.when(first)
        def _():
            dk_ref[...] = dkp
            dv_ref[...] = dvp

        @pl.when(jnp.logical_not(first))
        def _():
            dk_ref[...] += dkp
            dv_ref[...] += dvp

    return pl.pallas_call(
        body, grid=(batch, kv_heads, group, nq),
        in_specs=[pl.BlockSpec((tq, dk), lambda b, hk, g, i: (b * nq + i, hk * group + g)),
                  pl.BlockSpec((lp, dk), lambda b, hk, g, i: (b, hk)),
                  pl.BlockSpec((lp, dv), lambda b, hk, g, i: (b, hk)),
                  pl.BlockSpec((tq, dv), lambda b, hk, g, i: (b * nq + i, hk * group + g))],
        out_specs=[pl.BlockSpec((tq, dk), lambda b, hk, g, i: (b * nq + i, hk * group + g)),
                   pl.BlockSpec((lp, dk), lambda b, hk, g, i: (b, hk)),
                   pl.BlockSpec((lp, dv), lambda b, hk, g, i: (b, hk))],
        out_shape=[SDS((batch * lp, heads * dk), F32), SDS((batch * lp, kv_heads * dk), F32),
                   SDS((batch * lp, kv_heads * dv), F32)],
        compiler_params=_cparams(), name=name)(q, k, v, do)


def _meta_grad(dh3, *, name):
    b, _, d = dh3.shape

    def body(x_ref, o_ref):
        @pl.when(pl.program_id(0) == 0)
        def _():
            o_ref[...] = x_ref[...]

        @pl.when(pl.program_id(0) > 0)
        def _():
            o_ref[...] += x_ref[...]

    return pl.pallas_call(
        body, grid=(b,), in_specs=[pl.BlockSpec((None, N_META, d), lambda bi: (bi, 0, 0))],
        out_specs=pl.BlockSpec((N_META, d), lambda bi: (0, 0)), out_shape=SDS((N_META, d), F32),
        compiler_params=_cparams(), name=name)(dh3)


def _sum_slots(parts, *, out_dtype, name):
    _, r, c = parts[0].shape
    tr = _row_tile(r, c)
    counts = [p.shape[0] for p in parts]

    def body(*refs):
        o_ref = refs[-1]
        acc = None
        for ref, cnt in zip(refs[:-1], counts):
            for s in range(cnt):
                term = ref[s].astype(F32)
                acc = term if acc is None else acc + term
        o_ref[...] = acc.astype(o_ref.dtype)

    return pl.pallas_call(
        body, grid=(r // tr,),
        in_specs=[pl.BlockSpec((cnt, tr, c), lambda i: (0, i, 0)) for cnt in counts],
        out_specs=pl.BlockSpec((tr, c), lambda i: (i, 0)), out_shape=SDS((r, c), out_dtype),
        compiler_params=_cparams(), name=name)(*parts)


def _adamw(w, g, m, v, *, name):
    shape = w.shape
    cols = shape[-1]
    rows = w.size // cols
    tr = _tile(rows, max(8, ROW_BLOCK_BYTES // (4 * cols)), 8)
    bc1 = 1.0 - ADAM_B1 ** ADAM_STEP
    bc2 = 1.0 - ADAM_B2 ** ADAM_STEP

    def body(w_ref, g_ref, m_ref, v_ref, d_ref, mo_ref, vo_ref):
        gv = g_ref[...]
        mn = ADAM_B1 * m_ref[...] + (1.0 - ADAM_B1) * gv
        vn = ADAM_B2 * v_ref[...] + (1.0 - ADAM_B2) * (gv * gv)
        mo_ref[...] = mn
        vo_ref[...] = vn
        d_ref[...] = -ADAM_LR * ((mn / bc1) / (jnp.sqrt(vn / bc2) + ADAM_EPS) + ADAM_WD * w_ref[...])

    spec = pl.BlockSpec((tr, cols), lambda i: (i, 0))
    outs = pl.pallas_call(
        body, grid=(rows // tr,), in_specs=[spec] * 4, out_specs=[spec] * 3,
        out_shape=[SDS((rows, cols), F32)] * 3, compiler_params=_cparams(), name=name)(
            *[a.reshape(rows, cols) for a in (w, g, m, v)])
    return tuple(o.reshape(shape) for o in outs)


def _mesh_pos():
    return lax.axis_index("x"), lax.axis_index("y"), lax.axis_index("c")


def _other_chips(x, y):
    return [(1 - x, y), (x, 1 - y), (1 - x, 1 - y)]


def _chip_allgather_small(vec, *, name):
    r, c = vec.shape

    def body(x_ref, out_ref, send_sems, recv_sems):
        x, y, cc = _mesh_pos()
        mine = 2 * x + y
        out_ref[mine] = x_ref[...]
        chips = _other_chips(x, y)
        sends = [pltpu.make_async_remote_copy(
            src_ref=x_ref, dst_ref=out_ref.at[mine], send_sem=send_sems.at[j], recv_sem=recv_sems.at[j],
            device_id=(px, py, cc), device_id_type=MESH) for j, (px, py) in enumerate(chips)]
        for cp in sends:
            cp.start()
        for j, (px, py) in enumerate(chips):
            pltpu.make_async_remote_copy(
                src_ref=x_ref, dst_ref=out_ref.at[2 * px + py], send_sem=send_sems.at[j],
                recv_sem=recv_sems.at[j], device_id=(px, py, cc), device_id_type=MESH).wait_recv()
        for cp in sends:
            cp.wait_send()

    return pl.pallas_call(
        body, out_shape=SDS((N_CHIPS, r, c), F32),
        in_specs=[pl.BlockSpec(memory_space=pltpu.VMEM)], out_specs=pl.BlockSpec(memory_space=pltpu.VMEM),
        scratch_shapes=[pltpu.SemaphoreType.DMA((3,)), pltpu.SemaphoreType.DMA((3,))],
        compiler_params=_cparams(), name=name)(vec)


def _allreduce_small(vec, *, name):
    r, c = vec.shape

    def body(x_ref, out_ref, buf, send_sems, recv_sems):
        x, y, cc = _mesh_pos()
        me = 4 * x + 2 * y + cc
        buf[me] = x_ref[...]

        def peer(kk):
            fx, fy, fc = (kk >> 2) & 1, (kk >> 1) & 1, kk & 1
            px = x if fx == 0 else 1 - x
            py = y if fy == 0 else 1 - y
            pc = cc if fc == 0 else 1 - cc
            return (px, py, pc), 4 * px + 2 * py + pc

        sends = []
        for kk in range(1, N_DEV):
            dev, _ = peer(kk)
            sends.append(pltpu.make_async_remote_copy(
                src_ref=x_ref, dst_ref=buf.at[me], send_sem=send_sems.at[kk - 1], recv_sem=recv_sems.at[kk - 1],
                device_id=dev, device_id_type=MESH))
        for cp in sends:
            cp.start()
        for kk in range(1, N_DEV):
            dev, slot = peer(kk)
            pltpu.make_async_remote_copy(
                src_ref=x_ref, dst_ref=buf.at[slot], send_sem=send_sems.at[kk - 1], recv_sem=recv_sems.at[kk - 1],
                device_id=dev, device_id_type=MESH).wait_recv()
        for cp in sends:
            cp.wait_send()
        acc = buf[0]
        for s in range(1, N_DEV):
            acc = acc + buf[s]
        out_ref[...] = acc

    return pl.pallas_call(
        body, out_shape=SDS((r, c), F32),
        in_specs=[pl.BlockSpec(memory_space=pltpu.VMEM)], out_specs=pl.BlockSpec(memory_space=pltpu.VMEM),
        scratch_shapes=[pltpu.VMEM((N_DEV, r, c), F32), pltpu.SemaphoreType.DMA((N_DEV - 1,)),
                        pltpu.SemaphoreType.DMA((N_DEV - 1,))],
        compiler_params=_cparams(), name=name)(vec)


_HBM = pl.BlockSpec(memory_space=pltpu.HBM)


def _half(rows, half):
    hk = rows // 2
    assert hk % SUBLANES_BF16 == 0
    return pl.ds(pl.multiple_of(half * hk, SUBLANES_BF16), hk)


_SEM = pl.BlockSpec(memory_space=pltpu.SEMAPHORE)
_ANY = pl.BlockSpec(memory_space=pl.ANY)
_DATAFLOW = pltpu.SideEffectType.DATAFLOW_SIDE_EFFECTING


def _in_hbm(a):
    return pltpu.with_memory_space_constraint(a, pltpu.HBM)


def _split_start(srcs, lands, prev, make_copy, *, name):
    nw = len(srcs)

    def body(*refs):
        src_refs, land_refs = refs[:nw], refs[nw:2 * nw]
        send_sems, recv_sems = refs[2 * nw + 1], refs[2 * nw + 2]
        token = refs[-1]
        for j in range(3):
            for i in range(nw):
                make_copy(src_refs, land_refs, send_sems, recv_sems, j, i, True).start()
        token[...] = jnp.zeros_like(token)

    outs = pl.pallas_call(
        body, name=name,
        out_shape=(pltpu.SemaphoreType.DMA((3 * nw,)), pltpu.SemaphoreType.DMA((3 * nw,)),
                   *[pltpu.HBM(a.shape, a.dtype) for a in srcs], *[pltpu.HBM(a.shape, a.dtype) for a in lands],
                   SDS((8, LANES), F32)),
        in_specs=[_HBM] * (2 * nw) + [_ANY],
        out_specs=(_SEM, _SEM, *[_HBM] * (2 * nw), pl.BlockSpec(memory_space=pltpu.VMEM)),
        input_output_aliases={i: 2 + i for i in range(2 * nw)},
        compiler_params=pltpu.CompilerParams(has_side_effects=_DATAFLOW),
    )(*[_in_hbm(a) for a in srcs], *[_in_hbm(a) for a in lands], prev)
    return outs[0], outs[1], list(outs[2:2 + nw]), list(outs[2 + nw:2 + 2 * nw]), outs[-1]


def _split_wait(pending, after, make_copy, *, name):
    send_sems, recv_sems, srcs, lands, _ = pending
    nw = len(srcs)

    def body(*refs):
        src_refs, land_refs = refs[:nw], refs[nw:2 * nw]
        send_refs, recv_refs = refs[2 * nw], refs[2 * nw + 1]
        for j in range(3):
            for i in range(nw):
                cp = make_copy(src_refs, land_refs, send_refs, recv_refs, j, i, False)
                cp.wait_send()
                cp.wait_recv()

    outs = pl.pallas_call(
        body, name=name,
        out_shape=(*[pltpu.HBM(a.shape, a.dtype) for a in srcs], *[pltpu.HBM(a.shape, a.dtype) for a in lands]),
        in_specs=[_HBM] * (2 * nw) + [_SEM, _SEM, _ANY], out_specs=tuple([_HBM] * (2 * nw)),
        input_output_aliases={i: i for i in range(2 * nw)},
        compiler_params=pltpu.CompilerParams(has_side_effects=_DATAFLOW),
    )(*srcs, *lands, send_sems, recv_sems, after)
    return list(outs[:nw]), list(outs[nw:])


def _gather_copy(shapes):
    def make(src_refs, land_refs, send_sems, recv_sems, j, i, outgoing):
        x, y, cc = _mesh_pos()
        px, py = _other_chips(x, y)[j]
        rows = _half(shapes[i][0], cc)
        slot = 2 * x + y if outgoing else 2 * px + py
        return pltpu.make_async_remote_copy(
            src_ref=src_refs[i].at[rows], dst_ref=land_refs[i].at[slot, rows],
            send_sem=send_sems.at[j * len(shapes) + i], recv_sem=recv_sems.at[j * len(shapes) + i],
            device_id=(px, py, cc), device_id_type=MESH)
    return make


def _gather_forward(lands, *, name):
    nw = len(lands)

    def body(*refs):
        in_refs, out_refs = refs[:nw], refs[nw:2 * nw]
        send_sems, recv_sems = refs[2 * nw:]
        x, y, cc = _mesh_pos()
        cps = []
        for j, (px, py) in enumerate(_other_chips(x, y)):
            for i in range(nw):
                rows = _half(lands[i].shape[1], cc)
                cps.append(pltpu.make_async_remote_copy(
                    src_ref=in_refs[i].at[2 * px + py, rows], dst_ref=out_refs[i].at[2 * px + py, rows],
                    send_sem=send_sems.at[j, i], recv_sem=recv_sems.at[j, i],
                    device_id=(x, y, 1 - cc), device_id_type=MESH))
        for cp in cps:
            cp.start()
        for cp in cps:
            cp.wait()

    return pl.pallas_call(
        body, out_shape=[SDS(a.shape, a.dtype) for a in lands], in_specs=[_HBM] * nw, out_specs=[_HBM] * nw,
        input_output_aliases={i: i for i in range(nw)},
        scratch_shapes=[pltpu.SemaphoreType.DMA((3, nw)), pltpu.SemaphoreType.DMA((3, nw))], name=name)(*lands)


def _grads_to_sibling(gs, *, name):
    nw = len(gs)

    def body(*refs):
        g_refs, out_refs = refs[:nw], refs[nw:2 * nw]
        send_sems, recv_sems = refs[2 * nw:]
        x, y, cc = _mesh_pos()
        cps = [pltpu.make_async_remote_copy(
            src_ref=g_refs[i].at[pl.ds(0, N_CHIPS), _half(gs[i].shape[1], 1 - cc)], dst_ref=out_refs[i],
            send_sem=send_sems.at[i], recv_sem=recv_sems.at[i], device_id=(x, y, 1 - cc), device_id_type=MESH)
            for i in range(nw)]
        for cp in cps:
            cp.start()
        for cp in cps:
            cp.wait()

    return pl.pallas_call(
        body, out_shape=[SDS((N_CHIPS, g.shape[1] // 2, g.shape[2]), g.dtype) for g in gs],
        in_specs=[_HBM] * nw, out_specs=[_HBM] * nw,
        scratch_shapes=[pltpu.SemaphoreType.DMA((nw,)), pltpu.SemaphoreType.DMA((nw,))], name=name)(*gs)


def _chips_copy(nw):
    def make(src_refs, land_refs, send_sems, recv_sems, j, i, outgoing):
        x, y, cc = _mesh_pos()
        px, py = _other_chips(x, y)[j]
        return pltpu.make_async_remote_copy(
            src_ref=src_refs[i].at[2 * px + py], dst_ref=land_refs[i].at[j],
            send_sem=send_sems.at[j * nw + i], recv_sem=recv_sems.at[j * nw + i],
            device_id=(px, py, cc), device_id_type=MESH)
    return make


def _share_halves(tots, *, name):
    nw = len(tots)

    def body(*refs):
        t_refs, out_refs = refs[:nw], refs[nw:2 * nw]
        send_sems, recv_sems = refs[2 * nw:]
        x, y, cc = _mesh_pos()
        cps = [pltpu.make_async_remote_copy(
            src_ref=t_refs[i], dst_ref=out_refs[i], send_sem=send_sems.at[i], recv_sem=recv_sems.at[i],
            device_id=(x, y, 1 - cc), device_id_type=MESH) for i in range(nw)]
        for cp in cps:
            cp.start()
        for cp in cps:
            cp.wait()

    return pl.pallas_call(
        body, out_shape=[SDS(tt.shape, tt.dtype) for tt in tots],
        in_specs=[_HBM] * nw, out_specs=[_HBM] * nw,
        scratch_shapes=[pltpu.SemaphoreType.DMA((nw,)), pltpu.SemaphoreType.DMA((nw,))], name=name)(*tots)


def _pack_small(arrays):
    flat = jnp.concatenate([a.astype(F32).reshape(-1) for a in arrays])
    n = flat.shape[0]
    padded = -(-n // (8 * LANES)) * (8 * LANES)
    return jnp.pad(flat, (0, padded - n)).reshape(padded // LANES, LANES)


def _unpack_small(flat, shapes):
    out, off = [], 0
    for shp in shapes:
        n = math.prod(shp)
        out.append(flat[off:off + n].reshape(shp))
        off += n
    return out


BIG = (("w_in", 1), ("w_conv_out", 1), ("w_gqa_out", 1), ("w_mla_uq", 1), ("w_mla_ukv", 1), ("w_mla_out", 1),
       ("w_out", 0), ("w_ffn_gate", 1), ("w_ffn_up", 1), ("w_ffn_down", 0))
BIG_NAMES = tuple(nm for nm, _ in BIG)
BIG_GROUPS = (("w_in",), tuple(nm for nm in BIG_NAMES if nm != "w_in"))


def _cols_to_slots(full):
    k, n = full.shape
    return full.reshape(k, N_CHIPS, n // N_CHIPS).transpose(1, 0, 2)


def _slots_to_cols(slots):
    s, k, n = slots.shape
    return slots.transpose(1, 0, 2).reshape(k, s * n)


def _step(p, mom_m, mom_v, x, loss_target):
    b, seq, d = x.shape
    depth = p["w_in"].shape[0]
    l_valid = N_META + seq
    lp = -(-l_valid // LANES) * LANES
    t = b * lp
    c_conv = d // 2
    q_rank = p["mla_q_norm_g"].shape[1]
    kv_rank = p["mla_kv_norm_g"].shape[1]
    qw, kw = GQA_HEADS * HEAD, GQA_KV_HEADS * HEAD
    off_q = 2 * c_conv
    off_k = off_q + qw
    off_v = off_k + kw
    off_cq = off_v + kw
    off_ckv = off_cq + q_rank
    off_gate = off_ckv + kv_rank
    off_kpe = off_gate + N_BRANCH * d
    d_inp = off_kpe + LANES
    assert off_cq % q_rank == 0 and off_ckv % kv_rank == 0
    xi, yi, ci = _mesh_pos()
    chip = 2 * xi + yi

    small_sh = [p["meta_tokens"], p["conv_dw"]]
    small_all = _chip_allgather_small(_pack_small(small_sh), name="gather_small")
    got = [_unpack_small(small_all[s].reshape(-1), [a.shape for a in small_sh]) for s in range(N_CHIPS)]
    meta_full = jnp.concatenate([g[0] for g in got], axis=1)
    conv_dw_full = jnp.concatenate([g[1] for g in got], axis=-1).reshape(depth, CONV_K, c_conv)

    gather_copies = [_gather_copy([p[nm].shape[1:] for nm in grp]) for grp in BIG_GROUPS]
    gathers = []
    order = small_all
    for li in range(depth):
        gathers.append([])
        for gi, grp in enumerate(BIG_GROUPS):
            own = [p[nm][li].astype(BF16) for nm in grp]
            lands = [lax.empty((N_CHIPS,) + o.shape, BF16) for o in own]
            gathers[li].append(_split_start(own, lands, order, gather_copies[gi],
                                            name="gather_start_%d_%d" % (li, gi)))
            order = gathers[li][gi][4]
    meta_full = meta_full + order[0, 0]

    def gathered(li, gi, after):
        own, lands = _split_wait(gathers[li][gi], after, gather_copies[gi], name="gather_wait_%d_%d" % (li, gi))
        got = _gather_forward(lands, name="gather_forward_%d" % gi)
        return {nm: lax.dynamic_update_slice(g, o[None], (chip, 0, 0)) for nm, g, o in zip(BIG_GROUPS[gi], got, own)}

    def in_proj_weight(slots):
        w_in = _slots_to_cols(slots)
        return jnp.concatenate(
            [w_in[:, :off_gate], w_in[:, off_gate + MLA_ROPE:], w_in[:, off_gate:off_gate + MLA_ROPE],
             jnp.zeros((d, LANES - MLA_ROPE), BF16)], axis=1)

    def other_weights(full):
        full["w_out"] = full["w_out"].reshape(-1, d)
        full["w_ffn_down"] = full["w_ffn_down"].reshape(-1, d)
        uq = _slots_to_cols(full["w_mla_uq"]).reshape(q_rank, MLA_HEADS, HEAD + MLA_ROPE)
        full["w_uq"] = jnp.pad(uq, ((0, 0), (0, 0), (0, MLA_SLOT - HEAD - MLA_ROPE))).reshape(
            q_rank, MLA_HEADS * MLA_SLOT)
        ukv = _slots_to_cols(full["w_mla_ukv"]).reshape(kv_rank, MLA_HEADS, 2 * HEAD)
        full["w_uk"] = jnp.pad(ukv[:, :, :HEAD], ((0, 0), (0, 0), (0, MLA_SLOT - HEAD))).reshape(
            kv_rank, MLA_HEADS * MLA_SLOT)
        full["w_uv"] = ukv[:, :, HEAD:].reshape(kv_rank, MLA_HEADS * HEAD)
        return full

    cos_g, sin_g = _rope_tables(lp, seq, HEAD, HEAD)
    cos_m, sin_m = _rope_tables(lp, seq, MLA_ROPE, LANES)
    gqa_scale = 1.0 / math.sqrt(HEAD)
    mla_scale = 1.0 / math.sqrt(HEAD + MLA_ROPE)
    attn_kw = dict(batch=b, lp=lp, l_valid=l_valid)
    gqa_kw = dict(heads=GQA_HEADS, kv_heads=GQA_KV_HEADS, dk=HEAD, dv=HEAD, scale=gqa_scale, **attn_kw)
    mla_kw = dict(heads=MLA_HEADS, kv_heads=MLA_HEADS, dk=MLA_SLOT, dv=HEAD, scale=mla_scale, **attn_kw)

    h = jnp.concatenate([jnp.broadcast_to(meta_full[None], (b, N_META, d)), x,
                         jnp.zeros((b, lp - l_valid, d), F32)], axis=1).reshape(t, d)
    saved, weights = [], []
    for li in range(depth):
        w = {"w_in": in_proj_weight(gathered(li, 0, h)["w_in"])}
        weights.append(w)
        row = lambda nm: p[nm][li][None, :]
        s = {"h_in": h}
        s["u"] = _rmsnorm_fwd(h, row("mix_norm_g"), width=d, cblk=0, name="mix_norm")
        proj = _mm(s["u"], w["w_in"], name="in_proj")
        s["proj"] = proj
        proj3 = proj.reshape(b, lp, d_inp)
        s["yconv"] = _conv_fwd(proj3, conv_dw_full[li], row("conv_b"), c_conv=c_conv, l_valid=l_valid,
                               name="conv").reshape(t, c_conv)
        s["sconv"] = _ln_silu_fwd(s["yconv"], row("conv_ln_g"), row("conv_ln_b"), name="conv_ln_silu")
        w.update(other_weights(gathered(li, 1, s["sconv"])))
        s["ya"] = _mm(s["sconv"], w["w_conv_out"], b_sharded=True, name="conv_out")
        s["gq"], s["gk"], s["gv"] = _gqa_prep_fwd(
            proj, row("gqa_q_norm_g"), row("gqa_k_norm_g"), cos_g, sin_g,
            q_off=off_q, k_off=off_k, v_off=off_v, lp=lp, name="gqa_prep")
        s["go"] = _attn_fwd(s["gq"], s["gk"], s["gv"], name="gqa_attn", **gqa_kw)
        s["yb"] = _mm(s["go"], w["w_gqa_out"], b_sharded=True, name="gqa_out")
        s["cqn"] = _rmsnorm_fwd(proj, row("mla_q_norm_g"), width=q_rank, cblk=off_cq // q_rank, name="mla_q_norm")
        s["kvn"] = _rmsnorm_fwd(proj, row("mla_kv_norm_g"), width=kv_rank, cblk=off_ckv // kv_rank,
                                name="mla_kv_norm")
        q_pre = _mm(s["cqn"], w["w_uq"], name="mla_uq")
        k_pre = _mm(s["kvn"], w["w_uk"], name="mla_uk")
        s["mv"] = _mm(s["kvn"], w["w_uv"], out_dtype=BF16, name="mla_uv")
        s["mq"], s["mk"] = _mla_rope_fwd(q_pre, k_pre, proj, cos_m, sin_m, kpe_off=off_kpe, lp=lp, name="mla_rope")
        s["mo"] = _attn_fwd(s["mq"], s["mk"], s["mv"], name="mla_attn", **mla_kw)
        s["yc"] = _mm(s["mo"], w["w_mla_out"], b_sharded=True, name="mla_out")
        s["merged"] = _merge_fwd(proj, row("gate_b"), s["ya"], s["yb"], s["yc"], gate_off=off_gate, name="merge")
        h = _mm(s["merged"], w["w_out"], add=h, name="mix_out")
        s["h_mid"] = h
        s["v"] = _rmsnorm_fwd(h, row("ffn_norm_g"), width=d, cblk=0, name="ffn_norm")
        s["gate"] = _mm(s["v"], w["w_ffn_gate"], b_sharded=True, out_dtype=BF16, name="ffn_gate")
        s["up"] = _mm(s["v"], w["w_ffn_up"], b_sharded=True, out_dtype=BF16, name="ffn_up")
        s["act"] = _swiglu_fwd(s["gate"], s["up"], name="swiglu")
        h = _mm(s["act"], w["w_ffn_down"], add=h, name="ffn_down")
        saved.append(s)

    target = jnp.pad(loss_target, ((0, 0), (N_META, lp - l_valid), (0, 0))).reshape(t, d)
    loss_part, dh, dhb, g_final = _loss_head(h, p["final_norm_g"][None, :], target, lp=lp, l_valid=l_valid,
                                             name="loss_head")
    loss = lax.psum(loss_part[0, 0], ("x", "y", "c"))

    small_names = ["mix_norm_g", "conv_dw", "conv_b", "conv_ln_g", "conv_ln_b", "gqa_q_norm_g", "gqa_k_norm_g",
                   "mla_q_norm_g", "mla_kv_norm_g", "gate_b", "ffn_norm_g"]
    small_g = {nm: [None] * depth for nm in small_names}
    shard_g = {nm: [None] * depth for nm in BIG_NAMES}
    chips_copies = [_chips_copy(len(grp)) for grp in BIG_GROUPS]

    def start_reduce(lj, gi, big):
        grp = BIG_GROUPS[gi]
        glist = [big[nm] for nm in grp]
        from_sib = _grads_to_sibling(glist, name="grads_to_sibling_%d" % gi)
        parts = []
        for nm, g, fs in zip(grp, glist, from_sib):
            hk = g.shape[1] // 2
            mine = lax.dynamic_slice_in_dim(g, ci * hk, hk, axis=1)
            rows = (1, N_CHIPS * hk, g.shape[2])
            parts.append(_sum_slots([mine.reshape(rows), fs.reshape(rows)], out_dtype=BF16,
                                    name="sum_sibling_" + nm).reshape(fs.shape))
        lands = [lax.empty((3,) + pt.shape[1:], BF16) for pt in parts]
        return lj, gi, _split_start(parts, lands, order, chips_copies[gi], name="chips_start_%d_%d" % (lj, gi))

    def finish_reduce(item, after):
        lj, gi, pending = item
        grp = BIG_GROUPS[gi]
        parts, from_chips = _split_wait(pending, after, chips_copies[gi], name="chips_wait_%d_%d" % (lj, gi))
        tots = [_sum_slots([lax.dynamic_index_in_dim(pt, chip, 0, keepdims=True), fc], out_dtype=F32,
                           name="sum_chips_" + nm) for nm, pt, fc in zip(grp, parts, from_chips)]
        others = _share_halves(tots, name="share_halves_%d" % gi)
        for nm, tot, oth in zip(grp, tots, others):
            shard_g[nm][lj] = jnp.concatenate([jnp.where(ci == 0, tot, oth), jnp.where(ci == 0, oth, tot)], axis=0)

    in_flight = []
    for li in reversed(range(depth)):
        w, s = weights[li], saved[li]
        row = lambda nm: p[nm][li][None, :]
        big = {}
        dact = _mm(dhb, w["w_ffn_down"], mode="nt", deps=[item[2][4] for item in in_flight], name="d_ffn_act")
        big["w_ffn_down"] = _mm(s["act"], dhb, mode="tn", out_dtype=BF16, deps=[item[2][4] for item in in_flight],
                                name="dw_ffn_down").reshape(N_CHIPS, -1, d)
        dgate, dup = _swiglu_bwd(s["gate"], s["up"], dact, name="d_swiglu")
        dv = _mm(dgate, w["w_ffn_gate"], mode="nt", b_sharded=True, name="d_ffn_v_gate")
        dv = _mm(dup, w["w_ffn_up"], mode="nt", b_sharded=True, add=dv, name="d_ffn_v_up")
        big["w_ffn_gate"] = _mm(s["v"], dgate, mode="tn", out_dtype=BF16, out_shards=N_CHIPS, name="dw_ffn_gate")
        big["w_ffn_up"] = _mm(s["v"], dup, mode="tn", out_dtype=BF16, out_shards=N_CHIPS, name="dw_ffn_up")
        dh, dhb, g = _rmsnorm_bwd(s["h_mid"], row("ffn_norm_g"), dv, width=d, cblk=0, res=dh, name="d_ffn_norm")
        small_g["ffn_norm_g"][li] = g[0]
        dm = _mm(dhb, w["w_out"], mode="nt", name="d_merged")
        big["w_out"] = _mm(s["merged"], dhb, mode="tn", out_dtype=BF16, name="dw_out").reshape(N_CHIPS, -1, d)
        dya, dyb, dyc, dl0, dl1, dl2, db0, db1, db2 = _merge_bwd(
            s["proj"], row("gate_b"), s["ya"], s["yb"], s["yc"], dm, gate_off=off_gate, name="d_merge")
        small_g["gate_b"][li] = jnp.concatenate([db0[0], db1[0], db2[0]])
        ds = _mm(dya, w["w_conv_out"], mode="nt", b_sharded=True, name="d_conv_s")
        big["w_conv_out"] = _mm(s["sconv"], dya, mode="tn", out_dtype=BF16, out_shards=N_CHIPS, name="dw_conv_out")
        dyconv, g, gb = _ln_silu_bwd(s["yconv"], row("conv_ln_g"), row("conv_ln_b"), ds, name="d_conv_ln_silu")
        small_g["conv_ln_g"][li], small_g["conv_ln_b"][li] = g[0], gb[0]
        proj3 = s["proj"].reshape(b, lp, d_inp)
        da, dgte, ddw, dcb = _conv_bwd(proj3, conv_dw_full[li], dyconv.reshape(b, lp, c_conv), c_conv=c_conv,
                                       l_valid=l_valid, name="d_conv")
        small_g["conv_dw"][li], small_g["conv_b"][li] = ddw, dcb[0]
        dgo = _mm(dyb, w["w_gqa_out"], mode="nt", b_sharded=True, out_dtype=BF16, name="d_gqa_o")
        big["w_gqa_out"] = _mm(s["go"], dyb, mode="tn", out_dtype=BF16, out_shards=N_CHIPS, name="dw_gqa_out")
        dgq, dgk, dgv = _attn_bwd(s["gq"], s["gk"], s["gv"], dgo, name="d_gqa_attn", **gqa_kw)
        dq_g, dk_g, gq, gk = _gqa_prep_bwd(s["proj"], row("gqa_q_norm_g"), row("gqa_k_norm_g"), cos_g, sin_g,
                                           dgq, dgk, q_off=off_q, k_off=off_k, lp=lp, name="d_gqa_prep")
        small_g["gqa_q_norm_g"][li], small_g["gqa_k_norm_g"][li] = gq[0], gk[0]
        dmo = _mm(dyc, w["w_mla_out"], mode="nt", b_sharded=True, out_dtype=BF16, name="d_mla_o")
        big["w_mla_out"] = _mm(s["mo"], dyc, mode="tn", out_dtype=BF16, out_shards=N_CHIPS, name="dw_mla_out")
        dmq, dmk, dmv = _attn_bwd(s["mq"], s["mk"], s["mv"], dmo, name="d_mla_attn", **mla_kw)
        dq_pre, dk_pre, dkpe = _mla_rope_bwd(dmq, dmk, cos_m, sin_m, lp=lp, name="d_mla_rope")
        dmvb = dmv.astype(BF16)
        g_uq = _mm(s["cqn"], dq_pre, mode="tn", out_dtype=BF16, name="dw_mla_uq")
        g_uk = _mm(s["kvn"], dk_pre, mode="tn", out_dtype=BF16, name="dw_mla_uk")
        g_uv = _mm(s["kvn"], dmvb, mode="tn", out_dtype=BF16, name="dw_mla_uv")
        big["w_mla_uq"] = _cols_to_slots(
            g_uq.reshape(q_rank, MLA_HEADS, MLA_SLOT)[:, :, :HEAD + MLA_ROPE].reshape(q_rank, -1))
        big["w_mla_ukv"] = _cols_to_slots(jnp.concatenate(
            [g_uk.reshape(kv_rank, MLA_HEADS, MLA_SLOT)[:, :, :HEAD], g_uv.reshape(kv_rank, MLA_HEADS, HEAD)],
            axis=-1).reshape(kv_rank, -1))
        dcqn = _mm(dq_pre, w["w_uq"], mode="nt", name="d_mla_cqn")
        dkvn = _mm(dk_pre, w["w_uk"], mode="nt", name="d_mla_kvn_k")
        dkvn = _mm(dmvb, w["w_uv"], mode="nt", add=dkvn, name="d_mla_kvn_v")
        _, dcq, g = _rmsnorm_bwd(s["proj"], row("mla_q_norm_g"), dcqn, width=q_rank, cblk=off_cq // q_rank,
                                 name="d_mla_q_norm")
        small_g["mla_q_norm_g"][li] = g[0]
        _, dckv, g = _rmsnorm_bwd(s["proj"], row("mla_kv_norm_g"), dkvn, width=kv_rank, cblk=off_ckv // kv_rank,
                                  name="d_mla_kv_norm")
        small_g["mla_kv_norm_g"][li] = g[0]
        for item in in_flight:
            finish_reduce(item, dckv)
        in_flight = [start_reduce(li, 1, big)]
        dproj = jnp.concatenate(
            [da.reshape(t, c_conv), dgte.reshape(t, c_conv), dq_g, dk_g, dgv.astype(BF16), dcq, dckv,
             dl0, dl1, dl2, dkpe], axis=1)
        du = _mm(dproj, w["w_in"], mode="nt", deps=[in_flight[0][2][4]], name="d_u")
        g_in = _mm(s["u"], dproj, mode="tn", out_dtype=BF16, deps=[in_flight[0][2][4]], name="dw_in")
        big["w_in"] = _cols_to_slots(jnp.concatenate(
            [g_in[:, :off_gate], g_in[:, off_kpe:off_kpe + MLA_ROPE], g_in[:, off_gate:off_kpe]], axis=1))
        dh, dhb, g = _rmsnorm_bwd(s["h_in"], row("mix_norm_g"), du, width=d, cblk=0, res=dh, name="d_mix_norm")
        small_g["mix_norm_g"][li] = g[0]
        in_flight.append(start_reduce(li, 0, big))

    dh3 = dh.reshape(b, lp, d)
    grad_x = dh3[:, N_META:l_valid]
    g_meta = _meta_grad(dh3, name="meta_grad")
    for item in in_flight:
        finish_reduce(item, g_meta)

    small_list = [g_meta, g_final[0]] + [jnp.stack(small_g[nm]) for nm in small_names]
    red = _allreduce_small(_pack_small(small_list), name="allreduce_small")
    red = _unpack_small(red.reshape(-1), [a.shape for a in small_list])
    grads = {"meta_tokens": lax.dynamic_slice_in_dim(red[0], chip * (d // N_CHIPS), d // N_CHIPS, axis=1),
             "final_norm_g": red[1]}
    for nm, val in zip(small_names, red[2:]):
        grads[nm] = val
    cs = c_conv // N_CHIPS
    grads["conv_dw"] = lax.dynamic_slice_in_dim(grads["conv_dw"], chip * cs, cs, axis=2).reshape(p["conv_dw"].shape)
    for nm in BIG_NAMES:
        grads[nm] = jnp.stack(shard_g[nm])

    delta, new_m, new_v = {}, {}, {}
    for nm in p:
        delta[nm], new_m[nm], new_v[nm] = _adamw(p[nm], grads[nm], mom_m[nm], mom_v[nm], name="adamw_" + nm)
    return loss, grad_x, grads, delta, new_m, new_v


WEIGHT_NAMES = ("meta_tokens", "mix_norm_g", "w_in", "conv_dw", "conv_b", "conv_ln_g", "conv_ln_b", "w_conv_out",
                "gqa_q_norm_g", "gqa_k_norm_g", "w_gqa_out", "mla_q_norm_g", "w_mla_uq", "mla_kv_norm_g",
                "w_mla_ukv", "w_mla_out", "gate_b", "w_out", "ffn_norm_g", "w_ffn_gate", "w_ffn_up", "w_ffn_down",
                "final_norm_g")


def kernel(x, meta_tokens, mix_norm_g, w_in, conv_dw, conv_b, conv_ln_g, conv_ln_b, w_conv_out, gqa_q_norm_g, gqa_k_norm_g, w_gqa_out, mla_q_norm_g, w_mla_uq, mla_kv_norm_g, w_mla_ukv, w_mla_out, gate_b, w_out, ffn_norm_g, w_ffn_gate, w_ffn_up, w_ffn_down, final_norm_g, loss_target, m_meta_tokens, m_mix_norm_g, m_w_in, m_conv_dw, m_conv_b, m_conv_ln_g, m_conv_ln_b, m_w_conv_out, m_gqa_q_norm_g, m_gqa_k_norm_g, m_w_gqa_out, m_mla_q_norm_g, m_w_mla_uq, m_mla_kv_norm_g, m_w_mla_ukv, m_w_mla_out, m_gate_b, m_w_out, m_ffn_norm_g, m_w_ffn_gate, m_w_ffn_up, m_w_ffn_down, m_final_norm_g, v_meta_tokens, v_mix_norm_g, v_w_in, v_conv_dw, v_conv_b, v_conv_ln_g, v_conv_ln_b, v_w_conv_out, v_gqa_q_norm_g, v_gqa_k_norm_g, v_w_gqa_out, v_mla_q_norm_g, v_w_mla_uq, v_mla_kv_norm_g, v_w_mla_ukv, v_w_mla_out, v_gate_b, v_w_out, v_ffn_norm_g, v_w_ffn_gate, v_w_ffn_up, v_w_ffn_down, v_final_norm_g):
    ws = (meta_tokens, mix_norm_g, w_in, conv_dw, conv_b, conv_ln_g, conv_ln_b, w_conv_out, gqa_q_norm_g,
          gqa_k_norm_g, w_gqa_out, mla_q_norm_g, w_mla_uq, mla_kv_norm_g, w_mla_ukv, w_mla_out, gate_b, w_out,
          ffn_norm_g, w_ffn_gate, w_ffn_up, w_ffn_down, final_norm_g)
    ms = (m_meta_tokens, m_mix_norm_g, m_w_in, m_conv_dw, m_conv_b, m_conv_ln_g, m_conv_ln_b, m_w_conv_out,
          m_gqa_q_norm_g, m_gqa_k_norm_g, m_w_gqa_out, m_mla_q_norm_g, m_w_mla_uq, m_mla_kv_norm_g, m_w_mla_ukv,
          m_w_mla_out, m_gate_b, m_w_out, m_ffn_norm_g, m_w_ffn_gate, m_w_ffn_up, m_w_ffn_down, m_final_norm_g)
    vs = (v_meta_tokens, v_mix_norm_g, v_w_in, v_conv_dw, v_conv_b, v_conv_ln_g, v_conv_ln_b, v_w_conv_out,
          v_gqa_q_norm_g, v_gqa_k_norm_g, v_w_gqa_out, v_mla_q_norm_g, v_w_mla_uq, v_mla_kv_norm_g, v_w_mla_ukv,
          v_w_mla_out, v_gate_b, v_w_out, v_ffn_norm_g, v_w_ffn_gate, v_w_ffn_up, v_w_ffn_down, v_final_norm_g)
    p = dict(zip(WEIGHT_NAMES, ws))
    loss, grad_x, grads, delta, new_m, new_v = _step(p, dict(zip(WEIGHT_NAMES, ms)), dict(zip(WEIGHT_NAMES, vs)),
                                                     x, loss_target)
    return (loss, grad_x, *[grads[n] for n in WEIGHT_NAMES], *[delta[n] for n in WEIGHT_NAMES],
            *[new_m[n] for n in WEIGHT_NAMES], *[new_v[n] for n in WEIGHT_NAMES])
```

```python
import math

import jax
import jax.numpy as jnp
from jax import lax
from jax.experimental import pallas as pl
from jax.experimental.pallas import tpu as pltpu

F32 = jnp.float32
BF16 = jnp.bfloat16
SDS = jax.ShapeDtypeStruct
MESH = pl.DeviceIdType.MESH

N_META = 16
GRID_W = 64
ROPE_THETA = 10000.0
NORM_EPS = 1e-6
CONV_K = 31
CONV_HALO = 16
HEAD = 128
GQA_HEADS = 8
GQA_KV_HEADS = 2
MLA_HEADS = 8
MLA_ROPE = 64
MLA_SLOT = 256
N_BRANCH = 3

ADAM_LR = 0.001
ADAM_B1 = 0.9
ADAM_B2 = 0.999
ADAM_EPS = 1e-08
ADAM_WD = 0.01
ADAM_STEP = 10

LANES = 128
SUBLANES_BF16 = 16
VMEM_LIMIT_BYTES = 56 * 2 ** 20
ROW_BLOCK_BYTES = 2 << 20
ATTN_Q_ROWS = 1088
MM_BLOCK_BYTES = 48 * 2 ** 20
N_CHIPS = 4
N_DEV = 8


def _cparams():
    return pltpu.CompilerParams(vmem_limit_bytes=VMEM_LIMIT_BYTES)


def _tile(n, cap, mult):
    best = None
    for d in range(mult, min(n, cap) + 1, mult):
        if n % d == 0:
            best = d
    return n if best is None else best


def _row_tile(rows, width, mult=SUBLANES_BF16):
    return _tile(rows, max(mult, ROW_BLOCK_BYTES // (4 * width)), mult)


def _sigmoid(x):
    return 1.0 / (1.0 + jnp.exp(-x))


def _mm(a, b, *, mode="nn", add=None, out_dtype=F32, b_sharded=False, out_shards=None, deps=(), name):
    if b_sharded:
        s_b, r_b, c_b = b.shape
        b_rows, b_cols = r_b, s_b * c_b
    else:
        b_rows, b_cols = b.shape
    if mode == "nn":
        (m, k), n = a.shape, b_cols
        assert b_rows == k
    elif mode == "nt":
        (m, k), n = a.shape, b_rows
        assert b_cols == k
    else:
        (k, m), n = a.shape, b_cols
        assert b_rows == k
    n_unit = n
    if b_sharded and mode != "nt":
        n_unit = c_b
    if out_shards is not None:
        assert n % out_shards == 0
        n_unit = math.gcd(n_unit, n // out_shards)
    k_unit = c_b if (b_sharded and mode == "nt") else k
    if mode == "tn":
        tm = _tile(m, 1024, LANES)
        tk = _tile(k_unit, 2176, SUBLANES_BF16)
    else:
        tm = _tile(m, 1088, SUBLANES_BF16)
        tk = _tile(k_unit, 2176, LANES)
    nk = k // tk
    out_bytes = jnp.dtype(out_dtype).itemsize
    tn = LANES
    for cap in (2176, 1408, 1024, 512, 256):
        tn = _tile(n_unit, cap, LANES)
        blocks = tm * tk * a.dtype.itemsize + tk * tn * b.dtype.itemsize + tm * tn * out_bytes
        blocks += tm * tn * 4 if add is not None else 0
        scratch = tm * tn * 4 if (nk > 1 and out_dtype != F32) else 0
        if 2 * blocks + scratch <= MM_BLOCK_BYTES:
            break
    if mode == "nn":
        a_spec = pl.BlockSpec((tm, tk), lambda i, j, kk: (i, kk))
        dn = (((1,), (0,)), ((), ()))
        if b_sharded:
            per = c_b // tn
            b_spec = pl.BlockSpec((None, tk, tn), lambda i, j, kk: (j // per, kk, j % per))
        else:
            b_spec = pl.BlockSpec((tk, tn), lambda i, j, kk: (kk, j))
    elif mode == "nt":
        a_spec = pl.BlockSpec((tm, tk), lambda i, j, kk: (i, kk))
        dn = (((1,), (1,)), ((), ()))
        if b_sharded:
            per = c_b // tk
            b_spec = pl.BlockSpec((None, tn, tk), lambda i, j, kk: (kk // per, j, kk % per))
        else:
            b_spec = pl.BlockSpec((tn, tk), lambda i, j, kk: (j, kk))
    else:
        a_spec = pl.BlockSpec((tk, tm), lambda i, j, kk: (kk, i))
        dn = (((0,), (0,)), ((), ()))
        if b_sharded:
            per = c_b // tn
            b_spec = pl.BlockSpec((None, tk, tn), lambda i, j, kk: (j // per, kk, j % per))
        else:
            b_spec = pl.BlockSpec((tk, tn), lambda i, j, kk: (kk, j))
    if out_shards is None:
        o_spec = pl.BlockSpec((tm, tn), lambda i, j, kk: (i, j))
        out_shape = SDS((m, n), out_dtype)
    else:
        per_o = (n // out_shards) // tn
        o_spec = pl.BlockSpec((None, tm, tn), lambda i, j, kk: (j // per_o, i, j % per_o))
        out_shape = SDS((out_shards, m, n // out_shards), out_dtype)
    has_add = add is not None
    assert not (has_add and out_shards is not None)
    deps = list(deps)
    use_acc = nk > 1 and out_dtype != F32

    def body(a_ref, b_ref, *rest):
        add_ref = rest[0] if has_add else None
        o_ref = rest[int(has_add) + len(deps)]
        kk = pl.program_id(2)
        part = lax.dot_general(a_ref[...].astype(BF16), b_ref[...].astype(BF16), dn,
                               preferred_element_type=F32)
        if nk == 1:
            o_ref[...] = (part + add_ref[...] if has_add else part).astype(o_ref.dtype)
            return
        acc = rest[-1] if use_acc else o_ref

        @pl.when(kk == 0)
        def _():
            acc[...] = part

        @pl.when((kk > 0) & (kk < nk - 1))
        def _():
            acc[...] += part

        @pl.when(kk == nk - 1)
        def _():
            r = acc[...] + part
            if has_add:
                r = r + add_ref[...]
            o_ref[...] = r.astype(o_ref.dtype)

    in_specs = [a_spec, b_spec] + ([o_spec] if has_add else []) + [_ANY] * len(deps)
    args = (a, b) + ((add,) if has_add else ()) + tuple(deps)
    return pl.pallas_call(
        body, grid=(m // tm, n // tn, nk), in_specs=in_specs, out_specs=o_spec,
        out_shape=out_shape, scratch_shapes=[pltpu.VMEM((tm, tn), F32)] if use_acc else [],
        compiler_params=_cparams(), name=name)(*args)


def _rmsnorm_fwd(x, g, *, width, cblk, name):
    t = x.shape[0]
    tr = _row_tile(t, width)

    def body(x_ref, g_ref, o_ref):
        xv = x_ref[...]
        r = lax.rsqrt(jnp.mean(xv * xv, axis=-1, keepdims=True) + NORM_EPS)
        o_ref[...] = (xv * r * g_ref[...]).astype(o_ref.dtype)

    return pl.pallas_call(
        body, grid=(t // tr,),
        in_specs=[pl.BlockSpec((tr, width), lambda i: (i, cblk)), pl.BlockSpec((1, width), lambda i: (0, 0))],
        out_specs=pl.BlockSpec((tr, width), lambda i: (i, 0)),
        out_shape=SDS((t, width), BF16), compiler_params=_cparams(), name=name)(x, g)


def _rmsnorm_bwd(x, g, dy, *, width, cblk, res=None, name):
    t = x.shape[0]
    tr = _row_tile(t, width)
    has_res = res is not None

    def body(x_ref, g_ref, dy_ref, *rest):
        if has_res:
            res_ref, dx_ref, dxb_ref, dg_ref = rest
        else:
            dx_ref, dxb_ref, dg_ref = rest
        xv = x_ref[...]
        dyv = dy_ref[...].astype(F32)
        r = lax.rsqrt(jnp.mean(xv * xv, axis=-1, keepdims=True) + NORM_EPS)
        gy = dyv * g_ref[...]
        dx = r * gy - xv * (r * r * r) * jnp.mean(gy * xv, axis=-1, keepdims=True)
        if has_res:
            dx = dx + res_ref[...]
        dx_ref[...] = dx
        dxb_ref[...] = dx.astype(BF16)
        part = jnp.sum(dyv * xv * r, axis=0, keepdims=True)

        @pl.when(pl.program_id(0) == 0)
        def _():
            dg_ref[...] = part

        @pl.when(pl.program_id(0) > 0)
        def _():
            dg_ref[...] += part

    row = pl.BlockSpec((tr, width), lambda i: (i, 0))
    vec = pl.BlockSpec((1, width), lambda i: (0, 0))
    in_specs = [pl.BlockSpec((tr, width), lambda i: (i, cblk)), vec, row] + ([row] if has_res else [])
    args = (x, g, dy) + ((res,) if has_res else ())
    return pl.pallas_call(
        body, grid=(t // tr,), in_specs=in_specs, out_specs=[row, row, vec],
        out_shape=[SDS((t, width), F32), SDS((t, width), BF16), SDS((1, width), F32)],
        compiler_params=_cparams(), name=name)(*args)


def _valid_rows(tile_index, tr, lp, lo, hi):
    pos = (tile_index % (lp // tr)) * tr + lax.broadcasted_iota(jnp.int32, (tr, 1), 0)
    return (pos >= lo) & (pos < hi)


def _loss_head(h, g, target, *, lp, l_valid, name):
    t, d = h.shape
    tr = _tile(lp, max(SUBLANES_BF16, ROW_BLOCK_BYTES // (4 * d)), SUBLANES_BF16)

    def body(x_ref, g_ref, t_ref, loss_ref, dx_ref, dxb_ref, dg_ref):
        i = pl.program_id(0)
        xv = x_ref[...]
        r = lax.rsqrt(jnp.mean(xv * xv, axis=-1, keepdims=True) + NORM_EPS)
        y = xv * r * g_ref[...]
        valid = _valid_rows(i, tr, lp, N_META, l_valid)
        err = jnp.where(valid, y - t_ref[...], 0.0)
        lpart = 0.5 * jnp.sum(jnp.mean(err * err, axis=-1, keepdims=True), axis=0, keepdims=True)
        dyv = err * (1.0 / d)
        gy = dyv * g_ref[...]
        dx = r * gy - xv * (r * r * r) * jnp.mean(gy * xv, axis=-1, keepdims=True)
        dx_ref[...] = dx
        dxb_ref[...] = dx.astype(BF16)
        gpart = jnp.sum(dyv * xv * r, axis=0, keepdims=True)

        @pl.when(i == 0)
        def _():
            dg_ref[...] = gpart
            loss_ref[...] = jnp.broadcast_to(lpart, loss_ref.shape)

        @pl.when(i > 0)
        def _():
            dg_ref[...] += gpart
            loss_ref[...] += jnp.broadcast_to(lpart, loss_ref.shape)

    row = pl.BlockSpec((tr, d), lambda i: (i, 0))
    vec = pl.BlockSpec((1, d), lambda i: (0, 0))
    return pl.pallas_call(
        body, grid=(t // tr,), in_specs=[row, vec, row],
        out_specs=[pl.BlockSpec((1, LANES), lambda i: (0, 0)), row, row, vec],
        out_shape=[SDS((1, LANES), F32), SDS((t, d), F32), SDS((t, d), BF16), SDS((1, d), F32)],
        compiler_params=_cparams(), name=name)(h, g, target)


def _ln_silu_fwd(yc, g, b, *, name):
    t, c = yc.shape
    tr = _row_tile(t, c)

    def body(x_ref, g_ref, b_ref, o_ref):
        xv = x_ref[...]
        xc = xv - jnp.mean(xv, axis=-1, keepdims=True)
        r = lax.rsqrt(jnp.mean(xc * xc, axis=-1, keepdims=True) + NORM_EPS)
        ln = xc * r * g_ref[...] + b_ref[...]
        o_ref[...] = (ln * _sigmoid(ln)).astype(o_ref.dtype)

    row = pl.BlockSpec((tr, c), lambda i: (i, 0))
    vec = pl.BlockSpec((1, c), lambda i: (0, 0))
    return pl.pallas_call(body, grid=(t // tr,), in_specs=[row, vec, vec], out_specs=row,
                          out_shape=SDS((t, c), BF16), compiler_params=_cparams(), name=name)(yc, g, b)


def _ln_silu_bwd(yc, g, b, ds, *, name):
    t, c = yc.shape
    tr = _row_tile(t, c)

    def body(x_ref, g_ref, b_ref, ds_ref, dx_ref, dg_ref, db_ref):
        xv = x_ref[...]
        xc = xv - jnp.mean(xv, axis=-1, keepdims=True)
        r = lax.rsqrt(jnp.mean(xc * xc, axis=-1, keepdims=True) + NORM_EPS)
        xh = xc * r
        ln = xh * g_ref[...] + b_ref[...]
        sg = _sigmoid(ln)
        dln = ds_ref[...].astype(F32) * (sg * (1.0 + ln * (1.0 - sg)))
        gy = dln * g_ref[...]
        dx_ref[...] = r * (gy - jnp.mean(gy, axis=-1, keepdims=True)
                           - xh * jnp.mean(gy * xh, axis=-1, keepdims=True))
        gpart = jnp.sum(dln * xh, axis=0, keepdims=True)
        bpart = jnp.sum(dln, axis=0, keepdims=True)

        @pl.when(pl.program_id(0) == 0)
        def _():
            dg_ref[...] = gpart
            db_ref[...] = bpart

        @pl.when(pl.program_id(0) > 0)
        def _():
            dg_ref[...] += gpart
            db_ref[...] += bpart

    row = pl.BlockSpec((tr, c), lambda i: (i, 0))
    vec = pl.BlockSpec((1, c), lambda i: (0, 0))
    return pl.pallas_call(
        body, grid=(t // tr,), in_specs=[row, vec, vec, row], out_specs=[row, vec, vec],
        out_shape=[SDS((t, c), F32), SDS((1, c), F32), SDS((1, c), F32)],
        compiler_params=_cparams(), name=name)(yc, g, b, ds)


def _swiglu_fwd(gate, up, *, name):
    t, f = gate.shape
    tr = _row_tile(t, f)

    def body(g_ref, u_ref, o_ref):
        gv = g_ref[...].astype(F32)
        o_ref[...] = (gv * _sigmoid(gv) * u_ref[...].astype(F32)).astype(o_ref.dtype)

    row = pl.BlockSpec((tr, f), lambda i: (i, 0))
    return pl.pallas_call(body, grid=(t // tr,), in_specs=[row, row], out_specs=row,
                          out_shape=SDS((t, f), BF16), compiler_params=_cparams(), name=name)(gate, up)


def _swiglu_bwd(gate, up, dact, *, name):
    t, f = gate.shape
    tr = _row_tile(t, f)

    def body(g_ref, u_ref, d_ref, dg_ref, du_ref):
        gv = g_ref[...].astype(F32)
        sg = _sigmoid(gv)
        dv = d_ref[...]
        dg_ref[...] = (dv * u_ref[...].astype(F32) * (sg * (1.0 + gv * (1.0 - sg)))).astype(dg_ref.dtype)
        du_ref[...] = (dv * (gv * sg)).astype(du_ref.dtype)

    row = pl.BlockSpec((tr, f), lambda i: (i, 0))
    return pl.pallas_call(body, grid=(t // tr,), in_specs=[row, row, row], out_specs=[row, row],
                          out_shape=[SDS((t, f), BF16), SDS((t, f), BF16)],
                          compiler_params=_cparams(), name=name)(gate, up, dact)


def _merge_tiles(t, d):
    tc = _tile(d, 512, LANES)
    return _row_tile(t, tc), tc


def _merge_fwd(proj, gate_b, ya, yb, yc, *, gate_off, name):
    t, d = ya.shape
    tr, tc = _merge_tiles(t, d)
    nc = d // tc
    assert gate_off % tc == 0
    g0 = gate_off // tc

    def body(l0, l1, l2, b0, b1, b2, ya_ref, yb_ref, yc_ref, o_ref):
        acc = _sigmoid(l0[...] + b0[...]) * ya_ref[...]
        acc += _sigmoid(l1[...] + b1[...]) * yb_ref[...]
        acc += _sigmoid(l2[...] + b2[...]) * yc_ref[...]
        o_ref[...] = acc.astype(o_ref.dtype)

    lspec = [pl.BlockSpec((tr, tc), lambda i, j, br=br: (i, g0 + br * nc + j)) for br in range(N_BRANCH)]
    bspec = [pl.BlockSpec((1, tc), lambda i, j, br=br: (0, br * nc + j)) for br in range(N_BRANCH)]
    yspec = pl.BlockSpec((tr, tc), lambda i, j: (i, j))
    return pl.pallas_call(
        body, grid=(t // tr, nc), in_specs=lspec + bspec + [yspec] * 3, out_specs=yspec,
        out_shape=SDS((t, d), BF16), compiler_params=_cparams(), name=name)(
            proj, proj, proj, gate_b, gate_b, gate_b, ya, yb, yc)


def _merge_bwd(proj, gate_b, ya, yb, yc, dm, *, gate_off, name):
    t, d = ya.shape
    tr, tc = _merge_tiles(t, d)
    nc = d // tc
    g0 = gate_off // tc

    def body(l0, l1, l2, b0, b1, b2, ya_ref, yb_ref, yc_ref, dm_ref,
             dya, dyb, dyc, dl0, dl1, dl2, db0, db1, db2):
        i = pl.program_id(1)
        dmv = dm_ref[...]
        for l_ref, b_ref, y_ref, dy_ref, dl_ref, db_ref in (
                (l0, b0, ya_ref, dya, dl0, db0), (l1, b1, yb_ref, dyb, dl1, db1), (l2, b2, yc_ref, dyc, dl2, db2)):
            gt = _sigmoid(l_ref[...] + b_ref[...])
            dy_ref[...] = (dmv * gt).astype(dy_ref.dtype)
            dl = dmv * y_ref[...] * gt * (1.0 - gt)
            dl_ref[...] = dl.astype(dl_ref.dtype)
            part = jnp.sum(dl, axis=0, keepdims=True)

            @pl.when(i == 0)
            def _(db_ref=db_ref, part=part):
                db_ref[...] = part

            @pl.when(i > 0)
            def _(db_ref=db_ref, part=part):
                db_ref[...] += part

    lspec = [pl.BlockSpec((tr, tc), lambda j, i, br=br: (i, g0 + br * nc + j)) for br in range(N_BRANCH)]
    bspec = [pl.BlockSpec((1, tc), lambda j, i, br=br: (0, br * nc + j)) for br in range(N_BRANCH)]
    yspec = pl.BlockSpec((tr, tc), lambda j, i: (i, j))
    vspec = pl.BlockSpec((1, tc), lambda j, i: (0, j))
    return pl.pallas_call(
        body, grid=(nc, t // tr), in_specs=lspec + bspec + [yspec] * 4,
        out_specs=[yspec] * 6 + [vspec] * 3,
        out_shape=[SDS((t, d), BF16)] * 6 + [SDS((1, d), F32)] * 3,
        compiler_params=_cparams(), name=name)(proj, proj, proj, gate_b, gate_b, gate_b, ya, yb, yc, dm)


def _conv_fwd(proj3, dw, cb, *, c_conv, l_valid, name):
    b, lp, _ = proj3.shape
    cw = _tile(c_conv, 256, LANES)
    nc = c_conv // cw

    def body(a_ref, gte_ref, dw_ref, cb_ref, o_ref, zs):
        rows = lax.broadcasted_iota(jnp.int32, (lp, 1), 0)
        z = jnp.where(rows < l_valid, a_ref[...] * _sigmoid(gte_ref[...]), 0.0)
        zs[pl.ds(0, CONV_HALO), :] = jnp.zeros((CONV_HALO, cw), F32)
        zs[pl.ds(CONV_HALO, lp), :] = z
        zs[pl.ds(CONV_HALO + lp, CONV_HALO), :] = jnp.zeros((CONV_HALO, cw), F32)
        acc = jnp.broadcast_to(cb_ref[...], (lp, cw))
        for j in range(CONV_K):
            acc = acc + zs[pl.ds(j + 1, lp), :] * dw_ref[pl.ds(j, 1), :]
        o_ref[...] = acc

    seq = lambda off: pl.BlockSpec((None, lp, cw), lambda bi, ci: (bi, 0, off + ci))
    return pl.pallas_call(
        body, grid=(b, nc),
        in_specs=[seq(0), seq(nc), pl.BlockSpec((CONV_K, cw), lambda bi, ci: (0, ci)),
                  pl.BlockSpec((1, cw), lambda bi, ci: (0, ci))],
        out_specs=seq(0), out_shape=SDS((b, lp, c_conv), F32),
        scratch_shapes=[pltpu.VMEM((lp + 2 * CONV_HALO, cw), F32)],
        compiler_params=_cparams(), name=name)(proj3, proj3, dw, cb)


def _conv_bwd(proj3, dw, dyc3, *, c_conv, l_valid, name):
    b, lp, _ = proj3.shape
    cw = _tile(c_conv, 256, LANES)
    nc = c_conv // cw

    def body(a_ref, gte_ref, dw_ref, dy_ref, da_ref, dgte_ref, ddw_ref, dcb_ref, zs, dys):
        bi = pl.program_id(1)
        rows = lax.broadcasted_iota(jnp.int32, (lp, 1), 0)
        valid = rows < l_valid
        av = a_ref[...]
        sg = _sigmoid(gte_ref[...])
        dyv = dy_ref[...]
        zero = jnp.zeros((CONV_HALO, cw), F32)
        zs[pl.ds(0, CONV_HALO), :] = zero
        zs[pl.ds(CONV_HALO, lp), :] = jnp.where(valid, av * sg, 0.0)
        zs[pl.ds(CONV_HALO + lp, CONV_HALO), :] = zero
        dys[pl.ds(0, CONV_HALO), :] = zero
        dys[pl.ds(CONV_HALO, lp), :] = dyv
        dys[pl.ds(CONV_HALO + lp, CONV_HALO), :] = zero

        @pl.when(bi == 0)
        def _():
            ddw_ref[...] = jnp.zeros_like(ddw_ref)
            dcb_ref[...] = jnp.zeros_like(dcb_ref)

        dz = jnp.zeros((lp, cw), F32)
        for j in range(CONV_K):
            dz = dz + dys[pl.ds(CONV_K - j, lp), :] * dw_ref[pl.ds(j, 1), :]
            ddw_ref[pl.ds(j, 1), :] += jnp.sum(dyv * zs[pl.ds(j + 1, lp), :], axis=0, keepdims=True)
        dcb_ref[...] += jnp.sum(dyv, axis=0, keepdims=True)
        dz = jnp.where(valid, dz, 0.0)
        da_ref[...] = (dz * sg).astype(da_ref.dtype)
        dgte_ref[...] = (dz * av * sg * (1.0 - sg)).astype(dgte_ref.dtype)

    seq = lambda off: pl.BlockSpec((None, lp, cw), lambda ci, bi: (bi, 0, off + ci))
    return pl.pallas_call(
        body, grid=(nc, b),
        in_specs=[seq(0), seq(nc), pl.BlockSpec((CONV_K, cw), lambda ci, bi: (0, ci)), seq(0)],
        out_specs=[seq(0), seq(0), pl.BlockSpec((CONV_K, cw), lambda ci, bi: (0, ci)),
                   pl.BlockSpec((1, cw), lambda ci, bi: (0, ci))],
        out_shape=[SDS((b, lp, c_conv), BF16), SDS((b, lp, c_conv), BF16),
                   SDS((CONV_K, c_conv), F32), SDS((1, c_conv), F32)],
        scratch_shapes=[pltpu.VMEM((lp + 2 * CONV_HALO, cw), F32)] * 2,
        compiler_params=_cparams(), name=name)(proj3, proj3, dw, dyc3)


def _swap_halves(x, group):
    half = group // 2
    lane = lax.broadcasted_iota(jnp.int32, x.shape, 1)
    up = pltpu.roll(x, LANES - half, 1)
    down = pltpu.roll(x, half, 1)
    return jnp.where((lane % group) < half, up, down)


def _rope(x, cos, sin, group):
    return x * cos + _swap_halves(x, group) * sin


def _rope_bwd(dy, cos, sin, group):
    return dy * cos + _swap_halves(dy * sin, group)


def _rope_tables(lp, seq, dim, lanes):
    quarter = dim // 4
    tok = jnp.arange(seq, dtype=jnp.int32)
    zeros = jnp.zeros((N_META,), F32)
    pad = jnp.zeros((lp - N_META - seq,), F32)
    row = jnp.concatenate([zeros, (tok // GRID_W).astype(F32), pad])
    col = jnp.concatenate([zeros, (tok % GRID_W).astype(F32), pad])
    inv = ROPE_THETA ** (-jnp.arange(quarter, dtype=F32) / quarter)
    ar, ac = row[:, None] * inv[None, :], col[:, None] * inv[None, :]
    cos = jnp.concatenate([jnp.cos(ar), jnp.cos(ar), jnp.cos(ac), jnp.cos(ac)], axis=1)
    sin = jnp.concatenate([-jnp.sin(ar), jnp.sin(ar), -jnp.sin(ac), jnp.sin(ac)], axis=1)
    if lanes > dim:
        cos = jnp.concatenate([cos, jnp.ones((lp, lanes - dim), F32)], axis=1)
        sin = jnp.concatenate([sin, jnp.zeros((lp, lanes - dim), F32)], axis=1)
    return cos, sin


def _gqa_prep_fwd(proj, qg, kg, cos, sin, *, q_off, k_off, v_off, lp, name):
    t = proj.shape[0]
    qw, kw = GQA_HEADS * HEAD, GQA_KV_HEADS * HEAD
    tr = _tile(lp, max(SUBLANES_BF16, ROW_BLOCK_BYTES // (4 * qw)), SUBLANES_BF16)
    npos = lp // tr
    assert q_off % qw == 0 and k_off % kw == 0 and v_off % kw == 0

    def body(q_ref, k_ref, v_ref, qg_ref, kg_ref, cos_ref, sin_ref, qo_ref, ko_ref, vo_ref):
        cosv, sinv = cos_ref[...], sin_ref[...]
        for src, g_ref, dst, heads in ((q_ref, qg_ref, qo_ref, GQA_HEADS), (k_ref, kg_ref, ko_ref, GQA_KV_HEADS)):
            for hh in range(heads):
                xh = src[:, hh * HEAD:(hh + 1) * HEAD]
                r = lax.rsqrt(jnp.mean(xh * xh, axis=-1, keepdims=True) + NORM_EPS)
                dst[:, hh * HEAD:(hh + 1) * HEAD] = _rope(xh * r * g_ref[...], cosv, sinv, 64).astype(dst.dtype)
        vo_ref[...] = v_ref[...].astype(vo_ref.dtype)

    tab = pl.BlockSpec((tr, HEAD), lambda i: (i % npos, 0))
    vec = pl.BlockSpec((1, HEAD), lambda i: (0, 0))
    return pl.pallas_call(
        body, grid=(t // tr,),
        in_specs=[pl.BlockSpec((tr, qw), lambda i: (i, q_off // qw)),
                  pl.BlockSpec((tr, kw), lambda i: (i, k_off // kw)),
                  pl.BlockSpec((tr, kw), lambda i: (i, v_off // kw)), vec, vec, tab, tab],
        out_specs=[pl.BlockSpec((tr, qw), lambda i: (i, 0)), pl.BlockSpec((tr, kw), lambda i: (i, 0)),
                   pl.BlockSpec((tr, kw), lambda i: (i, 0))],
        out_shape=[SDS((t, qw), BF16), SDS((t, kw), BF16), SDS((t, kw), BF16)],
        compiler_params=_cparams(), name=name)(proj, proj, proj, qg, kg, cos, sin)


def _gqa_prep_bwd(proj, qg, kg, cos, sin, dqr, dkr, *, q_off, k_off, lp, name):
    t = proj.shape[0]
    qw, kw = GQA_HEADS * HEAD, GQA_KV_HEADS * HEAD
    tr = _tile(lp, max(SUBLANES_BF16, ROW_BLOCK_BYTES // (4 * qw)), SUBLANES_BF16)
    npos = lp // tr

    def body(q_ref, k_ref, qg_ref, kg_ref, cos_ref, sin_ref, dqr_ref, dkr_ref, dq_ref, dk_ref, dqg_ref, dkg_ref):
        cosv, sinv = cos_ref[...], sin_ref[...]
        for src, g_ref, dy_ref, dx_ref, dg_ref, heads in (
                (q_ref, qg_ref, dqr_ref, dq_ref, dqg_ref, GQA_HEADS),
                (k_ref, kg_ref, dkr_ref, dk_ref, dkg_ref, GQA_KV_HEADS)):
            gpart = jnp.zeros((1, HEAD), F32)
            for hh in range(heads):
                sl = slice(hh * HEAD, (hh + 1) * HEAD)
                xh = src[:, sl]
                r = lax.rsqrt(jnp.mean(xh * xh, axis=-1, keepdims=True) + NORM_EPS)
                dxn = _rope_bwd(dy_ref[:, sl], cosv, sinv, 64)
                gy = dxn * g_ref[...]
                dx = r * gy - xh * (r * r * r) * jnp.mean(gy * xh, axis=-1, keepdims=True)
                dx_ref[:, sl] = dx.astype(dx_ref.dtype)
                gpart = gpart + jnp.sum(dxn * xh * r, axis=0, keepdims=True)

            @pl.when(pl.program_id(0) == 0)
            def _(dg_ref=dg_ref, gpart=gpart):
                dg_ref[...] = gpart

            @pl.when(pl.program_id(0) > 0)
            def _(dg_ref=dg_ref, gpart=gpart):
                dg_ref[...] += gpart

    tab = pl.BlockSpec((tr, HEAD), lambda i: (i % npos, 0))
    vec = pl.BlockSpec((1, HEAD), lambda i: (0, 0))
    qrow = pl.BlockSpec((tr, qw), lambda i: (i, 0))
    krow = pl.BlockSpec((tr, kw), lambda i: (i, 0))
    return pl.pallas_call(
        body, grid=(t // tr,),
        in_specs=[pl.BlockSpec((tr, qw), lambda i: (i, q_off // qw)),
                  pl.BlockSpec((tr, kw), lambda i: (i, k_off // kw)), vec, vec, tab, tab, qrow, krow],
        out_specs=[qrow, krow, vec, vec],
        out_shape=[SDS((t, qw), BF16), SDS((t, kw), BF16), SDS((1, HEAD), F32), SDS((1, HEAD), F32)],
        compiler_params=_cparams(), name=name)(proj, proj, qg, kg, cos, sin, dqr, dkr)


def _mla_rope_fwd(q_pre, k_pre, proj, cos, sin, *, kpe_off, lp, name):
    t, w = q_pre.shape
    tr = _tile(lp, max(SUBLANES_BF16, ROW_BLOCK_BYTES // (4 * w)), SUBLANES_BF16)
    npos = lp // tr
    assert kpe_off % LANES == 0

    def body(q_ref, k_ref, kpe_ref, cos_ref, sin_ref, qo_ref, ko_ref):
        cosv, sinv = cos_ref[...], sin_ref[...]
        kr = _rope(kpe_ref[...], cosv, sinv, 32).astype(ko_ref.dtype)
        for hh in range(MLA_HEADS):
            base = hh * MLA_SLOT
            qo_ref[:, base:base + HEAD] = q_ref[:, base:base + HEAD].astype(qo_ref.dtype)
            qo_ref[:, base + HEAD:base + MLA_SLOT] = _rope(
                q_ref[:, base + HEAD:base + MLA_SLOT], cosv, sinv, 32).astype(qo_ref.dtype)
            ko_ref[:, base:base + HEAD] = k_ref[:, base:base + HEAD].astype(ko_ref.dtype)
            ko_ref[:, base + HEAD:base + MLA_SLOT] = kr

    row = pl.BlockSpec((tr, w), lambda i: (i, 0))
    tab = pl.BlockSpec((tr, LANES), lambda i: (i % npos, 0))
    return pl.pallas_call(
        body, grid=(t // tr,),
        in_specs=[row, row, pl.BlockSpec((tr, LANES), lambda i: (i, kpe_off // LANES)), tab, tab],
        out_specs=[row, row], out_shape=[SDS((t, w), BF16), SDS((t, w), BF16)],
        compiler_params=_cparams(), name=name)(q_pre, k_pre, proj, cos, sin)


def _mla_rope_bwd(dq, dk, cos, sin, *, lp, name):
    t, w = dq.shape
    tr = _tile(lp, max(SUBLANES_BF16, ROW_BLOCK_BYTES // (4 * w)), SUBLANES_BF16)
    npos = lp // tr

    def body(dq_ref, dk_ref, cos_ref, sin_ref, dqo_ref, dko_ref, dkpe_ref):
        cosv, sinv = cos_ref[...], sin_ref[...]
        pe = jnp.zeros((tr, LANES), F32)
        for hh in range(MLA_HEADS):
            base = hh * MLA_SLOT
            dqo_ref[:, base:base + HEAD] = dq_ref[:, base:base + HEAD].astype(dqo_ref.dtype)
            dqo_ref[:, base + HEAD:base + MLA_SLOT] = _rope_bwd(
                dq_ref[:, base + HEAD:base + MLA_SLOT], cosv, sinv, 32).astype(dqo_ref.dtype)
            dko_ref[:, base:base + HEAD] = dk_ref[:, base:base + HEAD].astype(dko_ref.dtype)
            dko_ref[:, base + HEAD:base + MLA_SLOT] = jnp.zeros((tr, LANES), dko_ref.dtype)
            pe = pe + dk_ref[:, base + HEAD:base + MLA_SLOT]
        dkpe_ref[...] = _rope_bwd(pe, cosv, sinv, 32).astype(dkpe_ref.dtype)

    row = pl.BlockSpec((tr, w), lambda i: (i, 0))
    tab = pl.BlockSpec((tr, LANES), lambda i: (i % npos, 0))
    return pl.pallas_call(
        body, grid=(t // tr,), in_specs=[row, row, tab, tab],
        out_specs=[row, row, pl.BlockSpec((tr, LANES), lambda i: (i, 0))],
        out_shape=[SDS((t, w), BF16), SDS((t, w), BF16), SDS((t, LANES), BF16)],
        compiler_params=_cparams(), name=name)(dq, dk, cos, sin)


def _softmax_rows(q, k, scale, lp, l_valid):
    s = lax.dot_general(q, k, (((1,), (1,)), ((), ())), preferred_element_type=F32) * scale
    cols = lax.broadcasted_iota(jnp.int32, (1, lp), 1)
    s = jnp.where(cols < l_valid, s, -1e30)
    e = jnp.exp(s - jnp.max(s, axis=-1, keepdims=True))
    return e, 1.0 / jnp.sum(e, axis=-1, keepdims=True)


def _attn_fwd(q, k, v, *, batch, lp, l_valid, heads, kv_heads, dk, dv, scale, name):
    group = heads // kv_heads
    tq = _tile(lp, ATTN_Q_ROWS, SUBLANES_BF16)
    nq = lp // tq

    def body(q_ref, k_ref, v_ref, o_ref):
        e, inv = _softmax_rows(q_ref[...], k_ref[...], scale, lp, l_valid)
        o_ref[...] = (jnp.dot(e.astype(BF16), v_ref[...], preferred_element_type=F32) * inv).astype(o_ref.dtype)

    return pl.pallas_call(
        body, grid=(batch, heads, nq),
        in_specs=[pl.BlockSpec((tq, dk), lambda b, h, i: (b * nq + i, h)),
                  pl.BlockSpec((lp, dk), lambda b, h, i: (b, h // group)),
                  pl.BlockSpec((lp, dv), lambda b, h, i: (b, h // group))],
        out_specs=pl.BlockSpec((tq, dv), lambda b, h, i: (b * nq + i, h)),
        out_shape=SDS((batch * lp, heads * dv), BF16), compiler_params=_cparams(), name=name)(q, k, v)


def _attn_bwd(q, k, v, do, *, batch, lp, l_valid, heads, kv_heads, dk, dv, scale, name):
    group = heads // kv_heads
    tq = _tile(lp, ATTN_Q_ROWS, SUBLANES_BF16)
    nq = lp // tq

    def body(q_ref, k_ref, v_ref, do_ref, dq_ref, dk_ref, dv_ref):
        first = (pl.program_id(2) == 0) & (pl.program_id(3) == 0)
        qv, kv, dov = q_ref[...], k_ref[...], do_ref[...]
        e, inv = _softmax_rows(qv, kv, scale, lp, l_valid)
        dp = lax.dot_general(dov, v_ref[...], (((1,), (1,)), ((), ())), preferred_element_type=F32)
        delta = inv * jnp.sum(e * dp, axis=-1, keepdims=True)
        ds = (e * ((dp - delta) * (inv * scale))).astype(BF16)
        dq_ref[...] = jnp.dot(ds, kv, preferred_element_type=F32)
        dkp = lax.dot_general(ds, qv, (((0,), (0,)), ((), ())), preferred_element_type=F32)
        dvp = lax.dot_general(e.astype(BF16), (dov * inv).astype(BF16), (((0,), (0,)), ((), ())),
                              preferred_element_type=F32)

        @pl.when(first)
        def _():
            dk_ref[...] = dkp
            dv_ref[...] = dvp

        @pl.when(jnp.logical_not(first))
        def _():
            dk_ref[...] += dkp
            dv_ref[...] += dvp

    return pl.pallas_call(
        body, grid=(batch, kv_heads, group, nq),
        in_specs=[pl.BlockSpec((tq, dk), lambda b, hk, g, i: (b * nq + i, hk * group + g)),
                  pl.BlockSpec((lp, dk), lambda b, hk, g, i: (b, hk)),
                  pl.BlockSpec((lp, dv), lambda b, hk, g, i: (b, hk)),
                  pl.BlockSpec((tq, dv), lambda b, hk, g, i: (b * nq + i, hk * group + g))],
        out_specs=[pl.BlockSpec((tq, dk), lambda b, hk, g, i: (b * nq + i, hk * group + g)),
                   pl.BlockSpec((lp, dk), lambda b, hk, g, i: (b, hk)),
                   pl.BlockSpec((lp, dv), lambda b, hk, g, i: (b, hk))],
        out_shape=[SDS((batch * lp, heads * dk), F32), SDS((batch * lp, kv_heads * dk), F32),
                   SDS((batch * lp, kv_heads * dv), F32)],
        compiler_params=_cparams(), name=name)(q, k, v, do)


def _meta_grad(dh3, *, name):
    b, _, d = dh3.shape

    def body(x_ref, o_ref):
        @pl.when(pl.program_id(0) == 0)
        def _():
            o_ref[...] = x_ref[...]

        @pl.when(pl.program_id(0) > 0)
        def _():
            o_ref[...] += x_ref[...]

    return pl.pallas_call(
        body, grid=(b,), in_specs=[pl.BlockSpec((None, N_META, d), lambda bi: (bi, 0, 0))],
        out_specs=pl.BlockSpec((N_META, d), lambda bi: (0, 0)), out_shape=SDS((N_META, d), F32),
        compiler_params=_cparams(), name=name)(dh3)


def _sum_slots(parts, *, out_dtype, name):
    _, r, c = parts[0].shape
    tr = _row_tile(r, c)
    counts = [p.shape[0] for p in parts]

    def body(*refs):
        o_ref = refs[-1]
        acc = None
        for ref, cnt in zip(refs[:-1], counts):
            for s in range(cnt):
                term = ref[s].astype(F32)
                acc = term if acc is None else acc + term
        o_ref[...] = acc.astype(o_ref.dtype)

    return pl.pallas_call(
        body, grid=(r // tr,),
        in_specs=[pl.BlockSpec((cnt, tr, c), lambda i: (0, i, 0)) for cnt in counts],
        out_specs=pl.BlockSpec((tr, c), lambda i: (i, 0)), out_shape=SDS((r, c), out_dtype),
        compiler_params=_cparams(), name=name)(*parts)


def _adamw(w, g, m, v, *, name):
    shape = w.shape
    cols = shape[-1]
    rows = w.size // cols
    tr = _tile(rows, max(8, ROW_BLOCK_BYTES // (4 * cols)), 8)
    bc1 = 1.0 - ADAM_B1 ** ADAM_STEP
    bc2 = 1.0 - ADAM_B2 ** ADAM_STEP

    def body(w_ref, g_ref, m_ref, v_ref, d_ref, mo_ref, vo_ref):
        gv = g_ref[...]
        mn = ADAM_B1 * m_ref[...] + (1.0 - ADAM_B1) * gv
        vn = ADAM_B2 * v_ref[...] + (1.0 - ADAM_B2) * (gv * gv)
        mo_ref[...] = mn
        vo_ref[...] = vn
        d_ref[...] = -ADAM_LR * ((mn / bc1) / (jnp.sqrt(vn / bc2) + ADAM_EPS) + ADAM_WD * w_ref[...])

    spec = pl.BlockSpec((tr, cols), lambda i: (i, 0))
    outs = pl.pallas_call(
        body, grid=(rows // tr,), in_specs=[spec] * 4, out_specs=[spec] * 3,
        out_shape=[SDS((rows, cols), F32)] * 3, compiler_params=_cparams(), name=name)(
            *[a.reshape(rows, cols) for a in (w, g, m, v)])
    return tuple(o.reshape(shape) for o in outs)


def _adamw_layer(w, g, m, v, li, outs, *, name):
    depth, k, n = w.shape
    tr = _tile(k, max(8, ROW_BLOCK_BYTES // (4 * n)), 8)
    bc1 = 1.0 - ADAM_B1 ** ADAM_STEP
    bc2 = 1.0 - ADAM_B2 ** ADAM_STEP
    if outs is None:
        outs = tuple(lax.empty(w.shape, F32) for _ in range(4))

    def body(w_ref, g_ref, m_ref, v_ref, *rest):
        go_ref, d_ref, mo_ref, vo_ref = rest[4:]
        gv = g_ref[...]
        mn = ADAM_B1 * m_ref[...] + (1.0 - ADAM_B1) * gv
        vn = ADAM_B2 * v_ref[...] + (1.0 - ADAM_B2) * (gv * gv)
        go_ref[...] = gv
        mo_ref[...] = mn
        vo_ref[...] = vn
        d_ref[...] = -ADAM_LR * ((mn / bc1) / (jnp.sqrt(vn / bc2) + ADAM_EPS) + ADAM_WD * w_ref[...])

    layer = pl.BlockSpec((None, tr, n), lambda i: (li, i, 0))
    return tuple(pl.pallas_call(
        body, grid=(k // tr,), in_specs=[layer, pl.BlockSpec((tr, n), lambda i: (i, 0)), layer, layer] + [_ANY] * 4,
        out_specs=[layer] * 4, out_shape=[SDS(w.shape, F32)] * 4,
        input_output_aliases={4 + j: j for j in range(4)},
        compiler_params=_cparams(), name=name)(w, g, m, v, *outs))


def _mesh_pos():
    return lax.axis_index("x"), lax.axis_index("y"), lax.axis_index("c")


def _other_chips(x, y):
    return [(1 - x, y), (x, 1 - y), (1 - x, 1 - y)]


def _chip_allgather_small(vec, *, name):
    r, c = vec.shape

    def body(x_ref, out_ref, send_sems, recv_sems):
        x, y, cc = _mesh_pos()
        mine = 2 * x + y
        out_ref[mine] = x_ref[...]
        chips = _other_chips(x, y)
        sends = [pltpu.make_async_remote_copy(
            src_ref=x_ref, dst_ref=out_ref.at[mine], send_sem=send_sems.at[j], recv_sem=recv_sems.at[j],
            device_id=(px, py, cc), device_id_type=MESH) for j, (px, py) in enumerate(chips)]
        for cp in sends:
            cp.start()
        for j, (px, py) in enumerate(chips):
            pltpu.make_async_remote_copy(
                src_ref=x_ref, dst_ref=out_ref.at[2 * px + py], send_sem=send_sems.at[j],
                recv_sem=recv_sems.at[j], device_id=(px, py, cc), device_id_type=MESH).wait_recv()
        for cp in sends:
            cp.wait_send()

    return pl.pallas_call(
        body, out_shape=SDS((N_CHIPS, r, c), F32),
        in_specs=[pl.BlockSpec(memory_space=pltpu.VMEM)], out_specs=pl.BlockSpec(memory_space=pltpu.VMEM),
        scratch_shapes=[pltpu.SemaphoreType.DMA((3,)), pltpu.SemaphoreType.DMA((3,))],
        compiler_params=_cparams(), name=name)(vec)


def _allreduce_small(vec, *, name):
    r, c = vec.shape

    def body(x_ref, out_ref, buf, send_sems, recv_sems):
        x, y, cc = _mesh_pos()
        me = 4 * x + 2 * y + cc
        buf[me] = x_ref[...]

        def peer(kk):
            fx, fy, fc = (kk >> 2) & 1, (kk >> 1) & 1, kk & 1
            px = x if fx == 0 else 1 - x
            py = y if fy == 0 else 1 - y
            pc = cc if fc == 0 else 1 - cc
            return (px, py, pc), 4 * px + 2 * py + pc

        sends = []
        for kk in range(1, N_DEV):
            dev, _ = peer(kk)
            sends.append(pltpu.make_async_remote_copy(
                src_ref=x_ref, dst_ref=buf.at[me], send_sem=send_sems.at[kk - 1], recv_sem=recv_sems.at[kk - 1],
                device_id=dev, device_id_type=MESH))
        for cp in sends:
            cp.start()
        for kk in range(1, N_DEV):
            dev, slot = peer(kk)
            pltpu.make_async_remote_copy(
                src_ref=x_ref, dst_ref=buf.at[slot], send_sem=send_sems.at[kk - 1], recv_sem=recv_sems.at[kk - 1],
                device_id=dev, device_id_type=MESH).wait_recv()
        for cp in sends:
            cp.wait_send()
        acc = buf[0]
        for s in range(1, N_DEV):
            acc = acc + buf[s]
        out_ref[...] = acc

    return pl.pallas_call(
        body, out_shape=SDS((r, c), F32),
        in_specs=[pl.BlockSpec(memory_space=pltpu.VMEM)], out_specs=pl.BlockSpec(memory_space=pltpu.VMEM),
        scratch_shapes=[pltpu.VMEM((N_DEV, r, c), F32), pltpu.SemaphoreType.DMA((N_DEV - 1,)),
                        pltpu.SemaphoreType.DMA((N_DEV - 1,))],
        compiler_params=_cparams(), name=name)(vec)


_HBM = pl.BlockSpec(memory_space=pltpu.HBM)


def _half(rows, half):
    hk = rows // 2
    assert hk % SUBLANES_BF16 == 0
    return pl.ds(pl.multiple_of(half * hk, SUBLANES_BF16), hk)


_SEM = pl.BlockSpec(memory_space=pltpu.SEMAPHORE)
_ANY = pl.BlockSpec(memory_space=pl.ANY)
_DATAFLOW = pltpu.SideEffectType.DATAFLOW_SIDE_EFFECTING


def _in_hbm(a):
    return pltpu.with_memory_space_constraint(a, pltpu.HBM)


def _split_start(srcs, lands, prev, make_copy, *, name):
    nw = len(srcs)

    def body(*refs):
        src_refs, land_refs = refs[:nw], refs[nw:2 * nw]
        send_sems, recv_sems = refs[2 * nw + 1], refs[2 * nw + 2]
        token = refs[-1]
        for j in range(3):
            for i in range(nw):
                make_copy(src_refs, land_refs, send_sems, recv_sems, j, i, True).start()
        token[...] = jnp.zeros_like(token)

    outs = pl.pallas_call(
        body, name=name,
        out_shape=(pltpu.SemaphoreType.DMA((3 * nw,)), pltpu.SemaphoreType.DMA((3 * nw,)),
                   *[pltpu.HBM(a.shape, a.dtype) for a in srcs], *[pltpu.HBM(a.shape, a.dtype) for a in lands],
                   SDS((8, LANES), F32)),
        in_specs=[_HBM] * (2 * nw) + [_ANY],
        out_specs=(_SEM, _SEM, *[_HBM] * (2 * nw), pl.BlockSpec(memory_space=pltpu.VMEM)),
        input_output_aliases={i: 2 + i for i in range(2 * nw)},
        compiler_params=pltpu.CompilerParams(has_side_effects=_DATAFLOW),
    )(*[_in_hbm(a) for a in srcs], *[_in_hbm(a) for a in lands], prev)
    return outs[0], outs[1], list(outs[2:2 + nw]), list(outs[2 + nw:2 + 2 * nw]), outs[-1]


def _split_wait(pending, after, make_copy, *, name):
    send_sems, recv_sems, srcs, lands, _ = pending
    nw = len(srcs)

    def body(*refs):
        src_refs, land_refs = refs[:nw], refs[nw:2 * nw]
        send_refs, recv_refs = refs[2 * nw], refs[2 * nw + 1]
        for j in range(3):
            for i in range(nw):
                cp = make_copy(src_refs, land_refs, send_refs, recv_refs, j, i, False)
                cp.wait_send()
                cp.wait_recv()

    outs = pl.pallas_call(
        body, name=name,
        out_shape=(*[pltpu.HBM(a.shape, a.dtype) for a in srcs], *[pltpu.HBM(a.shape, a.dtype) for a in lands]),
        in_specs=[_HBM] * (2 * nw) + [_SEM, _SEM, _ANY], out_specs=tuple([_HBM] * (2 * nw)),
        input_output_aliases={i: i for i in range(2 * nw)},
        compiler_params=pltpu.CompilerParams(has_side_effects=_DATAFLOW),
    )(*srcs, *lands, send_sems, recv_sems, after)
    return list(outs[:nw]), list(outs[nw:])


def _gather_copy(shapes):
    def make(src_refs, land_refs, send_sems, recv_sems, j, i, outgoing):
        x, y, cc = _mesh_pos()
        px, py = _other_chips(x, y)[j]
        rows = _half(shapes[i][0], cc)
        slot = 2 * x + y if outgoing else 2 * px + py
        return pltpu.make_async_remote_copy(
            src_ref=src_refs[i].at[rows], dst_ref=land_refs[i].at[slot, rows],
            send_sem=send_sems.at[j * len(shapes) + i], recv_sem=recv_sems.at[j * len(shapes) + i],
            device_id=(px, py, cc), device_id_type=MESH)
    return make


def _gather_forward(lands, *, name):
    nw = len(lands)

    def body(*refs):
        in_refs, out_refs = refs[:nw], refs[nw:2 * nw]
        send_sems, recv_sems = refs[2 * nw:]
        x, y, cc = _mesh_pos()
        cps = []
        for j, (px, py) in enumerate(_other_chips(x, y)):
            for i in range(nw):
                rows = _half(lands[i].shape[1], cc)
                cps.append(pltpu.make_async_remote_copy(
                    src_ref=in_refs[i].at[2 * px + py, rows], dst_ref=out_refs[i].at[2 * px + py, rows],
                    send_sem=send_sems.at[j, i], recv_sem=recv_sems.at[j, i],
                    device_id=(x, y, 1 - cc), device_id_type=MESH))
        for cp in cps:
            cp.start()
        for cp in cps:
            cp.wait()

    return pl.pallas_call(
        body, out_shape=[SDS(a.shape, a.dtype) for a in lands], in_specs=[_HBM] * nw, out_specs=[_HBM] * nw,
        input_output_aliases={i: i for i in range(nw)},
        scratch_shapes=[pltpu.SemaphoreType.DMA((3, nw)), pltpu.SemaphoreType.DMA((3, nw))], name=name)(*lands)


def _grads_to_sibling(gs, *, name):
    nw = len(gs)

    def body(*refs):
        g_refs, out_refs = refs[:nw], refs[nw:2 * nw]
        send_sems, recv_sems = refs[2 * nw:]
        x, y, cc = _mesh_pos()
        cps = [pltpu.make_async_remote_copy(
            src_ref=g_refs[i].at[pl.ds(0, N_CHIPS), _half(gs[i].shape[1], 1 - cc)], dst_ref=out_refs[i],
            send_sem=send_sems.at[i], recv_sem=recv_sems.at[i], device_id=(x, y, 1 - cc), device_id_type=MESH)
            for i in range(nw)]
        for cp in cps:
            cp.start()
        for cp in cps:
            cp.wait()

    return pl.pallas_call(
        body, out_shape=[SDS((N_CHIPS, g.shape[1] // 2, g.shape[2]), g.dtype) for g in gs],
        in_specs=[_HBM] * nw, out_specs=[_HBM] * nw,
        scratch_shapes=[pltpu.SemaphoreType.DMA((nw,)), pltpu.SemaphoreType.DMA((nw,))], name=name)(*gs)


def _chips_copy(nw):
    def make(src_refs, land_refs, send_sems, recv_sems, j, i, outgoing):
        x, y, cc = _mesh_pos()
        px, py = _other_chips(x, y)[j]
        return pltpu.make_async_remote_copy(
            src_ref=src_refs[i].at[2 * px + py], dst_ref=land_refs[i].at[j],
            send_sem=send_sems.at[j * nw + i], recv_sem=recv_sems.at[j * nw + i],
            device_id=(px, py, cc), device_id_type=MESH)
    return make


def _share_halves(tots, *, name):
    nw = len(tots)

    def body(*refs):
        t_refs, out_refs = refs[:nw], refs[nw:2 * nw]
        send_sems, recv_sems = refs[2 * nw:]
        x, y, cc = _mesh_pos()
        cps = [pltpu.make_async_remote_copy(
            src_ref=t_refs[i], dst_ref=out_refs[i], send_sem=send_sems.at[i], recv_sem=recv_sems.at[i],
            device_id=(x, y, 1 - cc), device_id_type=MESH) for i in range(nw)]
        for cp in cps:
            cp.start()
        for cp in cps:
            cp.wait()

    return pl.pallas_call(
        body, out_shape=[SDS(tt.shape, tt.dtype) for tt in tots],
        in_specs=[_HBM] * nw, out_specs=[_HBM] * nw,
        scratch_shapes=[pltpu.SemaphoreType.DMA((nw,)), pltpu.SemaphoreType.DMA((nw,))], name=name)(*tots)


def _pack_small(arrays):
    flat = jnp.concatenate([a.astype(F32).reshape(-1) for a in arrays])
    n = flat.shape[0]
    padded = -(-n // (8 * LANES)) * (8 * LANES)
    return jnp.pad(flat, (0, padded - n)).reshape(padded // LANES, LANES)


def _unpack_small(flat, shapes):
    out, off = [], 0
    for shp in shapes:
        n = math.prod(shp)
        out.append(flat[off:off + n].reshape(shp))
        off += n
    return out


BIG = (("w_in", 1), ("w_conv_out", 1), ("w_gqa_out", 1), ("w_mla_uq", 1), ("w_mla_ukv", 1), ("w_mla_out", 1),
       ("w_out", 0), ("w_ffn_gate", 1), ("w_ffn_up", 1), ("w_ffn_down", 0))
BIG_NAMES = tuple(nm for nm, _ in BIG)
BIG_GROUPS = (("w_in",), tuple(nm for nm in BIG_NAMES if nm != "w_in"))


def _cols_to_slots(full):
    k, n = full.shape
    return full.reshape(k, N_CHIPS, n // N_CHIPS).transpose(1, 0, 2)


def _slots_to_cols(slots):
    s, k, n = slots.shape
    return slots.transpose(1, 0, 2).reshape(k, s * n)


def _step(p, mom_m, mom_v, x, loss_target):
    b, seq, d = x.shape
    depth = p["w_in"].shape[0]
    l_valid = N_META + seq
    lp = -(-l_valid // LANES) * LANES
    t = b * lp
    c_conv = d // 2
    q_rank = p["mla_q_norm_g"].shape[1]
    kv_rank = p["mla_kv_norm_g"].shape[1]
    qw, kw = GQA_HEADS * HEAD, GQA_KV_HEADS * HEAD
    off_q = 2 * c_conv
    off_k = off_q + qw
    off_v = off_k + kw
    off_cq = off_v + kw
    off_ckv = off_cq + q_rank
    off_gate = off_ckv + kv_rank
    off_kpe = off_gate + N_BRANCH * d
    d_inp = off_kpe + LANES
    assert off_cq % q_rank == 0 and off_ckv % kv_rank == 0
    xi, yi, ci = _mesh_pos()
    chip = 2 * xi + yi

    small_sh = [p["meta_tokens"], p["conv_dw"]]
    small_all = _chip_allgather_small(_pack_small(small_sh), name="gather_small")
    got = [_unpack_small(small_all[s].reshape(-1), [a.shape for a in small_sh]) for s in range(N_CHIPS)]
    meta_full = jnp.concatenate([g[0] for g in got], axis=1)
    conv_dw_full = jnp.concatenate([g[1] for g in got], axis=-1).reshape(depth, CONV_K, c_conv)

    gather_copies = [_gather_copy([p[nm].shape[1:] for nm in grp]) for grp in BIG_GROUPS]
    gathers = []
    order = small_all
    for li in range(depth):
        gathers.append([])
        for gi, grp in enumerate(BIG_GROUPS):
            own = [p[nm][li].astype(BF16) for nm in grp]
            lands = [lax.empty((N_CHIPS,) + o.shape, BF16) for o in own]
            gathers[li].append(_split_start(own, lands, order, gather_copies[gi],
                                            name="gather_start_%d_%d" % (li, gi)))
            order = gathers[li][gi][4]
    meta_full = meta_full + order[0, 0]

    def gathered(li, gi, after):
        own, lands = _split_wait(gathers[li][gi], after, gather_copies[gi], name="gather_wait_%d_%d" % (li, gi))
        got = _gather_forward(lands, name="gather_forward_%d" % gi)
        return {nm: lax.dynamic_update_slice(g, o[None], (chip, 0, 0)) for nm, g, o in zip(BIG_GROUPS[gi], got, own)}

    def in_proj_weight(slots):
        w_in = _slots_to_cols(slots)
        return jnp.concatenate(
            [w_in[:, :off_gate], w_in[:, off_gate + MLA_ROPE:], w_in[:, off_gate:off_gate + MLA_ROPE],
             jnp.zeros((d, LANES - MLA_ROPE), BF16)], axis=1)

    def other_weights(full):
        full["w_out"] = full["w_out"].reshape(-1, d)
        full["w_ffn_down"] = full["w_ffn_down"].reshape(-1, d)
        uq = _slots_to_cols(full["w_mla_uq"]).reshape(q_rank, MLA_HEADS, HEAD + MLA_ROPE)
        full["w_uq"] = jnp.pad(uq, ((0, 0), (0, 0), (0, MLA_SLOT - HEAD - MLA_ROPE))).reshape(
            q_rank, MLA_HEADS * MLA_SLOT)
        ukv = _slots_to_cols(full["w_mla_ukv"]).reshape(kv_rank, MLA_HEADS, 2 * HEAD)
        full["w_uk"] = jnp.pad(ukv[:, :, :HEAD], ((0, 0), (0, 0), (0, MLA_SLOT - HEAD))).reshape(
            kv_rank, MLA_HEADS * MLA_SLOT)
        full["w_uv"] = ukv[:, :, HEAD:].reshape(kv_rank, MLA_HEADS * HEAD)
        return full

    cos_g, sin_g = _rope_tables(lp, seq, HEAD, HEAD)
    cos_m, sin_m = _rope_tables(lp, seq, MLA_ROPE, LANES)
    gqa_scale = 1.0 / math.sqrt(HEAD)
    mla_scale = 1.0 / math.sqrt(HEAD + MLA_ROPE)
    attn_kw = dict(batch=b, lp=lp, l_valid=l_valid)
    gqa_kw = dict(heads=GQA_HEADS, kv_heads=GQA_KV_HEADS, dk=HEAD, dv=HEAD, scale=gqa_scale, **attn_kw)
    mla_kw = dict(heads=MLA_HEADS, kv_heads=MLA_HEADS, dk=MLA_SLOT, dv=HEAD, scale=mla_scale, **attn_kw)

    h = jnp.concatenate([jnp.broadcast_to(meta_full[None], (b, N_META, d)), x,
                         jnp.zeros((b, lp - l_valid, d), F32)], axis=1).reshape(t, d)
    saved, weights = [], []
    for li in range(depth):
        w = {"w_in": in_proj_weight(gathered(li, 0, h)["w_in"])}
        weights.append(w)
        row = lambda nm: p[nm][li][None, :]
        s = {"h_in": h}
        s["u"] = _rmsnorm_fwd(h, row("mix_norm_g"), width=d, cblk=0, name="mix_norm")
        proj = _mm(s["u"], w["w_in"], name="in_proj")
        s["proj"] = proj
        proj3 = proj.reshape(b, lp, d_inp)
        s["yconv"] = _conv_fwd(proj3, conv_dw_full[li], row("conv_b"), c_conv=c_conv, l_valid=l_valid,
                               name="conv").reshape(t, c_conv)
        s["sconv"] = _ln_silu_fwd(s["yconv"], row("conv_ln_g"), row("conv_ln_b"), name="conv_ln_silu")
        w.update(other_weights(gathered(li, 1, s["sconv"])))
        s["ya"] = _mm(s["sconv"], w["w_conv_out"], b_sharded=True, name="conv_out")
        s["gq"], s["gk"], s["gv"] = _gqa_prep_fwd(
            proj, row("gqa_q_norm_g"), row("gqa_k_norm_g"), cos_g, sin_g,
            q_off=off_q, k_off=off_k, v_off=off_v, lp=lp, name="gqa_prep")
        s["go"] = _attn_fwd(s["gq"], s["gk"], s["gv"], name="gqa_attn", **gqa_kw)
        s["yb"] = _mm(s["go"], w["w_gqa_out"], b_sharded=True, name="gqa_out")
        s["cqn"] = _rmsnorm_fwd(proj, row("mla_q_norm_g"), width=q_rank, cblk=off_cq // q_rank, name="mla_q_norm")
        s["kvn"] = _rmsnorm_fwd(proj, row("mla_kv_norm_g"), width=kv_rank, cblk=off_ckv // kv_rank,
                                name="mla_kv_norm")
        q_pre = _mm(s["cqn"], w["w_uq"], name="mla_uq")
        k_pre = _mm(s["kvn"], w["w_uk"], name="mla_uk")
        s["mv"] = _mm(s["kvn"], w["w_uv"], out_dtype=BF16, name="mla_uv")
        s["mq"], s["mk"] = _mla_rope_fwd(q_pre, k_pre, proj, cos_m, sin_m, kpe_off=off_kpe, lp=lp, name="mla_rope")
        s["mo"] = _attn_fwd(s["mq"], s["mk"], s["mv"], name="mla_attn", **mla_kw)
        s["yc"] = _mm(s["mo"], w["w_mla_out"], b_sharded=True, name="mla_out")
        s["merged"] = _merge_fwd(proj, row("gate_b"), s["ya"], s["yb"], s["yc"], gate_off=off_gate, name="merge")
        h = _mm(s["merged"], w["w_out"], add=h, name="mix_out")
        s["h_mid"] = h
        s["v"] = _rmsnorm_fwd(h, row("ffn_norm_g"), width=d, cblk=0, name="ffn_norm")
        s["gate"] = _mm(s["v"], w["w_ffn_gate"], b_sharded=True, out_dtype=BF16, name="ffn_gate")
        s["up"] = _mm(s["v"], w["w_ffn_up"], b_sharded=True, out_dtype=BF16, name="ffn_up")
        s["act"] = _swiglu_fwd(s["gate"], s["up"], name="swiglu")
        h = _mm(s["act"], w["w_ffn_down"], add=h, name="ffn_down")
        saved.append(s)

    target = jnp.pad(loss_target, ((0, 0), (N_META, lp - l_valid), (0, 0))).reshape(t, d)
    loss_part, dh, dhb, g_final = _loss_head(h, p["final_norm_g"][None, :], target, lp=lp, l_valid=l_valid,
                                             name="loss_head")
    loss = lax.psum(loss_part[0, 0], ("x", "y", "c"))

    small_names = ["mix_norm_g", "conv_dw", "conv_b", "conv_ln_g", "conv_ln_b", "gqa_q_norm_g", "gqa_k_norm_g",
                   "mla_q_norm_g", "mla_kv_norm_g", "gate_b", "ffn_norm_g"]
    small_g = {nm: [None] * depth for nm in small_names}
    adam = {}
    chips_copies = [_chips_copy(len(grp)) for grp in BIG_GROUPS]

    def start_reduce(lj, gi, big):
        grp = BIG_GROUPS[gi]
        glist = [big[nm] for nm in grp]
        from_sib = _grads_to_sibling(glist, name="grads_to_sibling_%d" % gi)
        parts = []
        for nm, g, fs in zip(grp, glist, from_sib):
            hk = g.shape[1] // 2
            mine = lax.dynamic_slice_in_dim(g, ci * hk, hk, axis=1)
            rows = (1, N_CHIPS * hk, g.shape[2])
            parts.append(_sum_slots([mine.reshape(rows), fs.reshape(rows)], out_dtype=BF16,
                                    name="sum_sibling_" + nm).reshape(fs.shape))
        lands = [lax.empty((3,) + pt.shape[1:], BF16) for pt in parts]
        return lj, gi, _split_start(parts, lands, order, chips_copies[gi], name="chips_start_%d_%d" % (lj, gi))

    def finish_reduce(item, after):
        lj, gi, pending = item
        grp = BIG_GROUPS[gi]
        parts, from_chips = _split_wait(pending, after, chips_copies[gi], name="chips_wait_%d_%d" % (lj, gi))
        tots = [_sum_slots([lax.dynamic_index_in_dim(pt, chip, 0, keepdims=True), fc], out_dtype=F32,
                           name="sum_chips_" + nm) for nm, pt, fc in zip(grp, parts, from_chips)]
        others = _share_halves(tots, name="share_halves_%d" % gi)
        for nm, tot, oth in zip(grp, tots, others):
            g = jnp.concatenate([jnp.where(ci == 0, tot, oth), jnp.where(ci == 0, oth, tot)], axis=0)
            adam[nm] = _adamw_layer(p[nm], g, mom_m[nm], mom_v[nm], lj, adam.get(nm), name="adamw_" + nm)

    in_flight = []
    for li in reversed(range(depth)):
        w, s = weights[li], saved[li]
        row = lambda nm: p[nm][li][None, :]
        big = {}
        dact = _mm(dhb, w["w_ffn_down"], mode="nt", deps=[item[2][4] for item in in_flight], name="d_ffn_act")
        big["w_ffn_down"] = _mm(s["act"], dhb, mode="tn", out_dtype=BF16, deps=[item[2][4] for item in in_flight],
                                name="dw_ffn_down").reshape(N_CHIPS, -1, d)
        dgate, dup = _swiglu_bwd(s["gate"], s["up"], dact, name="d_swiglu")
        dv = _mm(dgate, w["w_ffn_gate"], mode="nt", b_sharded=True, name="d_ffn_v_gate")
        dv = _mm(dup, w["w_ffn_up"], mode="nt", b_sharded=True, add=dv, name="d_ffn_v_up")
        big["w_ffn_gate"] = _mm(s["v"], dgate, mode="tn", out_dtype=BF16, out_shards=N_CHIPS, name="dw_ffn_gate")
        big["w_ffn_up"] = _mm(s["v"], dup, mode="tn", out_dtype=BF16, out_shards=N_CHIPS, name="dw_ffn_up")
        dh, dhb, g = _rmsnorm_bwd(s["h_mid"], row("ffn_norm_g"), dv, width=d, cblk=0, res=dh, name="d_ffn_norm")
        small_g["ffn_norm_g"][li] = g[0]
        dm = _mm(dhb, w["w_out"], mode="nt", name="d_merged")
        big["w_out"] = _mm(s["merged"], dhb, mode="tn", out_dtype=BF16, name="dw_out").reshape(N_CHIPS, -1, d)
        dya, dyb, dyc, dl0, dl1, dl2, db0, db1, db2 = _merge_bwd(
            s["proj"], row("gate_b"), s["ya"], s["yb"], s["yc"], dm, gate_off=off_gate, name="d_merge")
        small_g["gate_b"][li] = jnp.concatenate([db0[0], db1[0], db2[0]])
        ds = _mm(dya, w["w_conv_out"], mode="nt", b_sharded=True, name="d_conv_s")
        big["w_conv_out"] = _mm(s["sconv"], dya, mode="tn", out_dtype=BF16, out_shards=N_CHIPS, name="dw_conv_out")
        dyconv, g, gb = _ln_silu_bwd(s["yconv"], row("conv_ln_g"), row("conv_ln_b"), ds, name="d_conv_ln_silu")
        small_g["conv_ln_g"][li], small_g["conv_ln_b"][li] = g[0], gb[0]
        proj3 = s["proj"].reshape(b, lp, d_inp)
        da, dgte, ddw, dcb = _conv_bwd(proj3, conv_dw_full[li], dyconv.reshape(b, lp, c_conv), c_conv=c_conv,
                                       l_valid=l_valid, name="d_conv")
        small_g["conv_dw"][li], small_g["conv_b"][li] = ddw, dcb[0]
        dgo = _mm(dyb, w["w_gqa_out"], mode="nt", b_sharded=True, out_dtype=BF16, name="d_gqa_o")
        big["w_gqa_out"] = _mm(s["go"], dyb, mode="tn", out_dtype=BF16, out_shards=N_CHIPS, name="dw_gqa_out")
        dgq, dgk, dgv = _attn_bwd(s["gq"], s["gk"], s["gv"], dgo, name="d_gqa_attn", **gqa_kw)
        dq_g, dk_g, gq, gk = _gqa_prep_bwd(s["proj"], row("gqa_q_norm_g"), row("gqa_k_norm_g"), cos_g, sin_g,
                                           dgq, dgk, q_off=off_q, k_off=off_k, lp=lp, name="d_gqa_prep")
        small_g["gqa_q_norm_g"][li], small_g["gqa_k_norm_g"][li] = gq[0], gk[0]
        dmo = _mm(dyc, w["w_mla_out"], mode="nt", b_sharded=True, out_dtype=BF16, name="d_mla_o")
        big["w_mla_out"] = _mm(s["mo"], dyc, mode="tn", out_dtype=BF16, out_shards=N_CHIPS, name="dw_mla_out")
        dmq, dmk, dmv = _attn_bwd(s["mq"], s["mk"], s["mv"], dmo, name="d_mla_attn", **mla_kw)
        dq_pre, dk_pre, dkpe = _mla_rope_bwd(dmq, dmk, cos_m, sin_m, lp=lp, name="d_mla_rope")
        dmvb = dmv.astype(BF16)
        g_uq = _mm(s["cqn"], dq_pre, mode="tn", out_dtype=BF16, name="dw_mla_uq")
        g_uk = _mm(s["kvn"], dk_pre, mode="tn", out_dtype=BF16, name="dw_mla_uk")
        g_uv = _mm(s["kvn"], dmvb, mode="tn", out_dtype=BF16, name="dw_mla_uv")
        big["w_mla_uq"] = _cols_to_slots(
            g_uq.reshape(q_rank, MLA_HEADS, MLA_SLOT)[:, :, :HEAD + MLA_ROPE].reshape(q_rank, -1))
        big["w_mla_ukv"] = _cols_to_slots(jnp.concatenate(
            [g_uk.reshape(kv_rank, MLA_HEADS, MLA_SLOT)[:, :, :HEAD], g_uv.reshape(kv_rank, MLA_HEADS, HEAD)],
            axis=-1).reshape(kv_rank, -1))
        dcqn = _mm(dq_pre, w["w_uq"], mode="nt", name="d_mla_cqn")
        dkvn = _mm(dk_pre, w["w_uk"], mode="nt", name="d_mla_kvn_k")
        dkvn = _mm(dmvb, w["w_uv"], mode="nt", add=dkvn, name="d_mla_kvn_v")
        _, dcq, g = _rmsnorm_bwd(s["proj"], row("mla_q_norm_g"), dcqn, width=q_rank, cblk=off_cq // q_rank,
                                 name="d_mla_q_norm")
        small_g["mla_q_norm_g"][li] = g[0]
        _, dckv, g = _rmsnorm_bwd(s["proj"], row("mla_kv_norm_g"), dkvn, width=kv_rank, cblk=off_ckv // kv_rank,
                                  name="d_mla_kv_norm")
        small_g["mla_kv_norm_g"][li] = g[0]
        for item in in_flight:
            finish_reduce(item, dckv)
        in_flight = [start_reduce(li, 1, big)]
        dproj = jnp.concatenate(
            [da.reshape(t, c_conv), dgte.reshape(t, c_conv), dq_g, dk_g, dgv.astype(BF16), dcq, dckv,
             dl0, dl1, dl2, dkpe], axis=1)
        du = _mm(dproj, w["w_in"], mode="nt", deps=[in_flight[0][2][4]], name="d_u")
        g_in = _mm(s["u"], dproj, mode="tn", out_dtype=BF16, deps=[in_flight[0][2][4]], name="dw_in")
        big["w_in"] = _cols_to_slots(jnp.concatenate(
            [g_in[:, :off_gate], g_in[:, off_kpe:off_kpe + MLA_ROPE], g_in[:, off_gate:off_kpe]], axis=1))
        dh, dhb, g = _rmsnorm_bwd(s["h_in"], row("mix_norm_g"), du, width=d, cblk=0, res=dh, name="d_mix_norm")
        small_g["mix_norm_g"][li] = g[0]
        in_flight.append(start_reduce(li, 0, big))

    dh3 = dh.reshape(b, lp, d)
    grad_x = dh3[:, N_META:l_valid]
    g_meta = _meta_grad(dh3, name="meta_grad")
    for item in in_flight:
        finish_reduce(item, g_meta)

    small_list = [g_meta, g_final[0]] + [jnp.stack(small_g[nm]) for nm in small_names]
    red = _allreduce_small(_pack_small(small_list), name="allreduce_small")
    red = _unpack_small(red.reshape(-1), [a.shape for a in small_list])
    grads = {"meta_tokens": lax.dynamic_slice_in_dim(red[0], chip * (d // N_CHIPS), d // N_CHIPS, axis=1),
             "final_norm_g": red[1]}
    for nm, val in zip(small_names, red[2:]):
        grads[nm] = val
    cs = c_conv // N_CHIPS
    grads["conv_dw"] = lax.dynamic_slice_in_dim(grads["conv_dw"], chip * cs, cs, axis=2).reshape(p["conv_dw"].shape)

    delta, new_m, new_v = {}, {}, {}
    for nm in p:
        if nm in adam:
            grads[nm], delta[nm], new_m[nm], new_v[nm] = adam[nm]
        else:
            delta[nm], new_m[nm], new_v[nm] = _adamw(p[nm], grads[nm], mom_m[nm], mom_v[nm], name="adamw_" + nm)
    return loss, grad_x, grads, delta, new_m, new_v


WEIGHT_NAMES = ("meta_tokens", "mix_norm_g", "w_in", "conv_dw", "conv_b", "conv_ln_g", "conv_ln_b", "w_conv_out",
                "gqa_q_norm_g", "gqa_k_norm_g", "w_gqa_out", "mla_q_norm_g", "w_mla_uq", "mla_kv_norm_g",
                "w_mla_ukv", "w_mla_out", "gate_b", "w_out", "ffn_norm_g", "w_ffn_gate", "w_ffn_up", "w_ffn_down",
                "final_norm_g")


def kernel(x, meta_tokens, mix_norm_g, w_in, conv_dw, conv_b, conv_ln_g, conv_ln_b, w_conv_out, gqa_q_norm_g, gqa_k_norm_g, w_gqa_out, mla_q_norm_g, w_mla_uq, mla_kv_norm_g, w_mla_ukv, w_mla_out, gate_b, w_out, ffn_norm_g, w_ffn_gate, w_ffn_up, w_ffn_down, final_norm_g, loss_target, m_meta_tokens, m_mix_norm_g, m_w_in, m_conv_dw, m_conv_b, m_conv_ln_g, m_conv_ln_b, m_w_conv_out, m_gqa_q_norm_g, m_gqa_k_norm_g, m_w_gqa_out, m_mla_q_norm_g, m_w_mla_uq, m_mla_kv_norm_g, m_w_mla_ukv, m_w_mla_out, m_gate_b, m_w_out, m_ffn_norm_g, m_w_ffn_gate, m_w_ffn_up, m_w_ffn_down, m_final_norm_g, v_meta_tokens, v_mix_norm_g, v_w_in, v_conv_dw, v_conv_b, v_conv_ln_g, v_conv_ln_b, v_w_conv_out, v_gqa_q_norm_g, v_gqa_k_norm_g, v_w_gqa_out, v_mla_q_norm_g, v_w_mla_uq, v_mla_kv_norm_g, v_w_mla_ukv, v_w_mla_out, v_gate_b, v_w_out, v_ffn_norm_g, v_w_ffn_gate, v_w_ffn_up, v_w_ffn_down, v_final_norm_g):
    ws = (meta_tokens, mix_norm_g, w_in, conv_dw, conv_b, conv_ln_g, conv_ln_b, w_conv_out, gqa_q_norm_g,
          gqa_k_norm_g, w_gqa_out, mla_q_norm_g, w_mla_uq, mla_kv_norm_g, w_mla_ukv, w_mla_out, gate_b, w_out,
          ffn_norm_g, w_ffn_gate, w_ffn_up, w_ffn_down, final_norm_g)
    ms = (m_meta_tokens, m_mix_norm_g, m_w_in, m_conv_dw, m_conv_b, m_conv_ln_g, m_conv_ln_b, m_w_conv_out,
          m_gqa_q_norm_g, m_gqa_k_norm_g, m_w_gqa_out, m_mla_q_norm_g, m_w_mla_uq, m_mla_kv_norm_g, m_w_mla_ukv,
          m_w_mla_out, m_gate_b, m_w_out, m_ffn_norm_g, m_w_ffn_gate, m_w_ffn_up, m_w_ffn_down, m_final_norm_g)
    vs = (v_meta_tokens, v_mix_norm_g, v_w_in, v_conv_dw, v_conv_b, v_conv_ln_g, v_conv_ln_b, v_w_conv_out,
          v_gqa_q_norm_g, v_gqa_k_norm_g, v_w_gqa_out, v_mla_q_norm_g, v_w_mla_uq, v_mla_kv_norm_g, v_w_mla_ukv,
          v_w_mla_out, v_gate_b, v_w_out, v_ffn_norm_g, v_w_ffn_gate, v_w_ffn_up, v_w_ffn_down, v_final_norm_g)
    p = dict(zip(WEIGHT_NAMES, ws))
    loss, grad_x, grads, delta, new_m, new_v = _step(p, dict(zip(WEIGHT_NAMES, ms)), dict(zip(WEIGHT_NAMES, vs)),
                                                     x, loss_target)
    return (loss, grad_x, *[grads[n] for n in WEIGHT_NAMES], *[delta[n] for n in WEIGHT_NAMES],
            *[new_m[n] for n in WEIGHT_NAMES], *[new_v[n] for n in WEIGHT_NAMES])
```

```python
import math

import jax
import jax.numpy as jnp
from jax import lax
from jax.experimental import pallas as pl
from jax.experimental.pallas import tpu as pltpu

F32 = jnp.float32
BF16 = jnp.bfloat16
SDS = jax.ShapeDtypeStruct
MESH = pl.DeviceIdType.MESH

N_META = 16
GRID_W = 64
ROPE_THETA = 10000.0
NORM_EPS = 1e-6
CONV_K = 31
CONV_HALO = 16
HEAD = 128
GQA_HEADS = 8
GQA_KV_HEADS = 2
MLA_HEADS = 8
MLA_ROPE = 64
MLA_SLOT = 256
N_BRANCH = 3

ADAM_LR = 0.001
ADAM_B1 = 0.9
ADAM_B2 = 0.999
ADAM_EPS = 1e-08
ADAM_WD = 0.01
ADAM_STEP = 10

LANES = 128
SUBLANES_BF16 = 16
VMEM_LIMIT_BYTES = 56 * 2 ** 20
ROW_BLOCK_BYTES = 2 << 20
ATTN_Q_ROWS = 1088
MM_BLOCK_BYTES = 48 * 2 ** 20
N_CHIPS = 4
N_DEV = 8


def _cparams():
    return pltpu.CompilerParams(vmem_limit_bytes=VMEM_LIMIT_BYTES)


def _tile(n, cap, mult):
    best = None
    for d in range(mult, min(n, cap) + 1, mult):
        if n % d == 0:
            best = d
    return n if best is None else best


def _row_tile(rows, width, mult=SUBLANES_BF16):
    return _tile(rows, max(mult, ROW_BLOCK_BYTES // (4 * width)), mult)


def _sigmoid(x):
    return 1.0 / (1.0 + jnp.exp(-x))


def _mm(a, b, *, mode="nn", add=None, out_dtype=F32, b_sharded=False, out_shards=None, deps=(), name):
    if b_sharded:
        s_b, r_b, c_b = b.shape
        b_rows, b_cols = r_b, s_b * c_b
    else:
        b_rows, b_cols = b.shape
    if mode == "nn":
        (m, k), n = a.shape, b_cols
        assert b_rows == k
    elif mode == "nt":
        (m, k), n = a.shape, b_rows
        assert b_cols == k
    else:
        (k, m), n = a.shape, b_cols
        assert b_rows == k
    n_unit = n
    if b_sharded and mode != "nt":
        n_unit = c_b
    if out_shards is not None:
        assert n % out_shards == 0
        n_unit = math.gcd(n_unit, n // out_shards)
    k_unit = c_b if (b_sharded and mode == "nt") else k
    if mode == "tn":
        tm = _tile(m, 1024, LANES)
        tk = _tile(k_unit, 2176, SUBLANES_BF16)
    else:
        tm = _tile(m, 1088, SUBLANES_BF16)
        tk = _tile(k_unit, 2176, LANES)
    nk = k // tk
    out_bytes = jnp.dtype(out_dtype).itemsize
    tn = LANES
    for cap in (2176, 1408, 1024, 512, 256):
        tn = _tile(n_unit, cap, LANES)
        blocks = tm * tk * a.dtype.itemsize + tk * tn * b.dtype.itemsize + tm * tn * out_bytes
        blocks += tm * tn * 4 if add is not None else 0
        scratch = tm * tn * 4 if (nk > 1 and out_dtype != F32) else 0
        if 2 * blocks + scratch <= MM_BLOCK_BYTES:
            break
    if mode == "nn":
        a_spec = pl.BlockSpec((tm, tk), lambda i, j, kk: (i, kk))
        dn = (((1,), (0,)), ((), ()))
        if b_sharded:
            per = c_b // tn
            b_spec = pl.BlockSpec((None, tk, tn), lambda i, j, kk: (j // per, kk, j % per))
        else:
            b_spec = pl.BlockSpec((tk, tn), lambda i, j, kk: (kk, j))
    elif mode == "nt":
        a_spec = pl.BlockSpec((tm, tk), lambda i, j, kk: (i, kk))
        dn = (((1,), (1,)), ((), ()))
        if b_sharded:
            per = c_b // tk
            b_spec = pl.BlockSpec((None, tn, tk), lambda i, j, kk: (kk // per, j, kk % per))
        else:
            b_spec = pl.BlockSpec((tn, tk), lambda i, j, kk: (j, kk))
    else:
        a_spec = pl.BlockSpec((tk, tm), lambda i, j, kk: (kk, i))
        dn = (((0,), (0,)), ((), ()))
        if b_sharded:
            per = c_b // tn
            b_spec = pl.BlockSpec((None, tk, tn), lambda i, j, kk: (j // per, kk, j % per))
        else:
            b_spec = pl.BlockSpec((tk, tn), lambda i, j, kk: (kk, j))
    if out_shards is None:
        o_spec = pl.BlockSpec((tm, tn), lambda i, j, kk: (i, j))
        out_shape = SDS((m, n), out_dtype)
    else:
        per_o = (n // out_shards) // tn
        o_spec = pl.BlockSpec((None, tm, tn), lambda i, j, kk: (j // per_o, i, j % per_o))
        out_shape = SDS((out_shards, m, n // out_shards), out_dtype)
    has_add = add is not None
    assert not (has_add and out_shards is not None)
    deps = list(deps)
    use_acc = nk > 1 and out_dtype != F32

    def body(a_ref, b_ref, *rest):
        add_ref = rest[0] if has_add else None
        o_ref = rest[int(has_add) + len(deps)]
        kk = pl.program_id(2)
        part = lax.dot_general(a_ref[...].astype(BF16), b_ref[...].astype(BF16), dn,
                               preferred_element_type=F32)
        if nk == 1:
            o_ref[...] = (part + add_ref[...] if has_add else part).astype(o_ref.dtype)
            return
        acc = rest[-1] if use_acc else o_ref

        @pl.when(kk == 0)
        def _():
            acc[...] = part

        @pl.when((kk > 0) & (kk < nk - 1))
        def _():
            acc[...] += part

        @pl.when(kk == nk - 1)
        def _():
            r = acc[...] + part
            if has_add:
                r = r + add_ref[...]
            o_ref[...] = r.astype(o_ref.dtype)

    in_specs = [a_spec, b_spec] + ([o_spec] if has_add else []) + [_ANY] * len(deps)
    args = (a, b) + ((add,) if has_add else ()) + tuple(deps)
    return pl.pallas_call(
        body, grid=(m // tm, n // tn, nk), in_specs=in_specs, out_specs=o_spec,
        out_shape=out_shape, scratch_shapes=[pltpu.VMEM((tm, tn), F32)] if use_acc else [],
        compiler_params=_cparams(), name=name)(*args)


def _rmsnorm_fwd(x, g, *, width, cblk, name):
    t = x.shape[0]
    tr = _row_tile(t, width)

    def body(x_ref, g_ref, o_ref):
        xv = x_ref[...]
        r = lax.rsqrt(jnp.mean(xv * xv, axis=-1, keepdims=True) + NORM_EPS)
        o_ref[...] = (xv * r * g_ref[...]).astype(o_ref.dtype)

    return pl.pallas_call(
        body, grid=(t // tr,),
        in_specs=[pl.BlockSpec((tr, width), lambda i: (i, cblk)), pl.BlockSpec((1, width), lambda i: (0, 0))],
        out_specs=pl.BlockSpec((tr, width), lambda i: (i, 0)),
        out_shape=SDS((t, width), BF16), compiler_params=_cparams(), name=name)(x, g)


def _rmsnorm_bwd(x, g, dy, *, width, cblk, res=None, name):
    t = x.shape[0]
    tr = _row_tile(t, width)
    has_res = res is not None

    def body(x_ref, g_ref, dy_ref, *rest):
        if has_res:
            res_ref, dx_ref, dxb_ref, dg_ref = rest
        else:
            dx_ref, dxb_ref, dg_ref = rest
        xv = x_ref[...]
        dyv = dy_ref[...].astype(F32)
        r = lax.rsqrt(jnp.mean(xv * xv, axis=-1, keepdims=True) + NORM_EPS)
        gy = dyv * g_ref[...]
        dx = r * gy - xv * (r * r * r) * jnp.mean(gy * xv, axis=-1, keepdims=True)
        if has_res:
            dx = dx + res_ref[...]
        dx_ref[...] = dx
        dxb_ref[...] = dx.astype(BF16)
        part = jnp.sum(dyv * xv * r, axis=0, keepdims=True)

        @pl.when(pl.program_id(0) == 0)
        def _():
            dg_ref[...] = part

        @pl.when(pl.program_id(0) > 0)
        def _():
            dg_ref[...] += part

    row = pl.BlockSpec((tr, width), lambda i: (i, 0))
    vec = pl.BlockSpec((1, width), lambda i: (0, 0))
    in_specs = [pl.BlockSpec((tr, width), lambda i: (i, cblk)), vec, row] + ([row] if has_res else [])
    args = (x, g, dy) + ((res,) if has_res else ())
    return pl.pallas_call(
        body, grid=(t // tr,), in_specs=in_specs, out_specs=[row, row, vec],
        out_shape=[SDS((t, width), F32), SDS((t, width), BF16), SDS((1, width), F32)],
        compiler_params=_cparams(), name=name)(*args)


def _valid_rows(tile_index, tr, lp, lo, hi):
    pos = (tile_index % (lp // tr)) * tr + lax.broadcasted_iota(jnp.int32, (tr, 1), 0)
    return (pos >= lo) & (pos < hi)


def _loss_head(h, g, target, *, lp, l_valid, name):
    t, d = h.shape
    tr = _tile(lp, max(SUBLANES_BF16, ROW_BLOCK_BYTES // (4 * d)), SUBLANES_BF16)

    def body(x_ref, g_ref, t_ref, loss_ref, dx_ref, dxb_ref, dg_ref):
        i = pl.program_id(0)
        xv = x_ref[...]
        r = lax.rsqrt(jnp.mean(xv * xv, axis=-1, keepdims=True) + NORM_EPS)
        y = xv * r * g_ref[...]
        valid = _valid_rows(i, tr, lp, N_META, l_valid)
        err = jnp.where(valid, y - t_ref[...], 0.0)
        lpart = 0.5 * jnp.sum(jnp.mean(err * err, axis=-1, keepdims=True), axis=0, keepdims=True)
        dyv = err * (1.0 / d)
        gy = dyv * g_ref[...]
        dx = r * gy - xv * (r * r * r) * jnp.mean(gy * xv, axis=-1, keepdims=True)
        dx_ref[...] = dx
        dxb_ref[...] = dx.astype(BF16)
        gpart = jnp.sum(dyv * xv * r, axis=0, keepdims=True)

        @pl.when(i == 0)
        def _():
            dg_ref[...] = gpart
            loss_ref[...] = jnp.broadcast_to(lpart, loss_ref.shape)

        @pl.when(i > 0)
        def _():
            dg_ref[...] += gpart
            loss_ref[...] += jnp.broadcast_to(lpart, loss_ref.shape)

    row = pl.BlockSpec((tr, d), lambda i: (i, 0))
    vec = pl.BlockSpec((1, d), lambda i: (0, 0))
    return pl.pallas_call(
        body, grid=(t // tr,), in_specs=[row, vec, row],
        out_specs=[pl.BlockSpec((1, LANES), lambda i: (0, 0)), row, row, vec],
        out_shape=[SDS((1, LANES), F32), SDS((t, d), F32), SDS((t, d), BF16), SDS((1, d), F32)],
        compiler_params=_cparams(), name=name)(h, g, target)


def _ln_silu_fwd(yc, g, b, *, name):
    t, c = yc.shape
    tr = _row_tile(t, c)

    def body(x_ref, g_ref, b_ref, o_ref):
        xv = x_ref[...]
        xc = xv - jnp.mean(xv, axis=-1, keepdims=True)
        r = lax.rsqrt(jnp.mean(xc * xc, axis=-1, keepdims=True) + NORM_EPS)
        ln = xc * r * g_ref[...] + b_ref[...]
        o_ref[...] = (ln * _sigmoid(ln)).astype(o_ref.dtype)

    row = pl.BlockSpec((tr, c), lambda i: (i, 0))
    vec = pl.BlockSpec((1, c), lambda i: (0, 0))
    return pl.pallas_call(body, grid=(t // tr,), in_specs=[row, vec, vec], out_specs=row,
                          out_shape=SDS((t, c), BF16), compiler_params=_cparams(), name=name)(yc, g, b)


def _ln_silu_bwd(yc, g, b, ds, *, name):
    t, c = yc.shape
    tr = _row_tile(t, c)

    def body(x_ref, g_ref, b_ref, ds_ref, dx_ref, dg_ref, db_ref):
        xv = x_ref[...]
        xc = xv - jnp.mean(xv, axis=-1, keepdims=True)
        r = lax.rsqrt(jnp.mean(xc * xc, axis=-1, keepdims=True) + NORM_EPS)
        xh = xc * r
        ln = xh * g_ref[...] + b_ref[...]
        sg = _sigmoid(ln)
        dln = ds_ref[...].astype(F32) * (sg * (1.0 + ln * (1.0 - sg)))
        gy = dln * g_ref[...]
        dx_ref[...] = r * (gy - jnp.mean(gy, axis=-1, keepdims=True)
                           - xh * jnp.mean(gy * xh, axis=-1, keepdims=True))
        gpart = jnp.sum(dln * xh, axis=0, keepdims=True)
        bpart = jnp.sum(dln, axis=0, keepdims=True)

        @pl.when(pl.program_id(0) == 0)
        def _():
            dg_ref[...] = gpart
            db_ref[...] = bpart

        @pl.when(pl.program_id(0) > 0)
        def _():
            dg_ref[...] += gpart
            db_ref[...] += bpart

    row = pl.BlockSpec((tr, c), lambda i: (i, 0))
    vec = pl.BlockSpec((1, c), lambda i: (0, 0))
    return pl.pallas_call(
        body, grid=(t // tr,), in_specs=[row, vec, vec, row], out_specs=[row, vec, vec],
        out_shape=[SDS((t, c), F32), SDS((1, c), F32), SDS((1, c), F32)],
        compiler_params=_cparams(), name=name)(yc, g, b, ds)


def _swiglu_fwd(gate, up, *, name):
    t, f = gate.shape
    tr = _row_tile(t, f)

    def body(g_ref, u_ref, o_ref):
        gv = g_ref[...].astype(F32)
        o_ref[...] = (gv * _sigmoid(gv) * u_ref[...].astype(F32)).astype(o_ref.dtype)

    row = pl.BlockSpec((tr, f), lambda i: (i, 0))
    return pl.pallas_call(body, grid=(t // tr,), in_specs=[row, row], out_specs=row,
                          out_shape=SDS((t, f), BF16), compiler_params=_cparams(), name=name)(gate, up)


def _swiglu_bwd(gate, up, dact, *, name):
    t, f = gate.shape
    tr = _row_tile(t, f)

    def body(g_ref, u_ref, d_ref, dg_ref, du_ref):
        gv = g_ref[...].astype(F32)
        sg = _sigmoid(gv)
        dv = d_ref[...]
        dg_ref[...] = (dv * u_ref[...].astype(F32) * (sg * (1.0 + gv * (1.0 - sg)))).astype(dg_ref.dtype)
        du_ref[...] = (dv * (gv * sg)).astype(du_ref.dtype)

    row = pl.BlockSpec((tr, f), lambda i: (i, 0))
    return pl.pallas_call(body, grid=(t // tr,), in_specs=[row, row, row], out_specs=[row, row],
                          out_shape=[SDS((t, f), BF16), SDS((t, f), BF16)],
                          compiler_params=_cparams(), name=name)(gate, up, dact)


def _merge_tiles(t, d):
    tc = _tile(d, 512, LANES)
    return _row_tile(t, tc), tc


def _merge_fwd(proj, gate_b, ya, yb, yc, *, gate_off, name):
    t, d = ya.shape
    tr, tc = _merge_tiles(t, d)
    nc = d // tc
    assert gate_off % tc == 0
    g0 = gate_off // tc

    def body(l0, l1, l2, b0, b1, b2, ya_ref, yb_ref, yc_ref, o_ref):
        acc = _sigmoid(l0[...] + b0[...]) * ya_ref[...]
        acc += _sigmoid(l1[...] + b1[...]) * yb_ref[...]
        acc += _sigmoid(l2[...] + b2[...]) * yc_ref[...]
        o_ref[...] = acc.astype(o_ref.dtype)

    lspec = [pl.BlockSpec((tr, tc), lambda i, j, br=br: (i, g0 + br * nc + j)) for br in range(N_BRANCH)]
    bspec = [pl.BlockSpec((1, tc), lambda i, j, br=br: (0, br * nc + j)) for br in range(N_BRANCH)]
    yspec = pl.BlockSpec((tr, tc), lambda i, j: (i, j))
    return pl.pallas_call(
        body, grid=(t // tr, nc), in_specs=lspec + bspec + [yspec] * 3, out_specs=yspec,
        out_shape=SDS((t, d), BF16), compiler_params=_cparams(), name=name)(
            proj, proj, proj, gate_b, gate_b, gate_b, ya, yb, yc)


def _merge_bwd(proj, gate_b, ya, yb, yc, dm, *, gate_off, name):
    t, d = ya.shape
    tr, tc = _merge_tiles(t, d)
    nc = d // tc
    g0 = gate_off // tc

    def body(l0, l1, l2, b0, b1, b2, ya_ref, yb_ref, yc_ref, dm_ref,
             dya, dyb, dyc, dl0, dl1, dl2, db0, db1, db2):
        i = pl.program_id(1)
        dmv = dm_ref[...]
        for l_ref, b_ref, y_ref, dy_ref, dl_ref, db_ref in (
                (l0, b0, ya_ref, dya, dl0, db0), (l1, b1, yb_ref, dyb, dl1, db1), (l2, b2, yc_ref, dyc, dl2, db2)):
            gt = _sigmoid(l_ref[...] + b_ref[...])
            dy_ref[...] = (dmv * gt).astype(dy_ref.dtype)
            dl = dmv * y_ref[...] * gt * (1.0 - gt)
            dl_ref[...] = dl.astype(dl_ref.dtype)
            part = jnp.sum(dl, axis=0, keepdims=True)

            @pl.when(i == 0)
            def _(db_ref=db_ref, part=part):
                db_ref[...] = part

            @pl.when(i > 0)
            def _(db_ref=db_ref, part=part):
                db_ref[...] += part

    lspec = [pl.BlockSpec((tr, tc), lambda j, i, br=br: (i, g0 + br * nc + j)) for br in range(N_BRANCH)]
    bspec = [pl.BlockSpec((1, tc), lambda j, i, br=br: (0, br * nc + j)) for br in range(N_BRANCH)]
    yspec = pl.BlockSpec((tr, tc), lambda j, i: (i, j))
    vspec = pl.BlockSpec((1, tc), lambda j, i: (0, j))
    return pl.pallas_call(
        body, grid=(nc, t // tr), in_specs=lspec + bspec + [yspec] * 4,
        out_specs=[yspec] * 6 + [vspec] * 3,
        out_shape=[SDS((t, d), BF16)] * 6 + [SDS((1, d), F32)] * 3,
        compiler_params=_cparams(), name=name)(proj, proj, proj, gate_b, gate_b, gate_b, ya, yb, yc, dm)


def _conv_fwd(proj3, dw, cb, *, c_conv, l_valid, name):
    b, lp, _ = proj3.shape
    cw = _tile(c_conv, 256, LANES)
    nc = c_conv // cw

    def body(a_ref, gte_ref, dw_ref, cb_ref, o_ref, zs):
        rows = lax.broadcasted_iota(jnp.int32, (lp, 1), 0)
        z = jnp.where(rows < l_valid, a_ref[...] * _sigmoid(gte_ref[...]), 0.0)
        zs[pl.ds(0, CONV_HALO), :] = jnp.zeros((CONV_HALO, cw), F32)
        zs[pl.ds(CONV_HALO, lp), :] = z
        zs[pl.ds(CONV_HALO + lp, CONV_HALO), :] = jnp.zeros((CONV_HALO, cw), F32)
        acc = jnp.broadcast_to(cb_ref[...], (lp, cw))
        for j in range(CONV_K):
            acc = acc + zs[pl.ds(j + 1, lp), :] * dw_ref[pl.ds(j, 1), :]
        o_ref[...] = acc

    seq = lambda off: pl.BlockSpec((None, lp, cw), lambda bi, ci: (bi, 0, off + ci))
    return pl.pallas_call(
        body, grid=(b, nc),
        in_specs=[seq(0), seq(nc), pl.BlockSpec((CONV_K, cw), lambda bi, ci: (0, ci)),
                  pl.BlockSpec((1, cw), lambda bi, ci: (0, ci))],
        out_specs=seq(0), out_shape=SDS((b, lp, c_conv), F32),
        scratch_shapes=[pltpu.VMEM((lp + 2 * CONV_HALO, cw), F32)],
        compiler_params=_cparams(), name=name)(proj3, proj3, dw, cb)


def _conv_bwd(proj3, dw, dyc3, *, c_conv, l_valid, name):
    b, lp, _ = proj3.shape
    cw = _tile(c_conv, 256, LANES)
    nc = c_conv // cw

    def body(a_ref, gte_ref, dw_ref, dy_ref, da_ref, dgte_ref, ddw_ref, dcb_ref, dys):
        bi = pl.program_id(1)
        rows = lax.broadcasted_iota(jnp.int32, (lp, 1), 0)
        valid = rows < l_valid
        av = a_ref[...]
        sg = _sigmoid(gte_ref[...])
        z = jnp.where(valid, av * sg, 0.0)
        dyv = dy_ref[...]
        zero = jnp.zeros((CONV_HALO, cw), F32)
        dys[pl.ds(0, CONV_HALO), :] = zero
        dys[pl.ds(CONV_HALO, lp), :] = dyv
        dys[pl.ds(CONV_HALO + lp, CONV_HALO), :] = zero

        @pl.when(bi == 0)
        def _():
            ddw_ref[...] = jnp.zeros_like(ddw_ref)
            dcb_ref[...] = jnp.zeros_like(dcb_ref)

        dz = jnp.zeros((lp, cw), F32)
        for j in range(CONV_K):
            shifted = dys[pl.ds(CONV_K - j, lp), :]
            dz = dz + shifted * dw_ref[pl.ds(j, 1), :]
            ddw_ref[pl.ds(j, 1), :] += jnp.sum(shifted * z, axis=0, keepdims=True)
        dcb_ref[...] += jnp.sum(dyv, axis=0, keepdims=True)
        dz = jnp.where(valid, dz, 0.0)
        da_ref[...] = (dz * sg).astype(da_ref.dtype)
        dgte_ref[...] = (dz * av * sg * (1.0 - sg)).astype(dgte_ref.dtype)

    seq = lambda off: pl.BlockSpec((None, lp, cw), lambda ci, bi: (bi, 0, off + ci))
    return pl.pallas_call(
        body, grid=(nc, b),
        in_specs=[seq(0), seq(nc), pl.BlockSpec((CONV_K, cw), lambda ci, bi: (0, ci)), seq(0)],
        out_specs=[seq(0), seq(0), pl.BlockSpec((CONV_K, cw), lambda ci, bi: (0, ci)),
                   pl.BlockSpec((1, cw), lambda ci, bi: (0, ci))],
        out_shape=[SDS((b, lp, c_conv), BF16), SDS((b, lp, c_conv), BF16),
                   SDS((CONV_K, c_conv), F32), SDS((1, c_conv), F32)],
        scratch_shapes=[pltpu.VMEM((lp + 2 * CONV_HALO, cw), F32)],
        compiler_params=_cparams(), name=name)(proj3, proj3, dw, dyc3)


def _swap_halves(x, group):
    half = group // 2
    lane = lax.broadcasted_iota(jnp.int32, x.shape, 1)
    up = pltpu.roll(x, LANES - half, 1)
    down = pltpu.roll(x, half, 1)
    return jnp.where((lane % group) < half, up, down)


def _rope(x, cos, sin, group):
    return x * cos + _swap_halves(x, group) * sin


def _rope_bwd(dy, cos, sin, group):
    return dy * cos + _swap_halves(dy * sin, group)


def _rope_tables(lp, seq, dim, lanes):
    quarter = dim // 4
    tok = jnp.arange(seq, dtype=jnp.int32)
    zeros = jnp.zeros((N_META,), F32)
    pad = jnp.zeros((lp - N_META - seq,), F32)
    row = jnp.concatenate([zeros, (tok // GRID_W).astype(F32), pad])
    col = jnp.concatenate([zeros, (tok % GRID_W).astype(F32), pad])
    inv = ROPE_THETA ** (-jnp.arange(quarter, dtype=F32) / quarter)
    ar, ac = row[:, None] * inv[None, :], col[:, None] * inv[None, :]
    cos = jnp.concatenate([jnp.cos(ar), jnp.cos(ar), jnp.cos(ac), jnp.cos(ac)], axis=1)
    sin = jnp.concatenate([-jnp.sin(ar), jnp.sin(ar), -jnp.sin(ac), jnp.sin(ac)], axis=1)
    if lanes > dim:
        cos = jnp.concatenate([cos, jnp.ones((lp, lanes - dim), F32)], axis=1)
        sin = jnp.concatenate([sin, jnp.zeros((lp, lanes - dim), F32)], axis=1)
    return cos, sin


def _gqa_prep_fwd(proj, qg, kg, cos, sin, *, q_off, k_off, v_off, lp, name):
    t = proj.shape[0]
    qw, kw = GQA_HEADS * HEAD, GQA_KV_HEADS * HEAD
    tr = _tile(lp, max(SUBLANES_BF16, ROW_BLOCK_BYTES // (4 * qw)), SUBLANES_BF16)
    npos = lp // tr
    assert q_off % qw == 0 and k_off % kw == 0 and v_off % kw == 0

    def body(q_ref, k_ref, v_ref, qg_ref, kg_ref, cos_ref, sin_ref, qo_ref, ko_ref, vo_ref):
        cosv, sinv = cos_ref[...], sin_ref[...]
        for src, g_ref, dst, heads in ((q_ref, qg_ref, qo_ref, GQA_HEADS), (k_ref, kg_ref, ko_ref, GQA_KV_HEADS)):
            for hh in range(heads):
                xh = src[:, hh * HEAD:(hh + 1) * HEAD]
                r = lax.rsqrt(jnp.mean(xh * xh, axis=-1, keepdims=True) + NORM_EPS)
                dst[:, hh * HEAD:(hh + 1) * HEAD] = _rope(xh * r * g_ref[...], cosv, sinv, 64).astype(dst.dtype)
        vo_ref[...] = v_ref[...].astype(vo_ref.dtype)

    tab = pl.BlockSpec((tr, HEAD), lambda i: (i % npos, 0))
    vec = pl.BlockSpec((1, HEAD), lambda i: (0, 0))
    return pl.pallas_call(
        body, grid=(t // tr,),
        in_specs=[pl.BlockSpec((tr, qw), lambda i: (i, q_off // qw)),
                  pl.BlockSpec((tr, kw), lambda i: (i, k_off // kw)),
                  pl.BlockSpec((tr, kw), lambda i: (i, v_off // kw)), vec, vec, tab, tab],
        out_specs=[pl.BlockSpec((tr, qw), lambda i: (i, 0)), pl.BlockSpec((tr, kw), lambda i: (i, 0)),
                   pl.BlockSpec((tr, kw), lambda i: (i, 0))],
        out_shape=[SDS((t, qw), BF16), SDS((t, kw), BF16), SDS((t, kw), BF16)],
        compiler_params=_cparams(), name=name)(proj, proj, proj, qg, kg, cos, sin)


def _gqa_prep_bwd(proj, qg, kg, cos, sin, dqr, dkr, *, q_off, k_off, lp, name):
    t = proj.shape[0]
    qw, kw = GQA_HEADS * HEAD, GQA_KV_HEADS * HEAD
    tr = _tile(lp, max(SUBLANES_BF16, ROW_BLOCK_BYTES // (4 * qw)), SUBLANES_BF16)
    npos = lp // tr

    def body(q_ref, k_ref, qg_ref, kg_ref, cos_ref, sin_ref, dqr_ref, dkr_ref, dq_ref, dk_ref, dqg_ref, dkg_ref):
        cosv, sinv = cos_ref[...], sin_ref[...]
        for src, g_ref, dy_ref, dx_ref, dg_ref, heads in (
                (q_ref, qg_ref, dqr_ref, dq_ref, dqg_ref, GQA_HEADS),
                (k_ref, kg_ref, dkr_ref, dk_ref, dkg_ref, GQA_KV_HEADS)):
            gpart = jnp.zeros((1, HEAD), F32)
            for hh in range(heads):
                sl = slice(hh * HEAD, (hh + 1) * HEAD)
                xh = src[:, sl]
                r = lax.rsqrt(jnp.mean(xh * xh, axis=-1, keepdims=True) + NORM_EPS)
                dxn = _rope_bwd(dy_ref[:, sl], cosv, sinv, 64)
                gy = dxn * g_ref[...]
                dx = r * gy - xh * (r * r * r) * jnp.mean(gy * xh, axis=-1, keepdims=True)
                dx_ref[:, sl] = dx.astype(dx_ref.dtype)
                gpart = gpart + jnp.sum(dxn * xh * r, axis=0, keepdims=True)

            @pl.when(pl.program_id(0) == 0)
            def _(dg_ref=dg_ref, gpart=gpart):
                dg_ref[...] = gpart

            @pl.when(pl.program_id(0) > 0)
            def _(dg_ref=dg_ref, gpart=gpart):
                dg_ref[...] += gpart

    tab = pl.BlockSpec((tr, HEAD), lambda i: (i % npos, 0))
    vec = pl.BlockSpec((1, HEAD), lambda i: (0, 0))
    qrow = pl.BlockSpec((tr, qw), lambda i: (i, 0))
    krow = pl.BlockSpec((tr, kw), lambda i: (i, 0))
    return pl.pallas_call(
        body, grid=(t // tr,),
        in_specs=[pl.BlockSpec((tr, qw), lambda i: (i, q_off // qw)),
                  pl.BlockSpec((tr, kw), lambda i: (i, k_off // kw)), vec, vec, tab, tab, qrow, krow],
        out_specs=[qrow, krow, vec, vec],
        out_shape=[SDS((t, qw), BF16), SDS((t, kw), BF16), SDS((1, HEAD), F32), SDS((1, HEAD), F32)],
        compiler_params=_cparams(), name=name)(proj, proj, qg, kg, cos, sin, dqr, dkr)


def _mla_rope_fwd(q_pre, k_pre, proj, cos, sin, *, kpe_off, lp, name):
    t, w = q_pre.shape
    tr = _tile(lp, max(SUBLANES_BF16, ROW_BLOCK_BYTES // (4 * w)), SUBLANES_BF16)
    npos = lp // tr
    assert kpe_off % LANES == 0

    def body(q_ref, k_ref, kpe_ref, cos_ref, sin_ref, qo_ref, ko_ref):
        cosv, sinv = cos_ref[...], sin_ref[...]
        kr = _rope(kpe_ref[...], cosv, sinv, 32).astype(ko_ref.dtype)
        for hh in range(MLA_HEADS):
            base = hh * MLA_SLOT
            qo_ref[:, base:base + HEAD] = q_ref[:, base:base + HEAD].astype(qo_ref.dtype)
            qo_ref[:, base + HEAD:base + MLA_SLOT] = _rope(
                q_ref[:, base + HEAD:base + MLA_SLOT], cosv, sinv, 32).astype(qo_ref.dtype)
            ko_ref[:, base:base + HEAD] = k_ref[:, base:base + HEAD].astype(ko_ref.dtype)
            ko_ref[:, base + HEAD:base + MLA_SLOT] = kr

    row = pl.BlockSpec((tr, w), lambda i: (i, 0))
    tab = pl.BlockSpec((tr, LANES), lambda i: (i % npos, 0))
    return pl.pallas_call(
        body, grid=(t // tr,),
        in_specs=[row, row, pl.BlockSpec((tr, LANES), lambda i: (i, kpe_off // LANES)), tab, tab],
        out_specs=[row, row], out_shape=[SDS((t, w), BF16), SDS((t, w), BF16)],
        compiler_params=_cparams(), name=name)(q_pre, k_pre, proj, cos, sin)


def _mla_rope_bwd(dq, dk, cos, sin, *, lp, name):
    t, w = dq.shape
    tr = _tile(lp, max(SUBLANES_BF16, ROW_BLOCK_BYTES // (4 * w)), SUBLANES_BF16)
    npos = lp // tr

    def body(dq_ref, dk_ref, cos_ref, sin_ref, dqo_ref, dko_ref, dkpe_ref):
        cosv, sinv = cos_ref[...], sin_ref[...]
        pe = jnp.zeros((tr, LANES), F32)
        for hh in range(MLA_HEADS):
            base = hh * MLA_SLOT
            dqo_ref[:, base:base + HEAD] = dq_ref[:, base:base + HEAD].astype(dqo_ref.dtype)
            dqo_ref[:, base + HEAD:base + MLA_SLOT] = _rope_bwd(
                dq_ref[:, base + HEAD:base + MLA_SLOT], cosv, sinv, 32).astype(dqo_ref.dtype)
            dko_ref[:, base:base + HEAD] = dk_ref[:, base:base + HEAD].astype(dko_ref.dtype)
            dko_ref[:, base + HEAD:base + MLA_SLOT] = jnp.zeros((tr, LANES), dko_ref.dtype)
            pe = pe + dk_ref[:, base + HEAD:base + MLA_SLOT]
        dkpe_ref[...] = _rope_bwd(pe, cosv, sinv, 32).astype(dkpe_ref.dtype)

    row = pl.BlockSpec((tr, w), lambda i: (i, 0))
    tab = pl.BlockSpec((tr, LANES), lambda i: (i % npos, 0))
    return pl.pallas_call(
        body, grid=(t // tr,), in_specs=[row, row, tab, tab],
        out_specs=[row, row, pl.BlockSpec((tr, LANES), lambda i: (i, 0))],
        out_shape=[SDS((t, w), BF16), SDS((t, w), BF16), SDS((t, LANES), BF16)],
        compiler_params=_cparams(), name=name)(dq, dk, cos, sin)


def _softmax_rows(q, k, scale, lp, l_valid):
    s = lax.dot_general(q, k, (((1,), (1,)), ((), ())), preferred_element_type=F32) * scale
    cols = lax.broadcasted_iota(jnp.int32, (1, lp), 1)
    s = jnp.where(cols < l_valid, s, -1e30)
    e = jnp.exp(s - jnp.max(s, axis=-1, keepdims=True))
    return e, 1.0 / jnp.sum(e, axis=-1, keepdims=True)


def _attn_fwd(q, k, v, *, batch, lp, l_valid, heads, kv_heads, dk, dv, scale, name):
    group = heads // kv_heads
    tq = _tile(lp, ATTN_Q_ROWS, SUBLANES_BF16)
    nq = lp // tq

    def body(q_ref, k_ref, v_ref, o_ref):
        e, inv = _softmax_rows(q_ref[...], k_ref[...], scale, lp, l_valid)
        o_ref[...] = (jnp.dot(e.astype(BF16), v_ref[...], preferred_element_type=F32) * inv).astype(o_ref.dtype)

    return pl.pallas_call(
        body, grid=(batch, heads, nq),
        in_specs=[pl.BlockSpec((tq, dk), lambda b, h, i: (b * nq + i, h)),
                  pl.BlockSpec((lp, dk), lambda b, h, i: (b, h // group)),
                  pl.BlockSpec((lp, dv), lambda b, h, i: (b, h // group))],
        out_specs=pl.BlockSpec((tq, dv), lambda b, h, i: (b * nq + i, h)),
        out_shape=SDS((batch * lp, heads * dv), BF16), compiler_params=_cparams(), name=name)(q, k, v)


def _attn_bwd(q, k, v, do, *, batch, lp, l_valid, heads, kv_heads, dk, dv, scale, name):
    group = heads // kv_heads
    tq = _tile(lp, ATTN_Q_ROWS, SUBLANES_BF16)
    nq = lp // tq

    def body(q_ref, k_ref, v_ref, do_ref, dq_ref, dk_ref, dv_ref):
        first = (pl.program_id(2) == 0) & (pl.program_id(3) == 0)
        qv, kv, dov = q_ref[...], k_ref[...], do_ref[...]
        e, inv = _softmax_rows(qv, kv, scale, lp, l_valid)
        dp = lax.dot_general(dov, v_ref[...], (((1,), (1,)), ((), ())), preferred_element_type=F32)
        delta = inv * jnp.sum(e * dp, axis=-1, keepdims=True)
        ds = (e * ((dp - delta) * (inv * scale))).astype(BF16)
        dq_ref[...] = jnp.dot(ds, kv, preferred_element_type=F32)
        dkp = lax.dot_general(ds, qv, (((0,), (0,)), ((), ())), preferred_element_type=F32)
        dvp = lax.dot_general(e.astype(BF16), (dov * inv).astype(BF16), (((0,), (0,)), ((), ())),
                              preferred_element_type=F32)

        @pl.when(first)
        def _():
            dk_ref[...] = dkp
            dv_ref[...] = dvp

        @pl.when(jnp.logical_not(first))
        def _():
            dk_ref[...] += dkp
            dv_ref[...] += dvp

    return pl.pallas_call(
        body, grid=(batch, kv_heads, group, nq),
        in_specs=[pl.BlockSpec((tq, dk), lambda b, hk, g, i: (b * nq + i, hk * group + g)),
                  pl.BlockSpec((lp, dk), lambda b, hk, g, i: (b, hk)),
                  pl.BlockSpec((lp, dv), lambda b, hk, g, i: (b, hk)),
                  pl.BlockSpec((tq, dv), lambda b, hk, g, i: (b * nq + i, hk * group + g))],
        out_specs=[pl.BlockSpec((tq, dk), lambda b, hk, g, i: (b * nq + i, hk * group + g)),
                   pl.BlockSpec((lp, dk), lambda b, hk, g, i: (b, hk)),
                   pl.BlockSpec((lp, dv), lambda b, hk, g, i: (b, hk))],
        out_shape=[SDS((batch * lp, heads * dk), F32), SDS((batch * lp, kv_heads * dk), F32),
                   SDS((batch * lp, kv_heads * dv), F32)],
        compiler_params=_cparams(), name=name)(q, k, v, do)


def _meta_grad(dh3, *, name):
    b, _, d = dh3.shape

    def body(x_ref, o_ref):
        @pl.when(pl.program_id(0) == 0)
        def _():
            o_ref[...] = x_ref[...]

        @pl.when(pl.program_id(0) > 0)
        def _():
            o_ref[...] += x_ref[...]

    return pl.pallas_call(
        body, grid=(b,), in_specs=[pl.BlockSpec((None, N_META, d), lambda bi: (bi, 0, 0))],
        out_specs=pl.BlockSpec((N_META, d), lambda bi: (0, 0)), out_shape=SDS((N_META, d), F32),
        compiler_params=_cparams(), name=name)(dh3)


def _sum_slots(parts, *, out_dtype, name):
    _, r, c = parts[0].shape
    tr = _row_tile(r, c)
    counts = [p.shape[0] for p in parts]

    def body(*refs):
        o_ref = refs[-1]
        acc = None
        for ref, cnt in zip(refs[:-1], counts):
            for s in range(cnt):
                term = ref[s].astype(F32)
                acc = term if acc is None else acc + term
        o_ref[...] = acc.astype(o_ref.dtype)

    return pl.pallas_call(
        body, grid=(r // tr,),
        in_specs=[pl.BlockSpec((cnt, tr, c), lambda i: (0, i, 0)) for cnt in counts],
        out_specs=pl.BlockSpec((tr, c), lambda i: (i, 0)), out_shape=SDS((r, c), out_dtype),
        compiler_params=_cparams(), name=name)(*parts)


def _adamw(w, g, m, v, *, name):
    shape = w.shape
    cols = shape[-1]
    rows = w.size // cols
    tr = _tile(rows, max(8, ROW_BLOCK_BYTES // (4 * cols)), 8)
    bc1 = 1.0 - ADAM_B1 ** ADAM_STEP
    bc2 = 1.0 - ADAM_B2 ** ADAM_STEP

    def body(w_ref, g_ref, m_ref, v_ref, d_ref, mo_ref, vo_ref):
        gv = g_ref[...]
        mn = ADAM_B1 * m_ref[...] + (1.0 - ADAM_B1) * gv
        vn = ADAM_B2 * v_ref[...] + (1.0 - ADAM_B2) * (gv * gv)
        mo_ref[...] = mn
        vo_ref[...] = vn
        d_ref[...] = -ADAM_LR * ((mn / bc1) / (jnp.sqrt(vn / bc2) + ADAM_EPS) + ADAM_WD * w_ref[...])

    spec = pl.BlockSpec((tr, cols), lambda i: (i, 0))
    outs = pl.pallas_call(
        body, grid=(rows // tr,), in_specs=[spec] * 4, out_specs=[spec] * 3,
        out_shape=[SDS((rows, cols), F32)] * 3, compiler_params=_cparams(), name=name)(
            *[a.reshape(rows, cols) for a in (w, g, m, v)])
    return tuple(o.reshape(shape) for o in outs)


def _adamw_layer(w, g, m, v, li, outs, *, name):
    depth, k, n = w.shape
    tr = _tile(k, max(8, ROW_BLOCK_BYTES // (4 * n)), 8)
    bc1 = 1.0 - ADAM_B1 ** ADAM_STEP
    bc2 = 1.0 - ADAM_B2 ** ADAM_STEP
    flat = (depth * k, n)
    if outs is None:
        outs = tuple(lax.empty(flat, F32) for _ in range(4))

    def body(w_ref, g_ref, m_ref, v_ref, *rest):
        go_ref, d_ref, mo_ref, vo_ref = rest[4:]
        gv = g_ref[...]
        mn = ADAM_B1 * m_ref[...] + (1.0 - ADAM_B1) * gv
        vn = ADAM_B2 * v_ref[...] + (1.0 - ADAM_B2) * (gv * gv)
        go_ref[...] = gv
        mo_ref[...] = mn
        vo_ref[...] = vn
        d_ref[...] = -ADAM_LR * ((mn / bc1) / (jnp.sqrt(vn / bc2) + ADAM_EPS) + ADAM_WD * w_ref[...])

    layer = pl.BlockSpec((tr, n), lambda i: (li * (k // tr) + i, 0))
    return tuple(pl.pallas_call(
        body, grid=(k // tr,), in_specs=[layer, pl.BlockSpec((tr, n), lambda i: (i, 0)), layer, layer] + [_ANY] * 4,
        out_specs=[layer] * 4, out_shape=[SDS(flat, F32)] * 4,
        input_output_aliases={4 + j: j for j in range(4)},
        compiler_params=_cparams(), name=name)(w.reshape(flat), g, m.reshape(flat), v.reshape(flat), *outs))


def _mesh_pos():
    return lax.axis_index("x"), lax.axis_index("y"), lax.axis_index("c")


def _other_chips(x, y):
    return [(1 - x, y), (x, 1 - y), (1 - x, 1 - y)]


def _chip_allgather_small(vec, *, name):
    r, c = vec.shape

    def body(x_ref, out_ref, send_sems, recv_sems):
        x, y, cc = _mesh_pos()
        mine = 2 * x + y
        out_ref[mine] = x_ref[...]
        chips = _other_chips(x, y)
        sends = [pltpu.make_async_remote_copy(
            src_ref=x_ref, dst_ref=out_ref.at[mine], send_sem=send_sems.at[j], recv_sem=recv_sems.at[j],
            device_id=(px, py, cc), device_id_type=MESH) for j, (px, py) in enumerate(chips)]
        for cp in sends:
            cp.start()
        for j, (px, py) in enumerate(chips):
            pltpu.make_async_remote_copy(
                src_ref=x_ref, dst_ref=out_ref.at[2 * px + py], send_sem=send_sems.at[j],
                recv_sem=recv_sems.at[j], device_id=(px, py, cc), device_id_type=MESH).wait_recv()
        for cp in sends:
            cp.wait_send()

    return pl.pallas_call(
        body, out_shape=SDS((N_CHIPS, r, c), F32),
        in_specs=[pl.BlockSpec(memory_space=pltpu.VMEM)], out_specs=pl.BlockSpec(memory_space=pltpu.VMEM),
        scratch_shapes=[pltpu.SemaphoreType.DMA((3,)), pltpu.SemaphoreType.DMA((3,))],
        compiler_params=_cparams(), name=name)(vec)


def _allreduce_small(vec, *, name):
    r, c = vec.shape

    def body(x_ref, out_ref, buf, send_sems, recv_sems):
        x, y, cc = _mesh_pos()
        me = 4 * x + 2 * y + cc
        buf[me] = x_ref[...]

        def peer(kk):
            fx, fy, fc = (kk >> 2) & 1, (kk >> 1) & 1, kk & 1
            px = x if fx == 0 else 1 - x
            py = y if fy == 0 else 1 - y
            pc = cc if fc == 0 else 1 - cc
            return (px, py, pc), 4 * px + 2 * py + pc

        sends = []
        for kk in range(1, N_DEV):
            dev, _ = peer(kk)
            sends.append(pltpu.make_async_remote_copy(
                src_ref=x_ref, dst_ref=buf.at[me], send_sem=send_sems.at[kk - 1], recv_sem=recv_sems.at[kk - 1],
                device_id=dev, device_id_type=MESH))
        for cp in sends:
            cp.start()
        for kk in range(1, N_DEV):
            dev, slot = peer(kk)
            pltpu.make_async_remote_copy(
                src_ref=x_ref, dst_ref=buf.at[slot], send_sem=send_sems.at[kk - 1], recv_sem=recv_sems.at[kk - 1],
                device_id=dev, device_id_type=MESH).wait_recv()
        for cp in sends:
            cp.wait_send()
        acc = buf[0]
        for s in range(1, N_DEV):
            acc = acc + buf[s]
        out_ref[...] = acc

    return pl.pallas_call(
        body, out_shape=SDS((r, c), F32),
        in_specs=[pl.BlockSpec(memory_space=pltpu.VMEM)], out_specs=pl.BlockSpec(memory_space=pltpu.VMEM),
        scratch_shapes=[pltpu.VMEM((N_DEV, r, c), F32), pltpu.SemaphoreType.DMA((N_DEV - 1,)),
                        pltpu.SemaphoreType.DMA((N_DEV - 1,))],
        compiler_params=_cparams(), name=name)(vec)


_HBM = pl.BlockSpec(memory_space=pltpu.HBM)


def _half(rows, half):
    hk = rows // 2
    assert hk % SUBLANES_BF16 == 0
    return pl.ds(pl.multiple_of(half * hk, SUBLANES_BF16), hk)


_SEM = pl.BlockSpec(memory_space=pltpu.SEMAPHORE)
_ANY = pl.BlockSpec(memory_space=pl.ANY)
_DATAFLOW = pltpu.SideEffectType.DATAFLOW_SIDE_EFFECTING


def _in_hbm(a):
    return pltpu.with_memory_space_constraint(a, pltpu.HBM)


def _split_start(srcs, lands, prev, make_copy, *, name):
    nw = len(srcs)

    def body(*refs):
        src_refs, land_refs = refs[:nw], refs[nw:2 * nw]
        send_sems, recv_sems = refs[2 * nw + 1], refs[2 * nw + 2]
        token = refs[-1]
        for j in range(3):
            for i in range(nw):
                make_copy(src_refs, land_refs, send_sems, recv_sems, j, i, True).start()
        token[...] = jnp.zeros_like(token)

    outs = pl.pallas_call(
        body, name=name,
        out_shape=(pltpu.SemaphoreType.DMA((3 * nw,)), pltpu.SemaphoreType.DMA((3 * nw,)),
                   *[pltpu.HBM(a.shape, a.dtype) for a in srcs], *[pltpu.HBM(a.shape, a.dtype) for a in lands],
                   SDS((8, LANES), F32)),
        in_specs=[_HBM] * (2 * nw) + [_ANY],
        out_specs=(_SEM, _SEM, *[_HBM] * (2 * nw), pl.BlockSpec(memory_space=pltpu.VMEM)),
        input_output_aliases={i: 2 + i for i in range(2 * nw)},
        compiler_params=pltpu.CompilerParams(has_side_effects=_DATAFLOW),
    )(*[_in_hbm(a) for a in srcs], *[_in_hbm(a) for a in lands], prev)
    return outs[0], outs[1], list(outs[2:2 + nw]), list(outs[2 + nw:2 + 2 * nw]), outs[-1]


def _split_wait(pending, after, make_copy, *, name):
    send_sems, recv_sems, srcs, lands, _ = pending
    nw = len(srcs)

    def body(*refs):
        src_refs, land_refs = refs[:nw], refs[nw:2 * nw]
        send_refs, recv_refs = refs[2 * nw], refs[2 * nw + 1]
        for j in range(3):
            for i in range(nw):
                cp = make_copy(src_refs, land_refs, send_refs, recv_refs, j, i, False)
                cp.wait_send()
                cp.wait_recv()

    outs = pl.pallas_call(
        body, name=name,
        out_shape=(*[pltpu.HBM(a.shape, a.dtype) for a in srcs], *[pltpu.HBM(a.shape, a.dtype) for a in lands]),
        in_specs=[_HBM] * (2 * nw) + [_SEM, _SEM, _ANY], out_specs=tuple([_HBM] * (2 * nw)),
        input_output_aliases={i: i for i in range(2 * nw)},
        compiler_params=pltpu.CompilerParams(has_side_effects=_DATAFLOW),
    )(*srcs, *lands, send_sems, recv_sems, after)
    return list(outs[:nw]), list(outs[nw:])


def _gather_copy(shapes):
    def make(src_refs, land_refs, send_sems, recv_sems, j, i, outgoing):
        x, y, cc = _mesh_pos()
        px, py = _other_chips(x, y)[j]
        rows = _half(shapes[i][0], cc)
        slot = 2 * x + y if outgoing else 2 * px + py
        return pltpu.make_async_remote_copy(
            src_ref=src_refs[i].at[rows], dst_ref=land_refs[i].at[slot, rows],
            send_sem=send_sems.at[j * len(shapes) + i], recv_sem=recv_sems.at[j * len(shapes) + i],
            device_id=(px, py, cc), device_id_type=MESH)
    return make


def _gather_forward(lands, *, name):
    nw = len(lands)

    def body(*refs):
        in_refs, out_refs = refs[:nw], refs[nw:2 * nw]
        send_sems, recv_sems = refs[2 * nw:]
        x, y, cc = _mesh_pos()
        cps = []
        for j, (px, py) in enumerate(_other_chips(x, y)):
            for i in range(nw):
                rows = _half(lands[i].shape[1], cc)
                cps.append(pltpu.make_async_remote_copy(
                    src_ref=in_refs[i].at[2 * px + py, rows], dst_ref=out_refs[i].at[2 * px + py, rows],
                    send_sem=send_sems.at[j, i], recv_sem=recv_sems.at[j, i],
                    device_id=(x, y, 1 - cc), device_id_type=MESH))
        for cp in cps:
            cp.start()
        for cp in cps:
            cp.wait()

    return pl.pallas_call(
        body, out_shape=[SDS(a.shape, a.dtype) for a in lands], in_specs=[_HBM] * nw, out_specs=[_HBM] * nw,
        input_output_aliases={i: i for i in range(nw)},
        scratch_shapes=[pltpu.SemaphoreType.DMA((3, nw)), pltpu.SemaphoreType.DMA((3, nw))], name=name)(*lands)


def _grads_to_sibling(gs, *, name):
    nw = len(gs)

    def body(*refs):
        g_refs, out_refs = refs[:nw], refs[nw:2 * nw]
        send_sems, recv_sems = refs[2 * nw:]
        x, y, cc = _mesh_pos()
        cps = [pltpu.make_async_remote_copy(
            src_ref=g_refs[i].at[pl.ds(0, N_CHIPS), _half(gs[i].shape[1], 1 - cc)], dst_ref=out_refs[i],
            send_sem=send_sems.at[i], recv_sem=recv_sems.at[i], device_id=(x, y, 1 - cc), device_id_type=MESH)
            for i in range(nw)]
        for cp in cps:
            cp.start()
        for cp in cps:
            cp.wait()

    return pl.pallas_call(
        body, out_shape=[SDS((N_CHIPS, g.shape[1] // 2, g.shape[2]), g.dtype) for g in gs],
        in_specs=[_HBM] * nw, out_specs=[_HBM] * nw,
        scratch_shapes=[pltpu.SemaphoreType.DMA((nw,)), pltpu.SemaphoreType.DMA((nw,))], name=name)(*gs)


def _chips_copy(nw):
    def make(src_refs, land_refs, send_sems, recv_sems, j, i, outgoing):
        x, y, cc = _mesh_pos()
        px, py = _other_chips(x, y)[j]
        return pltpu.make_async_remote_copy(
            src_ref=src_refs[i].at[2 * px + py], dst_ref=land_refs[i].at[j],
            send_sem=send_sems.at[j * nw + i], recv_sem=recv_sems.at[j * nw + i],
            device_id=(px, py, cc), device_id_type=MESH)
    return make


def _share_halves(tots, *, name):
    nw = len(tots)

    def body(*refs):
        t_refs, out_refs = refs[:nw], refs[nw:2 * nw]
        send_sems, recv_sems = refs[2 * nw:]
        x, y, cc = _mesh_pos()
        cps = [pltpu.make_async_remote_copy(
            src_ref=t_refs[i], dst_ref=out_refs[i], send_sem=send_sems.at[i], recv_sem=recv_sems.at[i],
            device_id=(x, y, 1 - cc), device_id_type=MESH) for i in range(nw)]
        for cp in cps:
            cp.start()
        for cp in cps:
            cp.wait()

    return pl.pallas_call(
        body, out_shape=[SDS(tt.shape, tt.dtype) for tt in tots],
        in_specs=[_HBM] * nw, out_specs=[_HBM] * nw,
        scratch_shapes=[pltpu.SemaphoreType.DMA((nw,)), pltpu.SemaphoreType.DMA((nw,))], name=name)(*tots)


def _pack_small(arrays):
    flat = jnp.concatenate([a.astype(F32).reshape(-1) for a in arrays])
    n = flat.shape[0]
    padded = -(-n // (8 * LANES)) * (8 * LANES)
    return jnp.pad(flat, (0, padded - n)).reshape(padded // LANES, LANES)


def _unpack_small(flat, shapes):
    out, off = [], 0
    for shp in shapes:
        n = math.prod(shp)
        out.append(flat[off:off + n].reshape(shp))
        off += n
    return out


BIG = (("w_in", 1), ("w_conv_out", 1), ("w_gqa_out", 1), ("w_mla_uq", 1), ("w_mla_ukv", 1), ("w_mla_out", 1),
       ("w_out", 0), ("w_ffn_gate", 1), ("w_ffn_up", 1), ("w_ffn_down", 0))
BIG_NAMES = tuple(nm for nm, _ in BIG)
BIG_GROUPS = (("w_in",), tuple(nm for nm in BIG_NAMES if nm != "w_in"))


def _cols_to_slots(full):
    k, n = full.shape
    return full.reshape(k, N_CHIPS, n // N_CHIPS).transpose(1, 0, 2)


def _slots_to_cols(slots):
    s, k, n = slots.shape
    return slots.transpose(1, 0, 2).reshape(k, s * n)


def _step(p, mom_m, mom_v, x, loss_target):
    b, seq, d = x.shape
    depth = p["w_in"].shape[0]
    l_valid = N_META + seq
    lp = -(-l_valid // LANES) * LANES
    t = b * lp
    c_conv = d // 2
    q_rank = p["mla_q_norm_g"].shape[1]
    kv_rank = p["mla_kv_norm_g"].shape[1]
    qw, kw = GQA_HEADS * HEAD, GQA_KV_HEADS * HEAD
    off_q = 2 * c_conv
    off_k = off_q + qw
    off_v = off_k + kw
    off_cq = off_v + kw
    off_ckv = off_cq + q_rank
    off_gate = off_ckv + kv_rank
    off_kpe = off_gate + N_BRANCH * d
    d_inp = off_kpe + LANES
    assert off_cq % q_rank == 0 and off_ckv % kv_rank == 0
    xi, yi, ci = _mesh_pos()
    chip = 2 * xi + yi

    small_sh = [p["meta_tokens"], p["conv_dw"]]
    small_all = _chip_allgather_small(_pack_small(small_sh), name="gather_small")
    got = [_unpack_small(small_all[s].reshape(-1), [a.shape for a in small_sh]) for s in range(N_CHIPS)]
    meta_full = jnp.concatenate([g[0] for g in got], axis=1)
    conv_dw_full = jnp.concatenate([g[1] for g in got], axis=-1).reshape(depth, CONV_K, c_conv)

    gather_copies = [_gather_copy([p[nm].shape[1:] for nm in grp]) for grp in BIG_GROUPS]
    gathers = []
    order = small_all
    for li in range(depth):
        gathers.append([])
        for gi, grp in enumerate(BIG_GROUPS):
            own = [p[nm][li].astype(BF16) for nm in grp]
            lands = [lax.empty((N_CHIPS,) + o.shape, BF16) for o in own]
            gathers[li].append(_split_start(own, lands, order, gather_copies[gi],
                                            name="gather_start_%d_%d" % (li, gi)))
            order = gathers[li][gi][4]
    meta_full = meta_full + order[0, 0]

    def gathered(li, gi, after):
        own, lands = _split_wait(gathers[li][gi], after, gather_copies[gi], name="gather_wait_%d_%d" % (li, gi))
        got = _gather_forward(lands, name="gather_forward_%d" % gi)
        return {nm: lax.dynamic_update_slice(g, o[None], (chip, 0, 0)) for nm, g, o in zip(BIG_GROUPS[gi], got, own)}

    def in_proj_weight(slots):
        w_in = _slots_to_cols(slots)
        return jnp.concatenate(
            [w_in[:, :off_gate], w_in[:, off_gate + MLA_ROPE:], w_in[:, off_gate:off_gate + MLA_ROPE],
             jnp.zeros((d, LANES - MLA_ROPE), BF16)], axis=1)

    def other_weights(full):
        full["w_out"] = full["w_out"].reshape(-1, d)
        full["w_ffn_down"] = full["w_ffn_down"].reshape(-1, d)
        uq = _slots_to_cols(full["w_mla_uq"]).reshape(q_rank, MLA_HEADS, HEAD + MLA_ROPE)
        full["w_uq"] = jnp.pad(uq, ((0, 0), (0, 0), (0, MLA_SLOT - HEAD - MLA_ROPE))).reshape(
            q_rank, MLA_HEADS * MLA_SLOT)
        ukv = _slots_to_cols(full["w_mla_ukv"]).reshape(kv_rank, MLA_HEADS, 2 * HEAD)
        full["w_uk"] = jnp.pad(ukv[:, :, :HEAD], ((0, 0), (0, 0), (0, MLA_SLOT - HEAD))).reshape(
            kv_rank, MLA_HEADS * MLA_SLOT)
        full["w_uv"] = ukv[:, :, HEAD:].reshape(kv_rank, MLA_HEADS * HEAD)
        return full

    cos_g, sin_g = _rope_tables(lp, seq, HEAD, HEAD)
    cos_m, sin_m = _rope_tables(lp, seq, MLA_ROPE, LANES)
    gqa_scale = 1.0 / math.sqrt(HEAD)
    mla_scale = 1.0 / math.sqrt(HEAD + MLA_ROPE)
    attn_kw = dict(batch=b, lp=lp, l_valid=l_valid)
    gqa_kw = dict(heads=GQA_HEADS, kv_heads=GQA_KV_HEADS, dk=HEAD, dv=HEAD, scale=gqa_scale, **attn_kw)
    mla_kw = dict(heads=MLA_HEADS, kv_heads=MLA_HEADS, dk=MLA_SLOT, dv=HEAD, scale=mla_scale, **attn_kw)

    h = jnp.concatenate([jnp.broadcast_to(meta_full[None], (b, N_META, d)), x,
                         jnp.zeros((b, lp - l_valid, d), F32)], axis=1).reshape(t, d)
    saved, weights = [], []
    for li in range(depth):
        w = {"w_in": in_proj_weight(gathered(li, 0, h)["w_in"])}
        weights.append(w)
        row = lambda nm: p[nm][li][None, :]
        s = {"h_in": h}
        s["u"] = _rmsnorm_fwd(h, row("mix_norm_g"), width=d, cblk=0, name="mix_norm")
        proj = _mm(s["u"], w["w_in"], name="in_proj")
        s["proj"] = proj
        proj3 = proj.reshape(b, lp, d_inp)
        s["yconv"] = _conv_fwd(proj3, conv_dw_full[li], row("conv_b"), c_conv=c_conv, l_valid=l_valid,
                               name="conv").reshape(t, c_conv)
        s["sconv"] = _ln_silu_fwd(s["yconv"], row("conv_ln_g"), row("conv_ln_b"), name="conv_ln_silu")
        w.update(other_weights(gathered(li, 1, s["sconv"])))
        s["ya"] = _mm(s["sconv"], w["w_conv_out"], b_sharded=True, name="conv_out")
        s["gq"], s["gk"], s["gv"] = _gqa_prep_fwd(
            proj, row("gqa_q_norm_g"), row("gqa_k_norm_g"), cos_g, sin_g,
            q_off=off_q, k_off=off_k, v_off=off_v, lp=lp, name="gqa_prep")
        s["go"] = _attn_fwd(s["gq"], s["gk"], s["gv"], name="gqa_attn", **gqa_kw)
        s["yb"] = _mm(s["go"], w["w_gqa_out"], b_sharded=True, name="gqa_out")
        s["cqn"] = _rmsnorm_fwd(proj, row("mla_q_norm_g"), width=q_rank, cblk=off_cq // q_rank, name="mla_q_norm")
        s["kvn"] = _rmsnorm_fwd(proj, row("mla_kv_norm_g"), width=kv_rank, cblk=off_ckv // kv_rank,
                                name="mla_kv_norm")
        q_pre = _mm(s["cqn"], w["w_uq"], name="mla_uq")
        k_pre = _mm(s["kvn"], w["w_uk"], name="mla_uk")
        s["mv"] = _mm(s["kvn"], w["w_uv"], out_dtype=BF16, name="mla_uv")
        s["mq"], s["mk"] = _mla_rope_fwd(q_pre, k_pre, proj, cos_m, sin_m, kpe_off=off_kpe, lp=lp, name="mla_rope")
        s["mo"] = _attn_fwd(s["mq"], s["mk"], s["mv"], name="mla_attn", **mla_kw)
        s["yc"] = _mm(s["mo"], w["w_mla_out"], b_sharded=True, name="mla_out")
        s["merged"] = _merge_fwd(proj, row("gate_b"), s["ya"], s["yb"], s["yc"], gate_off=off_gate, name="merge")
        h = _mm(s["merged"], w["w_out"], add=h, name="mix_out")
        s["h_mid"] = h
        s["v"] = _rmsnorm_fwd(h, row("ffn_norm_g"), width=d, cblk=0, name="ffn_norm")
        s["gate"] = _mm(s["v"], w["w_ffn_gate"], b_sharded=True, out_dtype=BF16, name="ffn_gate")
        s["up"] = _mm(s["v"], w["w_ffn_up"], b_sharded=True, out_dtype=BF16, name="ffn_up")
        s["act"] = _swiglu_fwd(s["gate"], s["up"], name="swiglu")
        h = _mm(s["act"], w["w_ffn_down"], add=h, name="ffn_down")
        saved.append(s)

    target = jnp.pad(loss_target, ((0, 0), (N_META, lp - l_valid), (0, 0))).reshape(t, d)
    loss_part, dh, dhb, g_final = _loss_head(h, p["final_norm_g"][None, :], target, lp=lp, l_valid=l_valid,
                                             name="loss_head")
    loss = lax.psum(loss_part[0, 0], ("x", "y", "c"))

    small_names = ["mix_norm_g", "conv_dw", "conv_b", "conv_ln_g", "conv_ln_b", "gqa_q_norm_g", "gqa_k_norm_g",
                   "mla_q_norm_g", "mla_kv_norm_g", "gate_b", "ffn_norm_g"]
    small_g = {nm: [None] * depth for nm in small_names}
    adam = {}
    chips_copies = [_chips_copy(len(grp)) for grp in BIG_GROUPS]

    def start_reduce(lj, gi, big):
        grp = BIG_GROUPS[gi]
        glist = [big[nm] for nm in grp]
        from_sib = _grads_to_sibling(glist, name="grads_to_sibling_%d" % gi)
        parts = []
        for nm, g, fs in zip(grp, glist, from_sib):
            hk = g.shape[1] // 2
            mine = lax.dynamic_slice_in_dim(g, ci * hk, hk, axis=1)
            rows = (1, N_CHIPS * hk, g.shape[2])
            parts.append(_sum_slots([mine.reshape(rows), fs.reshape(rows)], out_dtype=BF16,
                                    name="sum_sibling_" + nm).reshape(fs.shape))
        lands = [lax.empty((3,) + pt.shape[1:], BF16) for pt in parts]
        return lj, gi, _split_start(parts, lands, order, chips_copies[gi], name="chips_start_%d_%d" % (lj, gi))

    def finish_reduce(item, after):
        lj, gi, pending = item
        grp = BIG_GROUPS[gi]
        parts, from_chips = _split_wait(pending, after, chips_copies[gi], name="chips_wait_%d_%d" % (lj, gi))
        tots = [_sum_slots([lax.dynamic_index_in_dim(pt, chip, 0, keepdims=True), fc], out_dtype=F32,
                           name="sum_chips_" + nm) for nm, pt, fc in zip(grp, parts, from_chips)]
        others = _share_halves(tots, name="share_halves_%d" % gi)
        for nm, tot, oth in zip(grp, tots, others):
            g = jnp.concatenate([jnp.where(ci == 0, tot, oth), jnp.where(ci == 0, oth, tot)], axis=0)
            adam[nm] = _adamw_layer(p[nm], g, mom_m[nm], mom_v[nm], lj, adam.get(nm), name="adamw_" + nm)

    in_flight = []
    for li in reversed(range(depth)):
        w, s = weights[li], saved[li]
        row = lambda nm: p[nm][li][None, :]
        big = {}
        dact = _mm(dhb, w["w_ffn_down"], mode="nt", deps=[item[2][4] for item in in_flight], name="d_ffn_act")
        big["w_ffn_down"] = _mm(s["act"], dhb, mode="tn", out_dtype=BF16, deps=[item[2][4] for item in in_flight],
                                name="dw_ffn_down").reshape(N_CHIPS, -1, d)
        dgate, dup = _swiglu_bwd(s["gate"], s["up"], dact, name="d_swiglu")
        dv = _mm(dgate, w["w_ffn_gate"], mode="nt", b_sharded=True, name="d_ffn_v_gate")
        dv = _mm(dup, w["w_ffn_up"], mode="nt", b_sharded=True, add=dv, name="d_ffn_v_up")
        big["w_ffn_gate"] = _mm(s["v"], dgate, mode="tn", out_dtype=BF16, out_shards=N_CHIPS, name="dw_ffn_gate")
        big["w_ffn_up"] = _mm(s["v"], dup, mode="tn", out_dtype=BF16, out_shards=N_CHIPS, name="dw_ffn_up")
        dh, dhb, g = _rmsnorm_bwd(s["h_mid"], row("ffn_norm_g"), dv, width=d, cblk=0, res=dh, name="d_ffn_norm")
        small_g["ffn_norm_g"][li] = g[0]
        dm = _mm(dhb, w["w_out"], mode="nt", name="d_merged")
        big["w_out"] = _mm(s["merged"], dhb, mode="tn", out_dtype=BF16, name="dw_out").reshape(N_CHIPS, -1, d)
        dya, dyb, dyc, dl0, dl1, dl2, db0, db1, db2 = _merge_bwd(
            s["proj"], row("gate_b"), s["ya"], s["yb"], s["yc"], dm, gate_off=off_gate, name="d_merge")
        small_g["gate_b"][li] = jnp.concatenate([db0[0], db1[0], db2[0]])
        ds = _mm(dya, w["w_conv_out"], mode="nt", b_sharded=True, name="d_conv_s")
        big["w_conv_out"] = _mm(s["sconv"], dya, mode="tn", out_dtype=BF16, out_shards=N_CHIPS, name="dw_conv_out")
        dyconv, g, gb = _ln_silu_bwd(s["yconv"], row("conv_ln_g"), row("conv_ln_b"), ds, name="d_conv_ln_silu")
        small_g["conv_ln_g"][li], small_g["conv_ln_b"][li] = g[0], gb[0]
        proj3 = s["proj"].reshape(b, lp, d_inp)
        da, dgte, ddw, dcb = _conv_bwd(proj3, conv_dw_full[li], dyconv.reshape(b, lp, c_conv), c_conv=c_conv,
                                       l_valid=l_valid, name="d_conv")
        small_g["conv_dw"][li], small_g["conv_b"][li] = ddw, dcb[0]
        dgo = _mm(dyb, w["w_gqa_out"], mode="nt", b_sharded=True, out_dtype=BF16, name="d_gqa_o")
        big["w_gqa_out"] = _mm(s["go"], dyb, mode="tn", out_dtype=BF16, out_shards=N_CHIPS, name="dw_gqa_out")
        dgq, dgk, dgv = _attn_bwd(s["gq"], s["gk"], s["gv"], dgo, name="d_gqa_attn", **gqa_kw)
        dq_g, dk_g, gq, gk = _gqa_prep_bwd(s["proj"], row("gqa_q_norm_g"), row("gqa_k_norm_g"), cos_g, sin_g,
                                           dgq, dgk, q_off=off_q, k_off=off_k, lp=lp, name="d_gqa_prep")
        small_g["gqa_q_norm_g"][li], small_g["gqa_k_norm_g"][li] = gq[0], gk[0]
        dmo = _mm(dyc, w["w_mla_out"], mode="nt", b_sharded=True, out_dtype=BF16, name="d_mla_o")
        big["w_mla_out"] = _mm(s["mo"], dyc, mode="tn", out_dtype=BF16, out_shards=N_CHIPS, name="dw_mla_out")
        dmq, dmk, dmv = _attn_bwd(s["mq"], s["mk"], s["mv"], dmo, name="d_mla_attn", **mla_kw)
        dq_pre, dk_pre, dkpe = _mla_rope_bwd(dmq, dmk, cos_m, sin_m, lp=lp, name="d_mla_rope")
        dmvb = dmv.astype(BF16)
        g_uq = _mm(s["cqn"], dq_pre, mode="tn", out_dtype=BF16, name="dw_mla_uq")
        g_uk = _mm(s["kvn"], dk_pre, mode="tn", out_dtype=BF16, name="dw_mla_uk")
        g_uv = _mm(s["kvn"], dmvb, mode="tn", out_dtype=BF16, name="dw_mla_uv")
        big["w_mla_uq"] = _cols_to_slots(
            g_uq.reshape(q_rank, MLA_HEADS, MLA_SLOT)[:, :, :HEAD + MLA_ROPE].reshape(q_rank, -1))
        big["w_mla_ukv"] = _cols_to_slots(jnp.concatenate(
            [g_uk.reshape(kv_rank, MLA_HEADS, MLA_SLOT)[:, :, :HEAD], g_uv.reshape(kv_rank, MLA_HEADS, HEAD)],
            axis=-1).reshape(kv_rank, -1))
        dcqn = _mm(dq_pre, w["w_uq"], mode="nt", name="d_mla_cqn")
        dkvn = _mm(dk_pre, w["w_uk"], mode="nt", name="d_mla_kvn_k")
        dkvn = _mm(dmvb, w["w_uv"], mode="nt", add=dkvn, name="d_mla_kvn_v")
        _, dcq, g = _rmsnorm_bwd(s["proj"], row("mla_q_norm_g"), dcqn, width=q_rank, cblk=off_cq // q_rank,
                                 name="d_mla_q_norm")
        small_g["mla_q_norm_g"][li] = g[0]
        _, dckv, g = _rmsnorm_bwd(s["proj"], row("mla_kv_norm_g"), dkvn, width=kv_rank, cblk=off_ckv // kv_rank,
                                  name="d_mla_kv_norm")
        small_g["mla_kv_norm_g"][li] = g[0]
        for item in in_flight:
            finish_reduce(item, dckv)
        in_flight = [start_reduce(li, 1, big)]
        dproj = jnp.concatenate(
            [da.reshape(t, c_conv), dgte.reshape(t, c_conv), dq_g, dk_g, dgv.astype(BF16), dcq, dckv,
             dl0, dl1, dl2, dkpe], axis=1)
        du = _mm(dproj, w["w_in"], mode="nt", deps=[in_flight[0][2][4]], name="d_u")
        g_in = _mm(s["u"], dproj, mode="tn", out_dtype=BF16, deps=[in_flight[0][2][4]], name="dw_in")
        big["w_in"] = _cols_to_slots(jnp.concatenate(
            [g_in[:, :off_gate], g_in[:, off_kpe:off_kpe + MLA_ROPE], g_in[:, off_gate:off_kpe]], axis=1))
        dh, dhb, g = _rmsnorm_bwd(s["h_in"], row("mix_norm_g"), du, width=d, cblk=0, res=dh, name="d_mix_norm")
        small_g["mix_norm_g"][li] = g[0]
        in_flight.append(start_reduce(li, 0, big))

    dh3 = dh.reshape(b, lp, d)
    grad_x = dh3[:, N_META:l_valid]
    g_meta = _meta_grad(dh3, name="meta_grad")
    for item in in_flight:
        finish_reduce(item, g_meta)

    small_list = [g_meta, g_final[0]] + [jnp.stack(small_g[nm]) for nm in small_names]
    red = _allreduce_small(_pack_small(small_list), name="allreduce_small")
    red = _unpack_small(red.reshape(-1), [a.shape for a in small_list])
    grads = {"meta_tokens": lax.dynamic_slice_in_dim(red[0], chip * (d // N_CHIPS), d // N_CHIPS, axis=1),
             "final_norm_g": red[1]}
    for nm, val in zip(small_names, red[2:]):
        grads[nm] = val
    cs = c_conv // N_CHIPS
    grads["conv_dw"] = lax.dynamic_slice_in_dim(grads["conv_dw"], chip * cs, cs, axis=2).reshape(p["conv_dw"].shape)

    delta, new_m, new_v = {}, {}, {}
    for nm in p:
        if nm in adam:
            grads[nm], delta[nm], new_m[nm], new_v[nm] = (o.reshape(p[nm].shape) for o in adam[nm])
        else:
            delta[nm], new_m[nm], new_v[nm] = _adamw(p[nm], grads[nm], mom_m[nm], mom_v[nm], name="adamw_" + nm)
    return loss, grad_x, grads, delta, new_m, new_v


WEIGHT_NAMES = ("meta_tokens", "mix_norm_g", "w_in", "conv_dw", "conv_b", "conv_ln_g", "conv_ln_b", "w_conv_out",
                "gqa_q_norm_g", "gqa_k_norm_g", "w_gqa_out", "mla_q_norm_g", "w_mla_uq", "mla_kv_norm_g",
                "w_mla_ukv", "w_mla_out", "gate_b", "w_out", "ffn_norm_g", "w_ffn_gate", "w_ffn_up", "w_ffn_down",
                "final_norm_g")


def kernel(x, meta_tokens, mix_norm_g, w_in, conv_dw, conv_b, conv_ln_g, conv_ln_b, w_conv_out, gqa_q_norm_g, gqa_k_norm_g, w_gqa_out, mla_q_norm_g, w_mla_uq, mla_kv_norm_g, w_mla_ukv, w_mla_out, gate_b, w_out, ffn_norm_g, w_ffn_gate, w_ffn_up, w_ffn_down, final_norm_g, loss_target, m_meta_tokens, m_mix_norm_g, m_w_in, m_conv_dw, m_conv_b, m_conv_ln_g, m_conv_ln_b, m_w_conv_out, m_gqa_q_norm_g, m_gqa_k_norm_g, m_w_gqa_out, m_mla_q_norm_g, m_w_mla_uq, m_mla_kv_norm_g, m_w_mla_ukv, m_w_mla_out, m_gate_b, m_w_out, m_ffn_norm_g, m_w_ffn_gate, m_w_ffn_up, m_w_ffn_down, m_final_norm_g, v_meta_tokens, v_mix_norm_g, v_w_in, v_conv_dw, v_conv_b, v_conv_ln_g, v_conv_ln_b, v_w_conv_out, v_gqa_q_norm_g, v_gqa_k_norm_g, v_w_gqa_out, v_mla_q_norm_g, v_w_mla_uq, v_mla_kv_norm_g, v_w_mla_ukv, v_w_mla_out, v_gate_b, v_w_out, v_ffn_norm_g, v_w_ffn_gate, v_w_ffn_up, v_w_ffn_down, v_final_norm_g):
    ws = (meta_tokens, mix_norm_g, w_in, conv_dw, conv_b, conv_ln_g, conv_ln_b, w_conv_out, gqa_q_norm_g,
          gqa_k_norm_g, w_gqa_out, mla_q_norm_g, w_mla_uq, mla_kv_norm_g, w_mla_ukv, w_mla_out, gate_b, w_out,
          ffn_norm_g, w_ffn_gate, w_ffn_up, w_ffn_down, final_norm_g)
    ms = (m_meta_tokens, m_mix_norm_g, m_w_in, m_conv_dw, m_conv_b, m_conv_ln_g, m_conv_ln_b, m_w_conv_out,
          m_gqa_q_norm_g, m_gqa_k_norm_g, m_w_gqa_out, m_mla_q_norm_g, m_w_mla_uq, m_mla_kv_norm_g, m_w_mla_ukv,
          m_w_mla_out, m_gate_b, m_w_out, m_ffn_norm_g, m_w_ffn_gate, m_w_ffn_up, m_w_ffn_down, m_final_norm_g)
    vs = (v_meta_tokens, v_mix_norm_g, v_w_in, v_conv_dw, v_conv_b, v_conv_ln_g, v_conv_ln_b, v_w_conv_out,
          v_gqa_q_norm_g, v_gqa_k_norm_g, v_w_gqa_out, v_mla_q_norm_g, v_w_mla_uq, v_mla_kv_norm_g, v_w_mla_ukv,
          v_w_mla_out, v_gate_b, v_w_out, v_ffn_norm_g, v_w_ffn_gate, v_w_ffn_up, v_w_ffn_down, v_final_norm_g)
    p = dict(zip(WEIGHT_NAMES, ws))
    loss, grad_x, grads, delta, new_m, new_v = _step(p, dict(zip(WEIGHT_NAMES, ms)), dict(zip(WEIGHT_NAMES, vs)),
                                                     x, loss_target)
    return (loss, grad_x, *[grads[n] for n in WEIGHT_NAMES], *[delta[n] for n in WEIGHT_NAMES],
            *[new_m[n] for n in WEIGHT_NAMES], *[new_v[n] for n in WEIGHT_NAMES])
```

```python
import math

import jax
import jax.numpy as jnp
from jax import lax
from jax.experimental import pallas as pl
from jax.experimental.pallas import tpu as pltpu

F32 = jnp.float32
BF16 = jnp.bfloat16
SDS = jax.ShapeDtypeStruct
MESH = pl.DeviceIdType.MESH

N_META = 16
GRID_W = 64
ROPE_THETA = 10000.0
NORM_EPS = 1e-6
CONV_K = 31
CONV_HALO = 16
HEAD = 128
GQA_HEADS = 8
GQA_KV_HEADS = 2
MLA_HEADS = 8
MLA_ROPE = 64
MLA_SLOT = 256
N_BRANCH = 3

ADAM_LR = 0.001
ADAM_B1 = 0.9
ADAM_B2 = 0.999
ADAM_EPS = 1e-08
ADAM_WD = 0.01
ADAM_STEP = 10

LANES = 128
SUBLANES_BF16 = 16
VMEM_LIMIT_BYTES = 56 * 2 ** 20
ROW_BLOCK_BYTES = 2 << 20
ATTN_Q_ROWS = 1088
MM_BLOCK_BYTES = 48 * 2 ** 20
N_CHIPS = 4
N_DEV = 8


def _cparams():
    return pltpu.CompilerParams(vmem_limit_bytes=VMEM_LIMIT_BYTES)


def _tile(n, cap, mult):
    best = None
    for d in range(mult, min(n, cap) + 1, mult):
        if n % d == 0:
            best = d
    return n if best is None else best


def _row_tile(rows, width, mult=SUBLANES_BF16):
    return _tile(rows, max(mult, ROW_BLOCK_BYTES // (4 * width)), mult)


def _sigmoid(x):
    return 1.0 / (1.0 + jnp.exp(-x))


def _mm(a, b, *, mode="nn", add=None, out_dtype=F32, b_sharded=False, out_shards=None, deps=(), name):
    if b_sharded:
        s_b, r_b, c_b = b.shape
        b_rows, b_cols = r_b, s_b * c_b
    else:
        b_rows, b_cols = b.shape
    if mode == "nn":
        (m, k), n = a.shape, b_cols
        assert b_rows == k
    elif mode == "nt":
        (m, k), n = a.shape, b_rows
        assert b_cols == k
    else:
        (k, m), n = a.shape, b_cols
        assert b_rows == k
    n_unit = n
    if b_sharded and mode != "nt":
        n_unit = c_b
    if out_shards is not None:
        assert n % out_shards == 0
        n_unit = math.gcd(n_unit, n // out_shards)
    k_unit = c_b if (b_sharded and mode == "nt") else k
    if mode == "tn":
        tm = _tile(m, 1024, LANES)
        tk = _tile(k_unit, 2176, SUBLANES_BF16)
    else:
        tm = _tile(m, 1088, SUBLANES_BF16)
        tk = _tile(k_unit, 2176, LANES)
    nk = k // tk
    out_bytes = jnp.dtype(out_dtype).itemsize
    tn = LANES
    for cap in (2176, 1408, 1024, 512, 256):
        tn = _tile(n_unit, cap, LANES)
        blocks = tm * tk * a.dtype.itemsize + tk * tn * b.dtype.itemsize + tm * tn * out_bytes
        blocks += tm * tn * 4 if add is not None else 0
        scratch = tm * tn * 4 if (nk > 1 and out_dtype != F32) else 0
        if 2 * blocks + scratch <= MM_BLOCK_BYTES:
            break
    if mode == "nn":
        a_spec = pl.BlockSpec((tm, tk), lambda i, j, kk: (i, kk))
        dn = (((1,), (0,)), ((), ()))
        if b_sharded:
            per = c_b // tn
            b_spec = pl.BlockSpec((None, tk, tn), lambda i, j, kk: (j // per, kk, j % per))
        else:
            b_spec = pl.BlockSpec((tk, tn), lambda i, j, kk: (kk, j))
    elif mode == "nt":
        a_spec = pl.BlockSpec((tm, tk), lambda i, j, kk: (i, kk))
        dn = (((1,), (1,)), ((), ()))
        if b_sharded:
            per = c_b // tk
            b_spec = pl.BlockSpec((None, tn, tk), lambda i, j, kk: (kk // per, j, kk % per))
        else:
            b_spec = pl.BlockSpec((tn, tk), lambda i, j, kk: (j, kk))
    else:
        a_spec = pl.BlockSpec((tk, tm), lambda i, j, kk: (kk, i))
        dn = (((0,), (0,)), ((), ()))
        if b_sharded:
            per = c_b // tn
            b_spec = pl.BlockSpec((None, tk, tn), lambda i, j, kk: (j // per, kk, j % per))
        else:
            b_spec = pl.BlockSpec((tk, tn), lambda i, j, kk: (kk, j))
    if out_shards is None:
        o_spec = pl.BlockSpec((tm, tn), lambda i, j, kk: (i, j))
        out_shape = SDS((m, n), out_dtype)
    else:
        per_o = (n // out_shards) // tn
        o_spec = pl.BlockSpec((None, tm, tn), lambda i, j, kk: (j // per_o, i, j % per_o))
        out_shape = SDS((out_shards, m, n // out_shards), out_dtype)
    has_add = add is not None
    assert not (has_add and out_shards is not None)
    deps = list(deps)
    use_acc = nk > 1 and out_dtype != F32

    def body(a_ref, b_ref, *rest):
        add_ref = rest[0] if has_add else None
        o_ref = rest[int(has_add) + len(deps)]
        kk = pl.program_id(2)
        part = lax.dot_general(a_ref[...].astype(BF16), b_ref[...].astype(BF16), dn,
                               preferred_element_type=F32)
        if nk == 1:
            o_ref[...] = (part + add_ref[...] if has_add else part).astype(o_ref.dtype)
            return
        acc = rest[-1] if use_acc else o_ref

        @pl.when(kk == 0)
        def _():
            acc[...] = part

        @pl.when((kk > 0) & (kk < nk - 1))
        def _():
            acc[...] += part

        @pl.when(kk == nk - 1)
        def _():
            r = acc[...] + part
            if has_add:
                r = r + add_ref[...]
            o_ref[...] = r.astype(o_ref.dtype)

    in_specs = [a_spec, b_spec] + ([o_spec] if has_add else []) + [_ANY] * len(deps)
    args = (a, b) + ((add,) if has_add else ()) + tuple(deps)
    return pl.pallas_call(
        body, grid=(m // tm, n // tn, nk), in_specs=in_specs, out_specs=o_spec,
        out_shape=out_shape, scratch_shapes=[pltpu.VMEM((tm, tn), F32)] if use_acc else [],
        compiler_params=_cparams(), name=name)(*args)


def _rmsnorm_fwd(x, g, *, width, cblk, name):
    t = x.shape[0]
    tr = _row_tile(t, width)

    def body(x_ref, g_ref, o_ref):
        xv = x_ref[...]
        r = lax.rsqrt(jnp.mean(xv * xv, axis=-1, keepdims=True) + NORM_EPS)
        o_ref[...] = (xv * r * g_ref[...]).astype(o_ref.dtype)

    return pl.pallas_call(
        body, grid=(t // tr,),
        in_specs=[pl.BlockSpec((tr, width), lambda i: (i, cblk)), pl.BlockSpec((1, width), lambda i: (0, 0))],
        out_specs=pl.BlockSpec((tr, width), lambda i: (i, 0)),
        out_shape=SDS((t, width), BF16), compiler_params=_cparams(), name=name)(x, g)


def _rmsnorm_bwd(x, g, dy, *, width, cblk, res=None, name):
    t = x.shape[0]
    tr = _row_tile(t, width)
    has_res = res is not None

    def body(x_ref, g_ref, dy_ref, *rest):
        if has_res:
            res_ref, dx_ref, dxb_ref, dg_ref = rest
        else:
            dx_ref, dxb_ref, dg_ref = rest
        xv = x_ref[...]
        dyv = dy_ref[...].astype(F32)
        r = lax.rsqrt(jnp.mean(xv * xv, axis=-1, keepdims=True) + NORM_EPS)
        gy = dyv * g_ref[...]
        dx = r * gy - xv * (r * r * r) * jnp.mean(gy * xv, axis=-1, keepdims=True)
        if has_res:
            dx = dx + res_ref[...]
        dx_ref[...] = dx
        dxb_ref[...] = dx.astype(BF16)
        part = jnp.sum(dyv * xv * r, axis=0, keepdims=True)

        @pl.when(pl.program_id(0) == 0)
        def _():
            dg_ref[...] = part

        @pl.when(pl.program_id(0) > 0)
        def _():
            dg_ref[...] += part

    row = pl.BlockSpec((tr, width), lambda i: (i, 0))
    vec = pl.BlockSpec((1, width), lambda i: (0, 0))
    in_specs = [pl.BlockSpec((tr, width), lambda i: (i, cblk)), vec, row] + ([row] if has_res else [])
    args = (x, g, dy) + ((res,) if has_res else ())
    return pl.pallas_call(
        body, grid=(t // tr,), in_specs=in_specs, out_specs=[row, row, vec],
        out_shape=[SDS((t, width), F32), SDS((t, width), BF16), SDS((1, width), F32)],
        compiler_params=_cparams(), name=name)(*args)


def _valid_rows(tile_index, tr, lp, lo, hi):
    pos = (tile_index % (lp // tr)) * tr + lax.broadcasted_iota(jnp.int32, (tr, 1), 0)
    return (pos >= lo) & (pos < hi)


def _loss_head(h, g, target, *, lp, l_valid, name):
    t, d = h.shape
    tr = _tile(lp, max(SUBLANES_BF16, ROW_BLOCK_BYTES // (4 * d)), SUBLANES_BF16)

    def body(x_ref, g_ref, t_ref, loss_ref, dx_ref, dxb_ref, dg_ref):
        i = pl.program_id(0)
        xv = x_ref[...]
        r = lax.rsqrt(jnp.mean(xv * xv, axis=-1, keepdims=True) + NORM_EPS)
        y = xv * r * g_ref[...]
        valid = _valid_rows(i, tr, lp, N_META, l_valid)
        err = jnp.where(valid, y - t_ref[...], 0.0)
        lpart = 0.5 * jnp.sum(jnp.mean(err * err, axis=-1, keepdims=True), axis=0, keepdims=True)
        dyv = err * (1.0 / d)
        gy = dyv * g_ref[...]
        dx = r * gy - xv * (r * r * r) * jnp.mean(gy * xv, axis=-1, keepdims=True)
        dx_ref[...] = dx
        dxb_ref[...] = dx.astype(BF16)
        gpart = jnp.sum(dyv * xv * r, axis=0, keepdims=True)

        @pl.when(i == 0)
        def _():
            dg_ref[...] = gpart
            loss_ref[...] = jnp.broadcast_to(lpart, loss_ref.shape)

        @pl.when(i > 0)
        def _():
            dg_ref[...] += gpart
            loss_ref[...] += jnp.broadcast_to(lpart, loss_ref.shape)

    row = pl.BlockSpec((tr, d), lambda i: (i, 0))
    vec = pl.BlockSpec((1, d), lambda i: (0, 0))
    return pl.pallas_call(
        body, grid=(t // tr,), in_specs=[row, vec, row],
        out_specs=[pl.BlockSpec((1, LANES), lambda i: (0, 0)), row, row, vec],
        out_shape=[SDS((1, LANES), F32), SDS((t, d), F32), SDS((t, d), BF16), SDS((1, d), F32)],
        compiler_params=_cparams(), name=name)(h, g, target)


def _ln_silu_fwd(yc, g, b, *, name):
    t, c = yc.shape
    tr = _row_tile(t, c)

    def body(x_ref, g_ref, b_ref, o_ref):
        xv = x_ref[...]
        xc = xv - jnp.mean(xv, axis=-1, keepdims=True)
        r = lax.rsqrt(jnp.mean(xc * xc, axis=-1, keepdims=True) + NORM_EPS)
        ln = xc * r * g_ref[...] + b_ref[...]
        o_ref[...] = (ln * _sigmoid(ln)).astype(o_ref.dtype)

    row = pl.BlockSpec((tr, c), lambda i: (i, 0))
    vec = pl.BlockSpec((1, c), lambda i: (0, 0))
    return pl.pallas_call(body, grid=(t // tr,), in_specs=[row, vec, vec], out_specs=row,
                          out_shape=SDS((t, c), BF16), compiler_params=_cparams(), name=name)(yc, g, b)


def _ln_silu_bwd(yc, g, b, ds, *, name):
    t, c = yc.shape
    tr = _row_tile(t, c)

    def body(x_ref, g_ref, b_ref, ds_ref, dx_ref, dg_ref, db_ref):
        xv = x_ref[...]
        xc = xv - jnp.mean(xv, axis=-1, keepdims=True)
        r = lax.rsqrt(jnp.mean(xc * xc, axis=-1, keepdims=True) + NORM_EPS)
        xh = xc * r
        ln = xh * g_ref[...] + b_ref[...]
        sg = _sigmoid(ln)
        dln = ds_ref[...].astype(F32) * (sg * (1.0 + ln * (1.0 - sg)))
        gy = dln * g_ref[...]
        dx_ref[...] = r * (gy - jnp.mean(gy, axis=-1, keepdims=True)
                           - xh * jnp.mean(gy * xh, axis=-1, keepdims=True))
        gpart = jnp.sum(dln * xh, axis=0, keepdims=True)
        bpart = jnp.sum(dln, axis=0, keepdims=True)

        @pl.when(pl.program_id(0) == 0)
        def _():
            dg_ref[...] = gpart
            db_ref[...] = bpart

        @pl.when(pl.program_id(0) > 0)
        def _():
            dg_ref[...] += gpart
            db_ref[...] += bpart

    row = pl.BlockSpec((tr, c), lambda i: (i, 0))
    vec = pl.BlockSpec((1, c), lambda i: (0, 0))
    return pl.pallas_call(
        body, grid=(t // tr,), in_specs=[row, vec, vec, row], out_specs=[row, vec, vec],
        out_shape=[SDS((t, c), F32), SDS((1, c), F32), SDS((1, c), F32)],
        compiler_params=_cparams(), name=name)(yc, g, b, ds)


def _swiglu_fwd(gate, up, *, name):
    t, f = gate.shape
    tr = _row_tile(t, f)

    def body(g_ref, u_ref, o_ref):
        gv = g_ref[...].astype(F32)
        o_ref[...] = (gv * _sigmoid(gv) * u_ref[...].astype(F32)).astype(o_ref.dtype)

    row = pl.BlockSpec((tr, f), lambda i: (i, 0))
    return pl.pallas_call(body, grid=(t // tr,), in_specs=[row, row], out_specs=row,
                          out_shape=SDS((t, f), BF16), compiler_params=_cparams(), name=name)(gate, up)


def _swiglu_bwd(gate, up, dact, *, name):
    t, f = gate.shape
    tr = _row_tile(t, f)

    def body(g_ref, u_ref, d_ref, dg_ref, du_ref):
        gv = g_ref[...].astype(F32)
        sg = _sigmoid(gv)
        dv = d_ref[...]
        dg_ref[...] = (dv * u_ref[...].astype(F32) * (sg * (1.0 + gv * (1.0 - sg)))).astype(dg_ref.dtype)
        du_ref[...] = (dv * (gv * sg)).astype(du_ref.dtype)

    row = pl.BlockSpec((tr, f), lambda i: (i, 0))
    return pl.pallas_call(body, grid=(t // tr,), in_specs=[row, row, row], out_specs=[row, row],
                          out_shape=[SDS((t, f), BF16), SDS((t, f), BF16)],
                          compiler_params=_cparams(), name=name)(gate, up, dact)


def _merge_tiles(t, d):
    tc = _tile(d, 512, LANES)
    return _row_tile(t, tc), tc


def _merge_fwd(proj, gate_b, ya, yb, yc, *, gate_off, name):
    t, d = ya.shape
    tr, tc = _merge_tiles(t, d)
    nc = d // tc
    assert gate_off % tc == 0
    g0 = gate_off // tc

    def body(l0, l1, l2, b0, b1, b2, ya_ref, yb_ref, yc_ref, o_ref):
        acc = _sigmoid(l0[...] + b0[...]) * ya_ref[...]
        acc += _sigmoid(l1[...] + b1[...]) * yb_ref[...]
        acc += _sigmoid(l2[...] + b2[...]) * yc_ref[...]
        o_ref[...] = acc.astype(o_ref.dtype)

    lspec = [pl.BlockSpec((tr, tc), lambda i, j, br=br: (i, g0 + br * nc + j)) for br in range(N_BRANCH)]
    bspec = [pl.BlockSpec((1, tc), lambda i, j, br=br: (0, br * nc + j)) for br in range(N_BRANCH)]
    yspec = pl.BlockSpec((tr, tc), lambda i, j: (i, j))
    return pl.pallas_call(
        body, grid=(t // tr, nc), in_specs=lspec + bspec + [yspec] * 3, out_specs=yspec,
        out_shape=SDS((t, d), BF16), compiler_params=_cparams(), name=name)(
            proj, proj, proj, gate_b, gate_b, gate_b, ya, yb, yc)


def _merge_bwd(proj, gate_b, ya, yb, yc, dm, *, gate_off, name):
    t, d = ya.shape
    tr, tc = _merge_tiles(t, d)
    nc = d // tc
    g0 = gate_off // tc

    def body(l0, l1, l2, b0, b1, b2, ya_ref, yb_ref, yc_ref, dm_ref,
             dya, dyb, dyc, dl0, dl1, dl2, db0, db1, db2):
        i = pl.program_id(1)
        dmv = dm_ref[...]
        for l_ref, b_ref, y_ref, dy_ref, dl_ref, db_ref in (
                (l0, b0, ya_ref, dya, dl0, db0), (l1, b1, yb_ref, dyb, dl1, db1), (l2, b2, yc_ref, dyc, dl2, db2)):
            gt = _sigmoid(l_ref[...] + b_ref[...])
            dy_ref[...] = (dmv * gt).astype(dy_ref.dtype)
            dl = dmv * y_ref[...] * gt * (1.0 - gt)
            dl_ref[...] = dl.astype(dl_ref.dtype)
            part = jnp.sum(dl, axis=0, keepdims=True)

            @pl.when(i == 0)
            def _(db_ref=db_ref, part=part):
                db_ref[...] = part

            @pl.when(i > 0)
            def _(db_ref=db_ref, part=part):
                db_ref[...] += part

    lspec = [pl.BlockSpec((tr, tc), lambda j, i, br=br: (i, g0 + br * nc + j)) for br in range(N_BRANCH)]
    bspec = [pl.BlockSpec((1, tc), lambda j, i, br=br: (0, br * nc + j)) for br in range(N_BRANCH)]
    yspec = pl.BlockSpec((tr, tc), lambda j, i: (i, j))
    vspec = pl.BlockSpec((1, tc), lambda j, i: (0, j))
    return pl.pallas_call(
        body, grid=(nc, t // tr), in_specs=lspec + bspec + [yspec] * 4,
        out_specs=[yspec] * 6 + [vspec] * 3,
        out_shape=[SDS((t, d), BF16)] * 6 + [SDS((1, d), F32)] * 3,
        compiler_params=_cparams(), name=name)(proj, proj, proj, gate_b, gate_b, gate_b, ya, yb, yc, dm)


def _conv_fwd(proj3, dw, cb, *, c_conv, l_valid, name):
    b, lp, _ = proj3.shape
    cw = _tile(c_conv, 256, LANES)
    nc = c_conv // cw

    def body(a_ref, gte_ref, dw_ref, cb_ref, o_ref, zs):
        rows = lax.broadcasted_iota(jnp.int32, (lp, 1), 0)
        z = jnp.where(rows < l_valid, a_ref[...] * _sigmoid(gte_ref[...]), 0.0)
        zs[pl.ds(0, CONV_HALO), :] = jnp.zeros((CONV_HALO, cw), F32)
        zs[pl.ds(CONV_HALO, lp), :] = z
        zs[pl.ds(CONV_HALO + lp, CONV_HALO), :] = jnp.zeros((CONV_HALO, cw), F32)
        acc = jnp.broadcast_to(cb_ref[...], (lp, cw))
        for j in range(CONV_K):
            acc = acc + zs[pl.ds(j + 1, lp), :] * dw_ref[pl.ds(j, 1), :]
        o_ref[...] = acc

    seq = lambda off: pl.BlockSpec((None, lp, cw), lambda bi, ci: (bi, 0, off + ci))
    return pl.pallas_call(
        body, grid=(b, nc),
        in_specs=[seq(0), seq(nc), pl.BlockSpec((CONV_K, cw), lambda bi, ci: (0, ci)),
                  pl.BlockSpec((1, cw), lambda bi, ci: (0, ci))],
        out_specs=seq(0), out_shape=SDS((b, lp, c_conv), F32),
        scratch_shapes=[pltpu.VMEM((lp + 2 * CONV_HALO, cw), F32)],
        compiler_params=_cparams(), name=name)(proj3, proj3, dw, cb)


def _conv_bwd(proj3, dw, dyc3, *, c_conv, l_valid, name):
    b, lp, _ = proj3.shape
    cw = _tile(c_conv, 256, LANES)
    nc = c_conv // cw

    def body(a_ref, gte_ref, dw_ref, dy_ref, da_ref, dgte_ref, ddw_ref, dcb_ref, zs, dys):
        bi = pl.program_id(1)
        rows = lax.broadcasted_iota(jnp.int32, (lp, 1), 0)
        valid = rows < l_valid
        av = a_ref[...]
        sg = _sigmoid(gte_ref[...])
        dyv = dy_ref[...]
        zero = jnp.zeros((CONV_HALO, cw), F32)
        zs[pl.ds(0, CONV_HALO), :] = zero
        zs[pl.ds(CONV_HALO, lp), :] = jnp.where(valid, av * sg, 0.0)
        zs[pl.ds(CONV_HALO + lp, CONV_HALO), :] = zero
        dys[pl.ds(0, CONV_HALO), :] = zero
        dys[pl.ds(CONV_HALO, lp), :] = dyv
        dys[pl.ds(CONV_HALO + lp, CONV_HALO), :] = zero

        @pl.when(bi == 0)
        def _():
            ddw_ref[...] = jnp.zeros_like(ddw_ref)
            dcb_ref[...] = jnp.zeros_like(dcb_ref)

        dz = jnp.zeros((lp, cw), F32)
        for j in range(CONV_K):
            dz = dz + dys[pl.ds(CONV_K - j, lp), :] * dw_ref[pl.ds(j, 1), :]
            ddw_ref[pl.ds(j, 1), :] += jnp.sum(dyv * zs[pl.ds(j + 1, lp), :], axis=0, keepdims=True)
        dcb_ref[...] += jnp.sum(dyv, axis=0, keepdims=True)
        dz = jnp.where(valid, dz, 0.0)
        da_ref[...] = (dz * sg).astype(da_ref.dtype)
        dgte_ref[...] = (dz * av * sg * (1.0 - sg)).astype(dgte_ref.dtype)

    seq = lambda off: pl.BlockSpec((None, lp, cw), lambda ci, bi: (bi, 0, off + ci))
    return pl.pallas_call(
        body, grid=(nc, b),
        in_specs=[seq(0), seq(nc), pl.BlockSpec((CONV_K, cw), lambda ci, bi: (0, ci)), seq(0)],
        out_specs=[seq(0), seq(0), pl.BlockSpec((CONV_K, cw), lambda ci, bi: (0, ci)),
                   pl.BlockSpec((1, cw), lambda ci, bi: (0, ci))],
        out_shape=[SDS((b, lp, c_conv), BF16), SDS((b, lp, c_conv), BF16),
                   SDS((CONV_K, c_conv), F32), SDS((1, c_conv), F32)],
        scratch_shapes=[pltpu.VMEM((lp + 2 * CONV_HALO, cw), F32)] * 2,
        compiler_params=_cparams(), name=name)(proj3, proj3, dw, dyc3)


def _swap_halves(x, group):
    half = group // 2
    lane = lax.broadcasted_iota(jnp.int32, x.shape, 1)
    up = pltpu.roll(x, LANES - half, 1)
    down = pltpu.roll(x, half, 1)
    return jnp.where((lane % group) < half, up, down)


def _rope(x, cos, sin, group):
    return x * cos + _swap_halves(x, group) * sin


def _rope_bwd(dy, cos, sin, group):
    return dy * cos + _swap_halves(dy * sin, group)


def _rope_tables(lp, seq, dim, lanes):
    quarter = dim // 4
    tok = jnp.arange(seq, dtype=jnp.int32)
    zeros = jnp.zeros((N_META,), F32)
    pad = jnp.zeros((lp - N_META - seq,), F32)
    row = jnp.concatenate([zeros, (tok // GRID_W).astype(F32), pad])
    col = jnp.concatenate([zeros, (tok % GRID_W).astype(F32), pad])
    inv = ROPE_THETA ** (-jnp.arange(quarter, dtype=F32) / quarter)
    ar, ac = row[:, None] * inv[None, :], col[:, None] * inv[None, :]
    cos = jnp.concatenate([jnp.cos(ar), jnp.cos(ar), jnp.cos(ac), jnp.cos(ac)], axis=1)
    sin = jnp.concatenate([-jnp.sin(ar), jnp.sin(ar), -jnp.sin(ac), jnp.sin(ac)], axis=1)
    if lanes > dim:
        cos = jnp.concatenate([cos, jnp.ones((lp, lanes - dim), F32)], axis=1)
        sin = jnp.concatenate([sin, jnp.zeros((lp, lanes - dim), F32)], axis=1)
    return cos, sin


def _gqa_prep_fwd(proj, qg, kg, cos, sin, *, q_off, k_off, v_off, lp, name):
    t = proj.shape[0]
    qw, kw = GQA_HEADS * HEAD, GQA_KV_HEADS * HEAD
    tr = _tile(lp, max(SUBLANES_BF16, ROW_BLOCK_BYTES // (4 * qw)), SUBLANES_BF16)
    npos = lp // tr
    assert q_off % qw == 0 and k_off % kw == 0 and v_off % kw == 0

    def body(q_ref, k_ref, v_ref, qg_ref, kg_ref, cos_ref, sin_ref, qo_ref, ko_ref, vo_ref):
        cosv, sinv = cos_ref[...], sin_ref[...]
        for src, g_ref, dst, heads in ((q_ref, qg_ref, qo_ref, GQA_HEADS), (k_ref, kg_ref, ko_ref, GQA_KV_HEADS)):
            for hh in range(heads):
                xh = src[:, hh * HEAD:(hh + 1) * HEAD]
                r = lax.rsqrt(jnp.mean(xh * xh, axis=-1, keepdims=True) + NORM_EPS)
                dst[:, hh * HEAD:(hh + 1) * HEAD] = _rope(xh * r * g_ref[...], cosv, sinv, 64).astype(dst.dtype)
        vo_ref[...] = v_ref[...].astype(vo_ref.dtype)

    tab = pl.BlockSpec((tr, HEAD), lambda i: (i % npos, 0))
    vec = pl.BlockSpec((1, HEAD), lambda i: (0, 0))
    return pl.pallas_call(
        body, grid=(t // tr,),
        in_specs=[pl.BlockSpec((tr, qw), lambda i: (i, q_off // qw)),
                  pl.BlockSpec((tr, kw), lambda i: (i, k_off // kw)),
                  pl.BlockSpec((tr, kw), lambda i: (i, v_off // kw)), vec, vec, tab, tab],
        out_specs=[pl.BlockSpec((tr, qw), lambda i: (i, 0)), pl.BlockSpec((tr, kw), lambda i: (i, 0)),
                   pl.BlockSpec((tr, kw), lambda i: (i, 0))],
        out_shape=[SDS((t, qw), BF16), SDS((t, kw), BF16), SDS((t, kw), BF16)],
        compiler_params=_cparams(), name=name)(proj, proj, proj, qg, kg, cos, sin)


def _gqa_prep_bwd(proj, qg, kg, cos, sin, dqr, dkr, *, q_off, k_off, lp, name):
    t = proj.shape[0]
    qw, kw = GQA_HEADS * HEAD, GQA_KV_HEADS * HEAD
    tr = _tile(lp, max(SUBLANES_BF16, ROW_BLOCK_BYTES // (4 * qw)), SUBLANES_BF16)
    npos = lp // tr

    def body(q_ref, k_ref, qg_ref, kg_ref, cos_ref, sin_ref, dqr_ref, dkr_ref, dq_ref, dk_ref, dqg_ref, dkg_ref):
        cosv, sinv = cos_ref[...], sin_ref[...]
        for src, g_ref, dy_ref, dx_ref, dg_ref, heads in (
                (q_ref, qg_ref, dqr_ref, dq_ref, dqg_ref, GQA_HEADS),
                (k_ref, kg_ref, dkr_ref, dk_ref, dkg_ref, GQA_KV_HEADS)):
            gpart = jnp.zeros((1, HEAD), F32)
            for hh in range(heads):
                sl = slice(hh * HEAD, (hh + 1) * HEAD)
                xh = src[:, sl]
                r = lax.rsqrt(jnp.mean(xh * xh, axis=-1, keepdims=True) + NORM_EPS)
                dxn = _rope_bwd(dy_ref[:, sl], cosv, sinv, 64)
                gy = dxn * g_ref[...]
                dx = r * gy - xh * (r * r * r) * jnp.mean(gy * xh, axis=-1, keepdims=True)
                dx_ref[:, sl] = dx.astype(dx_ref.dtype)
                gpart = gpart + jnp.sum(dxn * xh * r, axis=0, keepdims=True)

            @pl.when(pl.program_id(0) == 0)
            def _(dg_ref=dg_ref, gpart=gpart):
                dg_ref[...] = gpart

            @pl.when(pl.program_id(0) > 0)
            def _(dg_ref=dg_ref, gpart=gpart):
                dg_ref[...] += gpart

    tab = pl.BlockSpec((tr, HEAD), lambda i: (i % npos, 0))
    vec = pl.BlockSpec((1, HEAD), lambda i: (0, 0))
    qrow = pl.BlockSpec((tr, qw), lambda i: (i, 0))
    krow = pl.BlockSpec((tr, kw), lambda i: (i, 0))
    return pl.pallas_call(
        body, grid=(t // tr,),
        in_specs=[pl.BlockSpec((tr, qw), lambda i: (i, q_off // qw)),
                  pl.BlockSpec((tr, kw), lambda i: (i, k_off // kw)), vec, vec, tab, tab, qrow, krow],
        out_specs=[qrow, krow, vec, vec],
        out_shape=[SDS((t, qw), BF16), SDS((t, kw), BF16), SDS((1, HEAD), F32), SDS((1, HEAD), F32)],
        compiler_params=_cparams(), name=name)(proj, proj, qg, kg, cos, sin, dqr, dkr)


def _mla_rope_fwd(q_pre, k_pre, proj, cos, sin, *, kpe_off, lp, name):
    t, w = q_pre.shape
    tr = _tile(lp, max(SUBLANES_BF16, ROW_BLOCK_BYTES // (4 * w)), SUBLANES_BF16)
    npos = lp // tr
    assert kpe_off % LANES == 0

    def body(q_ref, k_ref, kpe_ref, cos_ref, sin_ref, qo_ref, ko_ref):
        cosv, sinv = cos_ref[...], sin_ref[...]
        kr = _rope(kpe_ref[...], cosv, sinv, 32).astype(ko_ref.dtype)
        for hh in range(MLA_HEADS):
            base = hh * MLA_SLOT
            qo_ref[:, base:base + HEAD] = q_ref[:, base:base + HEAD].astype(qo_ref.dtype)
            qo_ref[:, base + HEAD:base + MLA_SLOT] = _rope(
                q_ref[:, base + HEAD:base + MLA_SLOT], cosv, sinv, 32).astype(qo_ref.dtype)
            ko_ref[:, base:base + HEAD] = k_ref[:, base:base + HEAD].astype(ko_ref.dtype)
            ko_ref[:, base + HEAD:base + MLA_SLOT] = kr

    row = pl.BlockSpec((tr, w), lambda i: (i, 0))
    tab = pl.BlockSpec((tr, LANES), lambda i: (i % npos, 0))
    return pl.pallas_call(
        body, grid=(t // tr,),
        in_specs=[row, row, pl.BlockSpec((tr, LANES), lambda i: (i, kpe_off // LANES)), tab, tab],
        out_specs=[row, row], out_shape=[SDS((t, w), BF16), SDS((t, w), BF16)],
        compiler_params=_cparams(), name=name)(q_pre, k_pre, proj, cos, sin)


def _mla_rope_bwd(dq, dk, cos, sin, *, lp, name):
    t, w = dq.shape
    tr = _tile(lp, max(SUBLANES_BF16, ROW_BLOCK_BYTES // (4 * w)), SUBLANES_BF16)
    npos = lp // tr

    def body(dq_ref, dk_ref, cos_ref, sin_ref, dqo_ref, dko_ref, dkpe_ref):
        cosv, sinv = cos_ref[...], sin_ref[...]
        pe = jnp.zeros((tr, LANES), F32)
        for hh in range(MLA_HEADS):
            base = hh * MLA_SLOT
            dqo_ref[:, base:base + HEAD] = dq_ref[:, base:base + HEAD].astype(dqo_ref.dtype)
            dqo_ref[:, base + HEAD:base + MLA_SLOT] = _rope_bwd(
                dq_ref[:, base + HEAD:base + MLA_SLOT], cosv, sinv, 32).astype(dqo_ref.dtype)
            dko_ref[:, base:base + HEAD] = dk_ref[:, base:base + HEAD].astype(dko_ref.dtype)
            dko_ref[:, base + HEAD:base + MLA_SLOT] = jnp.zeros((tr, LANES), dko_ref.dtype)
            pe = pe + dk_ref[:, base + HEAD:base + MLA_SLOT]
        dkpe_ref[...] = _rope_bwd(pe, cosv, sinv, 32).astype(dkpe_ref.dtype)

    row = pl.BlockSpec((tr, w), lambda i: (i, 0))
    tab = pl.BlockSpec((tr, LANES), lambda i: (i % npos, 0))
    return pl.pallas_call(
        body, grid=(t // tr,), in_specs=[row, row, tab, tab],
        out_specs=[row, row, pl.BlockSpec((tr, LANES), lambda i: (i, 0))],
        out_shape=[SDS((t, w), BF16), SDS((t, w), BF16), SDS((t, LANES), BF16)],
        compiler_params=_cparams(), name=name)(dq, dk, cos, sin)


def _softmax_rows(q, k, scale, lp, l_valid):
    s = lax.dot_general(q, k, (((1,), (1,)), ((), ())), preferred_element_type=F32) * scale
    cols = lax.broadcasted_iota(jnp.int32, (1, lp), 1)
    s = jnp.where(cols < l_valid, s, -1e30)
    e = jnp.exp(s - jnp.max(s, axis=-1, keepdims=True))
    return e, 1.0 / jnp.sum(e, axis=-1, keepdims=True)


def _attn_fwd(q, k, v, *, batch, lp, l_valid, heads, kv_heads, dk, dv, scale, name):
    group = heads // kv_heads
    tq = _tile(lp, ATTN_Q_ROWS, SUBLANES_BF16)
    nq = lp // tq

    def body(q_ref, k_ref, v_ref, o_ref):
        e, inv = _softmax_rows(q_ref[...], k_ref[...], scale, lp, l_valid)
        o_ref[...] = (jnp.dot(e.astype(BF16), v_ref[...], preferred_element_type=F32) * inv).astype(o_ref.dtype)

    return pl.pallas_call(
        body, grid=(batch, heads, nq),
        in_specs=[pl.BlockSpec((tq, dk), lambda b, h, i: (b * nq + i, h)),
                  pl.BlockSpec((lp, dk), lambda b, h, i: (b, h // group)),
                  pl.BlockSpec((lp, dv), lambda b, h, i: (b, h // group))],
        out_specs=pl.BlockSpec((tq, dv), lambda b, h, i: (b * nq + i, h)),
        out_shape=SDS((batch * lp, heads * dv), BF16), compiler_params=_cparams(), name=name)(q, k, v)


def _attn_bwd(q, k, v, do, *, batch, lp, l_valid, heads, kv_heads, dk, dv, scale, name):
    group = heads // kv_heads
    tq = _tile(lp, ATTN_Q_ROWS, SUBLANES_BF16)
    nq = lp // tq

    def body(q_ref, k_ref, v_ref, do_ref, dq_ref, dk_ref, dv_ref):
        first = (pl.program_id(2) == 0) & (pl.program_id(3) == 0)
        qv, kv, dov = q_ref[...], k_ref[...], do_ref[...]
        e, inv = _softmax_rows(qv, kv, scale, lp, l_valid)
        dp = lax.dot_general(dov, v_ref[...], (((1,), (1,)), ((), ())), preferred_element_type=F32)
        delta = inv * jnp.sum(e * dp, axis=-1, keepdims=True)
        ds = (e * ((dp - delta) * (inv * scale))).astype(BF16)
        dq_ref[...] = jnp.dot(ds, kv, preferred_element_type=F32)
        dkp = lax.dot_general(ds, qv, (((0,), (0,)), ((), ())), preferred_element_type=F32)
        dvp = lax.dot_general(e.astype(BF16), (dov * inv).astype(BF16), (((0,), (0,)), ((), ())),
                              preferred_element_type=F32)

        @pl.when(first)
        def _():
            dk_ref[...] = dkp
            dv_ref[...] = dvp

        @pl.when(jnp.logical_not(first))
        def _():
            dk_ref[...] += dkp
            dv_ref[...] += dvp

    return pl.pallas_call(
        body, grid=(batch, kv_heads, group, nq),
        in_specs=[pl.BlockSpec((tq, dk), lambda b, hk, g, i: (b * nq + i, hk * group + g)),
                  pl.BlockSpec((lp, dk), lambda b, hk, g, i: (b, hk)),
                  pl.BlockSpec((lp, dv), lambda b, hk, g, i: (b, hk)),
                  pl.BlockSpec((tq, dv), lambda b, hk, g, i: (b * nq + i, hk * group + g))],
        out_specs=[pl.BlockSpec((tq, dk), lambda b, hk, g, i: (b * nq + i, hk * group + g)),
                   pl.BlockSpec((lp, dk), lambda b, hk, g, i: (b, hk)),
                   pl.BlockSpec((lp, dv), lambda b, hk, g, i: (b, hk))],
        out_shape=[SDS((batch * lp, heads * dk), F32), SDS((batch * lp, kv_heads * dk), F32),
                   SDS((batch * lp, kv_heads * dv), F32)],
        compiler_params=_cparams(), name=name)(q, k, v, do)


def _meta_grad(dh3, *, name):
    b, _, d = dh3.shape

    def body(x_ref, o_ref):
        @pl.when(pl.program_id(0) == 0)
        def _():
            o_ref[...] = x_ref[...]

        @pl.when(pl.program_id(0) > 0)
        def _():
            o_ref[...] += x_ref[...]

    return pl.pallas_call(
        body, grid=(b,), in_specs=[pl.BlockSpec((None, N_META, d), lambda bi: (bi, 0, 0))],
        out_specs=pl.BlockSpec((N_META, d), lambda bi: (0, 0)), out_shape=SDS((N_META, d), F32),
        compiler_params=_cparams(), name=name)(dh3)


def _sum_slots(parts, *, out_dtype, name):
    _, r, c = parts[0].shape
    tr = _row_tile(r, c)
    counts = [p.shape[0] for p in parts]

    def body(*refs):
        o_ref = refs[-1]
        acc = None
        for ref, cnt in zip(refs[:-1], counts):
            for s in range(cnt):
                term = ref[s].astype(F32)
                acc = term if acc is None else acc + term
        o_ref[...] = acc.astype(o_ref.dtype)

    return pl.pallas_call(
        body, grid=(r // tr,),
        in_specs=[pl.BlockSpec((cnt, tr, c), lambda i: (0, i, 0)) for cnt in counts],
        out_specs=pl.BlockSpec((tr, c), lambda i: (i, 0)), out_shape=SDS((r, c), out_dtype),
        compiler_params=_cparams(), name=name)(*parts)


def _adamw(w, g, m, v, *, name):
    shape = w.shape
    cols = shape[-1]
    rows = w.size // cols
    tr = _tile(rows, max(8, ROW_BLOCK_BYTES // (4 * cols)), 8)
    bc1 = 1.0 - ADAM_B1 ** ADAM_STEP
    bc2 = 1.0 - ADAM_B2 ** ADAM_STEP

    def body(w_ref, g_ref, m_ref, v_ref, d_ref, mo_ref, vo_ref):
        gv = g_ref[...]
        mn = ADAM_B1 * m_ref[...] + (1.0 - ADAM_B1) * gv
        vn = ADAM_B2 * v_ref[...] + (1.0 - ADAM_B2) * (gv * gv)
        mo_ref[...] = mn
        vo_ref[...] = vn
        d_ref[...] = -ADAM_LR * ((mn / bc1) / (jnp.sqrt(vn / bc2) + ADAM_EPS) + ADAM_WD * w_ref[...])

    spec = pl.BlockSpec((tr, cols), lambda i: (i, 0))
    outs = pl.pallas_call(
        body, grid=(rows // tr,), in_specs=[spec] * 4, out_specs=[spec] * 3,
        out_shape=[SDS((rows, cols), F32)] * 3, compiler_params=_cparams(), name=name)(
            *[a.reshape(rows, cols) for a in (w, g, m, v)])
    return tuple(o.reshape(shape) for o in outs)


def _adamw_layer(w, g, m, v, li, outs, *, name):
    depth, k, n = w.shape
    tr = _tile(k, max(8, ROW_BLOCK_BYTES // (4 * n)), 8)
    bc1 = 1.0 - ADAM_B1 ** ADAM_STEP
    bc2 = 1.0 - ADAM_B2 ** ADAM_STEP
    flat = (depth * k, n)
    if outs is None:
        outs = tuple(lax.empty(flat, F32) for _ in range(4))

    def body(w_ref, g_ref, m_ref, v_ref, *rest):
        go_ref, d_ref, mo_ref, vo_ref = rest[4:]
        gv = g_ref[...]
        mn = ADAM_B1 * m_ref[...] + (1.0 - ADAM_B1) * gv
        vn = ADAM_B2 * v_ref[...] + (1.0 - ADAM_B2) * (gv * gv)
        go_ref[...] = gv
        mo_ref[...] = mn
        vo_ref[...] = vn
        d_ref[...] = -ADAM_LR * ((mn / bc1) / (jnp.sqrt(vn / bc2) + ADAM_EPS) + ADAM_WD * w_ref[...])

    layer = pl.BlockSpec((tr, n), lambda i: (li * (k // tr) + i, 0))
    return tuple(pl.pallas_call(
        body, grid=(k // tr,), in_specs=[layer, pl.BlockSpec((tr, n), lambda i: (i, 0)), layer, layer] + [_ANY] * 4,
        out_specs=[layer] * 4, out_shape=[SDS(flat, F32)] * 4,
        input_output_aliases={4 + j: j for j in range(4)},
        compiler_params=_cparams(), name=name)(w.reshape(flat), g, m.reshape(flat), v.reshape(flat), *outs))


def _mesh_pos():
    return lax.axis_index("x"), lax.axis_index("y"), lax.axis_index("c")


def _other_chips(x, y):
    return [(1 - x, y), (x, 1 - y), (1 - x, 1 - y)]


def _chip_allgather_small(vec, *, name):
    r, c = vec.shape

    def body(x_ref, out_ref, send_sems, recv_sems):
        x, y, cc = _mesh_pos()
        mine = 2 * x + y
        out_ref[mine] = x_ref[...]
        chips = _other_chips(x, y)
        sends = [pltpu.make_async_remote_copy(
            src_ref=x_ref, dst_ref=out_ref.at[mine], send_sem=send_sems.at[j], recv_sem=recv_sems.at[j],
            device_id=(px, py, cc), device_id_type=MESH) for j, (px, py) in enumerate(chips)]
        for cp in sends:
            cp.start()
        for j, (px, py) in enumerate(chips):
            pltpu.make_async_remote_copy(
                src_ref=x_ref, dst_ref=out_ref.at[2 * px + py], send_sem=send_sems.at[j],
                recv_sem=recv_sems.at[j], device_id=(px, py, cc), device_id_type=MESH).wait_recv()
        for cp in sends:
            cp.wait_send()

    return pl.pallas_call(
        body, out_shape=SDS((N_CHIPS, r, c), F32),
        in_specs=[pl.BlockSpec(memory_space=pltpu.VMEM)], out_specs=pl.BlockSpec(memory_space=pltpu.VMEM),
        scratch_shapes=[pltpu.SemaphoreType.DMA((3,)), pltpu.SemaphoreType.DMA((3,))],
        compiler_params=_cparams(), name=name)(vec)


def _allreduce_small(vec, *, name):
    r, c = vec.shape

    def body(x_ref, out_ref, buf, send_sems, recv_sems):
        x, y, cc = _mesh_pos()
        me = 4 * x + 2 * y + cc
        buf[me] = x_ref[...]

        def peer(kk):
            fx, fy, fc = (kk >> 2) & 1, (kk >> 1) & 1, kk & 1
            px = x if fx == 0 else 1 - x
            py = y if fy == 0 else 1 - y
            pc = cc if fc == 0 else 1 - cc
            return (px, py, pc), 4 * px + 2 * py + pc

        sends = []
        for kk in range(1, N_DEV):
            dev, _ = peer(kk)
            sends.append(pltpu.make_async_remote_copy(
                src_ref=x_ref, dst_ref=buf.at[me], send_sem=send_sems.at[kk - 1], recv_sem=recv_sems.at[kk - 1],
                device_id=dev, device_id_type=MESH))
        for cp in sends:
            cp.start()
        for kk in range(1, N_DEV):
            dev, slot = peer(kk)
            pltpu.make_async_remote_copy(
                src_ref=x_ref, dst_ref=buf.at[slot], send_sem=send_sems.at[kk - 1], recv_sem=recv_sems.at[kk - 1],
                device_id=dev, device_id_type=MESH).wait_recv()
        for cp in sends:
            cp.wait_send()
        acc = buf[0]
        for s in range(1, N_DEV):
            acc = acc + buf[s]
        out_ref[...] = acc

    return pl.pallas_call(
        body, out_shape=SDS((r, c), F32),
        in_specs=[pl.BlockSpec(memory_space=pltpu.VMEM)], out_specs=pl.BlockSpec(memory_space=pltpu.VMEM),
        scratch_shapes=[pltpu.VMEM((N_DEV, r, c), F32), pltpu.SemaphoreType.DMA((N_DEV - 1,)),
                        pltpu.SemaphoreType.DMA((N_DEV - 1,))],
        compiler_params=_cparams(), name=name)(vec)


_HBM = pl.BlockSpec(memory_space=pltpu.HBM)


def _half(rows, half):
    hk = rows // 2
    assert hk % SUBLANES_BF16 == 0
    return pl.ds(pl.multiple_of(half * hk, SUBLANES_BF16), hk)


_SEM = pl.BlockSpec(memory_space=pltpu.SEMAPHORE)
_ANY = pl.BlockSpec(memory_space=pl.ANY)
_DATAFLOW = pltpu.SideEffectType.DATAFLOW_SIDE_EFFECTING


def _in_hbm(a):
    return pltpu.with_memory_space_constraint(a, pltpu.HBM)


def _split_start(srcs, lands, prev, make_copy, *, name):
    nw = len(srcs)

    def body(*refs):
        src_refs, land_refs = refs[:nw], refs[nw:2 * nw]
        send_sems, recv_sems = refs[2 * nw + 1], refs[2 * nw + 2]
        token = refs[-1]
        for j in range(3):
            for i in range(nw):
                make_copy(src_refs, land_refs, send_sems, recv_sems, j, i, True).start()
        token[...] = jnp.zeros_like(token)

    outs = pl.pallas_call(
        body, name=name,
        out_shape=(pltpu.SemaphoreType.DMA((3 * nw,)), pltpu.SemaphoreType.DMA((3 * nw,)),
                   *[pltpu.HBM(a.shape, a.dtype) for a in srcs], *[pltpu.HBM(a.shape, a.dtype) for a in lands],
                   SDS((8, LANES), F32)),
        in_specs=[_HBM] * (2 * nw) + [_ANY],
        out_specs=(_SEM, _SEM, *[_HBM] * (2 * nw), pl.BlockSpec(memory_space=pltpu.VMEM)),
        input_output_aliases={i: 2 + i for i in range(2 * nw)},
        compiler_params=pltpu.CompilerParams(has_side_effects=_DATAFLOW),
    )(*[_in_hbm(a) for a in srcs], *[_in_hbm(a) for a in lands], prev)
    return outs[0], outs[1], list(outs[2:2 + nw]), list(outs[2 + nw:2 + 2 * nw]), outs[-1]


def _split_wait(pending, after, make_copy, *, name):
    send_sems, recv_sems, srcs, lands, _ = pending
    nw = len(srcs)

    def body(*refs):
        src_refs, land_refs = refs[:nw], refs[nw:2 * nw]
        send_refs, recv_refs = refs[2 * nw], refs[2 * nw + 1]
        for j in range(3):
            for i in range(nw):
                cp = make_copy(src_refs, land_refs, send_refs, recv_refs, j, i, False)
                cp.wait_send()
                cp.wait_recv()

    outs = pl.pallas_call(
        body, name=name,
        out_shape=(*[pltpu.HBM(a.shape, a.dtype) for a in srcs], *[pltpu.HBM(a.shape, a.dtype) for a in lands]),
        in_specs=[_HBM] * (2 * nw) + [_SEM, _SEM, _ANY], out_specs=tuple([_HBM] * (2 * nw)),
        input_output_aliases={i: i for i in range(2 * nw)},
        compiler_params=pltpu.CompilerParams(has_side_effects=_DATAFLOW),
    )(*srcs, *lands, send_sems, recv_sems, after)
    return list(outs[:nw]), list(outs[nw:])


def _gather_copy(shapes):
    def make(src_refs, land_refs, send_sems, recv_sems, j, i, outgoing):
        x, y, cc = _mesh_pos()
        px, py = _other_chips(x, y)[j]
        rows = _half(shapes[i][0], cc)
        slot = 2 * x + y if outgoing else 2 * px + py
        return pltpu.make_async_remote_copy(
            src_ref=src_refs[i].at[rows], dst_ref=land_refs[i].at[slot, rows],
            send_sem=send_sems.at[j * len(shapes) + i], recv_sem=recv_sems.at[j * len(shapes) + i],
            device_id=(px, py, cc), device_id_type=MESH)
    return make


def _gather_forward(lands, *, name):
    nw = len(lands)

    def body(*refs):
        in_refs, out_refs = refs[:nw], refs[nw:2 * nw]
        send_sems, recv_sems = refs[2 * nw:]
        x, y, cc = _mesh_pos()
        cps = []
        for j, (px, py) in enumerate(_other_chips(x, y)):
            for i in range(nw):
                rows = _half(lands[i].shape[1], cc)
                cps.append(pltpu.make_async_remote_copy(
                    src_ref=in_refs[i].at[2 * px + py, rows], dst_ref=out_refs[i].at[2 * px + py, rows],
                    send_sem=send_sems.at[j, i], recv_sem=recv_sems.at[j, i],
                    device_id=(x, y, 1 - cc), device_id_type=MESH))
        for cp in cps:
            cp.start()
        for cp in cps:
            cp.wait()

    return pl.pallas_call(
        body, out_shape=[SDS(a.shape, a.dtype) for a in lands], in_specs=[_HBM] * nw, out_specs=[_HBM] * nw,
        input_output_aliases={i: i for i in range(nw)},
        scratch_shapes=[pltpu.SemaphoreType.DMA((3, nw)), pltpu.SemaphoreType.DMA((3, nw))], name=name)(*lands)


def _grads_to_sibling(gs, *, name):
    nw = len(gs)

    def body(*refs):
        g_refs, out_refs = refs[:nw], refs[nw:2 * nw]
        send_sems, recv_sems = refs[2 * nw:]
        x, y, cc = _mesh_pos()
        cps = [pltpu.make_async_remote_copy(
            src_ref=g_refs[i].at[pl.ds(0, N_CHIPS), _half(gs[i].shape[1], 1 - cc)], dst_ref=out_refs[i],
            send_sem=send_sems.at[i], recv_sem=recv_sems.at[i], device_id=(x, y, 1 - cc), device_id_type=MESH)
            for i in range(nw)]
        for cp in cps:
            cp.start()
        for cp in cps:
            cp.wait()

    return pl.pallas_call(
        body, out_shape=[SDS((N_CHIPS, g.shape[1] // 2, g.shape[2]), g.dtype) for g in gs],
        in_specs=[_HBM] * nw, out_specs=[_HBM] * nw,
        scratch_shapes=[pltpu.SemaphoreType.DMA((nw,)), pltpu.SemaphoreType.DMA((nw,))], name=name)(*gs)


def _chips_copy(nw):
    def make(src_refs, land_refs, send_sems, recv_sems, j, i, outgoing):
        x, y, cc = _mesh_pos()
        px, py = _other_chips(x, y)[j]
        return pltpu.make_async_remote_copy(
            src_ref=src_refs[i].at[2 * px + py], dst_ref=land_refs[i].at[j],
            send_sem=send_sems.at[j * nw + i], recv_sem=recv_sems.at[j * nw + i],
            device_id=(px, py, cc), device_id_type=MESH)
    return make


def _share_halves(tots, *, name):
    nw = len(tots)

    def body(*refs):
        t_refs, out_refs = refs[:nw], refs[nw:2 * nw]
        send_sems, recv_sems = refs[2 * nw:]
        x, y, cc = _mesh_pos()
        cps = [pltpu.make_async_remote_copy(
            src_ref=t_refs[i], dst_ref=out_refs[i], send_sem=send_sems.at[i], recv_sem=recv_sems.at[i],
            device_id=(x, y, 1 - cc), device_id_type=MESH) for i in range(nw)]
        for cp in cps:
            cp.start()
        for cp in cps:
            cp.wait()

    return pl.pallas_call(
        body, out_shape=[SDS(tt.shape, tt.dtype) for tt in tots],
        in_specs=[_HBM] * nw, out_specs=[_HBM] * nw,
        scratch_shapes=[pltpu.SemaphoreType.DMA((nw,)), pltpu.SemaphoreType.DMA((nw,))], name=name)(*tots)


def _pack_small(arrays):
    flat = jnp.concatenate([a.astype(F32).reshape(-1) for a in arrays])
    n = flat.shape[0]
    padded = -(-n // (8 * LANES)) * (8 * LANES)
    return jnp.pad(flat, (0, padded - n)).reshape(padded // LANES, LANES)


def _unpack_small(flat, shapes):
    out, off = [], 0
    for shp in shapes:
        n = math.prod(shp)
        out.append(flat[off:off + n].reshape(shp))
        off += n
    return out


BIG = (("w_in", 1), ("w_conv_out", 1), ("w_gqa_out", 1), ("w_mla_uq", 1), ("w_mla_ukv", 1), ("w_mla_out", 1),
       ("w_out", 0), ("w_ffn_gate", 1), ("w_ffn_up", 1), ("w_ffn_down", 0))
BIG_NAMES = tuple(nm for nm, _ in BIG)
BIG_GROUPS = (("w_in",), tuple(nm for nm in BIG_NAMES if nm != "w_in"))


def _cols_to_slots(full):
    k, n = full.shape
    return full.reshape(k, N_CHIPS, n // N_CHIPS).transpose(1, 0, 2)


def _slots_to_cols(slots):
    s, k, n = slots.shape
    return slots.transpose(1, 0, 2).reshape(k, s * n)


def _step(p, mom_m, mom_v, x, loss_target):
    b, seq, d = x.shape
    depth = p["w_in"].shape[0]
    l_valid = N_META + seq
    lp = -(-l_valid // LANES) * LANES
    t = b * lp
    c_conv = d // 2
    q_rank = p["mla_q_norm_g"].shape[1]
    kv_rank = p["mla_kv_norm_g"].shape[1]
    qw, kw = GQA_HEADS * HEAD, GQA_KV_HEADS * HEAD
    off_q = 2 * c_conv
    off_k = off_q + qw
    off_v = off_k + kw
    off_cq = off_v + kw
    off_ckv = off_cq + q_rank
    off_gate = off_ckv + kv_rank
    off_kpe = off_gate + N_BRANCH * d
    d_inp = off_kpe + LANES
    assert off_cq % q_rank == 0 and off_ckv % kv_rank == 0
    xi, yi, ci = _mesh_pos()
    chip = 2 * xi + yi

    small_sh = [p["meta_tokens"], p["conv_dw"]]
    small_all = _chip_allgather_small(_pack_small(small_sh), name="gather_small")
    got = [_unpack_small(small_all[s].reshape(-1), [a.shape for a in small_sh]) for s in range(N_CHIPS)]
    meta_full = jnp.concatenate([g[0] for g in got], axis=1)
    conv_dw_full = jnp.concatenate([g[1] for g in got], axis=-1).reshape(depth, CONV_K, c_conv)

    gather_copies = [_gather_copy([p[nm].shape[1:] for nm in grp]) for grp in BIG_GROUPS]
    gathers = []
    order = small_all
    for li in range(depth):
        gathers.append([])
        for gi, grp in enumerate(BIG_GROUPS):
            own = [p[nm][li].astype(BF16) for nm in grp]
            lands = [lax.empty((N_CHIPS,) + o.shape, BF16) for o in own]
            gathers[li].append(_split_start(own, lands, order, gather_copies[gi],
                                            name="gather_start_%d_%d" % (li, gi)))
            order = gathers[li][gi][4]
    meta_full = meta_full + order[0, 0]

    def gathered(li, gi, after):
        own, lands = _split_wait(gathers[li][gi], after, gather_copies[gi], name="gather_wait_%d_%d" % (li, gi))
        got = _gather_forward(lands, name="gather_forward_%d" % gi)
        return {nm: lax.dynamic_update_slice(g, o[None], (chip, 0, 0)) for nm, g, o in zip(BIG_GROUPS[gi], got, own)}

    def in_proj_weight(slots):
        w_in = _slots_to_cols(slots)
        return jnp.concatenate(
            [w_in[:, :off_gate], w_in[:, off_gate + MLA_ROPE:], w_in[:, off_gate:off_gate + MLA_ROPE],
             jnp.zeros((d, LANES - MLA_ROPE), BF16)], axis=1)

    def other_weights(full):
        full["w_out"] = full["w_out"].reshape(-1, d)
        full["w_ffn_down"] = full["w_ffn_down"].reshape(-1, d)
        uq = _slots_to_cols(full["w_mla_uq"]).reshape(q_rank, MLA_HEADS, HEAD + MLA_ROPE)
        full["w_uq"] = jnp.pad(uq, ((0, 0), (0, 0), (0, MLA_SLOT - HEAD - MLA_ROPE))).reshape(
            q_rank, MLA_HEADS * MLA_SLOT)
        ukv = _slots_to_cols(full["w_mla_ukv"]).reshape(kv_rank, MLA_HEADS, 2 * HEAD)
        full["w_uk"] = jnp.pad(ukv[:, :, :HEAD], ((0, 0), (0, 0), (0, MLA_SLOT - HEAD))).reshape(
            kv_rank, MLA_HEADS * MLA_SLOT)
        full["w_uv"] = ukv[:, :, HEAD:].reshape(kv_rank, MLA_HEADS * HEAD)
        return full

    cos_g, sin_g = _rope_tables(lp, seq, HEAD, HEAD)
    cos_m, sin_m = _rope_tables(lp, seq, MLA_ROPE, LANES)
    gqa_scale = 1.0 / math.sqrt(HEAD)
    mla_scale = 1.0 / math.sqrt(HEAD + MLA_ROPE)
    attn_kw = dict(batch=b, lp=lp, l_valid=l_valid)
    gqa_kw = dict(heads=GQA_HEADS, kv_heads=GQA_KV_HEADS, dk=HEAD, dv=HEAD, scale=gqa_scale, **attn_kw)
    mla_kw = dict(heads=MLA_HEADS, kv_heads=MLA_HEADS, dk=MLA_SLOT, dv=HEAD, scale=mla_scale, **attn_kw)

    h = jnp.concatenate([jnp.broadcast_to(meta_full[None], (b, N_META, d)), x,
                         jnp.zeros((b, lp - l_valid, d), F32)], axis=1).reshape(t, d)
    saved, weights = [], []
    for li in range(depth):
        w = {"w_in": in_proj_weight(gathered(li, 0, h)["w_in"])}
        weights.append(w)
        row = lambda nm: p[nm][li][None, :]
        s = {"h_in": h}
        s["u"] = _rmsnorm_fwd(h, row("mix_norm_g"), width=d, cblk=0, name="mix_norm")
        proj = _mm(s["u"], w["w_in"], name="in_proj")
        s["proj"] = proj
        proj3 = proj.reshape(b, lp, d_inp)
        s["yconv"] = _conv_fwd(proj3, conv_dw_full[li], row("conv_b"), c_conv=c_conv, l_valid=l_valid,
                               name="conv").reshape(t, c_conv)
        s["sconv"] = _ln_silu_fwd(s["yconv"], row("conv_ln_g"), row("conv_ln_b"), name="conv_ln_silu")
        w.update(other_weights(gathered(li, 1, s["sconv"])))
        s["ya"] = _mm(s["sconv"], w["w_conv_out"], b_sharded=True, name="conv_out")
        s["gq"], s["gk"], s["gv"] = _gqa_prep_fwd(
            proj, row("gqa_q_norm_g"), row("gqa_k_norm_g"), cos_g, sin_g,
            q_off=off_q, k_off=off_k, v_off=off_v, lp=lp, name="gqa_prep")
        s["go"] = _attn_fwd(s["gq"], s["gk"], s["gv"], name="gqa_attn", **gqa_kw)
        s["yb"] = _mm(s["go"], w["w_gqa_out"], b_sharded=True, name="gqa_out")
        s["cqn"] = _rmsnorm_fwd(proj, row("mla_q_norm_g"), width=q_rank, cblk=off_cq // q_rank, name="mla_q_norm")
        s["kvn"] = _rmsnorm_fwd(proj, row("mla_kv_norm_g"), width=kv_rank, cblk=off_ckv // kv_rank,
                                name="mla_kv_norm")
        q_pre = _mm(s["cqn"], w["w_uq"], name="mla_uq")
        k_pre = _mm(s["kvn"], w["w_uk"], name="mla_uk")
        s["mv"] = _mm(s["kvn"], w["w_uv"], out_dtype=BF16, name="mla_uv")
        s["mq"], s["mk"] = _mla_rope_fwd(q_pre, k_pre, proj, cos_m, sin_m, kpe_off=off_kpe, lp=lp, name="mla_rope")
        s["mo"] = _attn_fwd(s["mq"], s["mk"], s["mv"], name="mla_attn", **mla_kw)
        s["yc"] = _mm(s["mo"], w["w_mla_out"], b_sharded=True, name="mla_out")
        s["merged"] = _merge_fwd(proj, row("gate_b"), s["ya"], s["yb"], s["yc"], gate_off=off_gate, name="merge")
        h = _mm(s["merged"], w["w_out"], add=h, name="mix_out")
        s["h_mid"] = h
        s["v"] = _rmsnorm_fwd(h, row("ffn_norm_g"), width=d, cblk=0, name="ffn_norm")
        s["gate"] = _mm(s["v"], w["w_ffn_gate"], b_sharded=True, out_dtype=BF16, name="ffn_gate")
        s["up"] = _mm(s["v"], w["w_ffn_up"], b_sharded=True, out_dtype=BF16, name="ffn_up")
        s["act"] = _swiglu_fwd(s["gate"], s["up"], name="swiglu")
        h = _mm(s["act"], w["w_ffn_down"], add=h, name="ffn_down")
        saved.append(s)

    target = jnp.pad(loss_target, ((0, 0), (N_META, lp - l_valid), (0, 0))).reshape(t, d)
    loss_part, dh, dhb, g_final = _loss_head(h, p["final_norm_g"][None, :], target, lp=lp, l_valid=l_valid,
                                             name="loss_head")
    loss = lax.psum(loss_part[0, 0], ("x", "y", "c"))

    small_names = ["mix_norm_g", "conv_dw", "conv_b", "conv_ln_g", "conv_ln_b", "gqa_q_norm_g", "gqa_k_norm_g",
                   "mla_q_norm_g", "mla_kv_norm_g", "gate_b", "ffn_norm_g"]
    small_g = {nm: [None] * depth for nm in small_names}
    adam = {}
    chips_copies = [_chips_copy(len(grp)) for grp in BIG_GROUPS]

    def start_reduce(lj, gi, big):
        grp = BIG_GROUPS[gi]
        glist = [big[nm] for nm in grp]
        from_sib = _grads_to_sibling(glist, name="grads_to_sibling_%d" % gi)
        parts = []
        for nm, g, fs in zip(grp, glist, from_sib):
            hk = g.shape[1] // 2
            mine = lax.dynamic_slice_in_dim(g, ci * hk, hk, axis=1)
            rows = (1, N_CHIPS * hk, g.shape[2])
            parts.append(_sum_slots([mine.reshape(rows), fs.reshape(rows)], out_dtype=BF16,
                                    name="sum_sibling_" + nm).reshape(fs.shape))
        lands = [lax.empty((3,) + pt.shape[1:], BF16) for pt in parts]
        return lj, gi, _split_start(parts, lands, order, chips_copies[gi], name="chips_start_%d_%d" % (lj, gi))

    def finish_reduce(item, after):
        lj, gi, pending = item
        grp = BIG_GROUPS[gi]
        parts, from_chips = _split_wait(pending, after, chips_copies[gi], name="chips_wait_%d_%d" % (lj, gi))
        tots = [_sum_slots([lax.dynamic_index_in_dim(pt, chip, 0, keepdims=True), fc], out_dtype=F32,
                           name="sum_chips_" + nm) for nm, pt, fc in zip(grp, parts, from_chips)]
        others = _share_halves(tots, name="share_halves_%d" % gi)
        for nm, tot, oth in zip(grp, tots, others):
            g = jnp.concatenate([jnp.where(ci == 0, tot, oth), jnp.where(ci == 0, oth, tot)], axis=0)
            adam[nm] = _adamw_layer(p[nm], g, mom_m[nm], mom_v[nm], lj, adam.get(nm), name="adamw_" + nm)

    in_flight = []
    for li in reversed(range(depth)):
        w, s = weights[li], saved[li]
        row = lambda nm: p[nm][li][None, :]
        big = {}
        dact = _mm(dhb, w["w_ffn_down"], mode="nt", deps=[item[2][4] for item in in_flight], name="d_ffn_act")
        big["w_ffn_down"] = _mm(s["act"], dhb, mode="tn", out_dtype=BF16, deps=[item[2][4] for item in in_flight],
                                name="dw_ffn_down").reshape(N_CHIPS, -1, d)
        dgate, dup = _swiglu_bwd(s["gate"], s["up"], dact, name="d_swiglu")
        dv = _mm(dgate, w["w_ffn_gate"], mode="nt", b_sharded=True, name="d_ffn_v_gate")
        dv = _mm(dup, w["w_ffn_up"], mode="nt", b_sharded=True, add=dv, name="d_ffn_v_up")
        big["w_ffn_gate"] = _mm(s["v"], dgate, mode="tn", out_dtype=BF16, out_shards=N_CHIPS, name="dw_ffn_gate")
        big["w_ffn_up"] = _mm(s["v"], dup, mode="tn", out_dtype=BF16, out_shards=N_CHIPS, name="dw_ffn_up")
        dh, dhb, g = _rmsnorm_bwd(s["h_mid"], row("ffn_norm_g"), dv, width=d, cblk=0, res=dh, name="d_ffn_norm")
        small_g["ffn_norm_g"][li] = g[0]
        dm = _mm(dhb, w["w_out"], mode="nt", name="d_merged")
        big["w_out"] = _mm(s["merged"], dhb, mode="tn", out_dtype=BF16, name="dw_out").reshape(N_CHIPS, -1, d)
        dya, dyb, dyc, dl0, dl1, dl2, db0, db1, db2 = _merge_bwd(
            s["proj"], row("gate_b"), s["ya"], s["yb"], s["yc"], dm, gate_off=off_gate, name="d_merge")
        small_g["gate_b"][li] = jnp.concatenate([db0[0], db1[0], db2[0]])
        ds = _mm(dya, w["w_conv_out"], mode="nt", b_sharded=True, name="d_conv_s")
        big["w_conv_out"] = _mm(s["sconv"], dya, mode="tn", out_dtype=BF16, out_shards=N_CHIPS, name="dw_conv_out")
        dyconv, g, gb = _ln_silu_bwd(s["yconv"], row("conv_ln_g"), row("conv_ln_b"), ds, name="d_conv_ln_silu")
        small_g["conv_ln_g"][li], small_g["conv_ln_b"][li] = g[0], gb[0]
        proj3 = s["proj"].reshape(b, lp, d_inp)
        da, dgte, ddw, dcb = _conv_bwd(proj3, conv_dw_full[li], dyconv.reshape(b, lp, c_conv), c_conv=c_conv,
                                       l_valid=l_valid, name="d_conv")
        small_g["conv_dw"][li], small_g["conv_b"][li] = ddw, dcb[0]
        dgo = _mm(dyb, w["w_gqa_out"], mode="nt", b_sharded=True, out_dtype=BF16, name="d_gqa_o")
        big["w_gqa_out"] = _mm(s["go"], dyb, mode="tn", out_dtype=BF16, out_shards=N_CHIPS, name="dw_gqa_out")
        dgq, dgk, dgv = _attn_bwd(s["gq"], s["gk"], s["gv"], dgo, name="d_gqa_attn", **gqa_kw)
        dq_g, dk_g, gq, gk = _gqa_prep_bwd(s["proj"], row("gqa_q_norm_g"), row("gqa_k_norm_g"), cos_g, sin_g,
                                           dgq, dgk, q_off=off_q, k_off=off_k, lp=lp, name="d_gqa_prep")
        small_g["gqa_q_norm_g"][li], small_g["gqa_k_norm_g"][li] = gq[0], gk[0]
        dmo = _mm(dyc, w["w_mla_out"], mode="nt", b_sharded=True, out_dtype=BF16, name="d_mla_o")
        big["w_mla_out"] = _mm(s["mo"], dyc, mode="tn", out_dtype=BF16, out_shards=N_CHIPS, name="dw_mla_out")
        dmq, dmk, dmv = _attn_bwd(s["mq"], s["mk"], s["mv"], dmo, name="d_mla_attn", **mla_kw)
        dq_pre, dk_pre, dkpe = _mla_rope_bwd(dmq, dmk, cos_m, sin_m, lp=lp, name="d_mla_rope")
        dmvb = dmv.astype(BF16)
        g_uq = _mm(s["cqn"], dq_pre, mode="tn", out_dtype=BF16, name="dw_mla_uq")
        g_uk = _mm(s["kvn"], dk_pre, mode="tn", out_dtype=BF16, name="dw_mla_uk")
        g_uv = _mm(s["kvn"], dmvb, mode="tn", out_dtype=BF16, name="dw_mla_uv")
        big["w_mla_uq"] = _cols_to_slots(
            g_uq.reshape(q_rank, MLA_HEADS, MLA_SLOT)[:, :, :HEAD + MLA_ROPE].reshape(q_rank, -1))
        big["w_mla_ukv"] = _cols_to_slots(jnp.concatenate(
            [g_uk.reshape(kv_rank, MLA_HEADS, MLA_SLOT)[:, :, :HEAD], g_uv.reshape(kv_rank, MLA_HEADS, HEAD)],
            axis=-1).reshape(kv_rank, -1))
        dcqn = _mm(dq_pre, w["w_uq"], mode="nt", name="d_mla_cqn")
        dkvn = _mm(dk_pre, w["w_uk"], mode="nt", name="d_mla_kvn_k")
        dkvn = _mm(dmvb, w["w_uv"], mode="nt", add=dkvn, name="d_mla_kvn_v")
        _, dcq, g = _rmsnorm_bwd(s["proj"], row("mla_q_norm_g"), dcqn, width=q_rank, cblk=off_cq // q_rank,
                                 name="d_mla_q_norm")
        small_g["mla_q_norm_g"][li] = g[0]
        _, dckv, g = _rmsnorm_bwd(s["proj"], row("mla_kv_norm_g"), dkvn, width=kv_rank, cblk=off_ckv // kv_rank,
                                  name="d_mla_kv_norm")
        small_g["mla_kv_norm_g"][li] = g[0]
        for item in in_flight:
            finish_reduce(item, dckv)
        in_flight = [start_reduce(li, 1, big)]
        dproj = jnp.concatenate(
            [da.reshape(t, c_conv), dgte.reshape(t, c_conv), dq_g, dk_g, dgv.astype(BF16), dcq, dckv,
             dl0, dl1, dl2, dkpe], axis=1)
        du = _mm(dproj, w["w_in"], mode="nt", deps=[in_flight[0][2][4]], name="d_u")
        g_in = _mm(s["u"], dproj, mode="tn", out_dtype=BF16, deps=[in_flight[0][2][4]], name="dw_in")
        big["w_in"] = _cols_to_slots(jnp.concatenate(
            [g_in[:, :off_gate], g_in[:, off_kpe:off_kpe + MLA_ROPE], g_in[:, off_gate:off_kpe]], axis=1))
        dh, dhb, g = _rmsnorm_bwd(s["h_in"], row("mix_norm_g"), du, width=d, cblk=0, res=dh, name="d_mix_norm")
        small_g["mix_norm_g"][li] = g[0]
        in_flight.append(start_reduce(li, 0, big))

    dh3 = dh.reshape(b, lp, d)
    grad_x = dh3[:, N_META:l_valid]
    g_meta = _meta_grad(dh3, name="meta_grad")
    for item in in_flight:
        finish_reduce(item, g_meta)

    small_list = [g_meta, g_final[0]] + [jnp.stack(small_g[nm]) for nm in small_names]
    red = _allreduce_small(_pack_small(small_list), name="allreduce_small")
    red = _unpack_small(red.reshape(-1), [a.shape for a in small_list])
    grads = {"meta_tokens": lax.dynamic_slice_in_dim(red[0], chip * (d // N_CHIPS), d // N_CHIPS, axis=1),
             "final_norm_g": red[1]}
    for nm, val in zip(small_names, red[2:]):
        grads[nm] = val
    cs = c_conv // N_CHIPS
    grads["conv_dw"] = lax.dynamic_slice_in_dim(grads["conv_dw"], chip * cs, cs, axis=2).reshape(p["conv_dw"].shape)

    delta, new_m, new_v = {}, {}, {}
    for nm in p:
        if nm in adam:
            grads[nm], delta[nm], new_m[nm], new_v[nm] = (o.reshape(p[nm].shape) for o in adam[nm])
        else:
            delta[nm], new_m[nm], new_v[nm] = _adamw(p[nm], grads[nm], mom_m[nm], mom_v[nm], name="adamw_" + nm)
    return loss, grad_x, grads, delta, new_m, new_v


WEIGHT_NAMES = ("meta_tokens", "mix_norm_g", "w_in", "conv_dw", "conv_b", "conv_ln_g", "conv_ln_b", "w_conv_out",
                "gqa_q_norm_g", "gqa_k_norm_g", "w_gqa_out", "mla_q_norm_g", "w_mla_uq", "mla_kv_norm_g",
                "w_mla_ukv", "w_mla_out", "gate_b", "w_out", "ffn_norm_g", "w_ffn_gate", "w_ffn_up", "w_ffn_down",
                "final_norm_g")


def kernel(x, meta_tokens, mix_norm_g, w_in, conv_dw, conv_b, conv_ln_g, conv_ln_b, w_conv_out, gqa_q_norm_g, gqa_k_norm_g, w_gqa_out, mla_q_norm_g, w_mla_uq, mla_kv_norm_g, w_mla_ukv, w_mla_out, gate_b, w_out, ffn_norm_g, w_ffn_gate, w_ffn_up, w_ffn_down, final_norm_g, loss_target, m_meta_tokens, m_mix_norm_g, m_w_in, m_conv_dw, m_conv_b, m_conv_ln_g, m_conv_ln_b, m_w_conv_out, m_gqa_q_norm_g, m_gqa_k_norm_g, m_w_gqa_out, m_mla_q_norm_g, m_w_mla_uq, m_mla_kv_norm_g, m_w_mla_ukv, m_w_mla_out, m_gate_b, m_w_out, m_ffn_norm_g, m_w_ffn_gate, m_w_ffn_up, m_w_ffn_down, m_final_norm_g, v_meta_tokens, v_mix_norm_g, v_w_in, v_conv_dw, v_conv_b, v_conv_ln_g, v_conv_ln_b, v_w_conv_out, v_gqa_q_norm_g, v_gqa_k_norm_g, v_w_gqa_out, v_mla_q_norm_g, v_w_mla_uq, v_mla_kv_norm_g, v_w_mla_ukv, v_w_mla_out, v_gate_b, v_w_out, v_ffn_norm_g, v_w_ffn_gate, v_w_ffn_up, v_w_ffn_down, v_final_norm_g):
    ws = (meta_tokens, mix_norm_g, w_in, conv_dw, conv_b, conv_ln_g, conv_ln_b, w_conv_out, gqa_q_norm_g,
          gqa_k_norm_g, w_gqa_out, mla_q_norm_g, w_mla_uq, mla_kv_norm_g, w_mla_ukv, w_mla_out, gate_b, w_out,
          ffn_norm_g, w_ffn_gate, w_ffn_up, w_ffn_down, final_norm_g)
    ms = (m_meta_tokens, m_mix_norm_g, m_w_in, m_conv_dw, m_conv_b, m_conv_ln_g, m_conv_ln_b, m_w_conv_out,
          m_gqa_q_norm_g, m_gqa_k_norm_g, m_w_gqa_out, m_mla_q_norm_g, m_w_mla_uq, m_mla_kv_norm_g, m_w_mla_ukv,
          m_w_mla_out, m_gate_b, m_w_out, m_ffn_norm_g, m_w_ffn_gate, m_w_ffn_up, m_w_ffn_down, m_final_norm_g)
    vs = (v_meta_tokens, v_mix_norm_g, v_w_in, v_conv_dw, v_conv_b, v_conv_ln_g, v_conv_ln_b, v_w_conv_out,
          v_gqa_q_norm_g, v_gqa_k_norm_g, v_w_gqa_out, v_mla_q_norm_g, v_w_mla_uq, v_mla_kv_norm_g, v_w_mla_ukv,
          v_w_mla_out, v_gate_b, v_w_out, v_ffn_norm_g, v_w_ffn_gate, v_w_ffn_up, v_w_ffn_down, v_final_norm_g)
    p = dict(zip(WEIGHT_NAMES, ws))
    loss, grad_x, grads, delta, new_m, new_v = _step(p, dict(zip(WEIGHT_NAMES, ms)), dict(zip(WEIGHT_NAMES, vs)),
                                                     x, loss_target)
    return (loss, grad_x, *[grads[n] for n in WEIGHT_NAMES], *[delta[n] for n in WEIGHT_NAMES],
            *[new_m[n] for n in WEIGHT_NAMES], *[new_v[n] for n in WEIGHT_NAMES])
```

```python
import math

import jax
import jax.numpy as jnp
from jax import lax
from jax.experimental import pallas as pl
from jax.experimental.pallas import tpu as pltpu

F32 = jnp.float32
BF16 = jnp.bfloat16
SDS = jax.ShapeDtypeStruct
MESH = pl.DeviceIdType.MESH

N_META = 16
GRID_W = 64
ROPE_THETA = 10000.0
NORM_EPS = 1e-6
CONV_K = 31
CONV_HALO = 16
HEAD = 128
GQA_HEADS = 8
GQA_KV_HEADS = 2
MLA_HEADS = 8
MLA_ROPE = 64
MLA_SLOT = 256
N_BRANCH = 3

ADAM_LR = 0.001
ADAM_B1 = 0.9
ADAM_B2 = 0.999
ADAM_EPS = 1e-08
ADAM_WD = 0.01
ADAM_STEP = 10

LANES = 128
SUBLANES_BF16 = 16
VMEM_LIMIT_BYTES = 56 * 2 ** 20
ROW_BLOCK_BYTES = 2 << 20
ATTN_Q_ROWS = 1088
MM_BLOCK_BYTES = 48 * 2 ** 20
N_CHIPS = 4
N_DEV = 8


def _cparams():
    return pltpu.CompilerParams(vmem_limit_bytes=VMEM_LIMIT_BYTES)


def _tile(n, cap, mult):
    best = None
    for d in range(mult, min(n, cap) + 1, mult):
        if n % d == 0:
            best = d
    return n if best is None else best


def _row_tile(rows, width, mult=SUBLANES_BF16):
    return _tile(rows, max(mult, ROW_BLOCK_BYTES // (4 * width)), mult)


def _sigmoid(x):
    return 1.0 / (1.0 + jnp.exp(-x))


def _mm(a, b, *, mode="nn", add=None, out_dtype=F32, b_sharded=False, out_shards=None, deps=(), name):
    if b_sharded:
        s_b, r_b, c_b = b.shape
        b_rows, b_cols = r_b, s_b * c_b
    else:
        b_rows, b_cols = b.shape
    if mode == "nn":
        (m, k), n = a.shape, b_cols
        assert b_rows == k
    elif mode == "nt":
        (m, k), n = a.shape, b_rows
        assert b_cols == k
    else:
        (k, m), n = a.shape, b_cols
        assert b_rows == k
    n_unit = n
    if b_sharded and mode != "nt":
        n_unit = c_b
    if out_shards is not None:
        assert n % out_shards == 0
        n_unit = math.gcd(n_unit, n // out_shards)
    k_unit = c_b if (b_sharded and mode == "nt") else k
    if mode == "tn":
        tm = _tile(m, 1024, LANES)
        tk = _tile(k_unit, 2176, SUBLANES_BF16)
    else:
        tm = _tile(m, 1088, SUBLANES_BF16)
        tk = _tile(k_unit, 2176, LANES)
    nk = k // tk
    out_bytes = jnp.dtype(out_dtype).itemsize
    tn = LANES
    for cap in (2176, 1408, 1024, 512, 256):
        tn = _tile(n_unit, cap, LANES)
        blocks = tm * tk * a.dtype.itemsize + tk * tn * b.dtype.itemsize + tm * tn * out_bytes
        blocks += tm * tn * 4 if add is not None else 0
        scratch = tm * tn * 4 if (nk > 1 and out_dtype != F32) else 0
        if 2 * blocks + scratch <= MM_BLOCK_BYTES:
            break
    if mode == "nn":
        a_spec = pl.BlockSpec((tm, tk), lambda i, j, kk: (i, kk))
        dn = (((1,), (0,)), ((), ()))
        if b_sharded:
            per = c_b // tn
            b_spec = pl.BlockSpec((None, tk, tn), lambda i, j, kk: (j // per, kk, j % per))
        else:
            b_spec = pl.BlockSpec((tk, tn), lambda i, j, kk: (kk, j))
    elif mode == "nt":
        a_spec = pl.BlockSpec((tm, tk), lambda i, j, kk: (i, kk))
        dn = (((1,), (1,)), ((), ()))
        if b_sharded:
            per = c_b // tk
            b_spec = pl.BlockSpec((None, tn, tk), lambda i, j, kk: (kk // per, j, kk % per))
        else:
            b_spec = pl.BlockSpec((tn, tk), lambda i, j, kk: (j, kk))
    else:
        a_spec = pl.BlockSpec((tk, tm), lambda i, j, kk: (kk, i))
        dn = (((0,), (0,)), ((), ()))
        if b_sharded:
            per = c_b // tn
            b_spec = pl.BlockSpec((None, tk, tn), lambda i, j, kk: (j // per, kk, j % per))
        else:
            b_spec = pl.BlockSpec((tk, tn), lambda i, j, kk: (kk, j))
    if out_shards is None:
        o_spec = pl.BlockSpec((tm, tn), lambda i, j, kk: (i, j))
        out_shape = SDS((m, n), out_dtype)
    else:
        per_o = (n // out_shards) // tn
        o_spec = pl.BlockSpec((None, tm, tn), lambda i, j, kk: (j // per_o, i, j % per_o))
        out_shape = SDS((out_shards, m, n // out_shards), out_dtype)
    has_add = add is not None
    assert not (has_add and out_shards is not None)
    deps = list(deps)
    use_acc = nk > 1 and out_dtype != F32

    def body(a_ref, b_ref, *rest):
        add_ref = rest[0] if has_add else None
        o_ref = rest[int(has_add) + len(deps)]
        kk = pl.program_id(2)
        part = lax.dot_general(a_ref[...].astype(BF16), b_ref[...].astype(BF16), dn,
                               preferred_element_type=F32)
        if nk == 1:
            o_ref[...] = (part + add_ref[...] if has_add else part).astype(o_ref.dtype)
            return
        acc = rest[-1] if use_acc else o_ref

        @pl.when(kk == 0)
        def _():
            acc[...] = part

        @pl.when((kk > 0) & (kk < nk - 1))
        def _():
            acc[...] += part

        @pl.when(kk == nk - 1)
        def _():
            r = acc[...] + part
            if has_add:
                r = r + add_ref[...]
            o_ref[...] = r.astype(o_ref.dtype)

    in_specs = [a_spec, b_spec] + ([o_spec] if has_add else []) + [_ANY] * len(deps)
    args = (a, b) + ((add,) if has_add else ()) + tuple(deps)
    return pl.pallas_call(
        body, grid=(m // tm, n // tn, nk), in_specs=in_specs, out_specs=o_spec,
        out_shape=out_shape, scratch_shapes=[pltpu.VMEM((tm, tn), F32)] if use_acc else [],
        compiler_params=_cparams(), name=name)(*args)


def _rmsnorm_fwd(x, g, *, width, cblk, name):
    t = x.shape[0]
    tr = _row_tile(t, width)

    def body(x_ref, g_ref, o_ref):
        xv = x_ref[...]
        r = lax.rsqrt(jnp.mean(xv * xv, axis=-1, keepdims=True) + NORM_EPS)
        o_ref[...] = (xv * r * g_ref[...]).astype(o_ref.dtype)

    return pl.pallas_call(
        body, grid=(t // tr,),
        in_specs=[pl.BlockSpec((tr, width), lambda i: (i, cblk)), pl.BlockSpec((1, width), lambda i: (0, 0))],
        out_specs=pl.BlockSpec((tr, width), lambda i: (i, 0)),
        out_shape=SDS((t, width), BF16), compiler_params=_cparams(), name=name)(x, g)


def _rmsnorm_bwd(x, g, dy, *, width, cblk, res=None, name):
    t = x.shape[0]
    tr = _row_tile(t, width)
    has_res = res is not None

    def body(x_ref, g_ref, dy_ref, *rest):
        if has_res:
            res_ref, dx_ref, dxb_ref, dg_ref = rest
        else:
            dx_ref, dxb_ref, dg_ref = rest
        xv = x_ref[...]
        dyv = dy_ref[...].astype(F32)
        r = lax.rsqrt(jnp.mean(xv * xv, axis=-1, keepdims=True) + NORM_EPS)
        gy = dyv * g_ref[...]
        dx = r * gy - xv * (r * r * r) * jnp.mean(gy * xv, axis=-1, keepdims=True)
        if has_res:
            dx = dx + res_ref[...]
        dx_ref[...] = dx
        dxb_ref[...] = dx.astype(BF16)
        part = jnp.sum(dyv * xv * r, axis=0, keepdims=True)

        @pl.when(pl.program_id(0) == 0)
        def _():
            dg_ref[...] = part

        @pl.when(pl.program_id(0) > 0)
        def _():
            dg_ref[...] += part

    row = pl.BlockSpec((tr, width), lambda i: (i, 0))
    vec = pl.BlockSpec((1, width), lambda i: (0, 0))
    in_specs = [pl.BlockSpec((tr, width), lambda i: (i, cblk)), vec, row] + ([row] if has_res else [])
    args = (x, g, dy) + ((res,) if has_res else ())
    return pl.pallas_call(
        body, grid=(t // tr,), in_specs=in_specs, out_specs=[row, row, vec],
        out_shape=[SDS((t, width), F32), SDS((t, width), BF16), SDS((1, width), F32)],
        compiler_params=_cparams(), name=name)(*args)


def _valid_rows(tile_index, tr, lp, lo, hi):
    pos = (tile_index % (lp // tr)) * tr + lax.broadcasted_iota(jnp.int32, (tr, 1), 0)
    return (pos >= lo) & (pos < hi)


def _loss_head(h, g, target, *, lp, l_valid, name):
    t, d = h.shape
    tr = _tile(lp, max(SUBLANES_BF16, ROW_BLOCK_BYTES // (4 * d)), SUBLANES_BF16)

    def body(x_ref, g_ref, t_ref, loss_ref, dx_ref, dxb_ref, dg_ref):
        i = pl.program_id(0)
        xv = x_ref[...]
        r = lax.rsqrt(jnp.mean(xv * xv, axis=-1, keepdims=True) + NORM_EPS)
        y = xv * r * g_ref[...]
        valid = _valid_rows(i, tr, lp, N_META, l_valid)
        err = jnp.where(valid, y - t_ref[...], 0.0)
        lpart = 0.5 * jnp.sum(jnp.mean(err * err, axis=-1, keepdims=True), axis=0, keepdims=True)
        dyv = err * (1.0 / d)
        gy = dyv * g_ref[...]
        dx = r * gy - xv * (r * r * r) * jnp.mean(gy * xv, axis=-1, keepdims=True)
        dx_ref[...] = dx
        dxb_ref[...] = dx.astype(BF16)
        gpart = jnp.sum(dyv * xv * r, axis=0, keepdims=True)

        @pl.when(i == 0)
        def _():
            dg_ref[...] = gpart
            loss_ref[...] = jnp.broadcast_to(lpart, loss_ref.shape)

        @pl.when(i > 0)
        def _():
            dg_ref[...] += gpart
            loss_ref[...] += jnp.broadcast_to(lpart, loss_ref.shape)

    row = pl.BlockSpec((tr, d), lambda i: (i, 0))
    vec = pl.BlockSpec((1, d), lambda i: (0, 0))
    return pl.pallas_call(
        body, grid=(t // tr,), in_specs=[row, vec, row],
        out_specs=[pl.BlockSpec((1, LANES), lambda i: (0, 0)), row, row, vec],
        out_shape=[SDS((1, LANES), F32), SDS((t, d), F32), SDS((t, d), BF16), SDS((1, d), F32)],
        compiler_params=_cparams(), name=name)(h, g, target)


def _ln_silu_fwd(yc, g, b, *, name):
    t, c = yc.shape
    tr = _row_tile(t, c)

    def body(x_ref, g_ref, b_ref, o_ref):
        xv = x_ref[...]
        xc = xv - jnp.mean(xv, axis=-1, keepdims=True)
        r = lax.rsqrt(jnp.mean(xc * xc, axis=-1, keepdims=True) + NORM_EPS)
        ln = xc * r * g_ref[...] + b_ref[...]
        o_ref[...] = (ln * _sigmoid(ln)).astype(o_ref.dtype)

    row = pl.BlockSpec((tr, c), lambda i: (i, 0))
    vec = pl.BlockSpec((1, c), lambda i: (0, 0))
    return pl.pallas_call(body, grid=(t // tr,), in_specs=[row, vec, vec], out_specs=row,
                          out_shape=SDS((t, c), BF16), compiler_params=_cparams(), name=name)(yc, g, b)


def _ln_silu_bwd(yc, g, b, ds, *, name):
    t, c = yc.shape
    tr = _row_tile(t, c)

    def body(x_ref, g_ref, b_ref, ds_ref, dx_ref, dg_ref, db_ref):
        xv = x_ref[...]
        xc = xv - jnp.mean(xv, axis=-1, keepdims=True)
        r = lax.rsqrt(jnp.mean(xc * xc, axis=-1, keepdims=True) + NORM_EPS)
        xh = xc * r
        ln = xh * g_ref[...] + b_ref[...]
        sg = _sigmoid(ln)
        dln = ds_ref[...].astype(F32) * (sg * (1.0 + ln * (1.0 - sg)))
        gy = dln * g_ref[...]
        dx_ref[...] = r * (gy - jnp.mean(gy, axis=-1, keepdims=True)
                           - xh * jnp.mean(gy * xh, axis=-1, keepdims=True))
        gpart = jnp.sum(dln * xh, axis=0, keepdims=True)
        bpart = jnp.sum(dln, axis=0, keepdims=True)

        @pl.when(pl.program_id(0) == 0)
        def _():
            dg_ref[...] = gpart
            db_ref[...] = bpart

        @pl.when(pl.program_id(0) > 0)
        def _():
            dg_ref[...] += gpart
            db_ref[...] += bpart

    row = pl.BlockSpec((tr, c), lambda i: (i, 0))
    vec = pl.BlockSpec((1, c), lambda i: (0, 0))
    return pl.pallas_call(
        body, grid=(t // tr,), in_specs=[row, vec, vec, row], out_specs=[row, vec, vec],
        out_shape=[SDS((t, c), F32), SDS((1, c), F32), SDS((1, c), F32)],
        compiler_params=_cparams(), name=name)(yc, g, b, ds)


def _swiglu_fwd(gate, up, *, name):
    t, f = gate.shape
    tr = _row_tile(t, f)

    def body(g_ref, u_ref, o_ref):
        gv = g_ref[...].astype(F32)
        o_ref[...] = (gv * _sigmoid(gv) * u_ref[...].astype(F32)).astype(o_ref.dtype)

    row = pl.BlockSpec((tr, f), lambda i: (i, 0))
    return pl.pallas_call(body, grid=(t // tr,), in_specs=[row, row], out_specs=row,
                          out_shape=SDS((t, f), BF16), compiler_params=_cparams(), name=name)(gate, up)


def _swiglu_bwd(gate, up, dact, *, name):
    t, f = gate.shape
    tr = _row_tile(t, f)

    def body(g_ref, u_ref, d_ref, dg_ref, du_ref):
        gv = g_ref[...].astype(F32)
        sg = _sigmoid(gv)
        dv = d_ref[...]
        dg_ref[...] = (dv * u_ref[...].astype(F32) * (sg * (1.0 + gv * (1.0 - sg)))).astype(dg_ref.dtype)
        du_ref[...] = (dv * (gv * sg)).astype(du_ref.dtype)

    row = pl.BlockSpec((tr, f), lambda i: (i, 0))
    return pl.pallas_call(body, grid=(t // tr,), in_specs=[row, row, row], out_specs=[row, row],
                          out_shape=[SDS((t, f), BF16), SDS((t, f), BF16)],
                          compiler_params=_cparams(), name=name)(gate, up, dact)


def _merge_tiles(t, d):
    tc = _tile(d, 512, LANES)
    return _row_tile(t, tc), tc


def _merge_fwd(proj, gate_b, ya, yb, yc, *, gate_off, name):
    t, d = ya.shape
    tr, tc = _merge_tiles(t, d)
    nc = d // tc
    assert gate_off % tc == 0
    g0 = gate_off // tc

    def body(l0, l1, l2, b0, b1, b2, ya_ref, yb_ref, yc_ref, o_ref):
        acc = _sigmoid(l0[...] + b0[...]) * ya_ref[...]
        acc += _sigmoid(l1[...] + b1[...]) * yb_ref[...]
        acc += _sigmoid(l2[...] + b2[...]) * yc_ref[...]
        o_ref[...] = acc.astype(o_ref.dtype)

    lspec = [pl.BlockSpec((tr, tc), lambda i, j, br=br: (i, g0 + br * nc + j)) for br in range(N_BRANCH)]
    bspec = [pl.BlockSpec((1, tc), lambda i, j, br=br: (0, br * nc + j)) for br in range(N_BRANCH)]
    yspec = pl.BlockSpec((tr, tc), lambda i, j: (i, j))
    return pl.pallas_call(
        body, grid=(t // tr, nc), in_specs=lspec + bspec + [yspec] * 3, out_specs=yspec,
        out_shape=SDS((t, d), BF16), compiler_params=_cparams(), name=name)(
            proj, proj, proj, gate_b, gate_b, gate_b, ya, yb, yc)


def _merge_bwd(proj, gate_b, ya, yb, yc, dm, *, gate_off, name):
    t, d = ya.shape
    tr, tc = _merge_tiles(t, d)
    nc = d // tc
    g0 = gate_off // tc

    def body(l0, l1, l2, b0, b1, b2, ya_ref, yb_ref, yc_ref, dm_ref,
             dya, dyb, dyc, dl0, dl1, dl2, db0, db1, db2):
        i = pl.program_id(1)
        dmv = dm_ref[...]
        for l_ref, b_ref, y_ref, dy_ref, dl_ref, db_ref in (
                (l0, b0, ya_ref, dya, dl0, db0), (l1, b1, yb_ref, dyb, dl1, db1), (l2, b2, yc_ref, dyc, dl2, db2)):
            gt = _sigmoid(l_ref[...] + b_ref[...])
            dy_ref[...] = (dmv * gt).astype(dy_ref.dtype)
            dl = dmv * y_ref[...] * gt * (1.0 - gt)
            dl_ref[...] = dl.astype(dl_ref.dtype)
            part = jnp.sum(dl, axis=0, keepdims=True)

            @pl.when(i == 0)
            def _(db_ref=db_ref, part=part):
                db_ref[...] = part

            @pl.when(i > 0)
            def _(db_ref=db_ref, part=part):
                db_ref[...] += part

    lspec = [pl.BlockSpec((tr, tc), lambda j, i, br=br: (i, g0 + br * nc + j)) for br in range(N_BRANCH)]
    bspec = [pl.BlockSpec((1, tc), lambda j, i, br=br: (0, br * nc + j)) for br in range(N_BRANCH)]
    yspec = pl.BlockSpec((tr, tc), lambda j, i: (i, j))
    vspec = pl.BlockSpec((1, tc), lambda j, i: (0, j))
    return pl.pallas_call(
        body, grid=(nc, t // tr), in_specs=lspec + bspec + [yspec] * 4,
        out_specs=[yspec] * 6 + [vspec] * 3,
        out_shape=[SDS((t, d), BF16)] * 6 + [SDS((1, d), F32)] * 3,
        compiler_params=_cparams(), name=name)(proj, proj, proj, gate_b, gate_b, gate_b, ya, yb, yc, dm)


def _conv_fwd(proj3, dw, cb, *, c_conv, l_valid, name):
    b, lp, _ = proj3.shape
    cw = _tile(c_conv, 256, LANES)
    nc = c_conv // cw

    def body(a_ref, gte_ref, dw_ref, cb_ref, o_ref, zs):
        rows = lax.broadcasted_iota(jnp.int32, (lp, 1), 0)
        z = jnp.where(rows < l_valid, a_ref[...] * _sigmoid(gte_ref[...]), 0.0)
        zs[pl.ds(0, CONV_HALO), :] = jnp.zeros((CONV_HALO, cw), F32)
        zs[pl.ds(CONV_HALO, lp), :] = z
        zs[pl.ds(CONV_HALO + lp, CONV_HALO), :] = jnp.zeros((CONV_HALO, cw), F32)
        ch = _tile(lp, 136, 8)
        for r0 in range(0, lp, ch):
            acc = jnp.broadcast_to(cb_ref[...], (ch, cw))
            for j in range(CONV_K):
                acc = acc + zs[pl.ds(r0 + j + 1, ch), :] * dw_ref[pl.ds(j, 1), :]
            o_ref[pl.ds(r0, ch), :] = acc

    seq = lambda off: pl.BlockSpec((None, lp, cw), lambda bi, ci: (bi, 0, off + ci))
    return pl.pallas_call(
        body, grid=(b, nc),
        in_specs=[seq(0), seq(nc), pl.BlockSpec((CONV_K, cw), lambda bi, ci: (0, ci)),
                  pl.BlockSpec((1, cw), lambda bi, ci: (0, ci))],
        out_specs=seq(0), out_shape=SDS((b, lp, c_conv), F32),
        scratch_shapes=[pltpu.VMEM((lp + 2 * CONV_HALO, cw), F32)],
        compiler_params=_cparams(), name=name)(proj3, proj3, dw, cb)


def _conv_bwd(proj3, dw, dyc3, *, c_conv, l_valid, name):
    b, lp, _ = proj3.shape
    cw = _tile(c_conv, 256, LANES)
    nc = c_conv // cw

    def body(a_ref, gte_ref, dw_ref, dy_ref, da_ref, dgte_ref, ddw_ref, dcb_ref, zs, dys):
        bi = pl.program_id(1)
        rows = lax.broadcasted_iota(jnp.int32, (lp, 1), 0)
        valid = rows < l_valid
        av = a_ref[...]
        sg = _sigmoid(gte_ref[...])
        dyv = dy_ref[...]
        zero = jnp.zeros((CONV_HALO, cw), F32)
        zs[pl.ds(0, CONV_HALO), :] = zero
        zs[pl.ds(CONV_HALO, lp), :] = jnp.where(valid, av * sg, 0.0)
        zs[pl.ds(CONV_HALO + lp, CONV_HALO), :] = zero
        dys[pl.ds(0, CONV_HALO), :] = zero
        dys[pl.ds(CONV_HALO, lp), :] = dyv
        dys[pl.ds(CONV_HALO + lp, CONV_HALO), :] = zero

        @pl.when(bi == 0)
        def _():
            ddw_ref[...] = jnp.zeros_like(ddw_ref)
            dcb_ref[...] = jnp.zeros_like(dcb_ref)

        dz = jnp.zeros((lp, cw), F32)
        for j in range(CONV_K):
            dz = dz + dys[pl.ds(CONV_K - j, lp), :] * dw_ref[pl.ds(j, 1), :]
            ddw_ref[pl.ds(j, 1), :] += jnp.sum(dyv * zs[pl.ds(j + 1, lp), :], axis=0, keepdims=True)
        dcb_ref[...] += jnp.sum(dyv, axis=0, keepdims=True)
        dz = jnp.where(valid, dz, 0.0)
        da_ref[...] = (dz * sg).astype(da_ref.dtype)
        dgte_ref[...] = (dz * av * sg * (1.0 - sg)).astype(dgte_ref.dtype)

    seq = lambda off: pl.BlockSpec((None, lp, cw), lambda ci, bi: (bi, 0, off + ci))
    return pl.pallas_call(
        body, grid=(nc, b),
        in_specs=[seq(0), seq(nc), pl.BlockSpec((CONV_K, cw), lambda ci, bi: (0, ci)), seq(0)],
        out_specs=[seq(0), seq(0), pl.BlockSpec((CONV_K, cw), lambda ci, bi: (0, ci)),
                   pl.BlockSpec((1, cw), lambda ci, bi: (0, ci))],
        out_shape=[SDS((b, lp, c_conv), BF16), SDS((b, lp, c_conv), BF16),
                   SDS((CONV_K, c_conv), F32), SDS((1, c_conv), F32)],
        scratch_shapes=[pltpu.VMEM((lp + 2 * CONV_HALO, cw), F32)] * 2,
        compiler_params=_cparams(), name=name)(proj3, proj3, dw, dyc3)


def _swap_halves(x, group):
    half = group // 2
    lane = lax.broadcasted_iota(jnp.int32, x.shape, 1)
    up = pltpu.roll(x, LANES - half, 1)
    down = pltpu.roll(x, half, 1)
    return jnp.where((lane % group) < half, up, down)


def _rope(x, cos, sin, group):
    return x * cos + _swap_halves(x, group) * sin


def _rope_bwd(dy, cos, sin, group):
    return dy * cos + _swap_halves(dy * sin, group)


def _rope_tables(lp, seq, dim, lanes):
    quarter = dim // 4
    tok = jnp.arange(seq, dtype=jnp.int32)
    zeros = jnp.zeros((N_META,), F32)
    pad = jnp.zeros((lp - N_META - seq,), F32)
    row = jnp.concatenate([zeros, (tok // GRID_W).astype(F32), pad])
    col = jnp.concatenate([zeros, (tok % GRID_W).astype(F32), pad])
    inv = ROPE_THETA ** (-jnp.arange(quarter, dtype=F32) / quarter)
    ar, ac = row[:, None] * inv[None, :], col[:, None] * inv[None, :]
    cos = jnp.concatenate([jnp.cos(ar), jnp.cos(ar), jnp.cos(ac), jnp.cos(ac)], axis=1)
    sin = jnp.concatenate([-jnp.sin(ar), jnp.sin(ar), -jnp.sin(ac), jnp.sin(ac)], axis=1)
    if lanes > dim:
        cos = jnp.concatenate([cos, jnp.ones((lp, lanes - dim), F32)], axis=1)
        sin = jnp.concatenate([sin, jnp.zeros((lp, lanes - dim), F32)], axis=1)
    return cos, sin


def _gqa_prep_fwd(proj, qg, kg, cos, sin, *, q_off, k_off, v_off, lp, name):
    t = proj.shape[0]
    qw, kw = GQA_HEADS * HEAD, GQA_KV_HEADS * HEAD
    tr = _tile(lp, max(SUBLANES_BF16, ROW_BLOCK_BYTES // (4 * qw)), SUBLANES_BF16)
    npos = lp // tr
    assert q_off % qw == 0 and k_off % kw == 0 and v_off % kw == 0

    def body(q_ref, k_ref, v_ref, qg_ref, kg_ref, cos_ref, sin_ref, qo_ref, ko_ref, vo_ref):
        cosv, sinv = cos_ref[...], sin_ref[...]
        for src, g_ref, dst, heads in ((q_ref, qg_ref, qo_ref, GQA_HEADS), (k_ref, kg_ref, ko_ref, GQA_KV_HEADS)):
            for hh in range(heads):
                xh = src[:, hh * HEAD:(hh + 1) * HEAD]
                r = lax.rsqrt(jnp.mean(xh * xh, axis=-1, keepdims=True) + NORM_EPS)
                dst[:, hh * HEAD:(hh + 1) * HEAD] = _rope(xh * r * g_ref[...], cosv, sinv, 64).astype(dst.dtype)
        vo_ref[...] = v_ref[...].astype(vo_ref.dtype)

    tab = pl.BlockSpec((tr, HEAD), lambda i: (i % npos, 0))
    vec = pl.BlockSpec((1, HEAD), lambda i: (0, 0))
    return pl.pallas_call(
        body, grid=(t // tr,),
        in_specs=[pl.BlockSpec((tr, qw), lambda i: (i, q_off // qw)),
                  pl.BlockSpec((tr, kw), lambda i: (i, k_off // kw)),
                  pl.BlockSpec((tr, kw), lambda i: (i, v_off // kw)), vec, vec, tab, tab],
        out_specs=[pl.BlockSpec((tr, qw), lambda i: (i, 0)), pl.BlockSpec((tr, kw), lambda i: (i, 0)),
                   pl.BlockSpec((tr, kw), lambda i: (i, 0))],
        out_shape=[SDS((t, qw), BF16), SDS((t, kw), BF16), SDS((t, kw), BF16)],
        compiler_params=_cparams(), name=name)(proj, proj, proj, qg, kg, cos, sin)


def _gqa_prep_bwd(proj, qg, kg, cos, sin, dqr, dkr, *, q_off, k_off, lp, name):
    t = proj.shape[0]
    qw, kw = GQA_HEADS * HEAD, GQA_KV_HEADS * HEAD
    tr = _tile(lp, max(SUBLANES_BF16, ROW_BLOCK_BYTES // (4 * qw)), SUBLANES_BF16)
    npos = lp // tr

    def body(q_ref, k_ref, qg_ref, kg_ref, cos_ref, sin_ref, dqr_ref, dkr_ref, dq_ref, dk_ref, dqg_ref, dkg_ref):
        cosv, sinv = cos_ref[...], sin_ref[...]
        for src, g_ref, dy_ref, dx_ref, dg_ref, heads in (
                (q_ref, qg_ref, dqr_ref, dq_ref, dqg_ref, GQA_HEADS),
                (k_ref, kg_ref, dkr_ref, dk_ref, dkg_ref, GQA_KV_HEADS)):
            gpart = jnp.zeros((1, HEAD), F32)
            for hh in range(heads):
                sl = slice(hh * HEAD, (hh + 1) * HEAD)
                xh = src[:, sl]
                r = lax.rsqrt(jnp.mean(xh * xh, axis=-1, keepdims=True) + NORM_EPS)
                dxn = _rope_bwd(dy_ref[:, sl], cosv, sinv, 64)
                gy = dxn * g_ref[...]
                dx = r * gy - xh * (r * r * r) * jnp.mean(gy * xh, axis=-1, keepdims=True)
                dx_ref[:, sl] = dx.astype(dx_ref.dtype)
                gpart = gpart + jnp.sum(dxn * xh * r, axis=0, keepdims=True)

            @pl.when(pl.program_id(0) == 0)
            def _(dg_ref=dg_ref, gpart=gpart):
                dg_ref[...] = gpart

            @pl.when(pl.program_id(0) > 0)
            def _(dg_ref=dg_ref, gpart=gpart):
                dg_ref[...] += gpart

    tab = pl.BlockSpec((tr, HEAD), lambda i: (i % npos, 0))
    vec = pl.BlockSpec((1, HEAD), lambda i: (0, 0))
    qrow = pl.BlockSpec((tr, qw), lambda i: (i, 0))
    krow = pl.BlockSpec((tr, kw), lambda i: (i, 0))
    return pl.pallas_call(
        body, grid=(t // tr,),
        in_specs=[pl.BlockSpec((tr, qw), lambda i: (i, q_off // qw)),
                  pl.BlockSpec((tr, kw), lambda i: (i, k_off // kw)), vec, vec, tab, tab, qrow, krow],
        out_specs=[qrow, krow, vec, vec],
        out_shape=[SDS((t, qw), BF16), SDS((t, kw), BF16), SDS((1, HEAD), F32), SDS((1, HEAD), F32)],
        compiler_params=_cparams(), name=name)(proj, proj, qg, kg, cos, sin, dqr, dkr)


def _mla_rope_fwd(q_pre, k_pre, proj, cos, sin, *, kpe_off, lp, name):
    t, w = q_pre.shape
    tr = _tile(lp, max(SUBLANES_BF16, ROW_BLOCK_BYTES // (4 * w)), SUBLANES_BF16)
    npos = lp // tr
    assert kpe_off % LANES == 0

    def body(q_ref, k_ref, kpe_ref, cos_ref, sin_ref, qo_ref, ko_ref):
        cosv, sinv = cos_ref[...], sin_ref[...]
        kr = _rope(kpe_ref[...], cosv, sinv, 32).astype(ko_ref.dtype)
        for hh in range(MLA_HEADS):
            base = hh * MLA_SLOT
            qo_ref[:, base:base + HEAD] = q_ref[:, base:base + HEAD].astype(qo_ref.dtype)
            qo_ref[:, base + HEAD:base + MLA_SLOT] = _rope(
                q_ref[:, base + HEAD:base + MLA_SLOT], cosv, sinv, 32).astype(qo_ref.dtype)
            ko_ref[:, base:base + HEAD] = k_ref[:, base:base + HEAD].astype(ko_ref.dtype)
            ko_ref[:, base + HEAD:base + MLA_SLOT] = kr

    row = pl.BlockSpec((tr, w), lambda i: (i, 0))
    tab = pl.BlockSpec((tr, LANES), lambda i: (i % npos, 0))
    return pl.pallas_call(
        body, grid=(t // tr,),
        in_specs=[row, row, pl.BlockSpec((tr, LANES), lambda i: (i, kpe_off // LANES)), tab, tab],
        out_specs=[row, row], out_shape=[SDS((t, w), BF16), SDS((t, w), BF16)],
        compiler_params=_cparams(), name=name)(q_pre, k_pre, proj, cos, sin)


def _mla_rope_bwd(dq, dk, cos, sin, *, lp, name):
    t, w = dq.shape
    tr = _tile(lp, max(SUBLANES_BF16, ROW_BLOCK_BYTES // (4 * w)), SUBLANES_BF16)
    npos = lp // tr

    def body(dq_ref, dk_ref, cos_ref, sin_ref, dqo_ref, dko_ref, dkpe_ref):
        cosv, sinv = cos_ref[...], sin_ref[...]
        pe = jnp.zeros((tr, LANES), F32)
        for hh in range(MLA_HEADS):
            base = hh * MLA_SLOT
            dqo_ref[:, base:base + HEAD] = dq_ref[:, base:base + HEAD].astype(dqo_ref.dtype)
            dqo_ref[:, base + HEAD:base + MLA_SLOT] = _rope_bwd(
                dq_ref[:, base + HEAD:base + MLA_SLOT], cosv, sinv, 32).astype(dqo_ref.dtype)
            dko_ref[:, base:base + HEAD] = dk_ref[:, base:base + HEAD].astype(dko_ref.dtype)
            dko_ref[:, base + HEAD:base + MLA_SLOT] = jnp.zeros((tr, LANES), dko_ref.dtype)
            pe = pe + dk_ref[:, base + HEAD:base + MLA_SLOT]
        dkpe_ref[...] = _rope_bwd(pe, cosv, sinv, 32).astype(dkpe_ref.dtype)

    row = pl.BlockSpec((tr, w), lambda i: (i, 0))
    tab = pl.BlockSpec((tr, LANES), lambda i: (i % npos, 0))
    return pl.pallas_call(
        body, grid=(t // tr,), in_specs=[row, row, tab, tab],
        out_specs=[row, row, pl.BlockSpec((tr, LANES), lambda i: (i, 0))],
        out_shape=[SDS((t, w), BF16), SDS((t, w), BF16), SDS((t, LANES), BF16)],
        compiler_params=_cparams(), name=name)(dq, dk, cos, sin)


def _softmax_rows(q, k, scale, lp, l_valid):
    s = lax.dot_general(q, k, (((1,), (1,)), ((), ())), preferred_element_type=F32) * scale
    cols = lax.broadcasted_iota(jnp.int32, (1, lp), 1)
    s = jnp.where(cols < l_valid, s, -1e30)
    e = jnp.exp(s - jnp.max(s, axis=-1, keepdims=True))
    return e, 1.0 / jnp.sum(e, axis=-1, keepdims=True)


def _attn_fwd(q, k, v, *, batch, lp, l_valid, heads, kv_heads, dk, dv, scale, name):
    group = heads // kv_heads
    tq = _tile(lp, ATTN_Q_ROWS, SUBLANES_BF16)
    nq = lp // tq

    def body(q_ref, k_ref, v_ref, o_ref):
        e, inv = _softmax_rows(q_ref[...], k_ref[...], scale, lp, l_valid)
        o_ref[...] = (jnp.dot(e.astype(BF16), v_ref[...], preferred_element_type=F32) * inv).astype(o_ref.dtype)

    return pl.pallas_call(
        body, grid=(batch, heads, nq),
        in_specs=[pl.BlockSpec((tq, dk), lambda b, h, i: (b * nq + i, h)),
                  pl.BlockSpec((lp, dk), lambda b, h, i: (b, h // group)),
                  pl.BlockSpec((lp, dv), lambda b, h, i: (b, h // group))],
        out_specs=pl.BlockSpec((tq, dv), lambda b, h, i: (b * nq + i, h)),
        out_shape=SDS((batch * lp, heads * dv), BF16), compiler_params=_cparams(), name=name)(q, k, v)


def _attn_bwd(q, k, v, do, *, batch, lp, l_valid, heads, kv_heads, dk, dv, scale, name):
    group = heads // kv_heads
    tq = _tile(lp, ATTN_Q_ROWS, SUBLANES_BF16)
    nq = lp // tq

    def body(q_ref, k_ref, v_ref, do_ref, dq_ref, dk_ref, dv_ref):
        first = (pl.program_id(2) == 0) & (pl.program_id(3) == 0)
        qv, kv, dov = q_ref[...], k_ref[...], do_ref[...]
        e, inv = _softmax_rows(qv, kv, scale, lp, l_valid)
        dp = lax.dot_general(dov, v_ref[...], (((1,), (1,)), ((), ())), preferred_element_type=F32)
        delta = inv * jnp.sum(e * dp, axis=-1, keepdims=True)
        ds = (e * ((dp - delta) * (inv * scale))).astype(BF16)
        dq_ref[...] = jnp.dot(ds, kv, preferred_element_type=F32)
        dkp = lax.dot_general(ds, qv, (((0,), (0,)), ((), ())), preferred_element_type=F32)
        dvp = lax.dot_general(e.astype(BF16), (dov * inv).astype(BF16), (((0,), (0,)), ((), ())),
                              preferred_element_type=F32)

        @pl.when(first)
        def _():
            dk_ref[...] = dkp
            dv_ref[...] = dvp

        @pl.when(jnp.logical_not(first))
        def _():
            dk_ref[...] += dkp
            dv_ref[...] += dvp

    return pl.pallas_call(
        body, grid=(batch, kv_heads, group, nq),
        in_specs=[pl.BlockSpec((tq, dk), lambda b, hk, g, i: (b * nq + i, hk * group + g)),
                  pl.BlockSpec((lp, dk), lambda b, hk, g, i: (b, hk)),
                  pl.BlockSpec((lp, dv), lambda b, hk, g, i: (b, hk)),
                  pl.BlockSpec((tq, dv), lambda b, hk, g, i: (b * nq + i, hk * group + g))],
        out_specs=[pl.BlockSpec((tq, dk), lambda b, hk, g, i: (b * nq + i, hk * group + g)),
                   pl.BlockSpec((lp, dk), lambda b, hk, g, i: (b, hk)),
                   pl.BlockSpec((lp, dv), lambda b, hk, g, i: (b, hk))],
        out_shape=[SDS((batch * lp, heads * dk), F32), SDS((batch * lp, kv_heads * dk), F32),
                   SDS((batch * lp, kv_heads * dv), F32)],
        compiler_params=_cparams(), name=name)(q, k, v, do)


def _meta_grad(dh3, *, name):
    b, _, d = dh3.shape

    def body(x_ref, o_ref):
        @pl.when(pl.program_id(0) == 0)
        def _():
            o_ref[...] = x_ref[...]

        @pl.when(pl.program_id(0) > 0)
        def _():
            o_ref[...] += x_ref[...]

    return pl.pallas_call(
        body, grid=(b,), in_specs=[pl.BlockSpec((None, N_META, d), lambda bi: (bi, 0, 0))],
        out_specs=pl.BlockSpec((N_META, d), lambda bi: (0, 0)), out_shape=SDS((N_META, d), F32),
        compiler_params=_cparams(), name=name)(dh3)


def _sum_slots(parts, *, out_dtype, name):
    _, r, c = parts[0].shape
    tr = _row_tile(r, c)
    counts = [p.shape[0] for p in parts]

    def body(*refs):
        o_ref = refs[-1]
        acc = None
        for ref, cnt in zip(refs[:-1], counts):
            for s in range(cnt):
                term = ref[s].astype(F32)
                acc = term if acc is None else acc + term
        o_ref[...] = acc.astype(o_ref.dtype)

    return pl.pallas_call(
        body, grid=(r // tr,),
        in_specs=[pl.BlockSpec((cnt, tr, c), lambda i: (0, i, 0)) for cnt in counts],
        out_specs=pl.BlockSpec((tr, c), lambda i: (i, 0)), out_shape=SDS((r, c), out_dtype),
        compiler_params=_cparams(), name=name)(*parts)


def _adamw(w, g, m, v, *, name):
    shape = w.shape
    cols = shape[-1]
    rows = w.size // cols
    tr = _tile(rows, max(8, ROW_BLOCK_BYTES // (4 * cols)), 8)
    bc1 = 1.0 - ADAM_B1 ** ADAM_STEP
    bc2 = 1.0 - ADAM_B2 ** ADAM_STEP

    def body(w_ref, g_ref, m_ref, v_ref, d_ref, mo_ref, vo_ref):
        gv = g_ref[...]
        mn = ADAM_B1 * m_ref[...] + (1.0 - ADAM_B1) * gv
        vn = ADAM_B2 * v_ref[...] + (1.0 - ADAM_B2) * (gv * gv)
        mo_ref[...] = mn
        vo_ref[...] = vn
        d_ref[...] = -ADAM_LR * ((mn / bc1) / (jnp.sqrt(vn / bc2) + ADAM_EPS) + ADAM_WD * w_ref[...])

    spec = pl.BlockSpec((tr, cols), lambda i: (i, 0))
    outs = pl.pallas_call(
        body, grid=(rows // tr,), in_specs=[spec] * 4, out_specs=[spec] * 3,
        out_shape=[SDS((rows, cols), F32)] * 3, compiler_params=_cparams(), name=name)(
            *[a.reshape(rows, cols) for a in (w, g, m, v)])
    return tuple(o.reshape(shape) for o in outs)


def _adamw_layer(w, g, m, v, li, outs, *, name):
    depth, k, n = w.shape
    tr = _tile(k, max(8, ROW_BLOCK_BYTES // (4 * n)), 8)
    bc1 = 1.0 - ADAM_B1 ** ADAM_STEP
    bc2 = 1.0 - ADAM_B2 ** ADAM_STEP
    flat = (depth * k, n)
    if outs is None:
        outs = tuple(lax.empty(flat, F32) for _ in range(4))

    def body(w_ref, g_ref, m_ref, v_ref, *rest):
        go_ref, d_ref, mo_ref, vo_ref = rest[4:]
        gv = g_ref[...]
        mn = ADAM_B1 * m_ref[...] + (1.0 - ADAM_B1) * gv
        vn = ADAM_B2 * v_ref[...] + (1.0 - ADAM_B2) * (gv * gv)
        go_ref[...] = gv
        mo_ref[...] = mn
        vo_ref[...] = vn
        d_ref[...] = -ADAM_LR * ((mn / bc1) / (jnp.sqrt(vn / bc2) + ADAM_EPS) + ADAM_WD * w_ref[...])

    layer = pl.BlockSpec((tr, n), lambda i: (li * (k // tr) + i, 0))
    return tuple(pl.pallas_call(
        body, grid=(k // tr,), in_specs=[layer, pl.BlockSpec((tr, n), lambda i: (i, 0)), layer, layer] + [_ANY] * 4,
        out_specs=[layer] * 4, out_shape=[SDS(flat, F32)] * 4,
        input_output_aliases={4 + j: j for j in range(4)},
        compiler_params=_cparams(), name=name)(w.reshape(flat), g, m.reshape(flat), v.reshape(flat), *outs))


def _mesh_pos():
    return lax.axis_index("x"), lax.axis_index("y"), lax.axis_index("c")


def _other_chips(x, y):
    return [(1 - x, y), (x, 1 - y), (1 - x, 1 - y)]


def _chip_allgather_small(vec, *, name):
    r, c = vec.shape

    def body(x_ref, out_ref, send_sems, recv_sems):
        x, y, cc = _mesh_pos()
        mine = 2 * x + y
        out_ref[mine] = x_ref[...]
        chips = _other_chips(x, y)
        sends = [pltpu.make_async_remote_copy(
            src_ref=x_ref, dst_ref=out_ref.at[mine], send_sem=send_sems.at[j], recv_sem=recv_sems.at[j],
            device_id=(px, py, cc), device_id_type=MESH) for j, (px, py) in enumerate(chips)]
        for cp in sends:
            cp.start()
        for j, (px, py) in enumerate(chips):
            pltpu.make_async_remote_copy(
                src_ref=x_ref, dst_ref=out_ref.at[2 * px + py], send_sem=send_sems.at[j],
                recv_sem=recv_sems.at[j], device_id=(px, py, cc), device_id_type=MESH).wait_recv()
        for cp in sends:
            cp.wait_send()

    return pl.pallas_call(
        body, out_shape=SDS((N_CHIPS, r, c), F32),
        in_specs=[pl.BlockSpec(memory_space=pltpu.VMEM)], out_specs=pl.BlockSpec(memory_space=pltpu.VMEM),
        scratch_shapes=[pltpu.SemaphoreType.DMA((3,)), pltpu.SemaphoreType.DMA((3,))],
        compiler_params=_cparams(), name=name)(vec)


def _allreduce_small(vec, *, name):
    r, c = vec.shape

    def body(x_ref, out_ref, buf, send_sems, recv_sems):
        x, y, cc = _mesh_pos()
        me = 4 * x + 2 * y + cc
        buf[me] = x_ref[...]

        def peer(kk):
            fx, fy, fc = (kk >> 2) & 1, (kk >> 1) & 1, kk & 1
            px = x if fx == 0 else 1 - x
            py = y if fy == 0 else 1 - y
            pc = cc if fc == 0 else 1 - cc
            return (px, py, pc), 4 * px + 2 * py + pc

        sends = []
        for kk in range(1, N_DEV):
            dev, _ = peer(kk)
            sends.append(pltpu.make_async_remote_copy(
                src_ref=x_ref, dst_ref=buf.at[me], send_sem=send_sems.at[kk - 1], recv_sem=recv_sems.at[kk - 1],
                device_id=dev, device_id_type=MESH))
        for cp in sends:
            cp.start()
        for kk in range(1, N_DEV):
            dev, slot = peer(kk)
            pltpu.make_async_remote_copy(
                src_ref=x_ref, dst_ref=buf.at[slot], send_sem=send_sems.at[kk - 1], recv_sem=recv_sems.at[kk - 1],
                device_id=dev, device_id_type=MESH).wait_recv()
        for cp in sends:
            cp.wait_send()
        acc = buf[0]
        for s in range(1, N_DEV):
            acc = acc + buf[s]
        out_ref[...] = acc

    return pl.pallas_call(
        body, out_shape=SDS((r, c), F32),
        in_specs=[pl.BlockSpec(memory_space=pltpu.VMEM)], out_specs=pl.BlockSpec(memory_space=pltpu.VMEM),
        scratch_shapes=[pltpu.VMEM((N_DEV, r, c), F32), pltpu.SemaphoreType.DMA((N_DEV - 1,)),
                        pltpu.SemaphoreType.DMA((N_DEV - 1,))],
        compiler_params=_cparams(), name=name)(vec)


_HBM = pl.BlockSpec(memory_space=pltpu.HBM)


def _half(rows, half):
    hk = rows // 2
    assert hk % SUBLANES_BF16 == 0
    return pl.ds(pl.multiple_of(half * hk, SUBLANES_BF16), hk)


_SEM = pl.BlockSpec(memory_space=pltpu.SEMAPHORE)
_ANY = pl.BlockSpec(memory_space=pl.ANY)
_DATAFLOW = pltpu.SideEffectType.DATAFLOW_SIDE_EFFECTING


def _in_hbm(a):
    return pltpu.with_memory_space_constraint(a, pltpu.HBM)


def _split_start(srcs, lands, prev, make_copy, *, name):
    nw = len(srcs)

    def body(*refs):
        src_refs, land_refs = refs[:nw], refs[nw:2 * nw]
        send_sems, recv_sems = refs[2 * nw + 1], refs[2 * nw + 2]
        token = refs[-1]
        for j in range(3):
            for i in range(nw):
                make_copy(src_refs, land_refs, send_sems, recv_sems, j, i, True).start()
        token[...] = jnp.zeros_like(token)

    outs = pl.pallas_call(
        body, name=name,
        out_shape=(pltpu.SemaphoreType.DMA((3 * nw,)), pltpu.SemaphoreType.DMA((3 * nw,)),
                   *[pltpu.HBM(a.shape, a.dtype) for a in srcs], *[pltpu.HBM(a.shape, a.dtype) for a in lands],
                   SDS((8, LANES), F32)),
        in_specs=[_HBM] * (2 * nw) + [_ANY],
        out_specs=(_SEM, _SEM, *[_HBM] * (2 * nw), pl.BlockSpec(memory_space=pltpu.VMEM)),
        input_output_aliases={i: 2 + i for i in range(2 * nw)},
        compiler_params=pltpu.CompilerParams(has_side_effects=_DATAFLOW),
    )(*[_in_hbm(a) for a in srcs], *[_in_hbm(a) for a in lands], prev)
    return outs[0], outs[1], list(outs[2:2 + nw]), list(outs[2 + nw:2 + 2 * nw]), outs[-1]


def _split_wait(pending, after, make_copy, *, name):
    send_sems, recv_sems, srcs, lands, _ = pending
    nw = len(srcs)

    def body(*refs):
        src_refs, land_refs = refs[:nw], refs[nw:2 * nw]
        send_refs, recv_refs = refs[2 * nw], refs[2 * nw + 1]
        for j in range(3):
            for i in range(nw):
                cp = make_copy(src_refs, land_refs, send_refs, recv_refs, j, i, False)
                cp.wait_send()
                cp.wait_recv()

    outs = pl.pallas_call(
        body, name=name,
        out_shape=(*[pltpu.HBM(a.shape, a.dtype) for a in srcs], *[pltpu.HBM(a.shape, a.dtype) for a in lands]),
        in_specs=[_HBM] * (2 * nw) + [_SEM, _SEM, _ANY], out_specs=tuple([_HBM] * (2 * nw)),
        input_output_aliases={i: i for i in range(2 * nw)},
        compiler_params=pltpu.CompilerParams(has_side_effects=_DATAFLOW),
    )(*srcs, *lands, send_sems, recv_sems, after)
    return list(outs[:nw]), list(outs[nw:])


def _gather_copy(shapes):
    def make(src_refs, land_refs, send_sems, recv_sems, j, i, outgoing):
        x, y, cc = _mesh_pos()
        px, py = _other_chips(x, y)[j]
        rows = _half(shapes[i][0], cc)
        slot = 2 * x + y if outgoing else 2 * px + py
        return pltpu.make_async_remote_copy(
            src_ref=src_refs[i].at[rows], dst_ref=land_refs[i].at[slot, rows],
            send_sem=send_sems.at[j * len(shapes) + i], recv_sem=recv_sems.at[j * len(shapes) + i],
            device_id=(px, py, cc), device_id_type=MESH)
    return make


def _gather_forward(lands, *, name):
    nw = len(lands)

    def body(*refs):
        in_refs, out_refs = refs[:nw], refs[nw:2 * nw]
        send_sems, recv_sems = refs[2 * nw:]
        x, y, cc = _mesh_pos()
        cps = []
        for j, (px, py) in enumerate(_other_chips(x, y)):
            for i in range(nw):
                rows = _half(lands[i].shape[1], cc)
                cps.append(pltpu.make_async_remote_copy(
                    src_ref=in_refs[i].at[2 * px + py, rows], dst_ref=out_refs[i].at[2 * px + py, rows],
                    send_sem=send_sems.at[j, i], recv_sem=recv_sems.at[j, i],
                    device_id=(x, y, 1 - cc), device_id_type=MESH))
        for cp in cps:
            cp.start()
        for cp in cps:
            cp.wait()

    return pl.pallas_call(
        body, out_shape=[SDS(a.shape, a.dtype) for a in lands], in_specs=[_HBM] * nw, out_specs=[_HBM] * nw,
        input_output_aliases={i: i for i in range(nw)},
        scratch_shapes=[pltpu.SemaphoreType.DMA((3, nw)), pltpu.SemaphoreType.DMA((3, nw))], name=name)(*lands)


def _grads_to_sibling(gs, *, name):
    nw = len(gs)

    def body(*refs):
        g_refs, out_refs = refs[:nw], refs[nw:2 * nw]
        send_sems, recv_sems = refs[2 * nw:]
        x, y, cc = _mesh_pos()
        cps = [pltpu.make_async_remote_copy(
            src_ref=g_refs[i].at[pl.ds(0, N_CHIPS), _half(gs[i].shape[1], 1 - cc)], dst_ref=out_refs[i],
            send_sem=send_sems.at[i], recv_sem=recv_sems.at[i], device_id=(x, y, 1 - cc), device_id_type=MESH)
            for i in range(nw)]
        for cp in cps:
            cp.start()
        for cp in cps:
            cp.wait()

    return pl.pallas_call(
        body, out_shape=[SDS((N_CHIPS, g.shape[1] // 2, g.shape[2]), g.dtype) for g in gs],
        in_specs=[_HBM] * nw, out_specs=[_HBM] * nw,
        scratch_shapes=[pltpu.SemaphoreType.DMA((nw,)), pltpu.SemaphoreType.DMA((nw,))], name=name)(*gs)


def _chips_copy(nw):
    def make(src_refs, land_refs, send_sems, recv_sems, j, i, outgoing):
        x, y, cc = _mesh_pos()
        px, py = _other_chips(x, y)[j]
        return pltpu.make_async_remote_copy(
            src_ref=src_refs[i].at[2 * px + py], dst_ref=land_refs[i].at[j],
            send_sem=send_sems.at[j * nw + i], recv_sem=recv_sems.at[j * nw + i],
            device_id=(px, py, cc), device_id_type=MESH)
    return make


def _share_halves(tots, *, name):
    nw = len(tots)

    def body(*refs):
        t_refs, out_refs = refs[:nw], refs[nw:2 * nw]
        send_sems, recv_sems = refs[2 * nw:]
        x, y, cc = _mesh_pos()
        cps = [pltpu.make_async_remote_copy(
            src_ref=t_refs[i], dst_ref=out_refs[i], send_sem=send_sems.at[i], recv_sem=recv_sems.at[i],
            device_id=(x, y, 1 - cc), device_id_type=MESH) for i in range(nw)]
        for cp in cps:
            cp.start()
        for cp in cps:
            cp.wait()

    return pl.pallas_call(
        body, out_shape=[SDS(tt.shape, tt.dtype) for tt in tots],
        in_specs=[_HBM] * nw, out_specs=[_HBM] * nw,
        scratch_shapes=[pltpu.SemaphoreType.DMA((nw,)), pltpu.SemaphoreType.DMA((nw,))], name=name)(*tots)


def _pack_small(arrays):
    flat = jnp.concatenate([a.astype(F32).reshape(-1) for a in arrays])
    n = flat.shape[0]
    padded = -(-n // (8 * LANES)) * (8 * LANES)
    return jnp.pad(flat, (0, padded - n)).reshape(padded // LANES, LANES)


def _unpack_small(flat, shapes):
    out, off = [], 0
    for shp in shapes:
        n = math.prod(shp)
        out.append(flat[off:off + n].reshape(shp))
        off += n
    return out


BIG = (("w_in", 1), ("w_conv_out", 1), ("w_gqa_out", 1), ("w_mla_uq", 1), ("w_mla_ukv", 1), ("w_mla_out", 1),
       ("w_out", 0), ("w_ffn_gate", 1), ("w_ffn_up", 1), ("w_ffn_down", 0))
BIG_NAMES = tuple(nm for nm, _ in BIG)
BIG_GROUPS = (("w_in",), tuple(nm for nm in BIG_NAMES if nm != "w_in"))


def _cols_to_slots(full):
    k, n = full.shape
    return full.reshape(k, N_CHIPS, n // N_CHIPS).transpose(1, 0, 2)


def _slots_to_cols(slots):
    s, k, n = slots.shape
    return slots.transpose(1, 0, 2).reshape(k, s * n)


def _step(p, mom_m, mom_v, x, loss_target):
    b, seq, d = x.shape
    depth = p["w_in"].shape[0]
    l_valid = N_META + seq
    lp = -(-l_valid // LANES) * LANES
    t = b * lp
    c_conv = d // 2
    q_rank = p["mla_q_norm_g"].shape[1]
    kv_rank = p["mla_kv_norm_g"].shape[1]
    qw, kw = GQA_HEADS * HEAD, GQA_KV_HEADS * HEAD
    off_q = 2 * c_conv
    off_k = off_q + qw
    off_v = off_k + kw
    off_cq = off_v + kw
    off_ckv = off_cq + q_rank
    off_gate = off_ckv + kv_rank
    off_kpe = off_gate + N_BRANCH * d
    d_inp = off_kpe + LANES
    assert off_cq % q_rank == 0 and off_ckv % kv_rank == 0
    xi, yi, ci = _mesh_pos()
    chip = 2 * xi + yi

    small_sh = [p["meta_tokens"], p["conv_dw"]]
    small_all = _chip_allgather_small(_pack_small(small_sh), name="gather_small")
    got = [_unpack_small(small_all[s].reshape(-1), [a.shape for a in small_sh]) for s in range(N_CHIPS)]
    meta_full = jnp.concatenate([g[0] for g in got], axis=1)
    conv_dw_full = jnp.concatenate([g[1] for g in got], axis=-1).reshape(depth, CONV_K, c_conv)

    gather_copies = [_gather_copy([p[nm].shape[1:] for nm in grp]) for grp in BIG_GROUPS]
    gathers = []
    order = small_all
    for li in range(depth):
        gathers.append([])
        for gi, grp in enumerate(BIG_GROUPS):
            own = [p[nm][li].astype(BF16) for nm in grp]
            lands = [lax.empty((N_CHIPS,) + o.shape, BF16) for o in own]
            gathers[li].append(_split_start(own, lands, order, gather_copies[gi],
                                            name="gather_start_%d_%d" % (li, gi)))
            order = gathers[li][gi][4]
    meta_full = meta_full + order[0, 0]

    def gathered(li, gi, after):
        own, lands = _split_wait(gathers[li][gi], after, gather_copies[gi], name="gather_wait_%d_%d" % (li, gi))
        got = _gather_forward(lands, name="gather_forward_%d" % gi)
        return {nm: lax.dynamic_update_slice(g, o[None], (chip, 0, 0)) for nm, g, o in zip(BIG_GROUPS[gi], got, own)}

    def in_proj_weight(slots):
        w_in = _slots_to_cols(slots)
        return jnp.concatenate(
            [w_in[:, :off_gate], w_in[:, off_gate + MLA_ROPE:], w_in[:, off_gate:off_gate + MLA_ROPE],
             jnp.zeros((d, LANES - MLA_ROPE), BF16)], axis=1)

    def other_weights(full):
        full["w_out"] = full["w_out"].reshape(-1, d)
        full["w_ffn_down"] = full["w_ffn_down"].reshape(-1, d)
        uq = _slots_to_cols(full["w_mla_uq"]).reshape(q_rank, MLA_HEADS, HEAD + MLA_ROPE)
        full["w_uq"] = jnp.pad(uq, ((0, 0), (0, 0), (0, MLA_SLOT - HEAD - MLA_ROPE))).reshape(
            q_rank, MLA_HEADS * MLA_SLOT)
        ukv = _slots_to_cols(full["w_mla_ukv"]).reshape(kv_rank, MLA_HEADS, 2 * HEAD)
        full["w_uk"] = jnp.pad(ukv[:, :, :HEAD], ((0, 0), (0, 0), (0, MLA_SLOT - HEAD))).reshape(
            kv_rank, MLA_HEADS * MLA_SLOT)
        full["w_uv"] = ukv[:, :, HEAD:].reshape(kv_rank, MLA_HEADS * HEAD)
        return full

    cos_g, sin_g = _rope_tables(lp, seq, HEAD, HEAD)
    cos_m, sin_m = _rope_tables(lp, seq, MLA_ROPE, LANES)
    gqa_scale = 1.0 / math.sqrt(HEAD)
    mla_scale = 1.0 / math.sqrt(HEAD + MLA_ROPE)
    attn_kw = dict(batch=b, lp=lp, l_valid=l_valid)
    gqa_kw = dict(heads=GQA_HEADS, kv_heads=GQA_KV_HEADS, dk=HEAD, dv=HEAD, scale=gqa_scale, **attn_kw)
    mla_kw = dict(heads=MLA_HEADS, kv_heads=MLA_HEADS, dk=MLA_SLOT, dv=HEAD, scale=mla_scale, **attn_kw)

    h = jnp.concatenate([jnp.broadcast_to(meta_full[None], (b, N_META, d)), x,
                         jnp.zeros((b, lp - l_valid, d), F32)], axis=1).reshape(t, d)
    saved, weights = [], []
    for li in range(depth):
        w = {"w_in": in_proj_weight(gathered(li, 0, h)["w_in"])}
        weights.append(w)
        row = lambda nm: p[nm][li][None, :]
        s = {"h_in": h}
        s["u"] = _rmsnorm_fwd(h, row("mix_norm_g"), width=d, cblk=0, name="mix_norm")
        proj = _mm(s["u"], w["w_in"], name="in_proj")
        s["proj"] = proj
        proj3 = proj.reshape(b, lp, d_inp)
        s["yconv"] = _conv_fwd(proj3, conv_dw_full[li], row("conv_b"), c_conv=c_conv, l_valid=l_valid,
                               name="conv").reshape(t, c_conv)
        s["sconv"] = _ln_silu_fwd(s["yconv"], row("conv_ln_g"), row("conv_ln_b"), name="conv_ln_silu")
        w.update(other_weights(gathered(li, 1, s["sconv"])))
        s["ya"] = _mm(s["sconv"], w["w_conv_out"], b_sharded=True, name="conv_out")
        s["gq"], s["gk"], s["gv"] = _gqa_prep_fwd(
            proj, row("gqa_q_norm_g"), row("gqa_k_norm_g"), cos_g, sin_g,
            q_off=off_q, k_off=off_k, v_off=off_v, lp=lp, name="gqa_prep")
        s["go"] = _attn_fwd(s["gq"], s["gk"], s["gv"], name="gqa_attn", **gqa_kw)
        s["yb"] = _mm(s["go"], w["w_gqa_out"], b_sharded=True, name="gqa_out")
        s["cqn"] = _rmsnorm_fwd(proj, row("mla_q_norm_g"), width=q_rank, cblk=off_cq // q_rank, name="mla_q_norm")
        s["kvn"] = _rmsnorm_fwd(proj, row("mla_kv_norm_g"), width=kv_rank, cblk=off_ckv // kv_rank,
                                name="mla_kv_norm")
        q_pre = _mm(s["cqn"], w["w_uq"], name="mla_uq")
        k_pre = _mm(s["kvn"], w["w_uk"], name="mla_uk")
        s["mv"] = _mm(s["kvn"], w["w_uv"], out_dtype=BF16, name="mla_uv")
        s["mq"], s["mk"] = _mla_rope_fwd(q_pre, k_pre, proj, cos_m, sin_m, kpe_off=off_kpe, lp=lp, name="mla_rope")
        s["mo"] = _attn_fwd(s["mq"], s["mk"], s["mv"], name="mla_attn", **mla_kw)
        s["yc"] = _mm(s["mo"], w["w_mla_out"], b_sharded=True, name="mla_out")
        s["merged"] = _merge_fwd(proj, row("gate_b"), s["ya"], s["yb"], s["yc"], gate_off=off_gate, name="merge")
        h = _mm(s["merged"], w["w_out"], add=h, name="mix_out")
        s["h_mid"] = h
        s["v"] = _rmsnorm_fwd(h, row("ffn_norm_g"), width=d, cblk=0, name="ffn_norm")
        s["gate"] = _mm(s["v"], w["w_ffn_gate"], b_sharded=True, out_dtype=BF16, name="ffn_gate")
        s["up"] = _mm(s["v"], w["w_ffn_up"], b_sharded=True, out_dtype=BF16, name="ffn_up")
        s["act"] = _swiglu_fwd(s["gate"], s["up"], name="swiglu")
        h = _mm(s["act"], w["w_ffn_down"], add=h, name="ffn_down")
        saved.append(s)

    target = jnp.pad(loss_target, ((0, 0), (N_META, lp - l_valid), (0, 0))).reshape(t, d)
    loss_part, dh, dhb, g_final = _loss_head(h, p["final_norm_g"][None, :], target, lp=lp, l_valid=l_valid,
                                             name="loss_head")
    loss = lax.psum(loss_part[0, 0], ("x", "y", "c"))

    small_names = ["mix_norm_g", "conv_dw", "conv_b", "conv_ln_g", "conv_ln_b", "gqa_q_norm_g", "gqa_k_norm_g",
                   "mla_q_norm_g", "mla_kv_norm_g", "gate_b", "ffn_norm_g"]
    small_g = {nm: [None] * depth for nm in small_names}
    adam = {}
    chips_copies = [_chips_copy(len(grp)) for grp in BIG_GROUPS]

    def start_reduce(lj, gi, big):
        grp = BIG_GROUPS[gi]
        glist = [big[nm] for nm in grp]
        from_sib = _grads_to_sibling(glist, name="grads_to_sibling_%d" % gi)
        parts = []
        for nm, g, fs in zip(grp, glist, from_sib):
            hk = g.shape[1] // 2
            mine = lax.dynamic_slice_in_dim(g, ci * hk, hk, axis=1)
            rows = (1, N_CHIPS * hk, g.shape[2])
            parts.append(_sum_slots([mine.reshape(rows), fs.reshape(rows)], out_dtype=BF16,
                                    name="sum_sibling_" + nm).reshape(fs.shape))
        lands = [lax.empty((3,) + pt.shape[1:], BF16) for pt in parts]
        return lj, gi, _split_start(parts, lands, order, chips_copies[gi], name="chips_start_%d_%d" % (lj, gi))

    def finish_reduce(item, after):
        lj, gi, pending = item
        grp = BIG_GROUPS[gi]
        parts, from_chips = _split_wait(pending, after, chips_copies[gi], name="chips_wait_%d_%d" % (lj, gi))
        tots = [_sum_slots([lax.dynamic_index_in_dim(pt, chip, 0, keepdims=True), fc], out_dtype=F32,
                           name="sum_chips_" + nm) for nm, pt, fc in zip(grp, parts, from_chips)]
        others = _share_halves(tots, name="share_halves_%d" % gi)
        for nm, tot, oth in zip(grp, tots, others):
            g = jnp.concatenate([jnp.where(ci == 0, tot, oth), jnp.where(ci == 0, oth, tot)], axis=0)
            adam[nm] = _adamw_layer(p[nm], g, mom_m[nm], mom_v[nm], lj, adam.get(nm), name="adamw_" + nm)

    in_flight = []
    for li in reversed(range(depth)):
        w, s = weights[li], saved[li]
        row = lambda nm: p[nm][li][None, :]
        big = {}
        dact = _mm(dhb, w["w_ffn_down"], mode="nt", deps=[item[2][4] for item in in_flight], name="d_ffn_act")
        big["w_ffn_down"] = _mm(s["act"], dhb, mode="tn", out_dtype=BF16, deps=[item[2][4] for item in in_flight],
                                name="dw_ffn_down").reshape(N_CHIPS, -1, d)
        dgate, dup = _swiglu_bwd(s["gate"], s["up"], dact, name="d_swiglu")
        dv = _mm(dgate, w["w_ffn_gate"], mode="nt", b_sharded=True, name="d_ffn_v_gate")
        dv = _mm(dup, w["w_ffn_up"], mode="nt", b_sharded=True, add=dv, name="d_ffn_v_up")
        big["w_ffn_gate"] = _mm(s["v"], dgate, mode="tn", out_dtype=BF16, out_shards=N_CHIPS, name="dw_ffn_gate")
        big["w_ffn_up"] = _mm(s["v"], dup, mode="tn", out_dtype=BF16, out_shards=N_CHIPS, name="dw_ffn_up")
        dh, dhb, g = _rmsnorm_bwd(s["h_mid"], row("ffn_norm_g"), dv, width=d, cblk=0, res=dh, name="d_ffn_norm")
        small_g["ffn_norm_g"][li] = g[0]
        dm = _mm(dhb, w["w_out"], mode="nt", name="d_merged")
        big["w_out"] = _mm(s["merged"], dhb, mode="tn", out_dtype=BF16, name="dw_out").reshape(N_CHIPS, -1, d)
        dya, dyb, dyc, dl0, dl1, dl2, db0, db1, db2 = _merge_bwd(
            s["proj"], row("gate_b"), s["ya"], s["yb"], s["yc"], dm, gate_off=off_gate, name="d_merge")
        small_g["gate_b"][li] = jnp.concatenate([db0[0], db1[0], db2[0]])
        ds = _mm(dya, w["w_conv_out"], mode="nt", b_sharded=True, name="d_conv_s")
        big["w_conv_out"] = _mm(s["sconv"], dya, mode="tn", out_dtype=BF16, out_shards=N_CHIPS, name="dw_conv_out")
        dyconv, g, gb = _ln_silu_bwd(s["yconv"], row("conv_ln_g"), row("conv_ln_b"), ds, name="d_conv_ln_silu")
        small_g["conv_ln_g"][li], small_g["conv_ln_b"][li] = g[0], gb[0]
        proj3 = s["proj"].reshape(b, lp, d_inp)
        da, dgte, ddw, dcb = _conv_bwd(proj3, conv_dw_full[li], dyconv.reshape(b, lp, c_conv), c_conv=c_conv,
                                       l_valid=l_valid, name="d_conv")
        small_g["conv_dw"][li], small_g["conv_b"][li] = ddw, dcb[0]
        dgo = _mm(dyb, w["w_gqa_out"], mode="nt", b_sharded=True, out_dtype=BF16, name="d_gqa_o")
        big["w_gqa_out"] = _mm(s["go"], dyb, mode="tn", out_dtype=BF16, out_shards=N_CHIPS, name="dw_gqa_out")
        dgq, dgk, dgv = _attn_bwd(s["gq"], s["gk"], s["gv"], dgo, name="d_gqa_attn", **gqa_kw)
        dq_g, dk_g, gq, gk = _gqa_prep_bwd(s["proj"], row("gqa_q_norm_g"), row("gqa_k_norm_g"), cos_g, sin_g,
                                           dgq, dgk, q_off=off_q, k_off=off_k, lp=lp, name="d_gqa_prep")
        small_g["gqa_q_norm_g"][li], small_g["gqa_k_norm_g"][li] = gq[0], gk[0]
        dmo = _mm(dyc, w["w_mla_out"], mode="nt", b_sharded=True, out_dtype=BF16, name="d_mla_o")
        big["w_mla_out"] = _mm(s["mo"], dyc, mode="tn", out_dtype=BF16, out_shards=N_CHIPS, name="dw_mla_out")
        dmq, dmk, dmv = _attn_bwd(s["mq"], s["mk"], s["mv"], dmo, name="d_mla_attn", **mla_kw)
        dq_pre, dk_pre, dkpe = _mla_rope_bwd(dmq, dmk, cos_m, sin_m, lp=lp, name="d_mla_rope")
        dmvb = dmv.astype(BF16)
        g_uq = _mm(s["cqn"], dq_pre, mode="tn", out_dtype=BF16, name="dw_mla_uq")
        g_uk = _mm(s["kvn"], dk_pre, mode="tn", out_dtype=BF16, name="dw_mla_uk")
        g_uv = _mm(s["kvn"], dmvb, mode="tn", out_dtype=BF16, name="dw_mla_uv")
        big["w_mla_uq"] = _cols_to_slots(
            g_uq.reshape(q_rank, MLA_HEADS, MLA_SLOT)[:, :, :HEAD + MLA_ROPE].reshape(q_rank, -1))
        big["w_mla_ukv"] = _cols_to_slots(jnp.concatenate(
            [g_uk.reshape(kv_rank, MLA_HEADS, MLA_SLOT)[:, :, :HEAD], g_uv.reshape(kv_rank, MLA_HEADS, HEAD)],
            axis=-1).reshape(kv_rank, -1))
        dcqn = _mm(dq_pre, w["w_uq"], mode="nt", name="d_mla_cqn")
        dkvn = _mm(dk_pre, w["w_uk"], mode="nt", name="d_mla_kvn_k")
        dkvn = _mm(dmvb, w["w_uv"], mode="nt", add=dkvn, name="d_mla_kvn_v")
        _, dcq, g = _rmsnorm_bwd(s["proj"], row("mla_q_norm_g"), dcqn, width=q_rank, cblk=off_cq // q_rank,
                                 name="d_mla_q_norm")
        small_g["mla_q_norm_g"][li] = g[0]
        _, dckv, g = _rmsnorm_bwd(s["proj"], row("mla_kv_norm_g"), dkvn, width=kv_rank, cblk=off_ckv // kv_rank,
                                  name="d_mla_kv_norm")
        small_g["mla_kv_norm_g"][li] = g[0]
        for item in in_flight:
            finish_reduce(item, dckv)
        in_flight = [start_reduce(li, 1, big)]
        dproj = jnp.concatenate(
            [da.reshape(t, c_conv), dgte.reshape(t, c_conv), dq_g, dk_g, dgv.astype(BF16), dcq, dckv,
             dl0, dl1, dl2, dkpe], axis=1)
        du = _mm(dproj, w["w_in"], mode="nt", deps=[in_flight[0][2][4]], name="d_u")
        g_in = _mm(s["u"], dproj, mode="tn", out_dtype=BF16, deps=[in_flight[0][2][4]], name="dw_in")
        big["w_in"] = _cols_to_slots(jnp.concatenate(
            [g_in[:, :off_gate], g_in[:, off_kpe:off_kpe + MLA_ROPE], g_in[:, off_gate:off_kpe]], axis=1))
        dh, dhb, g = _rmsnorm_bwd(s["h_in"], row("mix_norm_g"), du, width=d, cblk=0, res=dh, name="d_mix_norm")
        small_g["mix_norm_g"][li] = g[0]
        in_flight.append(start_reduce(li, 0, big))

    dh3 = dh.reshape(b, lp, d)
    grad_x = dh3[:, N_META:l_valid]
    g_meta = _meta_grad(dh3, name="meta_grad")
    for item in in_flight:
        finish_reduce(item, g_meta)

    small_list = [g_meta, g_final[0]] + [jnp.stack(small_g[nm]) for nm in small_names]
    red = _allreduce_small(_pack_small(small_list), name="allreduce_small")
    red = _unpack_small(red.reshape(-1), [a.shape for a in small_list])
    grads = {"meta_tokens": lax.dynamic_slice_in_dim(red[0], chip * (d // N_CHIPS), d // N_CHIPS, axis=1),
             "final_norm_g": red[1]}
    for nm, val in zip(small_names, red[2:]):
        grads[nm] = val
    cs = c_conv // N_CHIPS
    grads["conv_dw"] = lax.dynamic_slice_in_dim(grads["conv_dw"], chip * cs, cs, axis=2).reshape(p["conv_dw"].shape)

    delta, new_m, new_v = {}, {}, {}
    for nm in p:
        if nm in adam:
            grads[nm], delta[nm], new_m[nm], new_v[nm] = (o.reshape(p[nm].shape) for o in adam[nm])
        else:
            delta[nm], new_m[nm], new_v[nm] = _adamw(p[nm], grads[nm], mom_m[nm], mom_v[nm], name="adamw_" + nm)
    return loss, grad_x, grads, delta, new_m, new_v


WEIGHT_NAMES = ("meta_tokens", "mix_norm_g", "w_in", "conv_dw", "conv_b", "conv_ln_g", "conv_ln_b", "w_conv_out",
                "gqa_q_norm_g", "gqa_k_norm_g", "w_gqa_out", "mla_q_norm_g", "w_mla_uq", "mla_kv_norm_g",
                "w_mla_ukv", "w_mla_out", "gate_b", "w_out", "ffn_norm_g", "w_ffn_gate", "w_ffn_up", "w_ffn_down",
                "final_norm_g")


def kernel(x, meta_tokens, mix_norm_g, w_in, conv_dw, conv_b, conv_ln_g, conv_ln_b, w_conv_out, gqa_q_norm_g, gqa_k_norm_g, w_gqa_out, mla_q_norm_g, w_mla_uq, mla_kv_norm_g, w_mla_ukv, w_mla_out, gate_b, w_out, ffn_norm_g, w_ffn_gate, w_ffn_up, w_ffn_down, final_norm_g, loss_target, m_meta_tokens, m_mix_norm_g, m_w_in, m_conv_dw, m_conv_b, m_conv_ln_g, m_conv_ln_b, m_w_conv_out, m_gqa_q_norm_g, m_gqa_k_norm_g, m_w_gqa_out, m_mla_q_norm_g, m_w_mla_uq, m_mla_kv_norm_g, m_w_mla_ukv, m_w_mla_out, m_gate_b, m_w_out, m_ffn_norm_g, m_w_ffn_gate, m_w_ffn_up, m_w_ffn_down, m_final_norm_g, v_meta_tokens, v_mix_norm_g, v_w_in, v_conv_dw, v_conv_b, v_conv_ln_g, v_conv_ln_b, v_w_conv_out, v_gqa_q_norm_g, v_gqa_k_norm_g, v_w_gqa_out, v_mla_q_norm_g, v_w_mla_uq, v_mla_kv_norm_g, v_w_mla_ukv, v_w_mla_out, v_gate_b, v_w_out, v_ffn_norm_g, v_w_ffn_gate, v_w_ffn_up, v_w_ffn_down, v_final_norm_g):
    ws = (meta_tokens, mix_norm_g, w_in, conv_dw, conv_b, conv_ln_g, conv_ln_b, w_conv_out, gqa_q_norm_g,
          gqa_k_norm_g, w_gqa_out, mla_q_norm_g, w_mla_uq, mla_kv_norm_g, w_mla_ukv, w_mla_out, gate_b, w_out,
          ffn_norm_g, w_ffn_gate, w_ffn_up, w_ffn_down, final_norm_g)
    ms = (m_meta_tokens, m_mix_norm_g, m_w_in, m_conv_dw, m_conv_b, m_conv_ln_g, m_conv_ln_b, m_w_conv_out,
          m_gqa_q_norm_g, m_gqa_k_norm_g, m_w_gqa_out, m_mla_q_norm_g, m_w_mla_uq, m_mla_kv_norm_g, m_w_mla_ukv,
          m_w_mla_out, m_gate_b, m_w_out, m_ffn_norm_g, m_w_ffn_gate, m_w_ffn_up, m_w_ffn_down, m_final_norm_g)
    vs = (v_meta_tokens, v_mix_norm_g, v_w_in, v_conv_dw, v_conv_b, v_conv_ln_g, v_conv_ln_b, v_w_conv_out,
          v_gqa_q_norm_g, v_gqa_k_norm_g, v_w_gqa_out, v_mla_q_norm_g, v_w_mla_uq, v_mla_kv_norm_g, v_w_mla_ukv,
          v_w_mla_out, v_gate_b, v_w_out, v_ffn_norm_g, v_w_ffn_gate, v_w_ffn_up, v_w_ffn_down, v_final_norm_g)
    p = dict(zip(WEIGHT_NAMES, ws))
    loss, grad_x, grads, delta, new_m, new_v = _step(p, dict(zip(WEIGHT_NAMES, ms)), dict(zip(WEIGHT_NAMES, vs)),
                                                     x, loss_target)
    return (loss, grad_x, *[grads[n] for n in WEIGHT_NAMES], *[delta[n] for n in WEIGHT_NAMES],
            *[new_m[n] for n in WEIGHT_NAMES], *[new_v[n] for n in WEIGHT_NAMES])
```
